```python
import math
import jax
import jax.numpy as jnp
from jax import lax
import numpy as np

D_MODEL = 1024
BATCH = 16
SEQ = 2048
DEPTH = 2

GRID_W = 64
CTX_LEN = 256
EPS = 1e-6

D_MIX = D_MODEL
HY_WIDTH = D_MIX // 2
GLA_WIDTH = D_MIX - HY_WIDTH

HY_ORDER = 2
HY_CONV = 3
HY_EMB = 33
HY_HIDDEN = 64
HY_FREQ = 1.0
HY_TARGET = 1e-2
HY_FAST_PCT = 0.3
HY_SLOW_PCT = 1.5
HY_MIN_DECAY = math.log(1.0 / HY_TARGET) / HY_SLOW_PCT
HY_MAX_DECAY = math.log(1.0 / HY_TARGET) / HY_FAST_PCT
HY_FILTER_STD = 0.01
HY_COLS = (HY_ORDER + 1) * HY_WIDTH

GLA_HEADS = 4
GLA_QK = GLA_WIDTH // 2
GLA_DK = GLA_QK // GLA_HEADS
GLA_DV = GLA_WIDTH // GLA_HEADS
GLA_RANK = 16
GLA_TAU = 16.0
GLA_CHUNK = 64
GLA_COLS = 2 * GLA_QK + 2 * GLA_WIDTH + 2 * GLA_RANK

IN_COLS = HY_COLS + GLA_COLS

N_EXPERTS = 256
TOP_K = 8
N_GROUPS = 8
TOPK_GROUPS = 4
D_EXPERT = 256
ROUTED_SCALE = 2.5
MOE_BLOCK = 128

kernel_name = 'hybrid_hyena_gla_moe_dit_block'


def rmsnorm(x, g):
    x32 = x.astype(jnp.float32)
    y = x32 * lax.rsqrt(jnp.mean(x32 * x32, axis=-1, keepdims=True) + EPS) * g.astype(jnp.float32)
    return y.astype(x.dtype)


def modulate(x, shift, scale):
    return x * (1.0 + scale) + shift


def rev(t):
    return jnp.flip(t, axis=1)


def grid_pos_emb(rows):
    r, col = jnp.meshgrid(jnp.arange(rows, dtype=jnp.float32), jnp.arange(GRID_W, dtype=jnp.float32), indexing='ij')
    quarter = D_MODEL // 4
    omega = 1.0 / (10000.0 ** (jnp.arange(quarter, dtype=jnp.float32) / quarter))
    def emb(p):
        a = p.reshape(-1)[:, None] * omega[None, :]
        return jnp.concatenate([jnp.sin(a), jnp.cos(a)], axis=-1)
    return jnp.concatenate([emb(r), emb(col)], axis=-1)


def hyena_pos_features(L):
    t = jnp.linspace(0.0, 1.0, L, dtype=jnp.float32)[:, None]
    bands = (HY_EMB - 1) // 2
    w = 2.0 * math.pi * jnp.arange(L, dtype=jnp.float32)[:, None] / L
    f = jnp.linspace(1e-4, bands - 1, bands, dtype=jnp.float32)[None, :]
    return jnp.concatenate([t, jnp.cos(f * w), -jnp.sin(f * w)], axis=-1)


def hyena_filters(L, w1, b1, w2, b2, w3):
    pos = hyena_pos_features(L).astype(w1.dtype)
    hid = jnp.sin(HY_FREQ * (pos @ w1 + b1))
    hid = jnp.sin(HY_FREQ * (hid @ w2 + b2))
    h = (hid @ w3).astype(jnp.float32).reshape(L, HY_ORDER, 2, HY_WIDTH)
    t = jnp.linspace(0.0, 1.0, L, dtype=jnp.float32)[:, None]
    deltas = jnp.linspace(HY_MIN_DECAY, HY_MAX_DECAY, HY_WIDTH, dtype=jnp.float32)[None, :]
    return h * jnp.exp(-t * deltas)[:, None, None, :]


def short_conv3(u, w, b):
    up = jnp.pad(u, ((0, 0), (1, 1), (0, 0)))
    return up[:, :-2] * w[0] + up[:, 1:-1] * w[1] + up[:, 2:] * w[2] + b


def long_conv(u, h_fwd, h_bwd, d_skip):
    L = u.shape[1]
    n = 2 * L
    u32 = u.astype(jnp.float32)
    H = jnp.fft.rfft(h_fwd, n=n, axis=0) + jnp.conj(jnp.fft.rfft(h_bwd, n=n, axis=0))
    y = jnp.fft.irfft(jnp.fft.rfft(u32, n=n, axis=1) * H[None], n=n, axis=1)[:, :L]
    return y + u32 * d_skip.astype(jnp.float32)


def hyena_mixer(zh, conv_w, conv_b, filt, skip):
    u = short_conv3(zh, conv_w, conv_b)
    v, x1, x2 = jnp.split(u, 3, axis=-1)
    y = x1.astype(jnp.float32) * long_conv(v, filt[:, 0, 0], filt[:, 0, 1], skip[0])
    y = x2.astype(jnp.float32) * long_conv(y, filt[:, 1, 0], filt[:, 1, 1], skip[1])
    return y.astype(zh.dtype)


def gla_inputs(zg, wa2, ba2):
    B, L, _ = zg.shape
    q, k, v, r, a = jnp.split(zg, [GLA_QK, 2 * GLA_QK, 2 * GLA_QK + GLA_WIDTH, 2 * GLA_QK + 2 * GLA_WIDTH], axis=-1)
    q = q.reshape(B, L, GLA_HEADS, GLA_DK) * (GLA_DK ** -0.5)
    k = k.reshape(B, L, GLA_HEADS, GLA_DK)
    v = v.reshape(B, L, GLA_HEADS, GLA_DV)
    a_f, a_b = jnp.split(a, 2, axis=-1)
    def log_gate(a_dir, w, bias):
        g = jax.nn.log_sigmoid((a_dir @ w + bias).astype(jnp.float32)) / GLA_TAU
        return g.reshape(B, L, GLA_HEADS, GLA_DK)
    return q, k, v, r, log_gate(a_f, wa2[0], ba2[0]), log_gate(a_b, wa2[1], ba2[1])


def gla_chunked(q, k, v, log_a, s0):
    B, L, H, DK = q.shape
    DV = v.shape[-1]
    n = L // GLA_CHUNK
    def chunks(t):
        return t.astype(jnp.float32).reshape(B, n, GLA_CHUNK, H, t.shape[-1]).transpose(1, 0, 3, 2, 4)
    qc, kc, vc, ac = chunks(q), chunks(k), chunks(v), chunks(log_a)
    b = jnp.cumsum(ac, axis=3)
    b_last = b[:, :, :, -1:, :]
    q_e = qc * jnp.exp(b)
    k_e = kc * jnp.exp(-b)
    k_s = kc * jnp.exp(b_last - b)
    mask = jnp.tril(jnp.ones((GLA_CHUNK, GLA_CHUNK), dtype=bool))
    att = jnp.where(mask, jnp.einsum('nbhtd,nbhsd->nbhts', q_e, k_e), 0.0)
    o_intra = jnp.einsum('nbhts,nbhsv->nbhtv', att, vc)
    decay = jnp.exp(b_last[:, :, :, 0, :])
    def step(S, inp):
        qe, ks, vv, dcy = inp
        o = jnp.einsum('bhtd,bhdv->bhtv', qe, S)
        S = dcy[..., None] * S + jnp.einsum('bhsd,bhsv->bhdv', ks, vv)
        return S, o
    s_fin, o_inter = lax.scan(step, s0, (q_e, k_s, vc, decay))
    o = (o_intra + o_inter).transpose(1, 0, 3, 2, 4).reshape(B, L, H, DV)
    return o, s_fin


def gla_final_state(k, v, log_a):
    b = jnp.cumsum(log_a, axis=1)
    ks = k.astype(jnp.float32) * jnp.exp(b[:, -1:] - b)
    return jnp.einsum('blhd,blhv->bhdv', ks, v.astype(jnp.float32))


def gla_output(o, r, g):
    B, L = o.shape[:2]
    o = o * lax.rsqrt(jnp.mean(o * o, axis=-1, keepdims=True) + EPS) * g.astype(jnp.float32)
    o = o.reshape(B, L, GLA_WIDTH) * jax.nn.silu(r.astype(jnp.float32))
    return o.astype(r.dtype)


def moe_ffn(u, w_r, b_r, w1, w3, w2, ws1, ws3, ws2):
    N, D = u.shape
    s = jax.nn.sigmoid((u @ w_r).astype(jnp.float32))
    sel = (s + b_r.astype(jnp.float32)).reshape(N, N_GROUPS, N_EXPERTS // N_GROUPS)
    group_score = lax.top_k(sel, 2)[0].sum(-1)
    _, gidx = lax.top_k(group_score, TOPK_GROUPS)
    gmask = jnp.sum(jax.nn.one_hot(gidx, N_GROUPS), axis=1) > 0
    sel = jnp.where(gmask[:, :, None], sel, -jnp.inf).reshape(N, N_EXPERTS)
    _, idx = lax.top_k(sel, TOP_K)
    gate = jnp.take_along_axis(s, idx, axis=1)
    gate = gate / jnp.sum(gate, axis=-1, keepdims=True) * ROUTED_SCALE

    NK = N * TOP_K
    flat_e = idx.reshape(NK).astype(jnp.int32)
    flat_tok = jnp.arange(NK, dtype=jnp.int32) // TOP_K
    flat_g = gate.reshape(NK)
    order = jnp.argsort(flat_e)
    se = flat_e[order]
    counts = jnp.bincount(flat_e, length=N_EXPERTS).astype(jnp.int32)
    start = jnp.cumsum(counts) - counts
    padded = (counts + MOE_BLOCK - 1) // MOE_BLOCK * MOE_BLOCK
    pad_end = jnp.cumsum(padded)
    pad_start = pad_end - padded
    dest = pad_start[se] + jnp.arange(NK, dtype=jnp.int32) - start[se]
    n_blocks = -(-(NK + N_EXPERTS * (MOE_BLOCK - 1)) // MOE_BLOCK)
    n_slots = n_blocks * MOE_BLOCK
    slot_tok = jnp.full((n_slots,), N, dtype=jnp.int32).at[dest].set(flat_tok[order])
    slot_gate = jnp.zeros((n_slots,), jnp.float32).at[dest].set(flat_g[order])
    block_start = jnp.arange(n_blocks, dtype=jnp.int32) * MOE_BLOCK
    block_e = jnp.minimum(jnp.searchsorted(pad_end, block_start, side='right'), N_EXPERTS - 1)
    u_pad = jnp.concatenate([u, jnp.zeros((1, D), u.dtype)], axis=0)

    def expert_block(acc, blk):
        tok, g, e = blk
        xb = u_pad[tok]
        hb = jax.nn.silu(xb @ w1[e]) * (xb @ w3[e])
        yb = (hb @ w2[e]).astype(jnp.float32) * g[:, None]
        return acc.at[tok].add(yb), None

    acc, _ = lax.scan(expert_block, jnp.zeros((N + 1, D), jnp.float32),
                      (slot_tok.reshape(n_blocks, MOE_BLOCK), slot_gate.reshape(n_blocks, MOE_BLOCK), block_e))
    shared = (jax.nn.silu(u @ ws1) * (u @ ws3)) @ ws2
    return (acc[:N] + shared.astype(jnp.float32)).astype(u.dtype)


def setup_inputs(seed: int = 0) -> dict:
    key = jax.random.key(seed)
    k = jax.random.split(key, 30)
    D, E, F = D_MODEL, N_EXPERTS, D_EXPERT
    def nrm(kk, shape, scale):
        return jax.random.normal(kk, shape, jnp.float32) * scale
    return {
        'x': nrm(k[0], (BATCH, SEQ, D), 1.0),
        'c': nrm(k[1], (BATCH, D), 1.0),
        'ctx': nrm(k[2], (BATCH, CTX_LEN, D), 1.0),
        'c_ctx': nrm(k[3], (D,), 1.0),
        'ada_w': nrm(k[4], (DEPTH, D, 6 * D), 0.5 * D ** -0.5),
        'ada_b': nrm(k[5], (DEPTH, 6 * D), 0.02),
        'norm1_g': 1.0 + nrm(k[6], (DEPTH, D), 0.02),
        'norm2_g': 1.0 + nrm(k[7], (DEPTH, D), 0.02),
        'w_in': nrm(k[8], (DEPTH, D, IN_COLS), D ** -0.5),
        'w_out': nrm(k[9], (DEPTH, D_MIX, D), D_MIX ** -0.5),
        'hy_conv_w': nrm(k[10], (DEPTH, HY_CONV, HY_COLS), HY_CONV ** -0.5),
        'hy_conv_b': nrm(k[11], (DEPTH, HY_COLS), 0.02),
        'hy_f_w1': nrm(k[12], (DEPTH, HY_EMB, HY_HIDDEN), HY_EMB ** -0.5),
        'hy_f_b1': nrm(k[13], (DEPTH, HY_HIDDEN), 0.1),
        'hy_f_w2': nrm(k[14], (DEPTH, HY_HIDDEN, HY_HIDDEN), HY_HIDDEN ** -0.5),
        'hy_f_b2': nrm(k[15], (DEPTH, HY_HIDDEN), 0.1),
        'hy_f_w3': nrm(k[16], (DEPTH, HY_HIDDEN, HY_ORDER * 2 * HY_WIDTH), HY_FILTER_STD),
        'hy_skip': nrm(k[17], (DEPTH, HY_ORDER, HY_WIDTH), 0.5),
        'gla_wa2': nrm(k[18], (DEPTH, 2, GLA_RANK, GLA_QK), GLA_RANK ** -0.5),
        'gla_ba2': nrm(k[19], (DEPTH, 2, GLA_QK), 0.5),
        'gla_norm_g': 1.0 + nrm(k[20], (DEPTH, GLA_DV), 0.02),
        'router_w': nrm(k[21], (DEPTH, D, E), D ** -0.5),
        'router_b': nrm(k[22], (DEPTH, E), 0.01),
        'exp_w1': nrm(k[23], (DEPTH, E, D, F), D ** -0.5),
        'exp_w3': nrm(k[24], (DEPTH, E, D, F), D ** -0.5),
        'exp_w2': nrm(k[25], (DEPTH, E, F, D), F ** -0.5),
        'sh_w1': nrm(k[26], (DEPTH, D, F), D ** -0.5),
        'sh_w3': nrm(k[27], (DEPTH, D, F), D ** -0.5),
        'sh_w2': nrm(k[28], (DEPTH, F, D), F ** -0.5),
        'final_g': 1.0 + nrm(k[29], (D,), 0.02),
    }


def reference(x, c, ctx, c_ctx, ada_w, ada_b, norm1_g, norm2_g, w_in, w_out, hy_conv_w, hy_conv_b,
              hy_f_w1, hy_f_b1, hy_f_w2, hy_f_b2, hy_f_w3, hy_skip, gla_wa2, gla_ba2, gla_norm_g,
              router_w, router_b, exp_w1, exp_w3, exp_w2, sh_w1, sh_w3, sh_w2, final_g):
    B, L, D = x.shape
    Lc = ctx.shape[1]
    rows = L // GRID_W
    xs = x + grid_pos_emb(rows).astype(x.dtype)
    xc = ctx
    cond = jax.nn.silu(c)
    cond_ctx = jax.nn.silu(c_ctx)
    zero_state = jnp.zeros((B, GLA_HEADS, GLA_DK, GLA_DV), jnp.float32)
    for l in range(DEPTH):
        last = l == DEPTH - 1
        mod = (cond @ ada_w[l] + ada_b[l])[:, None, :]
        mod_c = (cond_ctx @ ada_w[l] + ada_b[l])[None, None, :]
        sh1, sc1, g1, sh2, sc2, g2 = jnp.split(mod, 6, axis=-1)
        csh1, csc1, cg1, csh2, csc2, cg2 = jnp.split(mod_c, 6, axis=-1)
        hy_w = (hy_f_w1[l], hy_f_b1[l], hy_f_w2[l], hy_f_b2[l], hy_f_w3[l])

        z = modulate(rmsnorm(xs, norm1_g[l]), sh1, sc1) @ w_in[l]
        hcn = modulate(rmsnorm(xc, norm1_g[l]), csh1, csc1)
        q, k, v, r, la_f, la_b = gla_inputs(z[..., HY_COLS:], gla_wa2[l], gla_ba2[l])
        if last:
            _, kc, vc, _, lac_f, lac_b = gla_inputs(hcn @ w_in[l][:, HY_COLS:], gla_wa2[l], gla_ba2[l])
            s_f = gla_final_state(kc, vc, lac_f)
            s_b = gla_final_state(rev(kc), rev(vc), rev(lac_b))
        else:
            zc = hcn @ w_in[l]
            qc, kc, vc, rc, lac_f, lac_b = gla_inputs(zc[..., HY_COLS:], gla_wa2[l], gla_ba2[l])
            oc_f, s_f = gla_chunked(qc, kc, vc, lac_f, zero_state)
            oc_b, s_b = gla_chunked(rev(qc), rev(kc), rev(vc), rev(lac_b), zero_state)
            yc_gla = gla_output(oc_f + rev(oc_b), rc, gla_norm_g[l])
            yc_hy = hyena_mixer(zc[..., :HY_COLS], hy_conv_w[l], hy_conv_b[l],
                                hyena_filters(Lc, *hy_w), hy_skip[l])
            xc = xc + cg1 * (jnp.concatenate([yc_hy, yc_gla], axis=-1) @ w_out[l])
        y_hy = hyena_mixer(z[..., :HY_COLS], hy_conv_w[l], hy_conv_b[l], hyena_filters(L, *hy_w), hy_skip[l])
        o_f, _ = gla_chunked(q, k, v, la_f, s_f)
        o_b, _ = gla_chunked(rev(q), rev(k), rev(v), rev(la_b), s_b)
        y_gla = gla_output(o_f + rev(o_b), r, gla_norm_g[l])
        xs = xs + g1 * (jnp.concatenate([y_hy, y_gla], axis=-1) @ w_out[l])

        moe_w = (router_w[l], router_b[l], exp_w1[l], exp_w3[l], exp_w2[l], sh_w1[l], sh_w3[l], sh_w2[l])
        u = modulate(rmsnorm(xs, norm2_g[l]), sh2, sc2).reshape(B * L, D)
        if last:
            xs = xs + g2 * moe_ffn(u, *moe_w).reshape(B, L, D)
        else:
            uc = modulate(rmsnorm(xc, norm2_g[l]), csh2, csc2).reshape(B * Lc, D)
            out = moe_ffn(jnp.concatenate([u, uc], axis=0), *moe_w)
            xs = xs + g2 * out[:B * L].reshape(B, L, D)
            xc = xc + cg2 * out[B * L:].reshape(B, Lc, D)
    return rmsnorm(xs, final_g)
```

```python
import functools
import math

import numpy as np
import jax
import jax.numpy as jnp
from jax import lax
from jax.experimental import pallas as pl
from jax.experimental.pallas import tpu as pltpu

GRID_W = 64
EPS = 1e-6

HY_ORDER = 2
HY_EMB = 33
HY_EMB_PAD = 40
HY_FREQ = 1.0
HY_TARGET = 1e-2
HY_FAST_PCT = 0.3
HY_SLOW_PCT = 1.5
HY_MIN_DECAY = math.log(1.0 / HY_TARGET) / HY_SLOW_PCT
HY_MAX_DECAY = math.log(1.0 / HY_TARGET) / HY_FAST_PCT

GLA_HEADS = 4
GLA_RANK = 16
GLA_TAU = 16.0
GLA_CHUNK = 64

TOP_K = 8
N_GROUPS = 8
TOPK_GROUPS = 4
ROUTED_SCALE = 2.5

MOE_ROWS = 256
SLAB = 8
LANES = 128
VMEM_LIMIT = 56 * 1024 * 1024

F32 = jnp.float32
BF16 = jnp.bfloat16


def _cparams(sem, vmem=None):
    return pltpu.CompilerParams(dimension_semantics=sem, vmem_limit_bytes=vmem)


def _split(a):
    hi = a.astype(BF16)
    lo = (a - hi.astype(F32)).astype(BF16)
    return hi, lo


def _dot(a, b):
    return jnp.dot(a, b, preferred_element_type=F32)


def _dot3(a, b):
    ah, al = _split(a)
    bh, bl = _split(b)
    return _dot(ah, bh) + _dot(ah, bl) + _dot(al, bh)


def _silu(x):
    return x * (1.0 / (1.0 + jnp.exp(-x)))


def _rms(x, g):
    return x * lax.rsqrt(jnp.mean(x * x, axis=-1, keepdims=True) + EPS) * g


def _ada_kernel(c_ref, w_ref, b_ref, o_ref):
    o_ref[0] = _dot3(_silu(c_ref[...]), w_ref[0]) + b_ref[0]


def ada_table(cond_rows, ada_w, ada_b):
    depth, d, six_d = ada_w.shape
    r = cond_rows.shape[0]
    tn = 1024
    return pl.pallas_call(
        _ada_kernel,
        grid=(depth, six_d // tn),
        in_specs=[pl.BlockSpec((r, d), lambda l, j: (0, 0)),
                  pl.BlockSpec((1, d, tn), lambda l, j: (l, 0, j)),
                  pl.BlockSpec((1, 1, tn), lambda l, j: (l, 0, j))],
        out_specs=pl.BlockSpec((1, r, tn), lambda l, j: (l, 0, j)),
        out_shape=jax.ShapeDtypeStruct((depth, r, six_d), F32),
        compiler_params=_cparams(("arbitrary", "arbitrary")),
        name="ada_table",
    )(cond_rows, ada_w, ada_b.reshape(depth, 1, six_d))


def _embed_kernel(n_lat_tiles, x_ref, p_ref, c_ref, o_ref):
    i = pl.program_id(0)

    @pl.when(i < n_lat_tiles)
    def _():
        o_ref[...] = x_ref[...] + p_ref[...]

    @pl.when(i >= n_lat_tiles)
    def _():
        o_ref[...] = c_ref[...]


def embed_tokens(x2, pos, ctx2, tm):
    nl, d = x2.shape
    nc = ctx2.shape[0]
    l = pos.shape[0]
    nlt, nct, lt = nl // tm, nc // tm, l // tm
    return pl.pallas_call(
        functools.partial(_embed_kernel, nlt),
        grid=(nlt + nct,),
        in_specs=[pl.BlockSpec((tm, d), lambda i: (jnp.minimum(i, nlt - 1), 0)),
                  pl.BlockSpec((tm, d), lambda i: (i % lt, 0)),
                  pl.BlockSpec((tm, d), lambda i: (jnp.maximum(i - nlt, 0), 0))],
        out_specs=pl.BlockSpec((tm, d), lambda i: (i, 0)),
        out_shape=jax.ShapeDtypeStruct((nl + nc, d), F32),
        compiler_params=_cparams(("arbitrary",)),
        name="embed_tokens",
    )(x2, pos, ctx2)


def _mod_row(i, tm, seq, n_lat_tiles, b):
    return jnp.where(i < n_lat_tiles, (i * tm) // seq, b)


def _inproj_kernel(x_ref, g_ref, sh_ref, sc_ref, wh_ref, wg_ref, wa_ref, wa2_ref, ba2_ref,
                   zh_ref, zg_ref, la_ref):
    x = x_ref[...]
    h = _rms(x, g_ref[...]) * (1.0 + sc_ref[0]) + sh_ref[0]
    hb = h.astype(BF16)
    zh_ref[...] = _dot(hb, wh_ref[...]).astype(BF16)
    zg_ref[...] = _dot(hb, wg_ref[...]).astype(BF16)
    za = _dot(hb, wa_ref[...])
    t = _dot3(za, wa2_ref[...]) + ba2_ref[...]
    la_ref[...] = (jnp.minimum(t, 0.0) - jnp.log(1.0 + jnp.exp(-jnp.abs(t)))) * (1.0 / GLA_TAU)


def in_proj(xs, norm_g, mod, w_hy, w_g, w_a, wa2_bd, ba2_cat, tm, seq, n_lat, b):
    n, d = xs.shape
    nlt = n_lat // tm
    row = functools.partial(_mod_row, tm=tm, seq=seq, n_lat_tiles=nlt, b=b)
    hy_cols, g_cols, la_cols = w_hy.shape[1], w_g.shape[1], wa2_bd.shape[1]
    full = lambda a: pl.BlockSpec(a.shape, lambda i: (0,) * a.ndim)
    return pl.pallas_call(
        _inproj_kernel,
        grid=(n // tm,),
        in_specs=[pl.BlockSpec((tm, d), lambda i: (i, 0)),
                  full(norm_g),
                  pl.BlockSpec((1, 1, d), lambda i: (row(i), 0, 0)),
                  pl.BlockSpec((1, 1, d), lambda i: (row(i), 0, 1)),
                  full(w_hy), full(w_g), full(w_a), full(wa2_bd), full(ba2_cat)],
        out_specs=[pl.BlockSpec((tm, hy_cols), lambda i: (i, 0)),
                   pl.BlockSpec((tm, g_cols), lambda i: (i, 0)),
                   pl.BlockSpec((tm, la_cols), lambda i: (i, 0))],
        out_shape=[jax.ShapeDtypeStruct((n, hy_cols), BF16),
                   jax.ShapeDtypeStruct((n, g_cols), BF16),
                   jax.ShapeDtypeStruct((n, la_cols), F32)],
        compiler_params=_cparams(("arbitrary",), VMEM_LIMIT),
        name="in_proj",
    )(xs, norm_g, mod, mod, w_hy, w_g, w_a, wa2_bd, ba2_cat)


def _trig_tables(l, half_shift):
    k = lax.broadcasted_iota(jnp.int32, (l, l), 0)
    n = lax.broadcasted_iota(jnp.int32, (l, l), 1)
    m = ((2 * k + 1) * (2 * n + (1 if half_shift else 0))) % (8 * l)
    ang = m.astype(F32) * (2.0 * math.pi / (8 * l))
    return jnp.cos(ang).astype(BF16), jnp.sin(ang).astype(BF16)


def _filter_kernel(n_orders, width, l, tk,
                   pos_ref, t_ref, dl_ref, w1_ref, b1_ref, w2_ref, b2_ref, w3_ref, skip_ref,
                   c0_ref, s0_ref, gr_ref, gi_ref, hs_ref, hd_ref):
    j = pl.program_id(0)

    @pl.when(j == 0)
    def _():
        hid = jnp.sin(HY_FREQ * (_dot3(pos_ref[...], w1_ref[...]) + b1_ref[...]))
        hid = jnp.sin(HY_FREQ * (_dot3(hid, w2_ref[...]) + b2_ref[...]))
        h = _dot3(hid, w3_ref[...])
        decay = jnp.exp(-t_ref[...] * dl_ref[...])
        for o in range(n_orders):
            hf = h[:, (2 * o) * width:(2 * o + 1) * width] * decay
            hb = h[:, (2 * o + 1) * width:(2 * o + 2) * width] * decay
            hs_ref[:, o * width:(o + 1) * width] = (hf + hb).astype(BF16)
            hd_ref[:, o * width:(o + 1) * width] = (hb - hf).astype(BF16)

    scale = 1.0 / l
    gr = _dot(c0_ref[...], hs_ref[...])
    gi = _dot(s0_ref[...], hd_ref[...])
    for o in range(n_orders):
        gr_ref[o] = (gr[:, o * width:(o + 1) * width] + skip_ref[o]) * scale
        gi_ref[o] = gi[:, o * width:(o + 1) * width] * scale


def hyena_spectra(l, w1, b1, w2, b2, w3, skip, c0, s0):
    width = skip.shape[-1]
    n_orders = skip.shape[0]
    t = jnp.linspace(0.0, 1.0, l, dtype=F32)[:, None]
    bands = (HY_EMB - 1) // 2
    w = 2.0 * math.pi * jnp.arange(l, dtype=F32)[:, None] / l
    f = jnp.linspace(1e-4, bands - 1, bands, dtype=F32)[None, :]
    pos = jnp.concatenate([t, jnp.cos(f * w), -jnp.sin(f * w),
                           jnp.zeros((l, HY_EMB_PAD - HY_EMB), F32)], axis=-1)
    w1p = jnp.concatenate([w1, jnp.zeros((HY_EMB_PAD - HY_EMB, w1.shape[1]), F32)], axis=0)
    deltas = jnp.linspace(HY_MIN_DECAY, HY_MAX_DECAY, width, dtype=F32)[None, :]
    tk = min(l, 256)
    full = lambda a: pl.BlockSpec(a.shape, lambda j: (0,) * a.ndim)
    skip3 = skip.reshape(n_orders, 1, width)
    args = (pos, t, deltas, w1p, b1.reshape(1, -1), w2, b2.reshape(1, -1), w3, skip3)
    return pl.pallas_call(
        functools.partial(_filter_kernel, n_orders, width, l, tk),
        grid=(l // tk,),
        in_specs=[full(a) for a in args] + [pl.BlockSpec((tk, l), lambda j: (j, 0)),
                                            pl.BlockSpec((tk, l), lambda j: (j, 0))],
        out_specs=[pl.BlockSpec((n_orders, tk, width), lambda j: (0, j, 0)),
                   pl.BlockSpec((n_orders, tk, width), lambda j: (0, j, 0))],
        out_shape=[jax.ShapeDtypeStruct((n_orders, l, width), F32)] * 2,
        scratch_shapes=[pltpu.VMEM((l, n_orders * width), BF16),
                        pltpu.VMEM((l, n_orders * width), BF16)],
        compiler_params=_cparams(("arbitrary",), VMEM_LIMIT),
        name="hyena_spectra",
    )(*args, c0, s0)


def _hyena_kernel(l, zv_ref, z1_ref, z2_ref, wv_ref, w1_ref, w2_ref, bv_ref, b1_ref, b2_ref,
                  c_ref, s_ref, gr_ref, gi_ref, y_ref):
    rows = lax.broadcasted_iota(jnp.int32, (l, 1), 0)

    def conv3(z_ref, w_ref, b_ref):
        z = z_ref[...].astype(F32)
        zm = jnp.where(rows == 0, 0.0, pltpu.roll(z, 1, 0))
        zp = jnp.where(rows == l - 1, 0.0, pltpu.roll(z, l - 1, 0))
        return zm * w_ref[0:1, :] + z * w_ref[1:2, :] + zp * w_ref[2:3, :] + b_ref[...]

    def long_conv(u, o):
        ub = u.astype(BF16)
        ur = _dot(c_ref[...], ub)
        us = _dot(s_ref[...], ub)
        gr = gr_ref[o]
        gi = gi_ref[o]
        a = (ur * gr + us * gi).astype(BF16)
        b = (us * gr - ur * gi).astype(BF16)
        return _dot(c_ref[...], a) + _dot(s_ref[...], b)

    y = conv3(z1_ref, w1_ref, b1_ref) * long_conv(conv3(zv_ref, wv_ref, bv_ref), 0)
    y = conv3(z2_ref, w2_ref, b2_ref) * long_conv(y, 1)
    y_ref[...] = y.astype(BF16)


def hyena_mixer(zh, conv_w, conv_b, ctab, stab, gr, gi, l, n_seq, row0, cb):
    width = gr.shape[-1]
    n_orders = gr.shape[0]
    ncb = width // cb
    sb0 = row0 // l
    conv_b2 = conv_b.reshape(1, -1)
    zspec = lambda part: pl.BlockSpec((l, cb), lambda c, s: (sb0 + s, part * ncb + c))
    wspec = lambda part: pl.BlockSpec((3, cb), lambda c, s: (0, part * ncb + c))
    bspec = lambda part: pl.BlockSpec((1, cb), lambda c, s: (0, part * ncb + c))
    once = pl.Buffered(1)
    tspec = pl.BlockSpec((l, l), lambda c, s: (0, 0), pipeline_mode=once)
    gspec = pl.BlockSpec((n_orders, l, cb), lambda c, s: (0, 0, c), pipeline_mode=once)
    return pl.pallas_call(
        functools.partial(_hyena_kernel, l),
        grid=(ncb, n_seq),
        in_specs=[zspec(0), zspec(1), zspec(2), wspec(0), wspec(1), wspec(2),
                  bspec(0), bspec(1), bspec(2), tspec, tspec, gspec, gspec],
        out_specs=pl.BlockSpec((l, cb), lambda c, s: (s, c)),
        out_shape=jax.ShapeDtypeStruct((n_seq * l, width), BF16),
        compiler_params=_cparams(("arbitrary", "arbitrary"), VMEM_LIMIT),
        name="hyena_mixer",
    )(zh, zh, zh, conv_w, conv_w, conv_w, conv_b2, conv_b2, conv_b2, ctab, stab, gr, gi)


def _gla_kernel(l, qk, wv, dk, dv,
                zg_ref, la_ref, s0_ref, g_ref, y_ref, sfin_ref, of_ref, ob_ref, st_ref):
    ch = GLA_CHUNK
    n = l // ch
    heads = qk // dk
    r_i = lax.broadcasted_iota(jnp.int32, (ch, ch), 0)
    c_i = lax.broadcasted_iota(jnp.int32, (ch, ch), 1)
    low = r_i >= c_i
    upp = r_i <= c_i
    tri = (low.astype(BF16), upp.astype(BF16))
    masks = (low, upp)
    bd = (lax.broadcasted_iota(jnp.int32, (wv, qk), 0) // dv
          == lax.broadcasted_iota(jnp.int32, (wv, qk), 1) // dk).astype(F32)
    lane_head = lax.broadcasted_iota(jnp.int32, (1, qk), 1) // dk
    scale = dk ** -0.5

    st_ref[...] = s0_ref[0]

    def chunk(i, d):
        r0 = pl.multiple_of(i * ch, ch)
        q = zg_ref[pl.ds(r0, ch), 0:qk].astype(F32) * scale
        k = zg_ref[pl.ds(r0, ch), qk:2 * qk].astype(F32)
        v = zg_ref[pl.ds(r0, ch), 2 * qk:2 * qk + wv]
        la = la_ref[pl.ds(r0, ch), d * qk:(d + 1) * qk]
        lh, ll = _split(la)
        b = _dot(tri[d], lh) + _dot(tri[d], ll)
        tot = b[ch - 1:ch, :] if d == 0 else b[0:1, :]
        qe = (q * jnp.exp(b)).astype(BF16)
        ke = (k * jnp.exp(-b)).astype(BF16)
        ks = (k * jnp.exp(tot - b)).astype(BF16)
        st = st_ref[d]
        o = lax.dot_general(qe, st.astype(BF16), (((1,), (1,)), ((), ())),
                            preferred_element_type=F32)
        parts = []
        for h in range(heads):
            qh = jnp.where(lane_head == h, qe, jnp.zeros_like(qe))
            att = lax.dot_general(qh, ke, (((1,), (1,)), ((), ())), preferred_element_type=F32)
            att = jnp.where(masks[d], att, 0.0).astype(BF16)
            parts.append(_dot(att, v[:, h * dv:(h + 1) * dv]))
        o = o + jnp.concatenate(parts, axis=1)
        upd = lax.dot_general(v, ks, (((0,), (0,)), ((), ())), preferred_element_type=F32)
        st_ref[d] = st * jnp.exp(tot) + upd * bd
        return r0, o

    def body(i, carry):
        r0, o = chunk(i, 0)
        of_ref[pl.ds(r0, ch), :] = o
        r0, o = chunk(n - 1 - i, 1)
        ob_ref[pl.ds(r0, ch), :] = o
        return carry

    lax.fori_loop(0, n, body, 0)
    sfin_ref[0] = st_ref[...]

    def finish(i, carry):
        r0 = pl.multiple_of(i * ch, ch)
        o = of_ref[pl.ds(r0, ch), :] + ob_ref[pl.ds(r0, ch), :]
        r = zg_ref[pl.ds(r0, ch), 2 * qk + wv:2 * qk + 2 * wv].astype(F32)
        parts = [_rms(o[:, h * dv:(h + 1) * dv], g_ref[...]) for h in range(heads)]
        y_ref[pl.ds(r0, ch), :] = (jnp.concatenate(parts, axis=1) * _silu(r)).astype(BF16)
        return carry

    lax.fori_loop(0, n, finish, 0)


def gla_mixer(zg, la, s0, norm_g, l, n_seq, row0):
    qk = la.shape[1] // 2
    wv = (zg.shape[1] - 2 * qk) // 2
    dk, dv = qk // GLA_HEADS, wv // GLA_HEADS
    sb0 = row0 // l
    return pl.pallas_call(
        functools.partial(_gla_kernel, l, qk, wv, dk, dv),
        grid=(n_seq,),
        in_specs=[pl.BlockSpec((l, zg.shape[1]), lambda s: (sb0 + s, 0)),
                  pl.BlockSpec((l, la.shape[1]), lambda s: (sb0 + s, 0)),
                  pl.BlockSpec((1, 2, wv, qk), lambda s: (s, 0, 0, 0)),
                  pl.BlockSpec((1, dv), lambda s: (0, 0))],
        out_specs=[pl.BlockSpec((l, wv), lambda s: (s, 0)),
                   pl.BlockSpec((1, 2, wv, qk), lambda s: (s, 0, 0, 0))],
        out_shape=[jax.ShapeDtypeStruct((n_seq * l, wv), BF16),
                   jax.ShapeDtypeStruct((n_seq, 2, wv, qk), F32)],
        scratch_shapes=[pltpu.VMEM((l, wv), F32), pltpu.VMEM((l, wv), F32),
                        pltpu.VMEM((2, wv, qk), F32)],
        compiler_params=_cparams(("arbitrary",), VMEM_LIMIT),
        name="gla_mixer",
    )(zg, la, s0, norm_g.reshape(1, dv))


def _outproj_kernel(yh_ref, yg_ref, wh_ref, wg_ref, x_ref, g1_ref, g_ref, sh_ref, sc_ref,
                    xo_ref, u_ref):
    delta = _dot(yh_ref[...], wh_ref[...]) + _dot(yg_ref[...], wg_ref[...])
    x = x_ref[...] + g1_ref[0] * delta
    xo_ref[...] = x
    u_ref[...] = _rms(x, g_ref[...]) * (1.0 + sc_ref[0]) + sh_ref[0]


def out_proj(y_hy, y_gla, w_hy, w_gla, xs, mod, norm_g, tm, seq, n_lat, b, n_tok):
    d = xs.shape[1]
    nlt = n_lat // tm
    row = functools.partial(_mod_row, tm=tm, seq=seq, n_lat_tiles=nlt, b=b)
    full = lambda a: pl.BlockSpec(a.shape, lambda i: (0,) * a.ndim)
    modspec = lambda col: pl.BlockSpec((1, 1, d), lambda i: (row(i), 0, col))
    tile = lambda w: pl.BlockSpec((tm, w), lambda i: (i, 0))
    return pl.pallas_call(
        _outproj_kernel,
        grid=(n_tok // tm,),
        in_specs=[tile(y_hy.shape[1]), tile(y_gla.shape[1]), full(w_hy), full(w_gla), tile(d),
                  modspec(2), full(norm_g), modspec(3), modspec(4)],
        out_specs=[tile(d), tile(d)],
        out_shape=[jax.ShapeDtypeStruct((n_tok, d), F32)] * 2,
        compiler_params=_cparams(("arbitrary",), VMEM_LIMIT),
        name="out_proj",
    )(y_hy, y_gla, w_hy, w_gla, xs, mod, norm_g, mod, mod)


def _router_kernel(n_exp, tm, u_ref, wh_ref, wl_ref, b_ref, tri_ref,
                   idx_ref, gate_ref, rank_ref, cnt_ref):
    i = pl.program_id(0)

    @pl.when(i == 0)
    def _():
        cnt_ref[...] = jnp.zeros_like(cnt_ref)

    uh, ul = _split(u_ref[...])
    nt = (((1,), (1,)), ((), ()))
    logits = (lax.dot_general(wh_ref[...], uh, nt, preferred_element_type=F32)
              + lax.dot_general(wh_ref[...], ul, nt, preferred_element_type=F32)
              + lax.dot_general(wl_ref[...], uh, nt, preferred_element_type=F32))
    s = 1.0 / (1.0 + jnp.exp(-logits))
    sel = s + b_ref[...]
    gsz = n_exp // N_GROUPS
    neg = -jnp.inf
    rows_g = lax.broadcasted_iota(jnp.int32, (gsz, tm), 0)
    blocks, gscore = [], []
    for g in range(N_GROUPS):
        blk = sel[g * gsz:(g + 1) * gsz, :]
        m1 = jnp.max(blk, axis=0, keepdims=True)
        i1 = jnp.min(jnp.where(blk == m1, rows_g, gsz), axis=0, keepdims=True)
        m2 = jnp.max(jnp.where(rows_g == i1, neg, blk), axis=0, keepdims=True)
        blocks.append(blk)
        gscore.append(m1 + m2)
    masked = []
    for g in range(N_GROUPS):
        beaten = jnp.zeros((1, tm), jnp.int32)
        for h in range(N_GROUPS):
            if h == g:
                continue
            wins = (gscore[h] > gscore[g]) | ((gscore[h] == gscore[g]) & (h < g))
            beaten = beaten + wins.astype(jnp.int32)
        masked.append(jnp.where(beaten < TOPK_GROUPS, blocks[g], neg))
    sel = jnp.concatenate(masked, axis=0)

    rows = lax.broadcasted_iota(jnp.int32, (n_exp, tm), 0)
    chosen = jnp.zeros((n_exp, tm), F32)
    idxs, gates = [], []
    for _ in range(TOP_K):
        m = jnp.max(sel, axis=0, keepdims=True)
        ik = jnp.min(jnp.where(sel == m, rows, n_exp), axis=0, keepdims=True)
        hit = rows == ik
        gates.append(jnp.sum(jnp.where(hit, s, 0.0), axis=0, keepdims=True))
        sel = jnp.where(hit, neg, sel)
        chosen = jnp.where(hit, 1.0, chosen)
        idxs.append(ik)
    gsum = gates[0]
    for g in gates[1:]:
        gsum = gsum + g
    inv = ROUTED_SCALE / gsum
    before = _dot(chosen.astype(BF16), tri_ref[...]) + cnt_ref[...]
    for k in range(TOP_K):
        idx_ref[k:k + 1, :] = idxs[k]
        gate_ref[k:k + 1, :] = gates[k] * inv
        rk = jnp.sum(jnp.where(rows == idxs[k], before, 0.0), axis=0, keepdims=True)
        rank_ref[k:k + 1, :] = rk.astype(jnp.int32)
    cnt_ref[...] += jnp.sum(chosen, axis=1, keepdims=True)


def moe_router(u, wr_hi, wr_lo, b_r, n_tok, tm):
    n_exp, d = wr_hi.shape
    tri = (lax.broadcasted_iota(jnp.int32, (tm, tm), 0)
           < lax.broadcasted_iota(jnp.int32, (tm, tm), 1)).astype(BF16)
    full = lambda a: pl.BlockSpec(a.shape, lambda i: (0,) * a.ndim)
    tok = pl.BlockSpec((TOP_K, tm), lambda i: (0, i))
    return pl.pallas_call(
        functools.partial(_router_kernel, n_exp, tm),
        grid=(n_tok // tm,),
        in_specs=[pl.BlockSpec((tm, d), lambda i: (i, 0)), full(wr_hi), full(wr_lo),
                  pl.BlockSpec((n_exp, 1), lambda i: (0, 0)), full(tri)],
        out_specs=[tok, tok, tok, pl.BlockSpec((n_exp, 1), lambda i: (0, 0))],
        out_shape=[jax.ShapeDtypeStruct((TOP_K, n_tok), jnp.int32),
                   jax.ShapeDtypeStruct((TOP_K, n_tok), F32),
                   jax.ShapeDtypeStruct((TOP_K, n_tok), jnp.int32),
                   jax.ShapeDtypeStruct((n_exp, 1), F32)],
        compiler_params=_cparams(("arbitrary",), VMEM_LIMIT),
        name="moe_router",
    )(u, wr_hi, wr_lo, b_r.reshape(n_exp, 1), tri)


def _row_copy(src, src_row, dst, dst_row, sem):
    return pltpu.make_async_copy(src.at[pl.ds(pl.multiple_of(src_row * SLAB, SLAB), SLAB)],
                                 dst.at[pl.ds(pl.multiple_of(dst_row * SLAB, SLAB), SLAB)], sem)


def _dispatch_kernel(tm, dest_ref, u_ref, xin_ref, xs_ref, slab_ref, sem):
    del xin_ref
    d = u_ref.shape[1]
    for j in range(d // LANES):
        slab_ref[pl.ds(j, tm, stride=SLAB), :] = u_ref[:, j * LANES:(j + 1) * LANES]

    def issue(t, carry):
        for k in range(TOP_K):
            _row_copy(slab_ref, t, xs_ref, dest_ref[k, t], sem).start()
        return carry

    lax.fori_loop(0, tm, issue, 0)

    def drain(t, carry):
        for k in range(TOP_K):
            _row_copy(slab_ref, t, xs_ref, dest_ref[k, t], sem).wait()
        return carry

    lax.fori_loop(0, tm, drain, 0)


def moe_dispatch(u, dest, n_tok, n_slots, tm):
    d = u.shape[1]
    zeros = jnp.zeros((n_slots * SLAB, LANES), F32)
    return pl.pallas_call(
        functools.partial(_dispatch_kernel, tm),
        grid=(n_tok // tm,),
        in_specs=[pl.BlockSpec((TOP_K, tm), lambda i: (0, i), memory_space=pltpu.SMEM),
                  pl.BlockSpec((tm, d), lambda i: (i, 0)),
                  pl.BlockSpec(memory_space=pl.ANY)],
        out_specs=pl.BlockSpec(memory_space=pl.ANY),
        out_shape=jax.ShapeDtypeStruct((n_slots * SLAB, LANES), F32),
        scratch_shapes=[pltpu.VMEM((tm * SLAB, LANES), F32), pltpu.SemaphoreType.DMA],
        input_output_aliases={2: 0},
        compiler_params=_cparams(("arbitrary",)),
        name="moe_dispatch",
    )(dest, u, zeros)


def _gmm_kernel(be_ref, nu_ref, x_ref, w1_ref, w3_ref, w2_ref, y_ref, w1b, w3b, w2b):
    i = pl.program_id(0)
    bm = x_ref.shape[0] // SLAB
    d = w1_ref.shape[1]
    prev = be_ref[jnp.maximum(i - 1, 0)]

    @pl.when(i < nu_ref[0])
    def _():
        @pl.when((i == 0) | (be_ref[i] != prev))
        def _():
            w1b[...] = w1_ref[0].astype(BF16)
            w3b[...] = w3_ref[0].astype(BF16)
            w2b[...] = w2_ref[0].astype(BF16)

        x = jnp.concatenate([x_ref[pl.ds(j, bm, stride=SLAB), :] for j in range(d // LANES)],
                            axis=1).astype(BF16)
        h = _silu(_dot(x, w1b[...])) * _dot(x, w3b[...])
        y = _dot(h.astype(BF16), w2b[...])
        for j in range(d // LANES):
            y_ref[pl.ds(j, bm, stride=SLAB), :] = y[:, j * LANES:(j + 1) * LANES]

    @pl.when(i >= nu_ref[0])
    def _():
        y_ref[...] = jnp.zeros_like(y_ref)


def moe_experts(x_sorted, block_e, n_used, w1, w3, w2):
    n_blocks = block_e.shape[0]
    _, d, f = w1.shape
    rows = MOE_ROWS * SLAB
    grid_spec = pltpu.PrefetchScalarGridSpec(
        num_scalar_prefetch=2,
        grid=(n_blocks,),
        in_specs=[pl.BlockSpec((rows, LANES), lambda i, be, nu: (i, 0)),
                  pl.BlockSpec((1, d, f), lambda i, be, nu: (be[i], 0, 0)),
                  pl.BlockSpec((1, d, f), lambda i, be, nu: (be[i], 0, 0)),
                  pl.BlockSpec((1, f, d), lambda i, be, nu: (be[i], 0, 0))],
        out_specs=pl.BlockSpec((rows, LANES), lambda i, be, nu: (i, 0)),
        scratch_shapes=[pltpu.VMEM((d, f), BF16), pltpu.VMEM((d, f), BF16),
                        pltpu.VMEM((f, d), BF16)])
    return pl.pallas_call(
        _gmm_kernel,
        grid_spec=grid_spec,
        out_shape=jax.ShapeDtypeStruct(x_sorted.shape, F32),
        compiler_params=_cparams(("arbitrary",), VMEM_LIMIT),
        name="moe_experts",
    )(block_e, n_used, x_sorted, w1, w3, w2)


def _combine_kernel(tm, final, dest_ref, y_ref, gate_ref, u_ref, ws1_ref, ws3_ref, ws2_ref,
                    x_ref, g2_ref, fg_ref, o_ref, buf_ref, acc_ref, sem):
    d = u_ref.shape[1]

    def issue(t, carry):
        for k in range(TOP_K):
            _row_copy(y_ref, dest_ref[k, t], buf_ref.at[k], t, sem).start()
        return carry

    lax.fori_loop(0, tm, issue, 0)

    ub = u_ref[...].astype(BF16)
    hs = _silu(_dot(ub, ws1_ref[...])) * _dot(ub, ws3_ref[...])
    shared = _dot(hs.astype(BF16), ws2_ref[...])

    def drain(t, carry):
        for k in range(TOP_K):
            _row_copy(y_ref, dest_ref[k, t], buf_ref.at[k], t, sem).wait()
        return carry

    lax.fori_loop(0, tm, drain, 0)

    acc = buf_ref[0] * gate_ref[:, 0:1]
    for k in range(1, TOP_K):
        acc = acc + buf_ref[k] * gate_ref[:, k:k + 1]
    acc_ref[...] = acc
    routed = jnp.concatenate([acc_ref[pl.ds(j, tm, stride=SLAB), :] for j in range(d // LANES)],
                             axis=1)
    x = x_ref[...] + g2_ref[0] * (routed + shared)
    o_ref[...] = _rms(x, fg_ref[...]) if final else x


def moe_combine(y_sorted, dest, gate_rep, u, ws1, ws3, ws2, xs, mod, final_g, final,
                tm, seq, n_lat, b, n_tok):
    d = u.shape[1]
    nlt = n_lat // tm
    row = functools.partial(_mod_row, tm=tm, seq=seq, n_lat_tiles=nlt, b=b)
    full = lambda a: pl.BlockSpec(a.shape, lambda i: (0,) * a.ndim)
    tile = pl.BlockSpec((tm, d), lambda i: (i, 0))
    return pl.pallas_call(
        functools.partial(_combine_kernel, tm, final),
        grid=(n_tok // tm,),
        in_specs=[pl.BlockSpec((TOP_K, tm), lambda i: (0, i), memory_space=pltpu.SMEM),
                  pl.BlockSpec(memory_space=pl.ANY),
                  pl.BlockSpec((tm * SLAB, TOP_K), lambda i: (i, 0)),
                  tile, full(ws1), full(ws3), full(ws2), tile,
                  pl.BlockSpec((1, 1, d), lambda i: (row(i), 0, 5)),
                  full(final_g)],
        out_specs=tile,
        out_shape=jax.ShapeDtypeStruct((n_tok, d), F32),
        scratch_shapes=[pltpu.VMEM((TOP_K, tm * SLAB, LANES), F32),
                        pltpu.VMEM((tm * SLAB, LANES), F32),
                        pltpu.SemaphoreType.DMA],
        compiler_params=_cparams(("arbitrary",), VMEM_LIMIT),
        name="moe_combine",
    )(dest, y_sorted, gate_rep, u, ws1, ws3, ws2, xs, mod, final_g)


def moe_layer(u, xs, mod, router_w, router_b, w1, w3, w2, ws1, ws3, ws2, final_g, final,
              tm, seq, n_lat, b, n_tok):
    n_exp = router_w.shape[1]
    wr_hi, wr_lo = _split(router_w.T)
    idx, gate, rank, cnt = moe_router(u, wr_hi, wr_lo, router_b, n_tok, tm)
    counts = cnt[:, 0].astype(jnp.int32)
    padded = (counts + MOE_ROWS - 1) // MOE_ROWS * MOE_ROWS
    pad_end = jnp.cumsum(padded)
    pad_start = pad_end - padded
    dest = pad_start[idx] + rank
    n_blocks = -(-(n_tok * TOP_K + n_exp * (MOE_ROWS - 1)) // MOE_ROWS)
    block_start = jnp.arange(n_blocks, dtype=jnp.int32) * MOE_ROWS
    block_e = jnp.minimum(jnp.searchsorted(pad_end, block_start, side='right'),
                          n_exp - 1).astype(jnp.int32)
    n_used = (pad_end[-1:] // MOE_ROWS).astype(jnp.int32)
    x_sorted = moe_dispatch(u, dest, n_tok, n_blocks * MOE_ROWS, tm)
    y_sorted = moe_experts(x_sorted, block_e, n_used, w1, w3, w2)
    gate_rep = jnp.repeat(gate.T, SLAB, axis=0)
    tmc = min(tm, 128)
    return moe_combine(y_sorted, dest, gate_rep, u, ws1.astype(BF16), ws3.astype(BF16),
                       ws2.astype(BF16), xs, mod, final_g.reshape(1, -1), final,
                       tmc, seq, n_lat, b, n_tok)


def _grid_pos_emb(rows, d):
    r, col = jnp.meshgrid(jnp.arange(rows, dtype=F32), jnp.arange(GRID_W, dtype=F32), indexing='ij')
    quarter = d // 4
    omega = 1.0 / (10000.0 ** (jnp.arange(quarter, dtype=F32) / quarter))

    def emb(p):
        a = p.reshape(-1)[:, None] * omega[None, :]
        return jnp.concatenate([jnp.sin(a), jnp.cos(a)], axis=-1)

    return jnp.concatenate([emb(r), emb(col)], axis=-1)


def kernel(x, c, ctx, c_ctx, ada_w, ada_b, norm1_g, norm2_g, w_in, w_out, hy_conv_w, hy_conv_b,
           hy_f_w1, hy_f_b1, hy_f_w2, hy_f_b2, hy_f_w3, hy_skip, gla_wa2, gla_ba2, gla_norm_g,
           router_w, router_b, exp_w1, exp_w3, exp_w2, sh_w1, sh_w3, sh_w2, final_g):
    b, l, d = x.shape
    lc = ctx.shape[1]
    depth = ada_w.shape[0]
    n_lat, n_ctx = b * l, b * lc
    n_all = n_lat + n_ctx
    hy_width = hy_skip.shape[-1]
    hy_cols = hy_conv_w.shape[-1]
    qk = gla_wa2.shape[-1]
    wv = d - hy_width
    g_cols = 2 * qk + 2 * wv
    tm = math.gcd(math.gcd(l, lc), 256)
    cb = min(hy_width, 256)

    cond = jnp.concatenate([c, c_ctx[None, :]], axis=0)
    n_rows = -(-(b + 1) // 8) * 8
    cond = jnp.concatenate([cond, jnp.zeros((n_rows - b - 1, d), F32)], axis=0)
    mods = ada_table(cond, ada_w, ada_b)

    xs = embed_tokens(x.reshape(n_lat, d), _grid_pos_emb(l // GRID_W, d), ctx.reshape(n_ctx, d), tm)

    tabs = {}
    for length in (l, lc):
        if length not in tabs:
            tabs[length] = _trig_tables(length, True) + _trig_tables(length, False)
    zero_state = jnp.zeros((b, 2, wv, qk), F32)

    for layer in range(depth):
        last = layer == depth - 1
        mod = mods[layer][:, None, :]
        g1 = norm1_g[layer].reshape(1, d)
        g2 = norm2_g[layer].reshape(1, d)
        w_l = w_in[layer].astype(BF16)
        wa2 = gla_wa2[layer]
        rank = wa2.shape[1]
        wa2_bd = jnp.zeros((2 * rank, 2 * qk), F32)
        wa2_bd = wa2_bd.at[:rank, :qk].set(wa2[0]).at[rank:, qk:].set(wa2[1])
        ba2_cat = gla_ba2[layer].reshape(1, 2 * qk)
        zh, zg, la = in_proj(xs, g1, mod, w_l[:, :hy_cols], w_l[:, hy_cols:hy_cols + g_cols],
                             w_l[:, hy_cols + g_cols:], wa2_bd, ba2_cat, tm, l, n_lat, b)

        filt = (hy_f_w1[layer], hy_f_b1[layer], hy_f_w2[layer], hy_f_b2[layer], hy_f_w3[layer],
                hy_skip[layer])
        yc_gla, s_ctx = gla_mixer(zg, la, zero_state, gla_norm_g[layer], lc, b, n_lat)
        y_gla, _ = gla_mixer(zg, la, s_ctx, gla_norm_g[layer], l, b, 0)
        ct, st, c0, s0 = tabs[l]
        gr, gi = hyena_spectra(l, *filt, c0, s0)
        y_hy = hyena_mixer(zh, hy_conv_w[layer], hy_conv_b[layer], ct, st, gr, gi, l, b, 0, cb)
        if last:
            n_tok = n_lat
        else:
            n_tok = n_all
            ct, st, c0, s0 = tabs[lc]
            gr, gi = hyena_spectra(lc, *filt, c0, s0)
            yc_hy = hyena_mixer(zh, hy_conv_w[layer], hy_conv_b[layer], ct, st, gr, gi,
                                lc, b, n_lat, cb)
            y_hy = jnp.concatenate([y_hy, yc_hy], axis=0)
            y_gla = jnp.concatenate([y_gla, yc_gla], axis=0)
        w_o = w_out[layer].astype(BF16)
        xs_mid, u = out_proj(y_hy, y_gla, w_o[:hy_width], w_o[hy_width:], xs, mod, g2,
                             tm, l, n_lat, b, n_tok)
        xs = moe_layer(u, xs_mid, mod, router_w[layer], router_b[layer], exp_w1[layer],
                       exp_w3[layer], exp_w2[layer], sh_w1[layer], sh_w3[layer], sh_w2[layer],
                       final_g, last, tm, l, n_lat, b, n_tok)
    return xs[:n_lat].reshape(b, l, d)
```

```python
import functools
import math

import numpy as np
import jax
import jax.numpy as jnp
from jax import lax
from jax.experimental import pallas as pl
from jax.experimental.pallas import tpu as pltpu

GRID_W = 64
EPS = 1e-6

HY_ORDER = 2
HY_EMB = 33
HY_EMB_PAD = 40
HY_FREQ = 1.0
HY_TARGET = 1e-2
HY_FAST_PCT = 0.3
HY_SLOW_PCT = 1.5
HY_MIN_DECAY = math.log(1.0 / HY_TARGET) / HY_SLOW_PCT
HY_MAX_DECAY = math.log(1.0 / HY_TARGET) / HY_FAST_PCT

GLA_HEADS = 4
GLA_RANK = 16
GLA_TAU = 16.0
GLA_CHUNK = 64

TOP_K = 8
N_GROUPS = 8
TOPK_GROUPS = 4
ROUTED_SCALE = 2.5

MOE_ROWS = 256
VMEM_LIMIT = 56 * 1024 * 1024

F32 = jnp.float32
BF16 = jnp.bfloat16


def _cparams(sem, vmem=None):
    return pltpu.CompilerParams(dimension_semantics=sem, vmem_limit_bytes=vmem)


def _split(a):
    hi = a.astype(BF16)
    lo = (a - hi.astype(F32)).astype(BF16)
    return hi, lo


def _dot(a, b):
    return jnp.dot(a, b, preferred_element_type=F32)


def _dot3(a, b):
    ah, al = _split(a)
    bh, bl = _split(b)
    return _dot(ah, bh) + _dot(ah, bl) + _dot(al, bh)


def _silu(x):
    return x * (1.0 / (1.0 + jnp.exp(-x)))


def _rms(x, g):
    return x * lax.rsqrt(jnp.mean(x * x, axis=-1, keepdims=True) + EPS) * g


def _ada_kernel(c_ref, w_ref, b_ref, o_ref):
    o_ref[0] = _dot3(_silu(c_ref[...]), w_ref[0]) + b_ref[0]


def ada_table(cond_rows, ada_w, ada_b):
    depth, d, six_d = ada_w.shape
    r = cond_rows.shape[0]
    tn = 1024
    return pl.pallas_call(
        _ada_kernel,
        grid=(depth, six_d // tn),
        in_specs=[pl.BlockSpec((r, d), lambda l, j: (0, 0)),
                  pl.BlockSpec((1, d, tn), lambda l, j: (l, 0, j)),
                  pl.BlockSpec((1, 1, tn), lambda l, j: (l, 0, j))],
        out_specs=pl.BlockSpec((1, r, tn), lambda l, j: (l, 0, j)),
        out_shape=jax.ShapeDtypeStruct((depth, r, six_d), F32),
        compiler_params=_cparams(("arbitrary", "arbitrary")),
        name="ada_table",
    )(cond_rows, ada_w, ada_b.reshape(depth, 1, six_d))


def _embed_kernel(n_lat_tiles, x_ref, p_ref, c_ref, o_ref):
    i = pl.program_id(0)

    @pl.when(i < n_lat_tiles)
    def _():
        o_ref[...] = x_ref[...] + p_ref[...]

    @pl.when(i >= n_lat_tiles)
    def _():
        o_ref[...] = c_ref[...]


def embed_tokens(x2, pos, ctx2, tm):
    nl, d = x2.shape
    nc = ctx2.shape[0]
    l = pos.shape[0]
    nlt, nct, lt = nl // tm, nc // tm, l // tm
    return pl.pallas_call(
        functools.partial(_embed_kernel, nlt),
        grid=(nlt + nct,),
        in_specs=[pl.BlockSpec((tm, d), lambda i: (jnp.minimum(i, nlt - 1), 0)),
                  pl.BlockSpec((tm, d), lambda i: (i % lt, 0)),
                  pl.BlockSpec((tm, d), lambda i: (jnp.maximum(i - nlt, 0), 0))],
        out_specs=pl.BlockSpec((tm, d), lambda i: (i, 0)),
        out_shape=jax.ShapeDtypeStruct((nl + nc, d), F32),
        compiler_params=_cparams(("arbitrary",)),
        name="embed_tokens",
    )(x2, pos, ctx2)


def _mod_row(i, tm, seq, n_lat_tiles, b):
    return jnp.where(i < n_lat_tiles, (i * tm) // seq, b)


def _inproj_kernel(x_ref, g_ref, sh_ref, sc_ref, wh_ref, wg_ref, wa_ref, wa2_ref, ba2_ref,
                   zh_ref, zg_ref, la_ref):
    x = x_ref[...]
    h = _rms(x, g_ref[...]) * (1.0 + sc_ref[0]) + sh_ref[0]
    hb = h.astype(BF16)
    zh_ref[...] = _dot(hb, wh_ref[...]).astype(BF16)
    zg_ref[...] = _dot(hb, wg_ref[...]).astype(BF16)
    za = _dot(hb, wa_ref[...])
    t = _dot3(za, wa2_ref[...]) + ba2_ref[...]
    la_ref[...] = (jnp.minimum(t, 0.0) - jnp.log(1.0 + jnp.exp(-jnp.abs(t)))) * (1.0 / GLA_TAU)


def in_proj(xs, norm_g, mod, w_hy, w_g, w_a, wa2_bd, ba2_cat, tm, seq, n_lat, b):
    n, d = xs.shape
    nlt = n_lat // tm
    row = functools.partial(_mod_row, tm=tm, seq=seq, n_lat_tiles=nlt, b=b)
    hy_cols, g_cols, la_cols = w_hy.shape[1], w_g.shape[1], wa2_bd.shape[1]
    full = lambda a: pl.BlockSpec(a.shape, lambda i: (0,) * a.ndim)
    return pl.pallas_call(
        _inproj_kernel,
        grid=(n // tm,),
        in_specs=[pl.BlockSpec((tm, d), lambda i: (i, 0)),
                  full(norm_g),
                  pl.BlockSpec((1, 1, d), lambda i: (row(i), 0, 0)),
                  pl.BlockSpec((1, 1, d), lambda i: (row(i), 0, 1)),
                  full(w_hy), full(w_g), full(w_a), full(wa2_bd), full(ba2_cat)],
        out_specs=[pl.BlockSpec((tm, hy_cols), lambda i: (i, 0)),
                   pl.BlockSpec((tm, g_cols), lambda i: (i, 0)),
                   pl.BlockSpec((tm, la_cols), lambda i: (i, 0))],
        out_shape=[jax.ShapeDtypeStruct((n, hy_cols), BF16),
                   jax.ShapeDtypeStruct((n, g_cols), BF16),
                   jax.ShapeDtypeStruct((n, la_cols), F32)],
        compiler_params=_cparams(("arbitrary",), VMEM_LIMIT),
        name="in_proj",
    )(xs, norm_g, mod, mod, w_hy, w_g, w_a, wa2_bd, ba2_cat)


def _trig_tables(l, half_shift):
    k = lax.broadcasted_iota(jnp.int32, (l, l), 0)
    n = lax.broadcasted_iota(jnp.int32, (l, l), 1)
    m = ((2 * k + 1) * (2 * n + (1 if half_shift else 0))) % (8 * l)
    ang = m.astype(F32) * (2.0 * math.pi / (8 * l))
    return jnp.cos(ang).astype(BF16), jnp.sin(ang).astype(BF16)


def _filter_kernel(n_orders, width, l, tk,
                   pos_ref, t_ref, dl_ref, w1_ref, b1_ref, w2_ref, b2_ref, w3_ref, skip_ref,
                   c0_ref, s0_ref, gr_ref, gi_ref, hs_ref, hd_ref):
    j = pl.program_id(0)

    @pl.when(j == 0)
    def _():
        hid = jnp.sin(HY_FREQ * (_dot3(pos_ref[...], w1_ref[...]) + b1_ref[...]))
        hid = jnp.sin(HY_FREQ * (_dot3(hid, w2_ref[...]) + b2_ref[...]))
        h = _dot3(hid, w3_ref[...])
        decay = jnp.exp(-t_ref[...] * dl_ref[...])
        for o in range(n_orders):
            hf = h[:, (2 * o) * width:(2 * o + 1) * width] * decay
            hb = h[:, (2 * o + 1) * width:(2 * o + 2) * width] * decay
            hs_ref[:, o * width:(o + 1) * width] = (hf + hb).astype(BF16)
            hd_ref[:, o * width:(o + 1) * width] = (hb - hf).astype(BF16)

    scale = 1.0 / l
    gr = _dot(c0_ref[...], hs_ref[...])
    gi = _dot(s0_ref[...], hd_ref[...])
    for o in range(n_orders):
        gr_ref[o] = (gr[:, o * width:(o + 1) * width] + skip_ref[o]) * scale
        gi_ref[o] = gi[:, o * width:(o + 1) * width] * scale


def hyena_spectra(l, w1, b1, w2, b2, w3, skip, c0, s0):
    width = skip.shape[-1]
    n_orders = skip.shape[0]
    t = jnp.linspace(0.0, 1.0, l, dtype=F32)[:, None]
    bands = (HY_EMB - 1) // 2
    w = 2.0 * math.pi * jnp.arange(l, dtype=F32)[:, None] / l
    f = jnp.linspace(1e-4, bands - 1, bands, dtype=F32)[None, :]
    pos = jnp.concatenate([t, jnp.cos(f * w), -jnp.sin(f * w),
                           jnp.zeros((l, HY_EMB_PAD - HY_EMB), F32)], axis=-1)
    w1p = jnp.concatenate([w1, jnp.zeros((HY_EMB_PAD - HY_EMB, w1.shape[1]), F32)], axis=0)
    deltas = jnp.linspace(HY_MIN_DECAY, HY_MAX_DECAY, width, dtype=F32)[None, :]
    tk = min(l, 256)
    full = lambda a: pl.BlockSpec(a.shape, lambda j: (0,) * a.ndim)
    skip3 = skip.reshape(n_orders, 1, width)
    args = (pos, t, deltas, w1p, b1.reshape(1, -1), w2, b2.reshape(1, -1), w3, skip3)
    return pl.pallas_call(
        functools.partial(_filter_kernel, n_orders, width, l, tk),
        grid=(l // tk,),
        in_specs=[full(a) for a in args] + [pl.BlockSpec((tk, l), lambda j: (j, 0)),
                                            pl.BlockSpec((tk, l), lambda j: (j, 0))],
        out_specs=[pl.BlockSpec((n_orders, tk, width), lambda j: (0, j, 0)),
                   pl.BlockSpec((n_orders, tk, width), lambda j: (0, j, 0))],
        out_shape=[jax.ShapeDtypeStruct((n_orders, l, width), F32)] * 2,
        scratch_shapes=[pltpu.VMEM((l, n_orders * width), BF16),
                        pltpu.VMEM((l, n_orders * width), BF16)],
        compiler_params=_cparams(("arbitrary",), VMEM_LIMIT),
        name="hyena_spectra",
    )(*args, c0, s0)


def _hyena_kernel(l, zv_ref, z1_ref, z2_ref, wv_ref, w1_ref, w2_ref, bv_ref, b1_ref, b2_ref,
                  c_ref, s_ref, gr_ref, gi_ref, y_ref):
    rows = lax.broadcasted_iota(jnp.int32, (l, 1), 0)

    def conv3(z_ref, w_ref, b_ref):
        z = z_ref[...].astype(F32)
        zm = jnp.where(rows == 0, 0.0, pltpu.roll(z, 1, 0))
        zp = jnp.where(rows == l - 1, 0.0, pltpu.roll(z, l - 1, 0))
        return zm * w_ref[0:1, :] + z * w_ref[1:2, :] + zp * w_ref[2:3, :] + b_ref[...]

    def long_conv(u, o):
        ub = u.astype(BF16)
        ur = _dot(c_ref[...], ub)
        us = _dot(s_ref[...], ub)
        gr = gr_ref[o]
        gi = gi_ref[o]
        a = (ur * gr + us * gi).astype(BF16)
        b = (us * gr - ur * gi).astype(BF16)
        return _dot(c_ref[...], a) + _dot(s_ref[...], b)

    y = conv3(z1_ref, w1_ref, b1_ref) * long_conv(conv3(zv_ref, wv_ref, bv_ref), 0)
    y = conv3(z2_ref, w2_ref, b2_ref) * long_conv(y, 1)
    y_ref[...] = y.astype(BF16)


def hyena_mixer(zh, conv_w, conv_b, ctab, stab, gr, gi, l, n_seq, row0, cb):
    width = gr.shape[-1]
    n_orders = gr.shape[0]
    ncb = width // cb
    sb0 = row0 // l
    conv_b2 = conv_b.reshape(1, -1)
    zspec = lambda part: pl.BlockSpec((l, cb), lambda c, s: (sb0 + s, part * ncb + c))
    wspec = lambda part: pl.BlockSpec((3, cb), lambda c, s: (0, part * ncb + c))
    bspec = lambda part: pl.BlockSpec((1, cb), lambda c, s: (0, part * ncb + c))
    once = pl.Buffered(1)
    tspec = pl.BlockSpec((l, l), lambda c, s: (0, 0), pipeline_mode=once)
    gspec = pl.BlockSpec((n_orders, l, cb), lambda c, s: (0, 0, c), pipeline_mode=once)
    return pl.pallas_call(
        functools.partial(_hyena_kernel, l),
        grid=(ncb, n_seq),
        in_specs=[zspec(0), zspec(1), zspec(2), wspec(0), wspec(1), wspec(2),
                  bspec(0), bspec(1), bspec(2), tspec, tspec, gspec, gspec],
        out_specs=pl.BlockSpec((l, cb), lambda c, s: (s, c)),
        out_shape=jax.ShapeDtypeStruct((n_seq * l, width), BF16),
        compiler_params=_cparams(("arbitrary", "arbitrary"), VMEM_LIMIT),
        name="hyena_mixer",
    )(zh, zh, zh, conv_w, conv_w, conv_w, conv_b2, conv_b2, conv_b2, ctab, stab, gr, gi)


def _gla_kernel(l, qk, wv, dk, dv,
                zg_ref, la_ref, s0_ref, g_ref, y_ref, sfin_ref, of_ref, ob_ref, st_ref):
    ch = GLA_CHUNK
    n = l // ch
    heads = qk // dk
    r_i = lax.broadcasted_iota(jnp.int32, (ch, ch), 0)
    c_i = lax.broadcasted_iota(jnp.int32, (ch, ch), 1)
    low = r_i >= c_i
    upp = r_i <= c_i
    tri = (low.astype(BF16), upp.astype(BF16))
    masks = (low, upp)
    bd = (lax.broadcasted_iota(jnp.int32, (wv, qk), 0) // dv
          == lax.broadcasted_iota(jnp.int32, (wv, qk), 1) // dk).astype(F32)
    lane_head = lax.broadcasted_iota(jnp.int32, (1, qk), 1) // dk
    scale = dk ** -0.5

    st_ref[...] = s0_ref[0]

    def chunk(i, d):
        r0 = pl.multiple_of(i * ch, ch)
        q = zg_ref[pl.ds(r0, ch), 0:qk].astype(F32) * scale
        k = zg_ref[pl.ds(r0, ch), qk:2 * qk].astype(F32)
        v = zg_ref[pl.ds(r0, ch), 2 * qk:2 * qk + wv]
        la = la_ref[pl.ds(r0, ch), d * qk:(d + 1) * qk]
        lh, ll = _split(la)
        b = _dot(tri[d], lh) + _dot(tri[d], ll)
        tot = b[ch - 1:ch, :] if d == 0 else b[0:1, :]
        qe = (q * jnp.exp(b)).astype(BF16)
        ke = (k * jnp.exp(-b)).astype(BF16)
        ks = (k * jnp.exp(tot - b)).astype(BF16)
        st = st_ref[d]
        o = lax.dot_general(qe, st.astype(BF16), (((1,), (1,)), ((), ())),
                            preferred_element_type=F32)
        parts = []
        for h in range(heads):
            qh = jnp.where(lane_head == h, qe, jnp.zeros_like(qe))
            att = lax.dot_general(qh, ke, (((1,), (1,)), ((), ())), preferred_element_type=F32)
            att = jnp.where(masks[d], att, 0.0).astype(BF16)
            parts.append(_dot(att, v[:, h * dv:(h + 1) * dv]))
        o = o + jnp.concatenate(parts, axis=1)
        upd = lax.dot_general(v, ks, (((0,), (0,)), ((), ())), preferred_element_type=F32)
        st_ref[d] = st * jnp.exp(tot) + upd * bd
        return r0, o

    def body(i, carry):
        r0, o = chunk(i, 0)
        of_ref[pl.ds(r0, ch), :] = o
        r0, o = chunk(n - 1 - i, 1)
        ob_ref[pl.ds(r0, ch), :] = o
        return carry

    lax.fori_loop(0, n, body, 0)
    sfin_ref[0] = st_ref[...]

    def finish(i, carry):
        r0 = pl.multiple_of(i * ch, ch)
        o = of_ref[pl.ds(r0, ch), :] + ob_ref[pl.ds(r0, ch), :]
        r = zg_ref[pl.ds(r0, ch), 2 * qk + wv:2 * qk + 2 * wv].astype(F32)
        parts = [_rms(o[:, h * dv:(h + 1) * dv], g_ref[...]) for h in range(heads)]
        y_ref[pl.ds(r0, ch), :] = (jnp.concatenate(parts, axis=1) * _silu(r)).astype(BF16)
        return carry

    lax.fori_loop(0, n, finish, 0)


def gla_mixer(zg, la, s0, norm_g, l, n_seq, row0):
    qk = la.shape[1] // 2
    wv = (zg.shape[1] - 2 * qk) // 2
    dk, dv = qk // GLA_HEADS, wv // GLA_HEADS
    sb0 = row0 // l
    return pl.pallas_call(
        functools.partial(_gla_kernel, l, qk, wv, dk, dv),
        grid=(n_seq,),
        in_specs=[pl.BlockSpec((l, zg.shape[1]), lambda s: (sb0 + s, 0)),
                  pl.BlockSpec((l, la.shape[1]), lambda s: (sb0 + s, 0)),
                  pl.BlockSpec((1, 2, wv, qk), lambda s: (s, 0, 0, 0)),
                  pl.BlockSpec((1, dv), lambda s: (0, 0))],
        out_specs=[pl.BlockSpec((l, wv), lambda s: (s, 0)),
                   pl.BlockSpec((1, 2, wv, qk), lambda s: (s, 0, 0, 0))],
        out_shape=[jax.ShapeDtypeStruct((n_seq * l, wv), BF16),
                   jax.ShapeDtypeStruct((n_seq, 2, wv, qk), F32)],
        scratch_shapes=[pltpu.VMEM((l, wv), F32), pltpu.VMEM((l, wv), F32),
                        pltpu.VMEM((2, wv, qk), F32)],
        compiler_params=_cparams(("arbitrary",), VMEM_LIMIT),
        name="gla_mixer",
    )(zg, la, s0, norm_g.reshape(1, dv))


def _outproj_kernel(yh_ref, yg_ref, wh_ref, wg_ref, x_ref, g1_ref, g_ref, sh_ref, sc_ref,
                    xo_ref, u_ref):
    delta = _dot(yh_ref[...], wh_ref[...]) + _dot(yg_ref[...], wg_ref[...])
    x = x_ref[...] + g1_ref[0] * delta
    xo_ref[...] = x
    u_ref[...] = _rms(x, g_ref[...]) * (1.0 + sc_ref[0]) + sh_ref[0]


def out_proj(y_hy, y_gla, w_hy, w_gla, xs, mod, norm_g, tm, seq, n_lat, b, n_tok):
    d = xs.shape[1]
    nlt = n_lat // tm
    row = functools.partial(_mod_row, tm=tm, seq=seq, n_lat_tiles=nlt, b=b)
    full = lambda a: pl.BlockSpec(a.shape, lambda i: (0,) * a.ndim)
    modspec = lambda col: pl.BlockSpec((1, 1, d), lambda i: (row(i), 0, col))
    tile = lambda w: pl.BlockSpec((tm, w), lambda i: (i, 0))
    return pl.pallas_call(
        _outproj_kernel,
        grid=(n_tok // tm,),
        in_specs=[tile(y_hy.shape[1]), tile(y_gla.shape[1]), full(w_hy), full(w_gla), tile(d),
                  modspec(2), full(norm_g), modspec(3), modspec(4)],
        out_specs=[tile(d), tile(d)],
        out_shape=[jax.ShapeDtypeStruct((n_tok, d), F32)] * 2,
        compiler_params=_cparams(("arbitrary",), VMEM_LIMIT),
        name="out_proj",
    )(y_hy, y_gla, w_hy, w_gla, xs, mod, norm_g, mod, mod)


def _router_kernel(n_exp, tm, u_ref, wh_ref, wl_ref, b_ref, tri_ref,
                   idx_ref, gate_ref, rank_ref, cnt_ref):
    i = pl.program_id(0)

    @pl.when(i == 0)
    def _():
        cnt_ref[...] = jnp.zeros_like(cnt_ref)

    uh, ul = _split(u_ref[...])
    nt = (((1,), (1,)), ((), ()))
    logits = (lax.dot_general(wh_ref[...], uh, nt, preferred_element_type=F32)
              + lax.dot_general(wh_ref[...], ul, nt, preferred_element_type=F32)
              + lax.dot_general(wl_ref[...], uh, nt, preferred_element_type=F32))
    s = 1.0 / (1.0 + jnp.exp(-logits))
    sel = s + b_ref[...]
    gsz = n_exp // N_GROUPS
    neg = -jnp.inf
    rows_g = lax.broadcasted_iota(jnp.int32, (gsz, tm), 0)
    blocks, gscore = [], []
    for g in range(N_GROUPS):
        blk = sel[g * gsz:(g + 1) * gsz, :]
        m1 = jnp.max(blk, axis=0, keepdims=True)
        i1 = jnp.min(jnp.where(blk == m1, rows_g, gsz), axis=0, keepdims=True)
        m2 = jnp.max(jnp.where(rows_g == i1, neg, blk), axis=0, keepdims=True)
        blocks.append(blk)
        gscore.append(m1 + m2)
    masked = []
    for g in range(N_GROUPS):
        beaten = jnp.zeros((1, tm), jnp.int32)
        for h in range(N_GROUPS):
            if h == g:
                continue
            wins = (gscore[h] > gscore[g]) | ((gscore[h] == gscore[g]) & (h < g))
            beaten = beaten + wins.astype(jnp.int32)
        masked.append(jnp.where(beaten < TOPK_GROUPS, blocks[g], neg))
    sel = jnp.concatenate(masked, axis=0)

    rows = lax.broadcasted_iota(jnp.int32, (n_exp, tm), 0)
    chosen = jnp.zeros((n_exp, tm), F32)
    idxs, gates = [], []
    for _ in range(TOP_K):
        m = jnp.max(sel, axis=0, keepdims=True)
        ik = jnp.min(jnp.where(sel == m, rows, n_exp), axis=0, keepdims=True)
        hit = rows == ik
        gates.append(jnp.sum(jnp.where(hit, s, 0.0), axis=0, keepdims=True))
        sel = jnp.where(hit, neg, sel)
        chosen = jnp.where(hit, 1.0, chosen)
        idxs.append(ik)
    gsum = gates[0]
    for g in gates[1:]:
        gsum = gsum + g
    inv = ROUTED_SCALE / gsum
    before = _dot(chosen.astype(BF16), tri_ref[...]) + cnt_ref[...]
    for k in range(TOP_K):
        idx_ref[k:k + 1, :] = idxs[k]
        gate_ref[k:k + 1, :] = gates[k] * inv
        rk = jnp.sum(jnp.where(rows == idxs[k], before, 0.0), axis=0, keepdims=True)
        rank_ref[k:k + 1, :] = rk.astype(jnp.int32)
    cnt_ref[...] += jnp.sum(chosen, axis=1, keepdims=True)


def moe_router(u, wr_hi, wr_lo, b_r, n_tok, tm):
    n_exp, d = wr_hi.shape
    tri = (lax.broadcasted_iota(jnp.int32, (tm, tm), 0)
           < lax.broadcasted_iota(jnp.int32, (tm, tm), 1)).astype(BF16)
    full = lambda a: pl.BlockSpec(a.shape, lambda i: (0,) * a.ndim)
    tok = pl.BlockSpec((TOP_K, tm), lambda i: (0, i))
    return pl.pallas_call(
        functools.partial(_router_kernel, n_exp, tm),
        grid=(n_tok // tm,),
        in_specs=[pl.BlockSpec((tm, d), lambda i: (i, 0)), full(wr_hi), full(wr_lo),
                  pl.BlockSpec((n_exp, 1), lambda i: (0, 0)), full(tri)],
        out_specs=[tok, tok, tok, pl.BlockSpec((n_exp, 1), lambda i: (0, 0))],
        out_shape=[jax.ShapeDtypeStruct((TOP_K, n_tok), jnp.int32),
                   jax.ShapeDtypeStruct((TOP_K, n_tok), F32),
                   jax.ShapeDtypeStruct((TOP_K, n_tok), jnp.int32),
                   jax.ShapeDtypeStruct((n_exp, 1), F32)],
        compiler_params=_cparams(("arbitrary",), VMEM_LIMIT),
        name="moe_router",
    )(u, wr_hi, wr_lo, b_r.reshape(n_exp, 1), tri)


def _row_copy(src, src_row, dst, dst_row, sem):
    return pltpu.make_async_copy(src.at[pl.ds(src_row, 1)], dst.at[pl.ds(dst_row, 1)], sem)


def _pack_bf16_pairs(x):
    w = x.shape[1] // 2
    lo = pltpu.bitcast(x[:, :w].astype(BF16).astype(F32), jnp.uint32) >> 16
    hi = pltpu.bitcast(x[:, w:].astype(BF16).astype(F32), jnp.uint32)
    return hi | lo


def _unpack_bf16_pairs(p):
    lo = pltpu.bitcast(p << 16, F32)
    hi = pltpu.bitcast(p & jnp.uint32(0xFFFF0000), F32)
    return jnp.concatenate([lo, hi], axis=1).astype(BF16)


def _dest_kernel(n_exp, tm, idx_ref, rank_ref, ps_ref, dest_ref):
    rows = lax.broadcasted_iota(jnp.int32, (n_exp, tm), 0)
    ps = ps_ref[...]
    for k in range(TOP_K):
        base = jnp.sum(jnp.where(rows == idx_ref[k:k + 1, :], ps, 0), axis=0, keepdims=True)
        dest_ref[k:k + 1, :] = base + rank_ref[k:k + 1, :]


def moe_dest(idx, rank, pad_start, tm):
    n_tok = idx.shape[1]
    n_exp = pad_start.shape[0]
    tok = pl.BlockSpec((TOP_K, tm), lambda i: (0, i))
    return pl.pallas_call(
        functools.partial(_dest_kernel, n_exp, tm),
        grid=(n_tok // tm,),
        in_specs=[tok, tok, pl.BlockSpec((n_exp, 1), lambda i: (0, 0))],
        out_specs=tok,
        out_shape=jax.ShapeDtypeStruct((TOP_K, n_tok), jnp.int32),
        compiler_params=_cparams(("arbitrary",)),
        name="moe_dest",
    )(idx, rank, pad_start.reshape(n_exp, 1))


def _dispatch_kernel(tm, dest_ref, u_ref, xin_ref, xs_ref, pk_ref, sem):
    del xin_ref
    pk_ref[...] = _pack_bf16_pairs(u_ref[...])

    def issue(t, carry):
        for k in range(TOP_K):
            _row_copy(pk_ref, t, xs_ref, dest_ref[k, t], sem).start()
        return carry

    lax.fori_loop(0, tm, issue, 0)

    def drain(t, carry):
        for k in range(TOP_K):
            _row_copy(pk_ref, t, xs_ref, dest_ref[k, t], sem).wait()
        return carry

    lax.fori_loop(0, tm, drain, 0)


def moe_dispatch(u, dest, n_tok, n_slots, tm):
    d = u.shape[1]
    zeros = jnp.zeros((n_slots, d // 2), jnp.uint32)
    return pl.pallas_call(
        functools.partial(_dispatch_kernel, tm),
        grid=(n_tok // tm,),
        in_specs=[pl.BlockSpec((TOP_K, tm), lambda i: (0, i), memory_space=pltpu.SMEM),
                  pl.BlockSpec((tm, d), lambda i: (i, 0)),
                  pl.BlockSpec(memory_space=pl.ANY)],
        out_specs=pl.BlockSpec(memory_space=pl.ANY),
        out_shape=jax.ShapeDtypeStruct((n_slots, d // 2), jnp.uint32),
        scratch_shapes=[pltpu.VMEM((tm, d // 2), jnp.uint32), pltpu.SemaphoreType.DMA],
        input_output_aliases={2: 0},
        compiler_params=_cparams(("arbitrary",)),
        name="moe_dispatch",
    )(dest, u, zeros)


def _gmm_kernel(be_ref, nu_ref, x_ref, w1_ref, w3_ref, w2_ref, y_ref, w1b, w3b, w2b):
    i = pl.program_id(0)
    prev = be_ref[jnp.maximum(i - 1, 0)]

    @pl.when(i < nu_ref[0])
    def _():
        @pl.when((i == 0) | (be_ref[i] != prev))
        def _():
            w1b[...] = w1_ref[0, 0].astype(BF16)
            w3b[...] = w3_ref[0, 0].astype(BF16)
            w2b[...] = w2_ref[0, 0].astype(BF16)

        x = _unpack_bf16_pairs(x_ref[...])
        h = _silu(_dot(x, w1b[...])) * _dot(x, w3b[...])
        y_ref[...] = _dot(h.astype(BF16), w2b[...])

    @pl.when(i >= nu_ref[0])
    def _():
        y_ref[...] = jnp.zeros_like(y_ref)


def moe_experts(x_sorted, block_e, n_used, w1, w3, w2, layer):
    n_blocks = block_e.shape[0]
    _, _, d, f = w1.shape
    grid_spec = pltpu.PrefetchScalarGridSpec(
        num_scalar_prefetch=2,
        grid=(n_blocks,),
        in_specs=[pl.BlockSpec((MOE_ROWS, d // 2), lambda i, be, nu: (i, 0)),
                  pl.BlockSpec((1, 1, d, f), lambda i, be, nu: (layer, be[i], 0, 0)),
                  pl.BlockSpec((1, 1, d, f), lambda i, be, nu: (layer, be[i], 0, 0)),
                  pl.BlockSpec((1, 1, f, d), lambda i, be, nu: (layer, be[i], 0, 0))],
        out_specs=pl.BlockSpec((MOE_ROWS, d), lambda i, be, nu: (i, 0)),
        scratch_shapes=[pltpu.VMEM((d, f), BF16), pltpu.VMEM((d, f), BF16),
                        pltpu.VMEM((f, d), BF16)])
    return pl.pallas_call(
        _gmm_kernel,
        grid_spec=grid_spec,
        out_shape=jax.ShapeDtypeStruct((x_sorted.shape[0], d), F32),
        compiler_params=_cparams(("arbitrary",), VMEM_LIMIT),
        name="moe_experts",
    )(block_e, n_used, x_sorted, w1, w3, w2)


def _combine_kernel(tm, final, dest_ref, y_ref, gate_ref, u_ref, ws1_ref, ws3_ref, ws2_ref,
                    x_ref, g2_ref, fg_ref, o_ref, buf_ref, sem):
    def issue(t, carry):
        for k in range(TOP_K):
            _row_copy(y_ref, dest_ref[k, t], buf_ref.at[k], t, sem).start()
        return carry

    lax.fori_loop(0, tm, issue, 0)

    ub = u_ref[...].astype(BF16)
    hs = _silu(_dot(ub, ws1_ref[...])) * _dot(ub, ws3_ref[...])
    shared = _dot(hs.astype(BF16), ws2_ref[...])

    def drain(t, carry):
        for k in range(TOP_K):
            _row_copy(y_ref, dest_ref[k, t], buf_ref.at[k], t, sem).wait()
        return carry

    lax.fori_loop(0, tm, drain, 0)

    routed = buf_ref[0] * gate_ref[:, 0:1]
    for k in range(1, TOP_K):
        routed = routed + buf_ref[k] * gate_ref[:, k:k + 1]
    x = x_ref[...] + g2_ref[0] * (routed + shared)
    o_ref[...] = _rms(x, fg_ref[...]) if final else x


def moe_combine(y_sorted, dest, gate_rep, u, ws1, ws3, ws2, xs, mod, final_g, final,
                tm, seq, n_lat, b, n_tok):
    d = u.shape[1]
    nlt = n_lat // tm
    row = functools.partial(_mod_row, tm=tm, seq=seq, n_lat_tiles=nlt, b=b)
    full = lambda a: pl.BlockSpec(a.shape, lambda i: (0,) * a.ndim)
    tile = pl.BlockSpec((tm, d), lambda i: (i, 0))
    return pl.pallas_call(
        functools.partial(_combine_kernel, tm, final),
        grid=(n_tok // tm,),
        in_specs=[pl.BlockSpec((TOP_K, tm), lambda i: (0, i), memory_space=pltpu.SMEM),
                  pl.BlockSpec(memory_space=pl.ANY),
                  pl.BlockSpec((tm, TOP_K), lambda i: (i, 0)),
                  tile, full(ws1), full(ws3), full(ws2), tile,
                  pl.BlockSpec((1, 1, d), lambda i: (row(i), 0, 5)),
                  full(final_g)],
        out_specs=tile,
        out_shape=jax.ShapeDtypeStruct((n_tok, d), F32),
        scratch_shapes=[pltpu.VMEM((TOP_K, tm, d), F32), pltpu.SemaphoreType.DMA],
        compiler_params=_cparams(("arbitrary",), VMEM_LIMIT),
        name="moe_combine",
    )(dest, y_sorted, gate_rep, u, ws1, ws3, ws2, xs, mod, final_g)


def moe_layer(u, xs, mod, router_w, router_b, w1, w3, w2, layer, ws1, ws3, ws2, final_g, final,
              tm, seq, n_lat, b, n_tok):
    n_exp = router_w.shape[1]
    wr_hi, wr_lo = _split(router_w.T)
    idx, gate, rank, cnt = moe_router(u, wr_hi, wr_lo, router_b, n_tok, tm)
    counts = cnt[:, 0].astype(jnp.int32)
    padded = (counts + MOE_ROWS - 1) // MOE_ROWS * MOE_ROWS
    pad_end = jnp.cumsum(padded)
    dest = moe_dest(idx, rank, pad_end - padded, tm)
    n_blocks = -(-(n_tok * TOP_K + n_exp * (MOE_ROWS - 1)) // MOE_ROWS)
    block_start = jnp.arange(n_blocks, dtype=jnp.int32) * MOE_ROWS
    block_e = jnp.minimum(jnp.searchsorted(pad_end, block_start, side='right'),
                          n_exp - 1).astype(jnp.int32)
    n_used = (pad_end[-1:] // MOE_ROWS).astype(jnp.int32)
    x_sorted = moe_dispatch(u, dest, n_tok, n_blocks * MOE_ROWS, tm)
    y_sorted = moe_experts(x_sorted, block_e, n_used, w1, w3, w2, layer)
    return moe_combine(y_sorted, dest, gate.T, u, ws1.astype(BF16), ws3.astype(BF16),
                       ws2.astype(BF16), xs, mod, final_g.reshape(1, -1), final,
                       tm, seq, n_lat, b, n_tok)


def _grid_pos_emb(rows, d):
    r, col = jnp.meshgrid(jnp.arange(rows, dtype=F32), jnp.arange(GRID_W, dtype=F32), indexing='ij')
    quarter = d // 4
    omega = 1.0 / (10000.0 ** (jnp.arange(quarter, dtype=F32) / quarter))

    def emb(p):
        a = p.reshape(-1)[:, None] * omega[None, :]
        return jnp.concatenate([jnp.sin(a), jnp.cos(a)], axis=-1)

    return jnp.concatenate([emb(r), emb(col)], axis=-1)


def kernel(x, c, ctx, c_ctx, ada_w, ada_b, norm1_g, norm2_g, w_in, w_out, hy_conv_w, hy_conv_b,
           hy_f_w1, hy_f_b1, hy_f_w2, hy_f_b2, hy_f_w3, hy_skip, gla_wa2, gla_ba2, gla_norm_g,
           router_w, router_b, exp_w1, exp_w3, exp_w2, sh_w1, sh_w3, sh_w2, final_g):
    b, l, d = x.shape
    lc = ctx.shape[1]
    depth = ada_w.shape[0]
    n_lat, n_ctx = b * l, b * lc
    n_all = n_lat + n_ctx
    hy_width = hy_skip.shape[-1]
    hy_cols = hy_conv_w.shape[-1]
    qk = gla_wa2.shape[-1]
    wv = d - hy_width
    g_cols = 2 * qk + 2 * wv
    tm = math.gcd(math.gcd(l, lc), 256)
    cb = min(hy_width, 256)

    cond = jnp.concatenate([c, c_ctx[None, :]], axis=0)
    n_rows = -(-(b + 1) // 8) * 8
    cond = jnp.concatenate([cond, jnp.zeros((n_rows - b - 1, d), F32)], axis=0)
    mods = ada_table(cond, ada_w, ada_b)

    xs = embed_tokens(x.reshape(n_lat, d), _grid_pos_emb(l // GRID_W, d), ctx.reshape(n_ctx, d), tm)

    tabs = {}
    for length in (l, lc):
        if length not in tabs:
            tabs[length] = _trig_tables(length, True) + _trig_tables(length, False)
    zero_state = jnp.zeros((b, 2, wv, qk), F32)

    for layer in range(depth):
        last = layer == depth - 1
        mod = mods[layer][:, None, :]
        g1 = norm1_g[layer].reshape(1, d)
        g2 = norm2_g[layer].reshape(1, d)
        w_l = w_in[layer].astype(BF16)
        wa2 = gla_wa2[layer]
        rank = wa2.shape[1]
        wa2_bd = jnp.zeros((2 * rank, 2 * qk), F32)
        wa2_bd = wa2_bd.at[:rank, :qk].set(wa2[0]).at[rank:, qk:].set(wa2[1])
        ba2_cat = gla_ba2[layer].reshape(1, 2 * qk)
        zh, zg, la = in_proj(xs, g1, mod, w_l[:, :hy_cols], w_l[:, hy_cols:hy_cols + g_cols],
                             w_l[:, hy_cols + g_cols:], wa2_bd, ba2_cat, tm, l, n_lat, b)

        filt = (hy_f_w1[layer], hy_f_b1[layer], hy_f_w2[layer], hy_f_b2[layer], hy_f_w3[layer],
                hy_skip[layer])
        yc_gla, s_ctx = gla_mixer(zg, la, zero_state, gla_norm_g[layer], lc, b, n_lat)
        y_gla, _ = gla_mixer(zg, la, s_ctx, gla_norm_g[layer], l, b, 0)
        ct, st, c0, s0 = tabs[l]
        gr, gi = hyena_spectra(l, *filt, c0, s0)
        y_hy = hyena_mixer(zh, hy_conv_w[layer], hy_conv_b[layer], ct, st, gr, gi, l, b, 0, cb)
        if last:
            n_tok = n_lat
        else:
            n_tok = n_all
            ct, st, c0, s0 = tabs[lc]
            gr, gi = hyena_spectra(lc, *filt, c0, s0)
            yc_hy = hyena_mixer(zh, hy_conv_w[layer], hy_conv_b[layer], ct, st, gr, gi,
                                lc, b, n_lat, cb)
            y_hy = jnp.concatenate([y_hy, yc_hy], axis=0)
            y_gla = jnp.concatenate([y_gla, yc_gla], axis=0)
        w_o = w_out[layer].astype(BF16)
        xs_mid, u = out_proj(y_hy, y_gla, w_o[:hy_width], w_o[hy_width:], xs, mod, g2,
                             tm, l, n_lat, b, n_tok)
        xs = moe_layer(u, xs_mid, mod, router_w[layer], router_b[layer], exp_w1, exp_w3, exp_w2,
                       layer, sh_w1[layer], sh_w3[layer], sh_w2[layer],
                       final_g, last, tm, l, n_lat, b, n_tok)
    return xs[:n_lat].reshape(b, l, d)
```

```python
import functools
import math

import numpy as np
import jax
import jax.numpy as jnp
from jax import lax
from jax.experimental import pallas as pl
from jax.experimental.pallas import tpu as pltpu

GRID_W = 64
EPS = 1e-6

HY_ORDER = 2
HY_EMB = 33
HY_EMB_PAD = 40
HY_FREQ = 1.0
HY_TARGET = 1e-2
HY_FAST_PCT = 0.3
HY_SLOW_PCT = 1.5
HY_MIN_DECAY = math.log(1.0 / HY_TARGET) / HY_SLOW_PCT
HY_MAX_DECAY = math.log(1.0 / HY_TARGET) / HY_FAST_PCT

GLA_HEADS = 4
GLA_RANK = 16
GLA_TAU = 16.0
GLA_CHUNK = 64

TOP_K = 8
N_GROUPS = 8
TOPK_GROUPS = 4
ROUTED_SCALE = 2.5

HY_FREQ_CHUNK = 1024
MOE_ROWS = 256
VMEM_LIMIT = 56 * 1024 * 1024

F32 = jnp.float32
BF16 = jnp.bfloat16


def _cparams(sem, vmem=None):
    return pltpu.CompilerParams(dimension_semantics=sem, vmem_limit_bytes=vmem)


def _split(a):
    hi = a.astype(BF16)
    lo = (a - hi.astype(F32)).astype(BF16)
    return hi, lo


def _dot(a, b):
    return jnp.dot(a, b, preferred_element_type=F32)


def _dot3(a, b):
    ah, al = _split(a)
    bh, bl = _split(b)
    return _dot(ah, bh) + _dot(ah, bl) + _dot(al, bh)


def _silu(x):
    return x * (1.0 / (1.0 + jnp.exp(-x)))


def _rms(x, g):
    return x * lax.rsqrt(jnp.mean(x * x, axis=-1, keepdims=True) + EPS) * g


def _ada_kernel(c_ref, w_ref, b_ref, o_ref):
    o_ref[0] = _dot3(_silu(c_ref[...]), w_ref[0]) + b_ref[0]


def ada_table(cond_rows, ada_w, ada_b):
    depth, d, six_d = ada_w.shape
    r = cond_rows.shape[0]
    tn = 1024
    return pl.pallas_call(
        _ada_kernel,
        grid=(depth, six_d // tn),
        in_specs=[pl.BlockSpec((r, d), lambda l, j: (0, 0)),
                  pl.BlockSpec((1, d, tn), lambda l, j: (l, 0, j)),
                  pl.BlockSpec((1, 1, tn), lambda l, j: (l, 0, j))],
        out_specs=pl.BlockSpec((1, r, tn), lambda l, j: (l, 0, j)),
        out_shape=jax.ShapeDtypeStruct((depth, r, six_d), F32),
        compiler_params=_cparams(("arbitrary", "arbitrary")),
        name="ada_table",
    )(cond_rows, ada_w, ada_b.reshape(depth, 1, six_d))


def _embed_kernel(n_lat_tiles, x_ref, p_ref, c_ref, o_ref):
    i = pl.program_id(0)

    @pl.when(i < n_lat_tiles)
    def _():
        o_ref[...] = x_ref[...] + p_ref[...]

    @pl.when(i >= n_lat_tiles)
    def _():
        o_ref[...] = c_ref[...]


def embed_tokens(x2, pos, ctx2, tm):
    nl, d = x2.shape
    nc = ctx2.shape[0]
    l = pos.shape[0]
    nlt, nct, lt = nl // tm, nc // tm, l // tm
    return pl.pallas_call(
        functools.partial(_embed_kernel, nlt),
        grid=(nlt + nct,),
        in_specs=[pl.BlockSpec((tm, d), lambda i: (jnp.minimum(i, nlt - 1), 0)),
                  pl.BlockSpec((tm, d), lambda i: (i % lt, 0)),
                  pl.BlockSpec((tm, d), lambda i: (jnp.maximum(i - nlt, 0), 0))],
        out_specs=pl.BlockSpec((tm, d), lambda i: (i, 0)),
        out_shape=jax.ShapeDtypeStruct((nl + nc, d), F32),
        compiler_params=_cparams(("arbitrary",)),
        name="embed_tokens",
    )(x2, pos, ctx2)


def _mod_row(i, tm, seq, n_lat_tiles, b):
    return jnp.where(i < n_lat_tiles, (i * tm) // seq, b)


def _inproj_kernel(x_ref, g_ref, sh_ref, sc_ref, wh_ref, wg_ref, wa_ref, wa2_ref, ba2_ref,
                   zh_ref, zg_ref, la_ref):
    x = x_ref[...]
    h = _rms(x, g_ref[...]) * (1.0 + sc_ref[0]) + sh_ref[0]
    hb = h.astype(BF16)
    zh_ref[...] = _dot(hb, wh_ref[...]).astype(BF16)
    zg_ref[...] = _dot(hb, wg_ref[...]).astype(BF16)
    za = _dot(hb, wa_ref[...])
    t = _dot3(za, wa2_ref[...]) + ba2_ref[...]
    la_ref[...] = (jnp.minimum(t, 0.0) - jnp.log(1.0 + jnp.exp(-jnp.abs(t)))) * (1.0 / GLA_TAU)


def in_proj(xs, norm_g, mod, w_hy, w_g, w_a, wa2_bd, ba2_cat, tm, seq, n_lat, b):
    n, d = xs.shape
    nlt = n_lat // tm
    row = functools.partial(_mod_row, tm=tm, seq=seq, n_lat_tiles=nlt, b=b)
    hy_cols, g_cols, la_cols = w_hy.shape[1], w_g.shape[1], wa2_bd.shape[1]
    full = lambda a: pl.BlockSpec(a.shape, lambda i: (0,) * a.ndim)
    return pl.pallas_call(
        _inproj_kernel,
        grid=(n // tm,),
        in_specs=[pl.BlockSpec((tm, d), lambda i: (i, 0)),
                  full(norm_g),
                  pl.BlockSpec((1, 1, d), lambda i: (row(i), 0, 0)),
                  pl.BlockSpec((1, 1, d), lambda i: (row(i), 0, 1)),
                  full(w_hy), full(w_g), full(w_a), full(wa2_bd), full(ba2_cat)],
        out_specs=[pl.BlockSpec((tm, hy_cols), lambda i: (i, 0)),
                   pl.BlockSpec((tm, g_cols), lambda i: (i, 0)),
                   pl.BlockSpec((tm, la_cols), lambda i: (i, 0))],
        out_shape=[jax.ShapeDtypeStruct((n, hy_cols), BF16),
                   jax.ShapeDtypeStruct((n, g_cols), BF16),
                   jax.ShapeDtypeStruct((n, la_cols), F32)],
        compiler_params=_cparams(("arbitrary",), VMEM_LIMIT),
        name="in_proj",
    )(xs, norm_g, mod, mod, w_hy, w_g, w_a, wa2_bd, ba2_cat)


def _trig_tables(l, half_shift):
    k = lax.broadcasted_iota(jnp.int32, (l, l), 0)
    n = lax.broadcasted_iota(jnp.int32, (l, l), 1)
    m = ((2 * k + 1) * (2 * n + (1 if half_shift else 0))) % (8 * l)
    ang = m.astype(F32) * (2.0 * math.pi / (8 * l))
    return jnp.cos(ang).astype(BF16), jnp.sin(ang).astype(BF16)


def _filter_kernel(n_orders, width, l, tk,
                   pos_ref, t_ref, dl_ref, w1_ref, b1_ref, w2_ref, b2_ref, w3_ref, skip_ref,
                   c0_ref, s0_ref, gr_ref, gi_ref, hs_ref, hd_ref):
    j = pl.program_id(0)

    @pl.when(j == 0)
    def _():
        hid = jnp.sin(HY_FREQ * (_dot3(pos_ref[...], w1_ref[...]) + b1_ref[...]))
        hid = jnp.sin(HY_FREQ * (_dot3(hid, w2_ref[...]) + b2_ref[...]))
        h = _dot3(hid, w3_ref[...])
        decay = jnp.exp(-t_ref[...] * dl_ref[...])
        for o in range(n_orders):
            hf = h[:, (2 * o) * width:(2 * o + 1) * width] * decay
            hb = h[:, (2 * o + 1) * width:(2 * o + 2) * width] * decay
            hs_ref[:, o * width:(o + 1) * width] = (hf + hb).astype(BF16)
            hd_ref[:, o * width:(o + 1) * width] = (hb - hf).astype(BF16)

    scale = 1.0 / l
    gr = _dot(c0_ref[...], hs_ref[...])
    gi = _dot(s0_ref[...], hd_ref[...])
    for o in range(n_orders):
        gr_ref[o] = (gr[:, o * width:(o + 1) * width] + skip_ref[o]) * scale
        gi_ref[o] = gi[:, o * width:(o + 1) * width] * scale


def hyena_spectra(l, w1, b1, w2, b2, w3, skip, c0, s0):
    width = skip.shape[-1]
    n_orders = skip.shape[0]
    t = jnp.linspace(0.0, 1.0, l, dtype=F32)[:, None]
    bands = (HY_EMB - 1) // 2
    w = 2.0 * math.pi * jnp.arange(l, dtype=F32)[:, None] / l
    f = jnp.linspace(1e-4, bands - 1, bands, dtype=F32)[None, :]
    pos = jnp.concatenate([t, jnp.cos(f * w), -jnp.sin(f * w),
                           jnp.zeros((l, HY_EMB_PAD - HY_EMB), F32)], axis=-1)
    w1p = jnp.concatenate([w1, jnp.zeros((HY_EMB_PAD - HY_EMB, w1.shape[1]), F32)], axis=0)
    deltas = jnp.linspace(HY_MIN_DECAY, HY_MAX_DECAY, width, dtype=F32)[None, :]
    tk = min(l, 256)
    full = lambda a: pl.BlockSpec(a.shape, lambda j: (0,) * a.ndim)
    skip3 = skip.reshape(n_orders, 1, width)
    args = (pos, t, deltas, w1p, b1.reshape(1, -1), w2, b2.reshape(1, -1), w3, skip3)
    return pl.pallas_call(
        functools.partial(_filter_kernel, n_orders, width, l, tk),
        grid=(l // tk,),
        in_specs=[full(a) for a in args] + [pl.BlockSpec((tk, l), lambda j: (j, 0)),
                                            pl.BlockSpec((tk, l), lambda j: (j, 0))],
        out_specs=[pl.BlockSpec((n_orders, tk, width), lambda j: (0, j, 0)),
                   pl.BlockSpec((n_orders, tk, width), lambda j: (0, j, 0))],
        out_shape=[jax.ShapeDtypeStruct((n_orders, l, width), F32)] * 2,
        scratch_shapes=[pltpu.VMEM((l, n_orders * width), BF16),
                        pltpu.VMEM((l, n_orders * width), BF16)],
        compiler_params=_cparams(("arbitrary",), VMEM_LIMIT),
        name="hyena_spectra",
    )(*args, c0, s0)


def _hyena_kernel(l, fc, zv_ref, z1_ref, z2_ref, wv_ref, w1_ref, w2_ref, bv_ref, b1_ref, b2_ref,
                  c_ref, s_ref, gr_ref, gi_ref, y_ref, ub_ref, a_ref, b_ref, x1_ref, x2_ref):
    rows = lax.broadcasted_iota(jnp.int32, (l, 1), 0)

    def conv3(z_ref, w_ref, bias_ref):
        z = z_ref[...].astype(F32)
        zm = jnp.where(rows == 0, 0.0, pltpu.roll(z, 1, 0))
        zp = jnp.where(rows == l - 1, 0.0, pltpu.roll(z, l - 1, 0))
        return zm * w_ref[0:1, :] + z * w_ref[1:2, :] + zp * w_ref[2:3, :] + bias_ref[...]

    ub_ref[...] = conv3(zv_ref, wv_ref, bv_ref).astype(BF16)
    x1_ref[...] = conv3(z1_ref, w1_ref, b1_ref)
    x2_ref[...] = conv3(z2_ref, w2_ref, b2_ref)
    n = l // fc

    def forward(o):
        def body(c, carry):
            r0 = pl.multiple_of(c * fc, fc)
            ur = _dot(c_ref[pl.ds(r0, fc), :], ub_ref[...])
            us = _dot(s_ref[pl.ds(r0, fc), :], ub_ref[...])
            gr = gr_ref[o, pl.ds(r0, fc), :]
            gi = gi_ref[o, pl.ds(r0, fc), :]
            a_ref[pl.ds(r0, fc), :] = (ur * gr + us * gi).astype(BF16)
            b_ref[pl.ds(r0, fc), :] = (us * gr - ur * gi).astype(BF16)
            return carry
        lax.fori_loop(0, n, body, 0)

    def inverse(gate_ref, dst_ref):
        def body(c, carry):
            r0 = pl.multiple_of(c * fc, fc)
            lc = (_dot(c_ref[pl.ds(r0, fc), :], a_ref[...])
                  + _dot(s_ref[pl.ds(r0, fc), :], b_ref[...]))
            dst_ref[pl.ds(r0, fc), :] = (gate_ref[pl.ds(r0, fc), :] * lc).astype(BF16)
            return carry
        lax.fori_loop(0, n, body, 0)

    forward(0)
    inverse(x1_ref, ub_ref)
    forward(1)
    inverse(x2_ref, y_ref)


def hyena_mixer(zh, conv_w, conv_b, ctab, stab, gr, gi, l, n_seq, row0, cb):
    width = gr.shape[-1]
    n_orders = gr.shape[0]
    ncb = width // cb
    sb0 = row0 // l
    conv_b2 = conv_b.reshape(1, -1)
    zspec = lambda part: pl.BlockSpec((l, cb), lambda c, s: (sb0 + s, part * ncb + c))
    wspec = lambda part: pl.BlockSpec((3, cb), lambda c, s: (0, part * ncb + c))
    bspec = lambda part: pl.BlockSpec((1, cb), lambda c, s: (0, part * ncb + c))
    once = pl.Buffered(1)
    tspec = pl.BlockSpec((l, l), lambda c, s: (0, 0), pipeline_mode=once)
    gspec = pl.BlockSpec((n_orders, l, cb), lambda c, s: (0, 0, c), pipeline_mode=once)
    return pl.pallas_call(
        functools.partial(_hyena_kernel, l, min(l, HY_FREQ_CHUNK)),
        grid=(ncb, n_seq),
        in_specs=[zspec(0), zspec(1), zspec(2), wspec(0), wspec(1), wspec(2),
                  bspec(0), bspec(1), bspec(2), tspec, tspec, gspec, gspec],
        out_specs=pl.BlockSpec((l, cb), lambda c, s: (s, c)),
        out_shape=jax.ShapeDtypeStruct((n_seq * l, width), BF16),
        scratch_shapes=[pltpu.VMEM((l, cb), BF16), pltpu.VMEM((l, cb), BF16),
                        pltpu.VMEM((l, cb), BF16), pltpu.VMEM((l, cb), F32),
                        pltpu.VMEM((l, cb), F32)],
        compiler_params=_cparams(("arbitrary", "arbitrary"), VMEM_LIMIT),
        name="hyena_mixer",
    )(zh, zh, zh, conv_w, conv_w, conv_w, conv_b2, conv_b2, conv_b2, ctab, stab, gr, gi)


def _gla_kernel(l, qk, wv, dk, dv,
                zg_ref, la_ref, s0_ref, g_ref, y_ref, sfin_ref, of_ref, ob_ref, st_ref):
    ch = GLA_CHUNK
    n = l // ch
    heads = qk // dk
    r_i = lax.broadcasted_iota(jnp.int32, (ch, ch), 0)
    c_i = lax.broadcasted_iota(jnp.int32, (ch, ch), 1)
    low = r_i >= c_i
    upp = r_i <= c_i
    tri = (low.astype(BF16), upp.astype(BF16))
    masks = (low, upp)
    bd = (lax.broadcasted_iota(jnp.int32, (wv, qk), 0) // dv
          == lax.broadcasted_iota(jnp.int32, (wv, qk), 1) // dk).astype(F32)
    lane_head = lax.broadcasted_iota(jnp.int32, (1, qk), 1) // dk
    scale = dk ** -0.5

    st_ref[...] = s0_ref[0]

    def chunk(i, d):
        r0 = pl.multiple_of(i * ch, ch)
        q = zg_ref[pl.ds(r0, ch), 0:qk].astype(F32) * scale
        k = zg_ref[pl.ds(r0, ch), qk:2 * qk].astype(F32)
        v = zg_ref[pl.ds(r0, ch), 2 * qk:2 * qk + wv]
        la = la_ref[pl.ds(r0, ch), d * qk:(d + 1) * qk]
        lh, ll = _split(la)
        b = _dot(tri[d], lh) + _dot(tri[d], ll)
        tot = b[ch - 1:ch, :] if d == 0 else b[0:1, :]
        qe = (q * jnp.exp(b)).astype(BF16)
        ke = (k * jnp.exp(-b)).astype(BF16)
        ks = (k * jnp.exp(tot - b)).astype(BF16)
        st = st_ref[d]
        o = lax.dot_general(qe, st.astype(BF16), (((1,), (1,)), ((), ())),
                            preferred_element_type=F32)
        parts = []
        for h in range(heads):
            qh = jnp.where(lane_head == h, qe, jnp.zeros_like(qe))
            att = lax.dot_general(qh, ke, (((1,), (1,)), ((), ())), preferred_element_type=F32)
            att = jnp.where(masks[d], att, 0.0).astype(BF16)
            parts.append(_dot(att, v[:, h * dv:(h + 1) * dv]))
        o = o + jnp.concatenate(parts, axis=1)
        upd = lax.dot_general(v, ks, (((0,), (0,)), ((), ())), preferred_element_type=F32)
        st_ref[d] = st * jnp.exp(tot) + upd * bd
        return r0, o

    def body(i, carry):
        r0, o = chunk(i, 0)
        of_ref[pl.ds(r0, ch), :] = o
        r0, o = chunk(n - 1 - i, 1)
        ob_ref[pl.ds(r0, ch), :] = o
        return carry

    lax.fori_loop(0, n, body, 0)
    sfin_ref[0] = st_ref[...]

    def finish(i, carry):
        r0 = pl.multiple_of(i * ch, ch)
        o = of_ref[pl.ds(r0, ch), :] + ob_ref[pl.ds(r0, ch), :]
        r = zg_ref[pl.ds(r0, ch), 2 * qk + wv:2 * qk + 2 * wv].astype(F32)
        parts = [_rms(o[:, h * dv:(h + 1) * dv], g_ref[...]) for h in range(heads)]
        y_ref[pl.ds(r0, ch), :] = (jnp.concatenate(parts, axis=1) * _silu(r)).astype(BF16)
        return carry

    lax.fori_loop(0, n, finish, 0)


def gla_mixer(zg, la, s0, norm_g, l, n_seq, row0):
    qk = la.shape[1] // 2
    wv = (zg.shape[1] - 2 * qk) // 2
    dk, dv = qk // GLA_HEADS, wv // GLA_HEADS
    sb0 = row0 // l
    return pl.pallas_call(
        functools.partial(_gla_kernel, l, qk, wv, dk, dv),
        grid=(n_seq,),
        in_specs=[pl.BlockSpec((l, zg.shape[1]), lambda s: (sb0 + s, 0)),
                  pl.BlockSpec((l, la.shape[1]), lambda s: (sb0 + s, 0)),
                  pl.BlockSpec((1, 2, wv, qk), lambda s: (s, 0, 0, 0)),
                  pl.BlockSpec((1, dv), lambda s: (0, 0))],
        out_specs=[pl.BlockSpec((l, wv), lambda s: (s, 0)),
                   pl.BlockSpec((1, 2, wv, qk), lambda s: (s, 0, 0, 0))],
        out_shape=[jax.ShapeDtypeStruct((n_seq * l, wv), BF16),
                   jax.ShapeDtypeStruct((n_seq, 2, wv, qk), F32)],
        scratch_shapes=[pltpu.VMEM((l, wv), F32), pltpu.VMEM((l, wv), F32),
                        pltpu.VMEM((2, wv, qk), F32)],
        compiler_params=_cparams(("arbitrary",), VMEM_LIMIT),
        name="gla_mixer",
    )(zg, la, s0, norm_g.reshape(1, dv))


def _outproj_kernel(yh_ref, yg_ref, wh_ref, wg_ref, x_ref, g1_ref, g_ref, sh_ref, sc_ref,
                    xo_ref, u_ref):
    delta = _dot(yh_ref[...], wh_ref[...]) + _dot(yg_ref[...], wg_ref[...])
    x = x_ref[...] + g1_ref[0] * delta
    xo_ref[...] = x
    u_ref[...] = _rms(x, g_ref[...]) * (1.0 + sc_ref[0]) + sh_ref[0]


def out_proj(y_hy, y_gla, w_hy, w_gla, xs, mod, norm_g, tm, seq, n_lat, b, n_tok):
    d = xs.shape[1]
    nlt = n_lat // tm
    row = functools.partial(_mod_row, tm=tm, seq=seq, n_lat_tiles=nlt, b=b)
    full = lambda a: pl.BlockSpec(a.shape, lambda i: (0,) * a.ndim)
    modspec = lambda col: pl.BlockSpec((1, 1, d), lambda i: (row(i), 0, col))
    tile = lambda w: pl.BlockSpec((tm, w), lambda i: (i, 0))
    return pl.pallas_call(
        _outproj_kernel,
        grid=(n_tok // tm,),
        in_specs=[tile(y_hy.shape[1]), tile(y_gla.shape[1]), full(w_hy), full(w_gla), tile(d),
                  modspec(2), full(norm_g), modspec(3), modspec(4)],
        out_specs=[tile(d), tile(d)],
        out_shape=[jax.ShapeDtypeStruct((n_tok, d), F32)] * 2,
        compiler_params=_cparams(("arbitrary",), VMEM_LIMIT),
        name="out_proj",
    )(y_hy, y_gla, w_hy, w_gla, xs, mod, norm_g, mod, mod)


def _router_kernel(n_exp, tm, u_ref, wh_ref, wl_ref, b_ref, tri_ref,
                   idx_ref, gate_ref, rank_ref, cnt_ref):
    i = pl.program_id(0)

    @pl.when(i == 0)
    def _():
        cnt_ref[...] = jnp.zeros_like(cnt_ref)

    uh, ul = _split(u_ref[...])
    nt = (((1,), (1,)), ((), ()))
    logits = (lax.dot_general(wh_ref[...], uh, nt, preferred_element_type=F32)
              + lax.dot_general(wh_ref[...], ul, nt, preferred_element_type=F32)
              + lax.dot_general(wl_ref[...], uh, nt, preferred_element_type=F32))
    s = 1.0 / (1.0 + jnp.exp(-logits))
    sel = s + b_ref[...]
    gsz = n_exp // N_GROUPS
    neg = -jnp.inf
    rows_g = lax.broadcasted_iota(jnp.int32, (gsz, tm), 0)
    blocks, gscore = [], []
    for g in range(N_GROUPS):
        blk = sel[g * gsz:(g + 1) * gsz, :]
        m1 = jnp.max(blk, axis=0, keepdims=True)
        i1 = jnp.min(jnp.where(blk == m1, rows_g, gsz), axis=0, keepdims=True)
        m2 = jnp.max(jnp.where(rows_g == i1, neg, blk), axis=0, keepdims=True)
        blocks.append(blk)
        gscore.append(m1 + m2)
    masked = []
    for g in range(N_GROUPS):
        beaten = jnp.zeros((1, tm), jnp.int32)
        for h in range(N_GROUPS):
            if h == g:
                continue
            wins = (gscore[h] > gscore[g]) | ((gscore[h] == gscore[g]) & (h < g))
            beaten = beaten + wins.astype(jnp.int32)
        masked.append(jnp.where(beaten < TOPK_GROUPS, blocks[g], neg))
    sel = jnp.concatenate(masked, axis=0)

    rows = lax.broadcasted_iota(jnp.int32, (n_exp, tm), 0)
    chosen = jnp.zeros((n_exp, tm), F32)
    idxs, gates = [], []
    for _ in range(TOP_K):
        m = jnp.max(sel, axis=0, keepdims=True)
        ik = jnp.min(jnp.where(sel == m, rows, n_exp), axis=0, keepdims=True)
        hit = rows == ik
        gates.append(jnp.sum(jnp.where(hit, s, 0.0), axis=0, keepdims=True))
        sel = jnp.where(hit, neg, sel)
        chosen = jnp.where(hit, 1.0, chosen)
        idxs.append(ik)
    gsum = gates[0]
    for g in gates[1:]:
        gsum = gsum + g
    inv = ROUTED_SCALE / gsum
    before = _dot(chosen.astype(BF16), tri_ref[...]) + cnt_ref[...]
    for k in range(TOP_K):
        idx_ref[k:k + 1, :] = idxs[k]
        gate_ref[k:k + 1, :] = gates[k] * inv
        rk = jnp.sum(jnp.where(rows == idxs[k], before, 0.0), axis=0, keepdims=True)
        rank_ref[k:k + 1, :] = rk.astype(jnp.int32)
    cnt_ref[...] += jnp.sum(chosen, axis=1, keepdims=True)


def moe_router(u, wr_hi, wr_lo, b_r, n_tok, tm):
    n_exp, d = wr_hi.shape
    tri = (lax.broadcasted_iota(jnp.int32, (tm, tm), 0)
           < lax.broadcasted_iota(jnp.int32, (tm, tm), 1)).astype(BF16)
    full = lambda a: pl.BlockSpec(a.shape, lambda i: (0,) * a.ndim)
    tok = pl.BlockSpec((TOP_K, tm), lambda i: (0, i))
    return pl.pallas_call(
        functools.partial(_router_kernel, n_exp, tm),
        grid=(n_tok // tm,),
        in_specs=[pl.BlockSpec((tm, d), lambda i: (i, 0)), full(wr_hi), full(wr_lo),
                  pl.BlockSpec((n_exp, 1), lambda i: (0, 0)), full(tri)],
        out_specs=[tok, tok, tok, pl.BlockSpec((n_exp, 1), lambda i: (0, 0))],
        out_shape=[jax.ShapeDtypeStruct((TOP_K, n_tok), jnp.int32),
                   jax.ShapeDtypeStruct((TOP_K, n_tok), F32),
                   jax.ShapeDtypeStruct((TOP_K, n_tok), jnp.int32),
                   jax.ShapeDtypeStruct((n_exp, 1), F32)],
        compiler_params=_cparams(("arbitrary",), VMEM_LIMIT),
        name="moe_router",
    )(u, wr_hi, wr_lo, b_r.reshape(n_exp, 1), tri)


def _row_copy(src, src_row, dst, dst_row, sem):
    return pltpu.make_async_copy(src.at[pl.ds(src_row, 1)], dst.at[pl.ds(dst_row, 1)], sem)


def _pack_bf16_pairs(x):
    w = x.shape[1] // 2
    lo = pltpu.bitcast(x[:, :w].astype(BF16).astype(F32), jnp.uint32) >> 16
    hi = pltpu.bitcast(x[:, w:].astype(BF16).astype(F32), jnp.uint32)
    return hi | lo


def _unpack_bf16_pairs(p):
    lo = pltpu.bitcast(p << 16, F32)
    hi = pltpu.bitcast(p & jnp.uint32(0xFFFF0000), F32)
    return jnp.concatenate([lo, hi], axis=1).astype(BF16)


def _dest_kernel(n_exp, tm, idx_ref, rank_ref, ps_ref, dest_ref):
    rows = lax.broadcasted_iota(jnp.int32, (n_exp, tm), 0)
    ps = ps_ref[...]
    for k in range(TOP_K):
        base = jnp.sum(jnp.where(rows == idx_ref[k:k + 1, :], ps, 0), axis=0, keepdims=True)
        dest_ref[k:k + 1, :] = base + rank_ref[k:k + 1, :]


def moe_dest(idx, rank, pad_start, tm):
    n_tok = idx.shape[1]
    n_exp = pad_start.shape[0]
    tok = pl.BlockSpec((TOP_K, tm), lambda i: (0, i))
    return pl.pallas_call(
        functools.partial(_dest_kernel, n_exp, tm),
        grid=(n_tok // tm,),
        in_specs=[tok, tok, pl.BlockSpec((n_exp, 1), lambda i: (0, 0))],
        out_specs=tok,
        out_shape=jax.ShapeDtypeStruct((TOP_K, n_tok), jnp.int32),
        compiler_params=_cparams(("arbitrary",)),
        name="moe_dest",
    )(idx, rank, pad_start.reshape(n_exp, 1))


def _wait_rows(src, dst, sem, tm):
    for _ in range(TOP_K):
        pltpu.make_async_copy(src.at[pl.ds(0, tm)], dst.at[pl.ds(0, tm)], sem).wait()


def _dispatch_kernel(tm, n_steps, dest_ref, u_ref, xin_ref, xs_ref, pk_ref, sems):
    del xin_ref
    i = pl.program_id(0)
    slot = i % 2
    pk = pk_ref.at[slot]
    sem = sems.at[slot]

    @pl.when(i >= 2)
    def _():
        _wait_rows(pk, xs_ref, sem, tm)

    pk[...] = _pack_bf16_pairs(u_ref[...])

    def issue(t, carry):
        for k in range(TOP_K):
            _row_copy(pk, t, xs_ref, dest_ref[k, t], sem).start()
        return carry

    lax.fori_loop(0, tm, issue, 0)

    @pl.when(i == n_steps - 1)
    def _():
        if n_steps > 1:
            _wait_rows(pk_ref.at[1 - slot], xs_ref, sems.at[1 - slot], tm)
        _wait_rows(pk, xs_ref, sem, tm)


def moe_dispatch(u, dest, n_tok, n_slots, tm):
    d = u.shape[1]
    zeros = jnp.zeros((n_slots, d // 2), jnp.uint32)
    return pl.pallas_call(
        functools.partial(_dispatch_kernel, tm, n_tok // tm),
        grid=(n_tok // tm,),
        in_specs=[pl.BlockSpec((TOP_K, tm), lambda i: (0, i), memory_space=pltpu.SMEM),
                  pl.BlockSpec((tm, d), lambda i: (i, 0)),
                  pl.BlockSpec(memory_space=pl.ANY)],
        out_specs=pl.BlockSpec(memory_space=pl.ANY),
        out_shape=jax.ShapeDtypeStruct((n_slots, d // 2), jnp.uint32),
        scratch_shapes=[pltpu.VMEM((2, tm, d // 2), jnp.uint32), pltpu.SemaphoreType.DMA((2,))],
        input_output_aliases={2: 0},
        compiler_params=_cparams(("arbitrary",)),
        name="moe_dispatch",
    )(dest, u, zeros)


def _gmm_kernel(be_ref, nu_ref, x_ref, w1_ref, w3_ref, w2_ref, y_ref, w1b, w3b, w2b):
    i = pl.program_id(0)
    prev = be_ref[jnp.maximum(i - 1, 0)]

    @pl.when(i < nu_ref[0])
    def _():
        @pl.when((i == 0) | (be_ref[i] != prev))
        def _():
            w1b[...] = w1_ref[0, 0].astype(BF16)
            w3b[...] = w3_ref[0, 0].astype(BF16)
            w2b[...] = w2_ref[0, 0].astype(BF16)

        x = _unpack_bf16_pairs(x_ref[...])
        h = _silu(_dot(x, w1b[...])) * _dot(x, w3b[...])
        y_ref[...] = _pack_bf16_pairs(_dot(h.astype(BF16), w2b[...]))

    @pl.when(i >= nu_ref[0])
    def _():
        y_ref[...] = jnp.zeros_like(y_ref)


def moe_experts(x_sorted, block_e, n_used, w1, w3, w2, layer):
    n_blocks = block_e.shape[0]
    _, _, d, f = w1.shape
    grid_spec = pltpu.PrefetchScalarGridSpec(
        num_scalar_prefetch=2,
        grid=(n_blocks,),
        in_specs=[pl.BlockSpec((MOE_ROWS, d // 2), lambda i, be, nu: (i, 0)),
                  pl.BlockSpec((1, 1, d, f), lambda i, be, nu: (layer, be[i], 0, 0)),
                  pl.BlockSpec((1, 1, d, f), lambda i, be, nu: (layer, be[i], 0, 0)),
                  pl.BlockSpec((1, 1, f, d), lambda i, be, nu: (layer, be[i], 0, 0))],
        out_specs=pl.BlockSpec((MOE_ROWS, d // 2), lambda i, be, nu: (i, 0)),
        scratch_shapes=[pltpu.VMEM((d, f), BF16), pltpu.VMEM((d, f), BF16),
                        pltpu.VMEM((f, d), BF16)])
    return pl.pallas_call(
        _gmm_kernel,
        grid_spec=grid_spec,
        out_shape=jax.ShapeDtypeStruct((x_sorted.shape[0], d // 2), jnp.uint32),
        compiler_params=_cparams(("arbitrary",), VMEM_LIMIT),
        name="moe_experts",
    )(block_e, n_used, x_sorted, w1, w3, w2)


def _combine_kernel(tm, n_steps, final, dest_ref, nxt_ref, y_ref, gate_ref, u_ref, ws1_ref, ws3_ref,
                    ws2_ref, x_ref, g2_ref, fg_ref, o_ref, buf_ref, sems):
    i = pl.program_id(0)
    slot = i % 2

    def gather(d_ref, s):
        def issue(t, carry):
            for k in range(TOP_K):
                _row_copy(y_ref, d_ref[k, t], buf_ref.at[s, k], t, sems.at[s]).start()
            return carry
        lax.fori_loop(0, tm, issue, 0)

    @pl.when(i == 0)
    def _():
        gather(dest_ref, slot)

    @pl.when(i + 1 < n_steps)
    def _():
        gather(nxt_ref, 1 - slot)

    ub = u_ref[...].astype(BF16)
    hs = _silu(_dot(ub, ws1_ref[...])) * _dot(ub, ws3_ref[...])
    shared = _dot(hs.astype(BF16), ws2_ref[...])

    _wait_rows(y_ref, buf_ref.at[slot, 0], sems.at[slot], tm)

    lo = hi = None
    for k in range(TOP_K):
        p = buf_ref[slot, k]
        g = gate_ref[:, k:k + 1]
        lo_k = pltpu.bitcast(p << 16, F32) * g
        hi_k = pltpu.bitcast(p & jnp.uint32(0xFFFF0000), F32) * g
        lo = lo_k if lo is None else lo + lo_k
        hi = hi_k if hi is None else hi + hi_k
    routed = jnp.concatenate([lo, hi], axis=1)
    x = x_ref[...] + g2_ref[0] * (routed + shared)
    o_ref[...] = _rms(x, fg_ref[...]) if final else x


def moe_combine(y_sorted, dest, gate_rep, u, ws1, ws3, ws2, xs, mod, final_g, final,
                tm, seq, n_lat, b, n_tok):
    d = u.shape[1]
    nlt = n_lat // tm
    row = functools.partial(_mod_row, tm=tm, seq=seq, n_lat_tiles=nlt, b=b)
    full = lambda a: pl.BlockSpec(a.shape, lambda i: (0,) * a.ndim)
    tile = pl.BlockSpec((tm, d), lambda i: (i, 0))
    n_steps = n_tok // tm
    return pl.pallas_call(
        functools.partial(_combine_kernel, tm, n_steps, final),
        grid=(n_steps,),
        in_specs=[pl.BlockSpec((TOP_K, tm), lambda i: (0, i), memory_space=pltpu.SMEM),
                  pl.BlockSpec((TOP_K, tm), lambda i: (0, jnp.minimum(i + 1, n_steps - 1)),
                               memory_space=pltpu.SMEM),
                  pl.BlockSpec(memory_space=pl.ANY),
                  pl.BlockSpec((tm, TOP_K), lambda i: (i, 0)),
                  tile, full(ws1), full(ws3), full(ws2), tile,
                  pl.BlockSpec((1, 1, d), lambda i: (row(i), 0, 5)),
                  full(final_g)],
        out_specs=tile,
        out_shape=jax.ShapeDtypeStruct((n_tok, d), F32),
        scratch_shapes=[pltpu.VMEM((2, TOP_K, tm, d // 2), jnp.uint32),
                        pltpu.SemaphoreType.DMA((2,))],
        compiler_params=_cparams(("arbitrary",), VMEM_LIMIT),
        name="moe_combine",
    )(dest, dest, y_sorted, gate_rep, u, ws1, ws3, ws2, xs, mod, final_g)


def moe_layer(u, xs, mod, router_w, router_b, w1, w3, w2, layer, ws1, ws3, ws2, final_g, final,
              tm, seq, n_lat, b, n_tok):
    n_exp = router_w.shape[1]
    wr_hi, wr_lo = _split(router_w.T)
    idx, gate, rank, cnt = moe_router(u, wr_hi, wr_lo, router_b, n_tok, tm)
    counts = cnt[:, 0].astype(jnp.int32)
    padded = (counts + MOE_ROWS - 1) // MOE_ROWS * MOE_ROWS
    pad_end = jnp.cumsum(padded)
    dest = moe_dest(idx, rank, pad_end - padded, tm)
    n_blocks = -(-(n_tok * TOP_K + n_exp * (MOE_ROWS - 1)) // MOE_ROWS)
    block_start = jnp.arange(n_blocks, dtype=jnp.int32) * MOE_ROWS
    block_e = jnp.minimum(jnp.searchsorted(pad_end, block_start, side='right'),
                          n_exp - 1).astype(jnp.int32)
    n_used = (pad_end[-1:] // MOE_ROWS).astype(jnp.int32)
    x_sorted = moe_dispatch(u, dest, n_tok, n_blocks * MOE_ROWS, tm)
    y_sorted = moe_experts(x_sorted, block_e, n_used, w1, w3, w2, layer)
    return moe_combine(y_sorted, dest, gate.T, u, ws1.astype(BF16), ws3.astype(BF16),
                       ws2.astype(BF16), xs, mod, final_g.reshape(1, -1), final,
                       tm, seq, n_lat, b, n_tok)


def _grid_pos_emb(rows, d):
    r, col = jnp.meshgrid(jnp.arange(rows, dtype=F32), jnp.arange(GRID_W, dtype=F32), indexing='ij')
    quarter = d // 4
    omega = 1.0 / (10000.0 ** (jnp.arange(quarter, dtype=F32) / quarter))

    def emb(p):
        a = p.reshape(-1)[:, None] * omega[None, :]
        return jnp.concatenate([jnp.sin(a), jnp.cos(a)], axis=-1)

    return jnp.concatenate([emb(r), emb(col)], axis=-1)


def kernel(x, c, ctx, c_ctx, ada_w, ada_b, norm1_g, norm2_g, w_in, w_out, hy_conv_w, hy_conv_b,
           hy_f_w1, hy_f_b1, hy_f_w2, hy_f_b2, hy_f_w3, hy_skip, gla_wa2, gla_ba2, gla_norm_g,
           router_w, router_b, exp_w1, exp_w3, exp_w2, sh_w1, sh_w3, sh_w2, final_g):
    b, l, d = x.shape
    lc = ctx.shape[1]
    depth = ada_w.shape[0]
    n_lat, n_ctx = b * l, b * lc
    n_all = n_lat + n_ctx
    hy_width = hy_skip.shape[-1]
    hy_cols = hy_conv_w.shape[-1]
    qk = gla_wa2.shape[-1]
    wv = d - hy_width
    g_cols = 2 * qk + 2 * wv
    tm = math.gcd(math.gcd(l, lc), 256)
    tm_proj = math.gcd(math.gcd(l, n_ctx), 512)
    cb = min(hy_width, 256)

    cond = jnp.concatenate([c, c_ctx[None, :]], axis=0)
    n_rows = -(-(b + 1) // 8) * 8
    cond = jnp.concatenate([cond, jnp.zeros((n_rows - b - 1, d), F32)], axis=0)
    mods = ada_table(cond, ada_w, ada_b)

    xs = embed_tokens(x.reshape(n_lat, d), _grid_pos_emb(l // GRID_W, d), ctx.reshape(n_ctx, d), tm)

    tabs = {}
    for length in (l, lc):
        if length not in tabs:
            tabs[length] = _trig_tables(length, True) + _trig_tables(length, False)
    zero_state = jnp.zeros((b, 2, wv, qk), F32)

    for layer in range(depth):
        last = layer == depth - 1
        mod = mods[layer][:, None, :]
        g1 = norm1_g[layer].reshape(1, d)
        g2 = norm2_g[layer].reshape(1, d)
        w_l = w_in[layer].astype(BF16)
        wa2 = gla_wa2[layer]
        rank = wa2.shape[1]
        wa2_bd = jnp.zeros((2 * rank, 2 * qk), F32)
        wa2_bd = wa2_bd.at[:rank, :qk].set(wa2[0]).at[rank:, qk:].set(wa2[1])
        ba2_cat = gla_ba2[layer].reshape(1, 2 * qk)
        zh, zg, la = in_proj(xs, g1, mod, w_l[:, :hy_cols], w_l[:, hy_cols:hy_cols + g_cols],
                             w_l[:, hy_cols + g_cols:], wa2_bd, ba2_cat, tm_proj, l, n_lat, b)

        filt = (hy_f_w1[layer], hy_f_b1[layer], hy_f_w2[layer], hy_f_b2[layer], hy_f_w3[layer],
                hy_skip[layer])
        yc_gla, s_ctx = gla_mixer(zg, la, zero_state, gla_norm_g[layer], lc, b, n_lat)
        y_gla, _ = gla_mixer(zg, la, s_ctx, gla_norm_g[layer], l, b, 0)
        ct, st, c0, s0 = tabs[l]
        gr, gi = hyena_spectra(l, *filt, c0, s0)
        y_hy = hyena_mixer(zh, hy_conv_w[layer], hy_conv_b[layer], ct, st, gr, gi, l, b, 0, cb)
        if last:
            n_tok = n_lat
        else:
            n_tok = n_all
            ct, st, c0, s0 = tabs[lc]
            gr, gi = hyena_spectra(lc, *filt, c0, s0)
            yc_hy = hyena_mixer(zh, hy_conv_w[layer], hy_conv_b[layer], ct, st, gr, gi,
                                lc, b, n_lat, cb)
            y_hy = jnp.concatenate([y_hy, yc_hy], axis=0)
            y_gla = jnp.concatenate([y_gla, yc_gla], axis=0)
        w_o = w_out[layer].astype(BF16)
        xs_mid, u = out_proj(y_hy, y_gla, w_o[:hy_width], w_o[hy_width:], xs, mod, g2,
                             tm, l, n_lat, b, n_tok)
        xs = moe_layer(u, xs_mid, mod, router_w[layer], router_b[layer], exp_w1, exp_w3, exp_w2,
                       layer, sh_w1[layer], sh_w3[layer], sh_w2[layer],
                       final_g, last, tm, l, n_lat, b, n_tok)
    return xs[:n_lat].reshape(b, l, d)
```

```python
import functools
import math

import numpy as np
import jax
import jax.numpy as jnp
from jax import lax
from jax.experimental import pallas as pl
from jax.experimental.pallas import tpu as pltpu

GRID_W = 64
EPS = 1e-6

HY_ORDER = 2
HY_EMB = 33
HY_EMB_PAD = 40
HY_FREQ = 1.0
HY_TARGET = 1e-2
HY_FAST_PCT = 0.3
HY_SLOW_PCT = 1.5
HY_MIN_DECAY = math.log(1.0 / HY_TARGET) / HY_SLOW_PCT
HY_MAX_DECAY = math.log(1.0 / HY_TARGET) / HY_FAST_PCT

GLA_HEADS = 4
GLA_RANK = 16
GLA_TAU = 16.0
GLA_CHUNK = 64
GLA_GROUP = 4
GLA_STATE_UNROLL = 4

TOP_K = 8
N_GROUPS = 8
TOPK_GROUPS = 4
ROUTED_SCALE = 2.5

HY_FREQ_CHUNK = 1024
MOE_ROWS = 256
VMEM_LIMIT = 56 * 1024 * 1024

F32 = jnp.float32
BF16 = jnp.bfloat16


def _cparams(sem, vmem=None):
    return pltpu.CompilerParams(dimension_semantics=sem, vmem_limit_bytes=vmem)


def _split(a):
    hi = a.astype(BF16)
    lo = (a - hi.astype(F32)).astype(BF16)
    return hi, lo


def _dot(a, b):
    return jnp.dot(a, b, preferred_element_type=F32)


def _dot3(a, b):
    ah, al = _split(a)
    bh, bl = _split(b)
    return _dot(ah, bh) + _dot(ah, bl) + _dot(al, bh)


def _silu(x):
    return x * (1.0 / (1.0 + jnp.exp(-x)))


def _rms(x, g):
    return x * lax.rsqrt(jnp.mean(x * x, axis=-1, keepdims=True) + EPS) * g


def _ada_kernel(c_ref, w_ref, b_ref, o_ref):
    o_ref[0] = _dot3(_silu(c_ref[...]), w_ref[0]) + b_ref[0]


def ada_table(cond_rows, ada_w, ada_b):
    depth, d, six_d = ada_w.shape
    r = cond_rows.shape[0]
    tn = 1024
    return pl.pallas_call(
        _ada_kernel,
        grid=(depth, six_d // tn),
        in_specs=[pl.BlockSpec((r, d), lambda l, j: (0, 0)),
                  pl.BlockSpec((1, d, tn), lambda l, j: (l, 0, j)),
                  pl.BlockSpec((1, 1, tn), lambda l, j: (l, 0, j))],
        out_specs=pl.BlockSpec((1, r, tn), lambda l, j: (l, 0, j)),
        out_shape=jax.ShapeDtypeStruct((depth, r, six_d), F32),
        compiler_params=_cparams(("arbitrary", "arbitrary")),
        name="ada_table",
    )(cond_rows, ada_w, ada_b.reshape(depth, 1, six_d))


def _embed_kernel(n_lat_tiles, x_ref, p_ref, c_ref, o_ref):
    i = pl.program_id(0)

    @pl.when(i < n_lat_tiles)
    def _():
        o_ref[...] = x_ref[...] + p_ref[...]

    @pl.when(i >= n_lat_tiles)
    def _():
        o_ref[...] = c_ref[...]


def embed_tokens(x2, pos, ctx2, tm):
    nl, d = x2.shape
    nc = ctx2.shape[0]
    l = pos.shape[0]
    nlt, nct, lt = nl // tm, nc // tm, l // tm
    return pl.pallas_call(
        functools.partial(_embed_kernel, nlt),
        grid=(nlt + nct,),
        in_specs=[pl.BlockSpec((tm, d), lambda i: (jnp.minimum(i, nlt - 1), 0)),
                  pl.BlockSpec((tm, d), lambda i: (i % lt, 0)),
                  pl.BlockSpec((tm, d), lambda i: (jnp.maximum(i - nlt, 0), 0))],
        out_specs=pl.BlockSpec((tm, d), lambda i: (i, 0)),
        out_shape=jax.ShapeDtypeStruct((nl + nc, d), F32),
        compiler_params=_cparams(("arbitrary",)),
        name="embed_tokens",
    )(x2, pos, ctx2)


def _mod_row(i, tm, seq, n_lat_tiles, b):
    return jnp.where(i < n_lat_tiles, (i * tm) // seq, b)


def _inproj_kernel(x_ref, g_ref, sh_ref, sc_ref, wh_ref, wg_ref, wa_ref, wa2_ref, ba2_ref,
                   zh_ref, zg_ref, la_ref):
    x = x_ref[...]
    h = _rms(x, g_ref[...]) * (1.0 + sc_ref[0]) + sh_ref[0]
    hb = h.astype(BF16)
    zh_ref[...] = _dot(hb, wh_ref[...]).astype(BF16)
    zg_ref[...] = _dot(hb, wg_ref[...]).astype(BF16)
    za = _dot(hb, wa_ref[...])
    t = _dot3(za, wa2_ref[...]) + ba2_ref[...]
    la_ref[...] = (jnp.minimum(t, 0.0) - jnp.log(1.0 + jnp.exp(-jnp.abs(t)))) * (1.0 / GLA_TAU)


def in_proj(xs, norm_g, mod, w_hy, w_g, w_a, wa2_bd, ba2_cat, tm, seq, n_lat, b):
    n, d = xs.shape
    nlt = n_lat // tm
    row = functools.partial(_mod_row, tm=tm, seq=seq, n_lat_tiles=nlt, b=b)
    hy_cols, g_cols, la_cols = w_hy.shape[1], w_g.shape[1], wa2_bd.shape[1]
    full = lambda a: pl.BlockSpec(a.shape, lambda i: (0,) * a.ndim)
    return pl.pallas_call(
        _inproj_kernel,
        grid=(n // tm,),
        in_specs=[pl.BlockSpec((tm, d), lambda i: (i, 0)),
                  full(norm_g),
                  pl.BlockSpec((1, 1, d), lambda i: (row(i), 0, 0)),
                  pl.BlockSpec((1, 1, d), lambda i: (row(i), 0, 1)),
                  full(w_hy), full(w_g), full(w_a), full(wa2_bd), full(ba2_cat)],
        out_specs=[pl.BlockSpec((tm, hy_cols), lambda i: (i, 0)),
                   pl.BlockSpec((tm, g_cols), lambda i: (i, 0)),
                   pl.BlockSpec((tm, la_cols), lambda i: (i, 0))],
        out_shape=[jax.ShapeDtypeStruct((n, hy_cols), BF16),
                   jax.ShapeDtypeStruct((n, g_cols), BF16),
                   jax.ShapeDtypeStruct((n, la_cols), F32)],
        compiler_params=_cparams(("arbitrary",), VMEM_LIMIT),
        name="in_proj",
    )(xs, norm_g, mod, mod, w_hy, w_g, w_a, wa2_bd, ba2_cat)


def _trig_tables(l, half_shift):
    k = lax.broadcasted_iota(jnp.int32, (l, l), 0)
    n = lax.broadcasted_iota(jnp.int32, (l, l), 1)
    m = ((2 * k + 1) * (2 * n + (1 if half_shift else 0))) % (8 * l)
    ang = m.astype(F32) * (2.0 * math.pi / (8 * l))
    return jnp.cos(ang).astype(BF16), jnp.sin(ang).astype(BF16)


def _filter_kernel(n_orders, width, l, tk,
                   pos_ref, t_ref, dl_ref, w1_ref, b1_ref, w2_ref, b2_ref, w3_ref, skip_ref,
                   c0_ref, s0_ref, gr_ref, gi_ref, hs_ref, hd_ref):
    j = pl.program_id(0)

    @pl.when(j == 0)
    def _():
        hid = jnp.sin(HY_FREQ * (_dot3(pos_ref[...], w1_ref[...]) + b1_ref[...]))
        hid = jnp.sin(HY_FREQ * (_dot3(hid, w2_ref[...]) + b2_ref[...]))
        h = _dot3(hid, w3_ref[...])
        decay = jnp.exp(-t_ref[...] * dl_ref[...])
        for o in range(n_orders):
            hf = h[:, (2 * o) * width:(2 * o + 1) * width] * decay
            hb = h[:, (2 * o + 1) * width:(2 * o + 2) * width] * decay
            hs_ref[:, o * width:(o + 1) * width] = (hf + hb).astype(BF16)
            hd_ref[:, o * width:(o + 1) * width] = (hb - hf).astype(BF16)

    scale = 1.0 / l
    gr = _dot(c0_ref[...], hs_ref[...])
    gi = _dot(s0_ref[...], hd_ref[...])
    for o in range(n_orders):
        gr_ref[o] = (gr[:, o * width:(o + 1) * width] + skip_ref[o]) * scale
        gi_ref[o] = gi[:, o * width:(o + 1) * width] * scale


def hyena_spectra(l, w1, b1, w2, b2, w3, skip, c0, s0):
    width = skip.shape[-1]
    n_orders = skip.shape[0]
    t = jnp.linspace(0.0, 1.0, l, dtype=F32)[:, None]
    bands = (HY_EMB - 1) // 2
    w = 2.0 * math.pi * jnp.arange(l, dtype=F32)[:, None] / l
    f = jnp.linspace(1e-4, bands - 1, bands, dtype=F32)[None, :]
    pos = jnp.concatenate([t, jnp.cos(f * w), -jnp.sin(f * w),
                           jnp.zeros((l, HY_EMB_PAD - HY_EMB), F32)], axis=-1)
    w1p = jnp.concatenate([w1, jnp.zeros((HY_EMB_PAD - HY_EMB, w1.shape[1]), F32)], axis=0)
    deltas = jnp.linspace(HY_MIN_DECAY, HY_MAX_DECAY, width, dtype=F32)[None, :]
    tk = min(l, 256)
    full = lambda a: pl.BlockSpec(a.shape, lambda j: (0,) * a.ndim)
    skip3 = skip.reshape(n_orders, 1, width)
    args = (pos, t, deltas, w1p, b1.reshape(1, -1), w2, b2.reshape(1, -1), w3, skip3)
    return pl.pallas_call(
        functools.partial(_filter_kernel, n_orders, width, l, tk),
        grid=(l // tk,),
        in_specs=[full(a) for a in args] + [pl.BlockSpec((tk, l), lambda j: (j, 0)),
                                            pl.BlockSpec((tk, l), lambda j: (j, 0))],
        out_specs=[pl.BlockSpec((n_orders, tk, width), lambda j: (0, j, 0)),
                   pl.BlockSpec((n_orders, tk, width), lambda j: (0, j, 0))],
        out_shape=[jax.ShapeDtypeStruct((n_orders, l, width), F32)] * 2,
        scratch_shapes=[pltpu.VMEM((l, n_orders * width), BF16),
                        pltpu.VMEM((l, n_orders * width), BF16)],
        compiler_params=_cparams(("arbitrary",), VMEM_LIMIT),
        name="hyena_spectra",
    )(*args, c0, s0)


def _hyena_kernel(l, fc, zv_ref, z1_ref, z2_ref, wv_ref, w1_ref, w2_ref, bv_ref, b1_ref, b2_ref,
                  c_ref, s_ref, gr_ref, gi_ref, y_ref, ub_ref, a_ref, b_ref, x1_ref, x2_ref):
    rows = lax.broadcasted_iota(jnp.int32, (l, 1), 0)

    def conv3(z_ref, w_ref, bias_ref):
        z = z_ref[...].astype(F32)
        zm = jnp.where(rows == 0, 0.0, pltpu.roll(z, 1, 0))
        zp = jnp.where(rows == l - 1, 0.0, pltpu.roll(z, l - 1, 0))
        return zm * w_ref[0:1, :] + z * w_ref[1:2, :] + zp * w_ref[2:3, :] + bias_ref[...]

    ub_ref[...] = conv3(zv_ref, wv_ref, bv_ref).astype(BF16)
    x1_ref[...] = conv3(z1_ref, w1_ref, b1_ref)
    x2_ref[...] = conv3(z2_ref, w2_ref, b2_ref)
    n = l // fc

    def forward(o):
        def body(c, carry):
            r0 = pl.multiple_of(c * fc, fc)
            ur = _dot(c_ref[pl.ds(r0, fc), :], ub_ref[...])
            us = _dot(s_ref[pl.ds(r0, fc), :], ub_ref[...])
            gr = gr_ref[o, pl.ds(r0, fc), :]
            gi = gi_ref[o, pl.ds(r0, fc), :]
            a_ref[pl.ds(r0, fc), :] = (ur * gr + us * gi).astype(BF16)
            b_ref[pl.ds(r0, fc), :] = (us * gr - ur * gi).astype(BF16)
            return carry
        lax.fori_loop(0, n, body, 0)

    def inverse(gate_ref, dst_ref):
        def body(c, carry):
            r0 = pl.multiple_of(c * fc, fc)
            lc = (_dot(c_ref[pl.ds(r0, fc), :], a_ref[...])
                  + _dot(s_ref[pl.ds(r0, fc), :], b_ref[...]))
            dst_ref[pl.ds(r0, fc), :] = (gate_ref[pl.ds(r0, fc), :] * lc).astype(BF16)
            return carry
        lax.fori_loop(0, n, body, 0)

    forward(0)
    inverse(x1_ref, ub_ref)
    forward(1)
    inverse(x2_ref, y_ref)


def hyena_mixer(zh, conv_w, conv_b, ctab, stab, gr, gi, l, n_seq, row0, cb):
    width = gr.shape[-1]
    n_orders = gr.shape[0]
    ncb = width // cb
    sb0 = row0 // l
    conv_b2 = conv_b.reshape(1, -1)
    zspec = lambda part: pl.BlockSpec((l, cb), lambda c, s: (sb0 + s, part * ncb + c))
    wspec = lambda part: pl.BlockSpec((3, cb), lambda c, s: (0, part * ncb + c))
    bspec = lambda part: pl.BlockSpec((1, cb), lambda c, s: (0, part * ncb + c))
    once = pl.Buffered(1)
    tspec = pl.BlockSpec((l, l), lambda c, s: (0, 0), pipeline_mode=once)
    gspec = pl.BlockSpec((n_orders, l, cb), lambda c, s: (0, 0, c), pipeline_mode=once)
    return pl.pallas_call(
        functools.partial(_hyena_kernel, l, min(l, HY_FREQ_CHUNK)),
        grid=(ncb, n_seq),
        in_specs=[zspec(0), zspec(1), zspec(2), wspec(0), wspec(1), wspec(2),
                  bspec(0), bspec(1), bspec(2), tspec, tspec, gspec, gspec],
        out_specs=pl.BlockSpec((l, cb), lambda c, s: (s, c)),
        out_shape=jax.ShapeDtypeStruct((n_seq * l, width), BF16),
        scratch_shapes=[pltpu.VMEM((l, cb), BF16), pltpu.VMEM((l, cb), BF16),
                        pltpu.VMEM((l, cb), BF16), pltpu.VMEM((l, cb), F32),
                        pltpu.VMEM((l, cb), F32)],
        compiler_params=_cparams(("arbitrary", "arbitrary"), VMEM_LIMIT),
        name="hyena_mixer",
    )(zh, zh, zh, conv_w, conv_w, conv_w, conv_b2, conv_b2, conv_b2, ctab, stab, gr, gi)


def _gla_kernel(l, qk, wv, dk, dv,
                zg_ref, la_ref, s0_ref, g_ref, y_ref, sfin_ref,
                of_ref, ob_ref, qe_ref, ks_ref, dec_ref, st_ref):
    ch = GLA_CHUNK
    n = l // ch
    heads = qk // dk
    grp = min(GLA_GROUP, n)
    rg = grp * ch
    r_i = lax.broadcasted_iota(jnp.int32, (rg, rg), 0)
    c_i = lax.broadcasted_iota(jnp.int32, (rg, rg), 1)
    same = (r_i // ch) == (c_i // ch)
    low = same & (r_i >= c_i)
    upp = same & (r_i <= c_i)
    masks = (low, upp)
    cum2 = tuple(jnp.concatenate([jnp.concatenate([m.astype(BF16)] * 2, axis=1),
                                  jnp.concatenate([same.astype(BF16)] * 2, axis=1)], axis=0)
                 for m in masks)
    lane_head = lax.broadcasted_iota(jnp.int32, (1, qk), 1) // dk
    scale = dk ** -0.5
    o_refs = (of_ref, ob_ref)

    def local(g, d):
        r0 = pl.multiple_of(g * rg, rg)
        q = zg_ref[pl.ds(r0, rg), 0:qk].astype(F32) * scale
        k = zg_ref[pl.ds(r0, rg), qk:2 * qk].astype(F32)
        v = zg_ref[pl.ds(r0, rg), 2 * qk:2 * qk + wv]
        la = la_ref[pl.ds(r0, rg), d * qk:(d + 1) * qk]
        lh, ll = _split(la)
        bt = _dot(cum2[d], jnp.concatenate([lh, ll], axis=0))
        b, tot = bt[:rg], bt[rg:]
        qe = (q * jnp.exp(b)).astype(BF16)
        ke = (k * jnp.exp(-b)).astype(BF16)
        qe_ref[d, pl.ds(r0, rg), :] = qe
        ks_ref[d, pl.ds(r0, rg), :] = (k * jnp.exp(tot - b)).astype(BF16)
        dec = jnp.exp(tot)
        for c in range(grp):
            dec_ref[d, pl.ds(pl.multiple_of((g * grp + c) * 8, 8), 8), :] = dec[c * ch:c * ch + 8]
        parts = []
        for h in range(heads):
            qh = jnp.where(lane_head == h, qe, jnp.zeros_like(qe))
            att = lax.dot_general(qh, ke, (((1,), (1,)), ((), ())), preferred_element_type=F32)
            att = jnp.where(masks[d], att, 0.0).astype(BF16)
            parts.append(_dot(att, v[:, h * dv:(h + 1) * dv]))
        o_refs[d][pl.ds(r0, rg), :] = jnp.concatenate(parts, axis=1)

    def local_body(g, carry):
        local(g, 0)
        local(g, 1)
        return carry

    lax.fori_loop(0, n // grp, local_body, 0)

    bd = (lax.broadcasted_iota(jnp.int32, (wv, qk), 0) // dv
          == lax.broadcasted_iota(jnp.int32, (wv, qk), 1) // dk).astype(F32)
    st_ref[...] = s0_ref[0]

    def carry_state(c, d):
        r0 = pl.multiple_of(c * ch, ch)
        st = st_ref[d]
        o = lax.dot_general(qe_ref[d, pl.ds(r0, ch), :], st.astype(BF16), (((1,), (1,)), ((), ())),
                            preferred_element_type=F32)
        o_refs[d][pl.ds(r0, ch), :] += o
        v = zg_ref[pl.ds(r0, ch), 2 * qk:2 * qk + wv]
        upd = lax.dot_general(v, ks_ref[d, pl.ds(r0, ch), :], (((0,), (0,)), ((), ())),
                              preferred_element_type=F32)
        dec = dec_ref[d, pl.ds(pl.multiple_of(c * 8, 8), 1), :]
        st_ref[d] = st * dec + upd * bd

    su = min(GLA_STATE_UNROLL, n)

    def state_body(i, carry):
        for u in range(su):
            carry_state(i * su + u, 0)
            carry_state(n - 1 - (i * su + u), 1)
        return carry

    lax.fori_loop(0, n // su, state_body, 0)
    sfin_ref[0] = st_ref[...]

    def finish(i, carry):
        r0 = pl.multiple_of(i * ch, ch)
        o = of_ref[pl.ds(r0, ch), :] + ob_ref[pl.ds(r0, ch), :]
        r = zg_ref[pl.ds(r0, ch), 2 * qk + wv:2 * qk + 2 * wv].astype(F32)
        parts = [_rms(o[:, h * dv:(h + 1) * dv], g_ref[...]) for h in range(heads)]
        y_ref[pl.ds(r0, ch), :] = (jnp.concatenate(parts, axis=1) * _silu(r)).astype(BF16)
        return carry

    lax.fori_loop(0, n, finish, 0)


def gla_mixer(zg, la, s0, norm_g, l, n_seq, row0):
    qk = la.shape[1] // 2
    wv = (zg.shape[1] - 2 * qk) // 2
    dk, dv = qk // GLA_HEADS, wv // GLA_HEADS
    sb0 = row0 // l
    return pl.pallas_call(
        functools.partial(_gla_kernel, l, qk, wv, dk, dv),
        grid=(n_seq,),
        in_specs=[pl.BlockSpec((l, zg.shape[1]), lambda s: (sb0 + s, 0)),
                  pl.BlockSpec((l, la.shape[1]), lambda s: (sb0 + s, 0)),
                  pl.BlockSpec((1, 2, wv, qk), lambda s: (s, 0, 0, 0)),
                  pl.BlockSpec((1, dv), lambda s: (0, 0))],
        out_specs=[pl.BlockSpec((l, wv), lambda s: (s, 0)),
                   pl.BlockSpec((1, 2, wv, qk), lambda s: (s, 0, 0, 0))],
        out_shape=[jax.ShapeDtypeStruct((n_seq * l, wv), BF16),
                   jax.ShapeDtypeStruct((n_seq, 2, wv, qk), F32)],
        scratch_shapes=[pltpu.VMEM((l, wv), F32), pltpu.VMEM((l, wv), F32),
                        pltpu.VMEM((2, l, qk), BF16), pltpu.VMEM((2, l, qk), BF16),
                        pltpu.VMEM((2, l // GLA_CHUNK * 8, qk), F32),
                        pltpu.VMEM((2, wv, qk), F32)],
        compiler_params=_cparams(("arbitrary",), VMEM_LIMIT),
        name="gla_mixer",
    )(zg, la, s0, norm_g.reshape(1, dv))


def _outproj_kernel(yh_ref, yg_ref, wh_ref, wg_ref, x_ref, g1_ref, g_ref, sh_ref, sc_ref,
                    xo_ref, u_ref):
    delta = _dot(yh_ref[...], wh_ref[...]) + _dot(yg_ref[...], wg_ref[...])
    x = x_ref[...] + g1_ref[0] * delta
    xo_ref[...] = x
    u_ref[...] = _rms(x, g_ref[...]) * (1.0 + sc_ref[0]) + sh_ref[0]


def out_proj(y_hy, y_gla, w_hy, w_gla, xs, mod, norm_g, tm, seq, n_lat, b, n_tok):
    d = xs.shape[1]
    nlt = n_lat // tm
    row = functools.partial(_mod_row, tm=tm, seq=seq, n_lat_tiles=nlt, b=b)
    full = lambda a: pl.BlockSpec(a.shape, lambda i: (0,) * a.ndim)
    modspec = lambda col: pl.BlockSpec((1, 1, d), lambda i: (row(i), 0, col))
    tile = lambda w: pl.BlockSpec((tm, w), lambda i: (i, 0))
    return pl.pallas_call(
        _outproj_kernel,
        grid=(n_tok // tm,),
        in_specs=[tile(y_hy.shape[1]), tile(y_gla.shape[1]), full(w_hy), full(w_gla), tile(d),
                  modspec(2), full(norm_g), modspec(3), modspec(4)],
        out_specs=[tile(d), tile(d)],
        out_shape=[jax.ShapeDtypeStruct((n_tok, d), F32)] * 2,
        compiler_params=_cparams(("arbitrary",), VMEM_LIMIT),
        name="out_proj",
    )(y_hy, y_gla, w_hy, w_gla, xs, mod, norm_g, mod, mod)


def _router_kernel(n_exp, tm, u_ref, wh_ref, wl_ref, b_ref, tri_ref,
                   idx_ref, gate_ref, rank_ref, cnt_ref):
    i = pl.program_id(0)

    @pl.when(i == 0)
    def _():
        cnt_ref[...] = jnp.zeros_like(cnt_ref)

    uh, ul = _split(u_ref[...])
    nt = (((1,), (1,)), ((), ()))
    logits = (lax.dot_general(wh_ref[...], uh, nt, preferred_element_type=F32)
              + lax.dot_general(wh_ref[...], ul, nt, preferred_element_type=F32)
              + lax.dot_general(wl_ref[...], uh, nt, preferred_element_type=F32))
    s = 1.0 / (1.0 + jnp.exp(-logits))
    sel = s + b_ref[...]
    gsz = n_exp // N_GROUPS
    neg = -jnp.inf
    rows_g = lax.broadcasted_iota(jnp.int32, (gsz, tm), 0)
    blocks, gscore = [], []
    for g in range(N_GROUPS):
        blk = sel[g * gsz:(g + 1) * gsz, :]
        m1 = jnp.max(blk, axis=0, keepdims=True)
        i1 = jnp.min(jnp.where(blk == m1, rows_g, gsz), axis=0, keepdims=True)
        m2 = jnp.max(jnp.where(rows_g == i1, neg, blk), axis=0, keepdims=True)
        blocks.append(blk)
        gscore.append(m1 + m2)
    masked = []
    for g in range(N_GROUPS):
        beaten = jnp.zeros((1, tm), jnp.int32)
        for h in range(N_GROUPS):
            if h == g:
                continue
            wins = (gscore[h] > gscore[g]) | ((gscore[h] == gscore[g]) & (h < g))
            beaten = beaten + wins.astype(jnp.int32)
        masked.append(jnp.where(beaten < TOPK_GROUPS, blocks[g], neg))
    sel = jnp.concatenate(masked, axis=0)

    rows = lax.broadcasted_iota(jnp.int32, (n_exp, tm), 0)
    chosen = jnp.zeros((n_exp, tm), F32)
    idxs, gates = [], []
    for _ in range(TOP_K):
        m = jnp.max(sel, axis=0, keepdims=True)
        ik = jnp.min(jnp.where(sel == m, rows, n_exp), axis=0, keepdims=True)
        hit = rows == ik
        gates.append(jnp.sum(jnp.where(hit, s, 0.0), axis=0, keepdims=True))
        sel = jnp.where(hit, neg, sel)
        chosen = jnp.where(hit, 1.0, chosen)
        idxs.append(ik)
    gsum = gates[0]
    for g in gates[1:]:
        gsum = gsum + g
    inv = ROUTED_SCALE / gsum
    before = _dot(chosen.astype(BF16), tri_ref[...]) + cnt_ref[...]
    for k in range(TOP_K):
        idx_ref[k:k + 1, :] = idxs[k]
        gate_ref[k:k + 1, :] = gates[k] * inv
        rk = jnp.sum(jnp.where(rows == idxs[k], before, 0.0), axis=0, keepdims=True)
        rank_ref[k:k + 1, :] = rk.astype(jnp.int32)
    cnt_ref[...] += jnp.sum(chosen, axis=1, keepdims=True)


def moe_router(u, wr_hi, wr_lo, b_r, n_tok, tm):
    n_exp, d = wr_hi.shape
    tri = (lax.broadcasted_iota(jnp.int32, (tm, tm), 0)
           < lax.broadcasted_iota(jnp.int32, (tm, tm), 1)).astype(BF16)
    full = lambda a: pl.BlockSpec(a.shape, lambda i: (0,) * a.ndim)
    tok = pl.BlockSpec((TOP_K, tm), lambda i: (0, i))
    return pl.pallas_call(
        functools.partial(_router_kernel, n_exp, tm),
        grid=(n_tok // tm,),
        in_specs=[pl.BlockSpec((tm, d), lambda i: (i, 0)), full(wr_hi), full(wr_lo),
                  pl.BlockSpec((n_exp, 1), lambda i: (0, 0)), full(tri)],
        out_specs=[tok, tok, tok, pl.BlockSpec((n_exp, 1), lambda i: (0, 0))],
        out_shape=[jax.ShapeDtypeStruct((TOP_K, n_tok), jnp.int32),
                   jax.ShapeDtypeStruct((TOP_K, n_tok), F32),
                   jax.ShapeDtypeStruct((TOP_K, n_tok), jnp.int32),
                   jax.ShapeDtypeStruct((n_exp, 1), F32)],
        compiler_params=_cparams(("arbitrary",), VMEM_LIMIT),
        name="moe_router",
    )(u, wr_hi, wr_lo, b_r.reshape(n_exp, 1), tri)


def _row_copy(src, src_row, dst, dst_row, sem):
    return pltpu.make_async_copy(src.at[pl.ds(src_row, 1)], dst.at[pl.ds(dst_row, 1)], sem)


def _pack_bf16_pairs(x):
    w = x.shape[1] // 2
    lo = pltpu.bitcast(x[:, :w].astype(BF16).astype(F32), jnp.uint32) >> 16
    hi = pltpu.bitcast(x[:, w:].astype(BF16).astype(F32), jnp.uint32)
    return hi | lo


def _unpack_bf16_pairs(p):
    lo = pltpu.bitcast(p << 16, F32)
    hi = pltpu.bitcast(p & jnp.uint32(0xFFFF0000), F32)
    return jnp.concatenate([lo, hi], axis=1).astype(BF16)


def _dest_kernel(n_exp, tm, idx_ref, rank_ref, ps_ref, dest_ref):
    rows = lax.broadcasted_iota(jnp.int32, (n_exp, tm), 0)
    ps = ps_ref[...]
    for k in range(TOP_K):
        base = jnp.sum(jnp.where(rows == idx_ref[k:k + 1, :], ps, 0), axis=0, keepdims=True)
        dest_ref[k:k + 1, :] = base + rank_ref[k:k + 1, :]


def moe_dest(idx, rank, pad_start, tm):
    n_tok = idx.shape[1]
    n_exp = pad_start.shape[0]
    tok = pl.BlockSpec((TOP_K, tm), lambda i: (0, i))
    return pl.pallas_call(
        functools.partial(_dest_kernel, n_exp, tm),
        grid=(n_tok // tm,),
        in_specs=[tok, tok, pl.BlockSpec((n_exp, 1), lambda i: (0, 0))],
        out_specs=tok,
        out_shape=jax.ShapeDtypeStruct((TOP_K, n_tok), jnp.int32),
        compiler_params=_cparams(("arbitrary",)),
        name="moe_dest",
    )(idx, rank, pad_start.reshape(n_exp, 1))


def _wait_rows(src, dst, sem, tm):
    for _ in range(TOP_K):
        pltpu.make_async_copy(src.at[pl.ds(0, tm)], dst.at[pl.ds(0, tm)], sem).wait()


def _dispatch_kernel(tm, n_steps, dest_ref, u_ref, xin_ref, xs_ref, pk_ref, sems):
    del xin_ref
    i = pl.program_id(0)
    slot = i % 2
    pk = pk_ref.at[slot]
    sem = sems.at[slot]

    @pl.when(i >= 2)
    def _():
        _wait_rows(pk, xs_ref, sem, tm)

    pk[...] = _pack_bf16_pairs(u_ref[...])

    def issue(t, carry):
        for k in range(TOP_K):
            _row_copy(pk, t, xs_ref, dest_ref[k, t], sem).start()
        return carry

    lax.fori_loop(0, tm, issue, 0)

    @pl.when(i == n_steps - 1)
    def _():
        if n_steps > 1:
            _wait_rows(pk_ref.at[1 - slot], xs_ref, sems.at[1 - slot], tm)
        _wait_rows(pk, xs_ref, sem, tm)


def moe_dispatch(u, dest, n_tok, n_slots, tm):
    d = u.shape[1]
    zeros = jnp.zeros((n_slots, d // 2), jnp.uint32)
    return pl.pallas_call(
        functools.partial(_dispatch_kernel, tm, n_tok // tm),
        grid=(n_tok // tm,),
        in_specs=[pl.BlockSpec((TOP_K, tm), lambda i: (0, i), memory_space=pltpu.SMEM),
                  pl.BlockSpec((tm, d), lambda i: (i, 0)),
                  pl.BlockSpec(memory_space=pl.ANY)],
        out_specs=pl.BlockSpec(memory_space=pl.ANY),
        out_shape=jax.ShapeDtypeStruct((n_slots, d // 2), jnp.uint32),
        scratch_shapes=[pltpu.VMEM((2, tm, d // 2), jnp.uint32), pltpu.SemaphoreType.DMA((2,))],
        input_output_aliases={2: 0},
        compiler_params=_cparams(("arbitrary",)),
        name="moe_dispatch",
    )(dest, u, zeros)


def _gmm_kernel(layer, be_ref, nx_ref, fl_ref, nu_ref, x_ref, w1_hbm, w3_hbm, w2_hbm, y_ref,
                w1f, w3f, w2f, w1b, w3b, w2b, sems):
    i = pl.program_id(0)
    fl = fl_ref[i]

    def fetch(e, s):
        return (pltpu.make_async_copy(w1_hbm.at[layer, e], w1f.at[s], sems.at[s]),
                pltpu.make_async_copy(w3_hbm.at[layer, e], w3f.at[s], sems.at[s]),
                pltpu.make_async_copy(w2_hbm.at[layer, e], w2f.at[s], sems.at[s]))

    @pl.when(i < nu_ref[0])
    def _():
        @pl.when((fl & 1) != 0)
        def _():
            s = (fl >> 1) & 1

            @pl.when(i == 0)
            def _():
                for c in fetch(be_ref[i], s):
                    c.start()

            for c in fetch(be_ref[i], s):
                c.wait()

            @pl.when((fl & 4) != 0)
            def _():
                for c in fetch(nx_ref[i], 1 - s):
                    c.start()

            w1b[...] = w1f[s].astype(BF16)
            w3b[...] = w3f[s].astype(BF16)
            w2b[...] = w2f[s].astype(BF16)

        x = _unpack_bf16_pairs(x_ref[...])
        h = _silu(_dot(x, w1b[...])) * _dot(x, w3b[...])
        y_ref[...] = _pack_bf16_pairs(_dot(h.astype(BF16), w2b[...]))

    @pl.when(i >= nu_ref[0])
    def _():
        y_ref[...] = jnp.zeros_like(y_ref)


def moe_experts(x_sorted, block_e, n_used, w1, w3, w2, layer):
    n_blocks = block_e.shape[0]
    _, _, d, f = w1.shape
    idx = jnp.arange(n_blocks, dtype=jnp.int32)
    first = (idx == 0) | (block_e != jnp.roll(block_e, 1))
    seg = jnp.cumsum(first.astype(jnp.int32)) - 1
    first_at_or_after = lax.cummin(jnp.where(first, idx, n_blocks)[::-1])[::-1]
    nxt_idx = jnp.concatenate([first_at_or_after[1:], jnp.full((1,), n_blocks, jnp.int32)])
    has_next = nxt_idx < n_used[0]
    nxt_e = block_e[jnp.minimum(nxt_idx, n_blocks - 1)]
    flags = (first.astype(jnp.int32) | ((seg & 1) << 1) | (has_next.astype(jnp.int32) << 2))
    grid_spec = pltpu.PrefetchScalarGridSpec(
        num_scalar_prefetch=4,
        grid=(n_blocks,),
        in_specs=[pl.BlockSpec((MOE_ROWS, d // 2), lambda i, *_: (i, 0)),
                  pl.BlockSpec(memory_space=pl.ANY),
                  pl.BlockSpec(memory_space=pl.ANY),
                  pl.BlockSpec(memory_space=pl.ANY)],
        out_specs=pl.BlockSpec((MOE_ROWS, d // 2), lambda i, *_: (i, 0)),
        scratch_shapes=[pltpu.VMEM((2, d, f), F32), pltpu.VMEM((2, d, f), F32),
                        pltpu.VMEM((2, f, d), F32),
                        pltpu.VMEM((d, f), BF16), pltpu.VMEM((d, f), BF16),
                        pltpu.VMEM((f, d), BF16), pltpu.SemaphoreType.DMA((2,))])
    return pl.pallas_call(
        functools.partial(_gmm_kernel, layer),
        grid_spec=grid_spec,
        out_shape=jax.ShapeDtypeStruct((x_sorted.shape[0], d // 2), jnp.uint32),
        compiler_params=_cparams(("arbitrary",), VMEM_LIMIT),
        name="moe_experts",
    )(block_e, nxt_e, flags, n_used, x_sorted, w1, w3, w2)


def _combine_kernel(tm, n_steps, final, dest_ref, nxt_ref, y_ref, gate_ref, u_ref, ws1_ref, ws3_ref,
                    ws2_ref, x_ref, g2_ref, fg_ref, o_ref, buf_ref, sems):
    i = pl.program_id(0)
    slot = i % 2

    def gather(d_ref, s):
        def issue(t, carry):
            for k in range(TOP_K):
                _row_copy(y_ref, d_ref[k, t], buf_ref.at[s, k], t, sems.at[s]).start()
            return carry
        lax.fori_loop(0, tm, issue, 0)

    @pl.when(i == 0)
    def _():
        gather(dest_ref, slot)

    @pl.when(i + 1 < n_steps)
    def _():
        gather(nxt_ref, 1 - slot)

    ub = u_ref[...].astype(BF16)
    hs = _silu(_dot(ub, ws1_ref[...])) * _dot(ub, ws3_ref[...])
    shared = _dot(hs.astype(BF16), ws2_ref[...])

    _wait_rows(y_ref, buf_ref.at[slot, 0], sems.at[slot], tm)

    lo = hi = None
    for k in range(TOP_K):
        p = buf_ref[slot, k]
        g = gate_ref[:, k:k + 1]
        lo_k = pltpu.bitcast(p << 16, F32) * g
        hi_k = pltpu.bitcast(p & jnp.uint32(0xFFFF0000), F32) * g
        lo = lo_k if lo is None else lo + lo_k
        hi = hi_k if hi is None else hi + hi_k
    routed = jnp.concatenate([lo, hi], axis=1)
    x = x_ref[...] + g2_ref[0] * (routed + shared)
    o_ref[...] = _rms(x, fg_ref[...]) if final else x


def moe_combine(y_sorted, dest, gate_rep, u, ws1, ws3, ws2, xs, mod, final_g, final,
                tm, seq, n_lat, b, n_tok):
    d = u.shape[1]
    nlt = n_lat // tm
    row = functools.partial(_mod_row, tm=tm, seq=seq, n_lat_tiles=nlt, b=b)
    full = lambda a: pl.BlockSpec(a.shape, lambda i: (0,) * a.ndim)
    tile = pl.BlockSpec((tm, d), lambda i: (i, 0))
    n_steps = n_tok // tm
    return pl.pallas_call(
        functools.partial(_combine_kernel, tm, n_steps, final),
        grid=(n_steps,),
        in_specs=[pl.BlockSpec((TOP_K, tm), lambda i: (0, i), memory_space=pltpu.SMEM),
                  pl.BlockSpec((TOP_K, tm), lambda i: (0, jnp.minimum(i + 1, n_steps - 1)),
                               memory_space=pltpu.SMEM),
                  pl.BlockSpec(memory_space=pl.ANY),
                  pl.BlockSpec((tm, TOP_K), lambda i: (i, 0)),
                  tile, full(ws1), full(ws3), full(ws2), tile,
                  pl.BlockSpec((1, 1, d), lambda i: (row(i), 0, 5)),
                  full(final_g)],
        out_specs=tile,
        out_shape=jax.ShapeDtypeStruct((n_tok, d), F32),
        scratch_shapes=[pltpu.VMEM((2, TOP_K, tm, d // 2), jnp.uint32),
                        pltpu.SemaphoreType.DMA((2,))],
        compiler_params=_cparams(("arbitrary",), VMEM_LIMIT),
        name="moe_combine",
    )(dest, dest, y_sorted, gate_rep, u, ws1, ws3, ws2, xs, mod, final_g)


def moe_layer(u, xs, mod, router_w, router_b, w1, w3, w2, layer, ws1, ws3, ws2, final_g, final,
              tm, seq, n_lat, b, n_tok):
    n_exp = router_w.shape[1]
    wr_hi, wr_lo = _split(router_w.T)
    idx, gate, rank, cnt = moe_router(u, wr_hi, wr_lo, router_b, n_tok, tm)
    counts = cnt[:, 0].astype(jnp.int32)
    padded = (counts + MOE_ROWS - 1) // MOE_ROWS * MOE_ROWS
    pad_end = jnp.cumsum(padded)
    dest = moe_dest(idx, rank, pad_end - padded, tm)
    n_blocks = -(-(n_tok * TOP_K + n_exp * (MOE_ROWS - 1)) // MOE_ROWS)
    block_start = jnp.arange(n_blocks, dtype=jnp.int32) * MOE_ROWS
    block_e = jnp.minimum(jnp.searchsorted(pad_end, block_start, side='right'),
                          n_exp - 1).astype(jnp.int32)
    n_used = (pad_end[-1:] // MOE_ROWS).astype(jnp.int32)
    x_sorted = moe_dispatch(u, dest, n_tok, n_blocks * MOE_ROWS, tm)
    y_sorted = moe_experts(x_sorted, block_e, n_used, w1, w3, w2, layer)
    return moe_combine(y_sorted, dest, gate.T, u, ws1.astype(BF16), ws3.astype(BF16),
                       ws2.astype(BF16), xs, mod, final_g.reshape(1, -1), final,
                       tm, seq, n_lat, b, n_tok)


def _grid_pos_emb(rows, d):
    r, col = jnp.meshgrid(jnp.arange(rows, dtype=F32), jnp.arange(GRID_W, dtype=F32), indexing='ij')
    quarter = d // 4
    omega = 1.0 / (10000.0 ** (jnp.arange(quarter, dtype=F32) / quarter))

    def emb(p):
        a = p.reshape(-1)[:, None] * omega[None, :]
        return jnp.concatenate([jnp.sin(a), jnp.cos(a)], axis=-1)

    return jnp.concatenate([emb(r), emb(col)], axis=-1)


def kernel(x, c, ctx, c_ctx, ada_w, ada_b, norm1_g, norm2_g, w_in, w_out, hy_conv_w, hy_conv_b,
           hy_f_w1, hy_f_b1, hy_f_w2, hy_f_b2, hy_f_w3, hy_skip, gla_wa2, gla_ba2, gla_norm_g,
           router_w, router_b, exp_w1, exp_w3, exp_w2, sh_w1, sh_w3, sh_w2, final_g):
    b, l, d = x.shape
    lc = ctx.shape[1]
    depth = ada_w.shape[0]
    n_lat, n_ctx = b * l, b * lc
    n_all = n_lat + n_ctx
    hy_width = hy_skip.shape[-1]
    hy_cols = hy_conv_w.shape[-1]
    qk = gla_wa2.shape[-1]
    wv = d - hy_width
    g_cols = 2 * qk + 2 * wv
    tm = math.gcd(math.gcd(l, lc), 256)
    tm_proj = math.gcd(math.gcd(l, n_ctx), 512)
    cb = min(hy_width, 256)

    cond = jnp.concatenate([c, c_ctx[None, :]], axis=0)
    n_rows = -(-(b + 1) // 8) * 8
    cond = jnp.concatenate([cond, jnp.zeros((n_rows - b - 1, d), F32)], axis=0)
    mods = ada_table(cond, ada_w, ada_b)

    xs = embed_tokens(x.reshape(n_lat, d), _grid_pos_emb(l // GRID_W, d), ctx.reshape(n_ctx, d), tm)

    tabs = {}
    for length in (l, lc):
        if length not in tabs:
            tabs[length] = _trig_tables(length, True) + _trig_tables(length, False)
    zero_state = jnp.zeros((b, 2, wv, qk), F32)

    for layer in range(depth):
        last = layer == depth - 1
        mod = mods[layer][:, None, :]
        g1 = norm1_g[layer].reshape(1, d)
        g2 = norm2_g[layer].reshape(1, d)
        w_l = w_in[layer].astype(BF16)
        wa2 = gla_wa2[layer]
        rank = wa2.shape[1]
        wa2_bd = jnp.zeros((2 * rank, 2 * qk), F32)
        wa2_bd = wa2_bd.at[:rank, :qk].set(wa2[0]).at[rank:, qk:].set(wa2[1])
        ba2_cat = gla_ba2[layer].reshape(1, 2 * qk)
        zh, zg, la = in_proj(xs, g1, mod, w_l[:, :hy_cols], w_l[:, hy_cols:hy_cols + g_cols],
                             w_l[:, hy_cols + g_cols:], wa2_bd, ba2_cat, tm_proj, l, n_lat, b)

        filt = (hy_f_w1[layer], hy_f_b1[layer], hy_f_w2[layer], hy_f_b2[layer], hy_f_w3[layer],
                hy_skip[layer])
        yc_gla, s_ctx = gla_mixer(zg, la, zero_state, gla_norm_g[layer], lc, b, n_lat)
        y_gla, _ = gla_mixer(zg, la, s_ctx, gla_norm_g[layer], l, b, 0)
        ct, st, c0, s0 = tabs[l]
        gr, gi = hyena_spectra(l, *filt, c0, s0)
        y_hy = hyena_mixer(zh, hy_conv_w[layer], hy_conv_b[layer], ct, st, gr, gi, l, b, 0, cb)
        if last:
            n_tok = n_lat
        else:
            n_tok = n_all
            ct, st, c0, s0 = tabs[lc]
            gr, gi = hyena_spectra(lc, *filt, c0, s0)
            yc_hy = hyena_mixer(zh, hy_conv_w[layer], hy_conv_b[layer], ct, st, gr, gi,
                                lc, b, n_lat, cb)
            y_hy = jnp.concatenate([y_hy, yc_hy], axis=0)
            y_gla = jnp.concatenate([y_gla, yc_gla], axis=0)
        w_o = w_out[layer].astype(BF16)
        xs_mid, u = out_proj(y_hy, y_gla, w_o[:hy_width], w_o[hy_width:], xs, mod, g2,
                             tm, l, n_lat, b, n_tok)
        xs = moe_layer(u, xs_mid, mod, router_w[layer], router_b[layer], exp_w1, exp_w3, exp_w2,
                       layer, sh_w1[layer], sh_w3[layer], sh_w2[layer],
                       final_g, last, tm, l, n_lat, b, n_tok)
    return xs[:n_lat].reshape(b, l, d)
```

```python
import functools
import math

import numpy as np
import jax
import jax.numpy as jnp
from jax import lax
from jax.experimental import pallas as pl
from jax.experimental.pallas import tpu as pltpu

GRID_W = 64
EPS = 1e-6

HY_ORDER = 2
HY_EMB = 33
HY_EMB_PAD = 40
HY_FREQ = 1.0
HY_TARGET = 1e-2
HY_FAST_PCT = 0.3
HY_SLOW_PCT = 1.5
HY_MIN_DECAY = math.log(1.0 / HY_TARGET) / HY_SLOW_PCT
HY_MAX_DECAY = math.log(1.0 / HY_TARGET) / HY_FAST_PCT

GLA_HEADS = 4
GLA_RANK = 16
GLA_TAU = 16.0
GLA_CHUNK = 64
GLA_GROUP = 4
GLA_STATE_UNROLL = 4

TOP_K = 8
N_GROUPS = 8
TOPK_GROUPS = 4
ROUTED_SCALE = 2.5

HY_FREQ_CHUNK = 1024
MOE_ROWS = 256
VMEM_LIMIT = 56 * 1024 * 1024

F32 = jnp.float32
BF16 = jnp.bfloat16


def _cparams(sem, vmem=None):
    return pltpu.CompilerParams(dimension_semantics=sem, vmem_limit_bytes=vmem)


def _split(a):
    hi = a.astype(BF16)
    lo = (a - hi.astype(F32)).astype(BF16)
    return hi, lo


def _dot(a, b):
    return jnp.dot(a, b, preferred_element_type=F32)


def _dot3(a, b):
    ah, al = _split(a)
    bh, bl = _split(b)
    return _dot(ah, bh) + _dot(ah, bl) + _dot(al, bh)


def _silu(x):
    return x * (1.0 / (1.0 + jnp.exp(-x)))


def _rms(x, g):
    return x * lax.rsqrt(jnp.mean(x * x, axis=-1, keepdims=True) + EPS) * g


def _ada_kernel(c_ref, w_ref, b_ref, o_ref):
    o_ref[0] = _dot3(_silu(c_ref[...]), w_ref[0]) + b_ref[0]


def ada_table(cond_rows, ada_w, ada_b):
    depth, d, six_d = ada_w.shape
    r = cond_rows.shape[0]
    tn = 1024
    return pl.pallas_call(
        _ada_kernel,
        grid=(depth, six_d // tn),
        in_specs=[pl.BlockSpec((r, d), lambda l, j: (0, 0)),
                  pl.BlockSpec((1, d, tn), lambda l, j: (l, 0, j)),
                  pl.BlockSpec((1, 1, tn), lambda l, j: (l, 0, j))],
        out_specs=pl.BlockSpec((1, r, tn), lambda l, j: (l, 0, j)),
        out_shape=jax.ShapeDtypeStruct((depth, r, six_d), F32),
        compiler_params=_cparams(("arbitrary", "arbitrary")),
        name="ada_table",
    )(cond_rows, ada_w, ada_b.reshape(depth, 1, six_d))


def _embed_kernel(n_lat_tiles, x_ref, p_ref, c_ref, o_ref):
    i = pl.program_id(0)

    @pl.when(i < n_lat_tiles)
    def _():
        o_ref[...] = x_ref[...] + p_ref[...]

    @pl.when(i >= n_lat_tiles)
    def _():
        o_ref[...] = c_ref[...]


def embed_tokens(x2, pos, ctx2, tm):
    nl, d = x2.shape
    nc = ctx2.shape[0]
    l = pos.shape[0]
    nlt, nct, lt = nl // tm, nc // tm, l // tm
    return pl.pallas_call(
        functools.partial(_embed_kernel, nlt),
        grid=(nlt + nct,),
        in_specs=[pl.BlockSpec((tm, d), lambda i: (jnp.minimum(i, nlt - 1), 0)),
                  pl.BlockSpec((tm, d), lambda i: (i % lt, 0)),
                  pl.BlockSpec((tm, d), lambda i: (jnp.maximum(i - nlt, 0), 0))],
        out_specs=pl.BlockSpec((tm, d), lambda i: (i, 0)),
        out_shape=jax.ShapeDtypeStruct((nl + nc, d), F32),
        compiler_params=_cparams(("arbitrary",)),
        name="embed_tokens",
    )(x2, pos, ctx2)


def _mod_row(i, tm, seq, n_lat_tiles, b):
    return jnp.where(i < n_lat_tiles, (i * tm) // seq, b)


def _inproj_kernel(x_ref, g_ref, sh_ref, sc_ref, wh_ref, wg_ref, wa_ref, wa2_ref, ba2_ref,
                   zh_ref, zg_ref, la_ref):
    x = x_ref[...]
    h = _rms(x, g_ref[...]) * (1.0 + sc_ref[0]) + sh_ref[0]
    hb = h.astype(BF16)
    zh_ref[...] = _dot(hb, wh_ref[...]).astype(BF16)
    zg_ref[...] = _dot(hb, wg_ref[...]).astype(BF16)
    za = _dot(hb, wa_ref[...])
    t = _dot3(za, wa2_ref[...]) + ba2_ref[...]
    la_ref[...] = (jnp.minimum(t, 0.0) - jnp.log(1.0 + jnp.exp(-jnp.abs(t)))) * (1.0 / GLA_TAU)


def in_proj(xs, norm_g, mod, w_hy, w_g, w_a, wa2_bd, ba2_cat, tm, seq, n_lat, b):
    n, d = xs.shape
    nlt = n_lat // tm
    row = functools.partial(_mod_row, tm=tm, seq=seq, n_lat_tiles=nlt, b=b)
    hy_cols, g_cols, la_cols = w_hy.shape[1], w_g.shape[1], wa2_bd.shape[1]
    full = lambda a: pl.BlockSpec(a.shape, lambda i: (0,) * a.ndim)
    return pl.pallas_call(
        _inproj_kernel,
        grid=(n // tm,),
        in_specs=[pl.BlockSpec((tm, d), lambda i: (i, 0)),
                  full(norm_g),
                  pl.BlockSpec((1, 1, d), lambda i: (row(i), 0, 0)),
                  pl.BlockSpec((1, 1, d), lambda i: (row(i), 0, 1)),
                  full(w_hy), full(w_g), full(w_a), full(wa2_bd), full(ba2_cat)],
        out_specs=[pl.BlockSpec((tm, hy_cols), lambda i: (i, 0)),
                   pl.BlockSpec((tm, g_cols), lambda i: (i, 0)),
                   pl.BlockSpec((tm, la_cols), lambda i: (i, 0))],
        out_shape=[jax.ShapeDtypeStruct((n, hy_cols), BF16),
                   jax.ShapeDtypeStruct((n, g_cols), BF16),
                   jax.ShapeDtypeStruct((n, la_cols), F32)],
        compiler_params=_cparams(("arbitrary",), VMEM_LIMIT),
        name="in_proj",
    )(xs, norm_g, mod, mod, w_hy, w_g, w_a, wa2_bd, ba2_cat)


def _trig_tables(l, half_shift):
    k = lax.broadcasted_iota(jnp.int32, (l, l), 0)
    n = lax.broadcasted_iota(jnp.int32, (l, l), 1)
    m = ((2 * k + 1) * (2 * n + (1 if half_shift else 0))) % (8 * l)
    ang = m.astype(F32) * (2.0 * math.pi / (8 * l))
    return jnp.cos(ang).astype(BF16), jnp.sin(ang).astype(BF16)


def _filter_kernel(n_orders, width, l, tk,
                   pos_ref, t_ref, dl_ref, w1_ref, b1_ref, w2_ref, b2_ref, w3_ref, skip_ref,
                   c0_ref, s0_ref, gr_ref, gi_ref, hs_ref, hd_ref):
    j = pl.program_id(0)

    @pl.when(j == 0)
    def _():
        hid = jnp.sin(HY_FREQ * (_dot3(pos_ref[...], w1_ref[...]) + b1_ref[...]))
        hid = jnp.sin(HY_FREQ * (_dot3(hid, w2_ref[...]) + b2_ref[...]))
        h = _dot3(hid, w3_ref[...])
        decay = jnp.exp(-t_ref[...] * dl_ref[...])
        for o in range(n_orders):
            hf = h[:, (2 * o) * width:(2 * o + 1) * width] * decay
            hb = h[:, (2 * o + 1) * width:(2 * o + 2) * width] * decay
            hs_ref[:, o * width:(o + 1) * width] = (hf + hb).astype(BF16)
            hd_ref[:, o * width:(o + 1) * width] = (hb - hf).astype(BF16)

    scale = 1.0 / l
    gr = _dot(c0_ref[...], hs_ref[...])
    gi = _dot(s0_ref[...], hd_ref[...])
    for o in range(n_orders):
        gr_ref[o] = (gr[:, o * width:(o + 1) * width] + skip_ref[o]) * scale
        gi_ref[o] = gi[:, o * width:(o + 1) * width] * scale


def hyena_spectra(l, w1, b1, w2, b2, w3, skip, c0, s0):
    width = skip.shape[-1]
    n_orders = skip.shape[0]
    t = jnp.linspace(0.0, 1.0, l, dtype=F32)[:, None]
    bands = (HY_EMB - 1) // 2
    w = 2.0 * math.pi * jnp.arange(l, dtype=F32)[:, None] / l
    f = jnp.linspace(1e-4, bands - 1, bands, dtype=F32)[None, :]
    pos = jnp.concatenate([t, jnp.cos(f * w), -jnp.sin(f * w),
                           jnp.zeros((l, HY_EMB_PAD - HY_EMB), F32)], axis=-1)
    w1p = jnp.concatenate([w1, jnp.zeros((HY_EMB_PAD - HY_EMB, w1.shape[1]), F32)], axis=0)
    deltas = jnp.linspace(HY_MIN_DECAY, HY_MAX_DECAY, width, dtype=F32)[None, :]
    tk = min(l, 256)
    full = lambda a: pl.BlockSpec(a.shape, lambda j: (0,) * a.ndim)
    skip3 = skip.reshape(n_orders, 1, width)
    args = (pos, t, deltas, w1p, b1.reshape(1, -1), w2, b2.reshape(1, -1), w3, skip3)
    return pl.pallas_call(
        functools.partial(_filter_kernel, n_orders, width, l, tk),
        grid=(l // tk,),
        in_specs=[full(a) for a in args] + [pl.BlockSpec((tk, l), lambda j: (j, 0)),
                                            pl.BlockSpec((tk, l), lambda j: (j, 0))],
        out_specs=[pl.BlockSpec((n_orders, tk, width), lambda j: (0, j, 0)),
                   pl.BlockSpec((n_orders, tk, width), lambda j: (0, j, 0))],
        out_shape=[jax.ShapeDtypeStruct((n_orders, l, width), F32)] * 2,
        scratch_shapes=[pltpu.VMEM((l, n_orders * width), BF16),
                        pltpu.VMEM((l, n_orders * width), BF16)],
        compiler_params=_cparams(("arbitrary",), VMEM_LIMIT),
        name="hyena_spectra",
    )(*args, c0, s0)


def _hyena_kernel(l, fc, zv_ref, z1_ref, z2_ref, wv_ref, w1_ref, w2_ref, bv_ref, b1_ref, b2_ref,
                  c_ref, s_ref, gr_ref, gi_ref, y_ref, ub_ref, a_ref, b_ref, x1_ref, x2_ref):
    rows = lax.broadcasted_iota(jnp.int32, (l, 1), 0)

    def conv3(z_ref, w_ref, bias_ref):
        z = z_ref[...].astype(F32)
        zm = jnp.where(rows == 0, 0.0, pltpu.roll(z, 1, 0))
        zp = jnp.where(rows == l - 1, 0.0, pltpu.roll(z, l - 1, 0))
        return zm * w_ref[0:1, :] + z * w_ref[1:2, :] + zp * w_ref[2:3, :] + bias_ref[...]

    ub_ref[...] = conv3(zv_ref, wv_ref, bv_ref).astype(BF16)
    x1_ref[...] = conv3(z1_ref, w1_ref, b1_ref)
    x2_ref[...] = conv3(z2_ref, w2_ref, b2_ref)
    n = l // fc

    def forward(o):
        def body(c, carry):
            r0 = pl.multiple_of(c * fc, fc)
            ur = _dot(c_ref[pl.ds(r0, fc), :], ub_ref[...])
            us = _dot(s_ref[pl.ds(r0, fc), :], ub_ref[...])
            gr = gr_ref[o, pl.ds(r0, fc), :]
            gi = gi_ref[o, pl.ds(r0, fc), :]
            a_ref[pl.ds(r0, fc), :] = (ur * gr + us * gi).astype(BF16)
            b_ref[pl.ds(r0, fc), :] = (us * gr - ur * gi).astype(BF16)
            return carry
        lax.fori_loop(0, n, body, 0)

    def inverse(gate_ref, dst_ref):
        def body(c, carry):
            r0 = pl.multiple_of(c * fc, fc)
            lc = (_dot(c_ref[pl.ds(r0, fc), :], a_ref[...])
                  + _dot(s_ref[pl.ds(r0, fc), :], b_ref[...]))
            dst_ref[pl.ds(r0, fc), :] = (gate_ref[pl.ds(r0, fc), :] * lc).astype(BF16)
            return carry
        lax.fori_loop(0, n, body, 0)

    forward(0)
    inverse(x1_ref, ub_ref)
    forward(1)
    inverse(x2_ref, y_ref)


def hyena_mixer(zh, conv_w, conv_b, ctab, stab, gr, gi, l, n_seq, row0, cb):
    width = gr.shape[-1]
    n_orders = gr.shape[0]
    ncb = width // cb
    sb0 = row0 // l
    conv_b2 = conv_b.reshape(1, -1)
    zspec = lambda part: pl.BlockSpec((l, cb), lambda c, s: (sb0 + s, part * ncb + c))
    wspec = lambda part: pl.BlockSpec((3, cb), lambda c, s: (0, part * ncb + c))
    bspec = lambda part: pl.BlockSpec((1, cb), lambda c, s: (0, part * ncb + c))
    once = pl.Buffered(1)
    tspec = pl.BlockSpec((l, l), lambda c, s: (0, 0), pipeline_mode=once)
    gspec = pl.BlockSpec((n_orders, l, cb), lambda c, s: (0, 0, c), pipeline_mode=once)
    return pl.pallas_call(
        functools.partial(_hyena_kernel, l, min(l, HY_FREQ_CHUNK)),
        grid=(ncb, n_seq),
        in_specs=[zspec(0), zspec(1), zspec(2), wspec(0), wspec(1), wspec(2),
                  bspec(0), bspec(1), bspec(2), tspec, tspec, gspec, gspec],
        out_specs=pl.BlockSpec((l, cb), lambda c, s: (s, c)),
        out_shape=jax.ShapeDtypeStruct((n_seq * l, width), BF16),
        scratch_shapes=[pltpu.VMEM((l, cb), BF16), pltpu.VMEM((l, cb), BF16),
                        pltpu.VMEM((l, cb), BF16), pltpu.VMEM((l, cb), F32),
                        pltpu.VMEM((l, cb), F32)],
        compiler_params=_cparams(("arbitrary", "arbitrary"), VMEM_LIMIT),
        name="hyena_mixer",
    )(zh, zh, zh, conv_w, conv_w, conv_w, conv_b2, conv_b2, conv_b2, ctab, stab, gr, gi)


def _gla_kernel(l, qk, wv, dk, dv,
                zg_ref, la_ref, s0_ref, g_ref, y_ref, sfin_ref,
                of_ref, ob_ref, qe_ref, ks_ref, dec_ref, st_ref):
    ch = GLA_CHUNK
    n = l // ch
    heads = qk // dk
    grp = min(GLA_GROUP, n)
    rg = grp * ch
    r_i = lax.broadcasted_iota(jnp.int32, (rg, rg), 0)
    c_i = lax.broadcasted_iota(jnp.int32, (rg, rg), 1)
    same = (r_i // ch) == (c_i // ch)
    low = same & (r_i >= c_i)
    upp = same & (r_i <= c_i)
    masks = (low, upp)
    cum2 = tuple(jnp.concatenate([jnp.concatenate([m.astype(BF16)] * 2, axis=1),
                                  jnp.concatenate([same.astype(BF16)] * 2, axis=1)], axis=0)
                 for m in masks)
    lane_head = lax.broadcasted_iota(jnp.int32, (1, qk), 1) // dk
    scale = dk ** -0.5
    o_refs = (of_ref, ob_ref)

    def local(g, d):
        r0 = pl.multiple_of(g * rg, rg)
        q = zg_ref[pl.ds(r0, rg), 0:qk].astype(F32) * scale
        k = zg_ref[pl.ds(r0, rg), qk:2 * qk].astype(F32)
        v = zg_ref[pl.ds(r0, rg), 2 * qk:2 * qk + wv]
        la = la_ref[pl.ds(r0, rg), d * qk:(d + 1) * qk]
        lh, ll = _split(la)
        bt = _dot(cum2[d], jnp.concatenate([lh, ll], axis=0))
        b, tot = bt[:rg], bt[rg:]
        qe = (q * jnp.exp(b)).astype(BF16)
        ke = (k * jnp.exp(-b)).astype(BF16)
        qe_ref[d, pl.ds(r0, rg), :] = qe
        ks_ref[d, pl.ds(r0, rg), :] = (k * jnp.exp(tot - b)).astype(BF16)
        dec = jnp.exp(tot)
        for c in range(grp):
            dec_ref[d, pl.ds(pl.multiple_of((g * grp + c) * 8, 8), 8), :] = dec[c * ch:c * ch + 8]
        parts = []
        for h in range(heads):
            qh = jnp.where(lane_head == h, qe, jnp.zeros_like(qe))
            att = lax.dot_general(qh, ke, (((1,), (1,)), ((), ())), preferred_element_type=F32)
            att = jnp.where(masks[d], att, 0.0).astype(BF16)
            parts.append(_dot(att, v[:, h * dv:(h + 1) * dv]))
        o_refs[d][pl.ds(r0, rg), :] = jnp.concatenate(parts, axis=1)

    def local_body(g, carry):
        local(g, 0)
        local(g, 1)
        return carry

    lax.fori_loop(0, n // grp, local_body, 0)

    bd = (lax.broadcasted_iota(jnp.int32, (wv, qk), 0) // dv
          == lax.broadcasted_iota(jnp.int32, (wv, qk), 1) // dk).astype(F32)
    st_ref[...] = s0_ref[0]

    def carry_state(c, d):
        r0 = pl.multiple_of(c * ch, ch)
        st = st_ref[d]
        o = lax.dot_general(qe_ref[d, pl.ds(r0, ch), :], st.astype(BF16), (((1,), (1,)), ((), ())),
                            preferred_element_type=F32)
        o_refs[d][pl.ds(r0, ch), :] += o
        v = zg_ref[pl.ds(r0, ch), 2 * qk:2 * qk + wv]
        upd = lax.dot_general(v, ks_ref[d, pl.ds(r0, ch), :], (((0,), (0,)), ((), ())),
                              preferred_element_type=F32)
        dec = dec_ref[d, pl.ds(pl.multiple_of(c * 8, 8), 1), :]
        st_ref[d] = st * dec + upd * bd

    su = min(GLA_STATE_UNROLL, n)

    def state_body(i, carry):
        for u in range(su):
            carry_state(i * su + u, 0)
            carry_state(n - 1 - (i * su + u), 1)
        return carry

    lax.fori_loop(0, n // su, state_body, 0)
    sfin_ref[0] = st_ref[...]

    def finish(i, carry):
        r0 = pl.multiple_of(i * ch, ch)
        o = of_ref[pl.ds(r0, ch), :] + ob_ref[pl.ds(r0, ch), :]
        r = zg_ref[pl.ds(r0, ch), 2 * qk + wv:2 * qk + 2 * wv].astype(F32)
        parts = [_rms(o[:, h * dv:(h + 1) * dv], g_ref[...]) for h in range(heads)]
        y_ref[pl.ds(r0, ch), :] = (jnp.concatenate(parts, axis=1) * _silu(r)).astype(BF16)
        return carry

    lax.fori_loop(0, n, finish, 0)


def gla_mixer(zg, la, s0, norm_g, l, n_seq, row0):
    qk = la.shape[1] // 2
    wv = (zg.shape[1] - 2 * qk) // 2
    dk, dv = qk // GLA_HEADS, wv // GLA_HEADS
    sb0 = row0 // l
    return pl.pallas_call(
        functools.partial(_gla_kernel, l, qk, wv, dk, dv),
        grid=(n_seq,),
        in_specs=[pl.BlockSpec((l, zg.shape[1]), lambda s: (sb0 + s, 0)),
                  pl.BlockSpec((l, la.shape[1]), lambda s: (sb0 + s, 0)),
                  pl.BlockSpec((1, 2, wv, qk), lambda s: (s, 0, 0, 0)),
                  pl.BlockSpec((1, dv), lambda s: (0, 0))],
        out_specs=[pl.BlockSpec((l, wv), lambda s: (s, 0)),
                   pl.BlockSpec((1, 2, wv, qk), lambda s: (s, 0, 0, 0))],
        out_shape=[jax.ShapeDtypeStruct((n_seq * l, wv), BF16),
                   jax.ShapeDtypeStruct((n_seq, 2, wv, qk), F32)],
        scratch_shapes=[pltpu.VMEM((l, wv), F32), pltpu.VMEM((l, wv), F32),
                        pltpu.VMEM((2, l, qk), BF16), pltpu.VMEM((2, l, qk), BF16),
                        pltpu.VMEM((2, l // GLA_CHUNK * 8, qk), F32),
                        pltpu.VMEM((2, wv, qk), F32)],
        compiler_params=_cparams(("arbitrary",), VMEM_LIMIT),
        name="gla_mixer",
    )(zg, la, s0, norm_g.reshape(1, dv))


def _outproj_kernel(yh_ref, yg_ref, wh_ref, wg_ref, x_ref, g1_ref, g_ref, sh_ref, sc_ref,
                    xo_ref, u_ref):
    delta = _dot(yh_ref[...], wh_ref[...]) + _dot(yg_ref[...], wg_ref[...])
    x = x_ref[...] + g1_ref[0] * delta
    xo_ref[...] = x
    u_ref[...] = _rms(x, g_ref[...]) * (1.0 + sc_ref[0]) + sh_ref[0]


def out_proj(y_hy, y_gla, w_hy, w_gla, xs, mod, norm_g, tm, seq, n_lat, b, n_tok):
    d = xs.shape[1]
    nlt = n_lat // tm
    row = functools.partial(_mod_row, tm=tm, seq=seq, n_lat_tiles=nlt, b=b)
    full = lambda a: pl.BlockSpec(a.shape, lambda i: (0,) * a.ndim)
    modspec = lambda col: pl.BlockSpec((1, 1, d), lambda i: (row(i), 0, col))
    tile = lambda w: pl.BlockSpec((tm, w), lambda i: (i, 0))
    return pl.pallas_call(
        _outproj_kernel,
        grid=(n_tok // tm,),
        in_specs=[tile(y_hy.shape[1]), tile(y_gla.shape[1]), full(w_hy), full(w_gla), tile(d),
                  modspec(2), full(norm_g), modspec(3), modspec(4)],
        out_specs=[tile(d), tile(d)],
        out_shape=[jax.ShapeDtypeStruct((n_tok, d), F32)] * 2,
        compiler_params=_cparams(("arbitrary",), VMEM_LIMIT),
        name="out_proj",
    )(y_hy, y_gla, w_hy, w_gla, xs, mod, norm_g, mod, mod)


def _router_kernel(n_exp, tm, u_ref, wh_ref, wl_ref, b_ref, tri_ref,
                   idx_ref, gate_ref, rank_ref, cnt_ref):
    i = pl.program_id(0)

    @pl.when(i == 0)
    def _():
        cnt_ref[...] = jnp.zeros_like(cnt_ref)

    uh, ul = _split(u_ref[...])
    nt = (((1,), (1,)), ((), ()))
    logits = (lax.dot_general(wh_ref[...], uh, nt, preferred_element_type=F32)
              + lax.dot_general(wh_ref[...], ul, nt, preferred_element_type=F32)
              + lax.dot_general(wl_ref[...], uh, nt, preferred_element_type=F32))
    s = 1.0 / (1.0 + jnp.exp(-logits))
    sel = s + b_ref[...]
    gsz = n_exp // N_GROUPS
    neg = -jnp.inf
    rows_g = lax.broadcasted_iota(jnp.int32, (gsz, tm), 0)
    blocks, gscore = [], []
    for g in range(N_GROUPS):
        blk = sel[g * gsz:(g + 1) * gsz, :]
        m1 = jnp.max(blk, axis=0, keepdims=True)
        i1 = jnp.min(jnp.where(blk == m1, rows_g, gsz), axis=0, keepdims=True)
        m2 = jnp.max(jnp.where(rows_g == i1, neg, blk), axis=0, keepdims=True)
        blocks.append(blk)
        gscore.append(m1 + m2)
    masked = []
    for g in range(N_GROUPS):
        beaten = jnp.zeros((1, tm), jnp.int32)
        for h in range(N_GROUPS):
            if h == g:
                continue
            wins = (gscore[h] > gscore[g]) | ((gscore[h] == gscore[g]) & (h < g))
            beaten = beaten + wins.astype(jnp.int32)
        masked.append(jnp.where(beaten < TOPK_GROUPS, blocks[g], neg))
    sel = jnp.concatenate(masked, axis=0)

    rows = lax.broadcasted_iota(jnp.int32, (n_exp, tm), 0)
    chosen = jnp.zeros((n_exp, tm), F32)
    idxs, gates = [], []
    for _ in range(TOP_K):
        m = jnp.max(sel, axis=0, keepdims=True)
        ik = jnp.min(jnp.where(sel == m, rows, n_exp), axis=0, keepdims=True)
        hit = rows == ik
        gates.append(jnp.sum(jnp.where(hit, s, 0.0), axis=0, keepdims=True))
        sel = jnp.where(hit, neg, sel)
        chosen = jnp.where(hit, 1.0, chosen)
        idxs.append(ik)
    gsum = gates[0]
    for g in gates[1:]:
        gsum = gsum + g
    inv = ROUTED_SCALE / gsum
    before = _dot(chosen.astype(BF16), tri_ref[...]) + cnt_ref[...]
    for k in range(TOP_K):
        idx_ref[k:k + 1, :] = idxs[k]
        gate_ref[k:k + 1, :] = gates[k] * inv
        rk = jnp.sum(jnp.where(rows == idxs[k], before, 0.0), axis=0, keepdims=True)
        rank_ref[k:k + 1, :] = rk.astype(jnp.int32)
    cnt_ref[...] += jnp.sum(chosen, axis=1, keepdims=True)


def moe_router(u, wr_hi, wr_lo, b_r, n_tok, tm):
    n_exp, d = wr_hi.shape
    tri = (lax.broadcasted_iota(jnp.int32, (tm, tm), 0)
           < lax.broadcasted_iota(jnp.int32, (tm, tm), 1)).astype(BF16)
    full = lambda a: pl.BlockSpec(a.shape, lambda i: (0,) * a.ndim)
    tok = pl.BlockSpec((TOP_K, tm), lambda i: (0, i))
    return pl.pallas_call(
        functools.partial(_router_kernel, n_exp, tm),
        grid=(n_tok // tm,),
        in_specs=[pl.BlockSpec((tm, d), lambda i: (i, 0)), full(wr_hi), full(wr_lo),
                  pl.BlockSpec((n_exp, 1), lambda i: (0, 0)), full(tri)],
        out_specs=[tok, tok, tok, pl.BlockSpec((n_exp, 1), lambda i: (0, 0))],
        out_shape=[jax.ShapeDtypeStruct((TOP_K, n_tok), jnp.int32),
                   jax.ShapeDtypeStruct((TOP_K, n_tok), F32),
                   jax.ShapeDtypeStruct((TOP_K, n_tok), jnp.int32),
                   jax.ShapeDtypeStruct((n_exp, 1), F32)],
        compiler_params=_cparams(("arbitrary",), VMEM_LIMIT),
        name="moe_router",
    )(u, wr_hi, wr_lo, b_r.reshape(n_exp, 1), tri)


def _row_copy(src, src_row, dst, dst_row, sem):
    return pltpu.make_async_copy(src.at[pl.ds(src_row, 1)], dst.at[pl.ds(dst_row, 1)], sem)


def _pack_bf16_pairs(x):
    w = x.shape[1] // 2
    lo = pltpu.bitcast(x[:, :w].astype(BF16).astype(F32), jnp.uint32) >> 16
    hi = pltpu.bitcast(x[:, w:].astype(BF16).astype(F32), jnp.uint32)
    return hi | lo


def _unpack_bf16_pairs(p):
    lo = pltpu.bitcast(p << 16, F32)
    hi = pltpu.bitcast(p & jnp.uint32(0xFFFF0000), F32)
    return jnp.concatenate([lo, hi], axis=1).astype(BF16)


def _dest_kernel(n_exp, tm, idx_ref, rank_ref, ps_ref, dest_ref):
    rows = lax.broadcasted_iota(jnp.int32, (n_exp, tm), 0)
    ps = ps_ref[...]
    for k in range(TOP_K):
        base = jnp.sum(jnp.where(rows == idx_ref[k:k + 1, :], ps, 0), axis=0, keepdims=True)
        dest_ref[k:k + 1, :] = base + rank_ref[k:k + 1, :]


def moe_dest(idx, rank, pad_start, tm):
    n_tok = idx.shape[1]
    n_exp = pad_start.shape[0]
    tok = pl.BlockSpec((TOP_K, tm), lambda i: (0, i))
    return pl.pallas_call(
        functools.partial(_dest_kernel, n_exp, tm),
        grid=(n_tok // tm,),
        in_specs=[tok, tok, pl.BlockSpec((n_exp, 1), lambda i: (0, 0))],
        out_specs=tok,
        out_shape=jax.ShapeDtypeStruct((TOP_K, n_tok), jnp.int32),
        compiler_params=_cparams(("arbitrary",)),
        name="moe_dest",
    )(idx, rank, pad_start.reshape(n_exp, 1))


def _wait_rows(src, dst, sem, tm):
    for _ in range(TOP_K):
        pltpu.make_async_copy(src.at[pl.ds(0, tm)], dst.at[pl.ds(0, tm)], sem).wait()


ZERO_ROWS = 128


def _zero_padding(n_exp, ps_ref, pe_ref, xs_ref, z_ref, sem, wait):
    def go(copy):
        copy.wait() if wait else copy.start()

    def per_expert(e, carry):
        s = ps_ref[e]
        head = (-s) & 7
        for j in range(7):
            @pl.when(j < head)
            def _():
                go(pltpu.make_async_copy(z_ref.at[pl.ds(0, 1)], xs_ref.at[pl.ds(s + j, 1)], sem))
        s8 = s + head
        m = (pe_ref[e] - s8) >> 3
        bit = ZERO_ROWS // 8
        while bit:
            rows = bit * 8
            start = pl.multiple_of(s8 + (m & ~(2 * bit - 1)) * 8, 8)

            @pl.when((m & bit) != 0)
            def _():
                go(pltpu.make_async_copy(z_ref.at[pl.ds(0, rows)], xs_ref.at[pl.ds(start, rows)], sem))
            bit //= 2
        return carry

    lax.fori_loop(0, n_exp, per_expert, 0)


def _dispatch_kernel(tm, n_steps, n_exp, ps_ref, pe_ref, dest_ref, u_ref, xs_ref, pk_ref, z_ref,
                     sems, zsem):
    i = pl.program_id(0)
    slot = i % 2
    pk = pk_ref.at[slot]
    sem = sems.at[slot]

    @pl.when(i == 0)
    def _():
        z_ref[...] = jnp.zeros_like(z_ref)
        _zero_padding(n_exp, ps_ref, pe_ref, xs_ref, z_ref, zsem, wait=False)

    @pl.when(i >= 2)
    def _():
        _wait_rows(pk, xs_ref, sem, tm)

    pk[...] = _pack_bf16_pairs(u_ref[...])

    def issue(t, carry):
        for k in range(TOP_K):
            _row_copy(pk, t, xs_ref, dest_ref[0, 0, k * tm + t], sem).start()
        return carry

    lax.fori_loop(0, tm, issue, 0)

    @pl.when(i == n_steps - 1)
    def _():
        if n_steps > 1:
            _wait_rows(pk_ref.at[1 - slot], xs_ref, sems.at[1 - slot], tm)
        _wait_rows(pk, xs_ref, sem, tm)
        _zero_padding(n_exp, ps_ref, pe_ref, xs_ref, z_ref, zsem, wait=True)


def moe_dispatch(u, dest, pad_from, pad_end, n_tok, n_slots, tm):
    d = u.shape[1]
    n_exp = pad_end.shape[0]
    grid_spec = pltpu.PrefetchScalarGridSpec(
        num_scalar_prefetch=2,
        grid=(n_tok // tm,),
        in_specs=[pl.BlockSpec((1, 1, TOP_K * tm), lambda i, *_: (i, 0, 0), memory_space=pltpu.SMEM),
                  pl.BlockSpec((tm, d), lambda i, *_: (i, 0))],
        out_specs=pl.BlockSpec(memory_space=pl.ANY),
        scratch_shapes=[pltpu.VMEM((2, tm, d // 2), jnp.uint32),
                        pltpu.VMEM((ZERO_ROWS, d // 2), jnp.uint32),
                        pltpu.SemaphoreType.DMA((2,)), pltpu.SemaphoreType.DMA])
    return pl.pallas_call(
        functools.partial(_dispatch_kernel, tm, n_tok // tm, n_exp),
        grid_spec=grid_spec,
        out_shape=jax.ShapeDtypeStruct((n_slots, d // 2), jnp.uint32),
        compiler_params=_cparams(("arbitrary",)),
        name="moe_dispatch",
    )(pad_from, pad_end, dest, u)


def _gmm_kernel(layer, be_ref, nx_ref, fl_ref, nu_ref, x_ref, w1_hbm, w3_hbm, w2_hbm, y_ref,
                w1f, w3f, w2f, w1b, w3b, w2b, sems):
    i = pl.program_id(0)
    fl = fl_ref[i]

    def fetch(e, s):
        return (pltpu.make_async_copy(w1_hbm.at[layer, e], w1f.at[s], sems.at[s]),
                pltpu.make_async_copy(w3_hbm.at[layer, e], w3f.at[s], sems.at[s]),
                pltpu.make_async_copy(w2_hbm.at[layer, e], w2f.at[s], sems.at[s]))

    @pl.when(i < nu_ref[0])
    def _():
        @pl.when((fl & 1) != 0)
        def _():
            s = (fl >> 1) & 1

            @pl.when(i == 0)
            def _():
                for c in fetch(be_ref[i], s):
                    c.start()

            for c in fetch(be_ref[i], s):
                c.wait()

            @pl.when((fl & 4) != 0)
            def _():
                for c in fetch(nx_ref[i], 1 - s):
                    c.start()

            w1b[...] = w1f[s].astype(BF16)
            w3b[...] = w3f[s].astype(BF16)
            w2b[...] = w2f[s].astype(BF16)

        x = _unpack_bf16_pairs(x_ref[...])
        h = _silu(_dot(x, w1b[...])) * _dot(x, w3b[...])
        y_ref[...] = _pack_bf16_pairs(_dot(h.astype(BF16), w2b[...]))

    @pl.when(i >= nu_ref[0])
    def _():
        y_ref[...] = jnp.zeros_like(y_ref)


def moe_experts(x_sorted, block_e, n_used, w1, w3, w2, layer):
    n_blocks = block_e.shape[0]
    _, _, d, f = w1.shape
    idx = jnp.arange(n_blocks, dtype=jnp.int32)
    first = (idx == 0) | (block_e != jnp.roll(block_e, 1))
    seg = jnp.cumsum(first.astype(jnp.int32)) - 1
    first_at_or_after = lax.cummin(jnp.where(first, idx, n_blocks)[::-1])[::-1]
    nxt_idx = jnp.concatenate([first_at_or_after[1:], jnp.full((1,), n_blocks, jnp.int32)])
    has_next = nxt_idx < n_used[0]
    nxt_e = block_e[jnp.minimum(nxt_idx, n_blocks - 1)]
    flags = (first.astype(jnp.int32) | ((seg & 1) << 1) | (has_next.astype(jnp.int32) << 2))
    grid_spec = pltpu.PrefetchScalarGridSpec(
        num_scalar_prefetch=4,
        grid=(n_blocks,),
        in_specs=[pl.BlockSpec((MOE_ROWS, d // 2),
                               lambda i, be, nx, fl, nu: (jnp.where(i < nu[0], i, 0), 0)),
                  pl.BlockSpec(memory_space=pl.ANY),
                  pl.BlockSpec(memory_space=pl.ANY),
                  pl.BlockSpec(memory_space=pl.ANY)],
        out_specs=pl.BlockSpec((MOE_ROWS, d // 2), lambda i, *_: (i, 0)),
        scratch_shapes=[pltpu.VMEM((2, d, f), F32), pltpu.VMEM((2, d, f), F32),
                        pltpu.VMEM((2, f, d), F32),
                        pltpu.VMEM((d, f), BF16), pltpu.VMEM((d, f), BF16),
                        pltpu.VMEM((f, d), BF16), pltpu.SemaphoreType.DMA((2,))])
    return pl.pallas_call(
        functools.partial(_gmm_kernel, layer),
        grid_spec=grid_spec,
        out_shape=jax.ShapeDtypeStruct((x_sorted.shape[0], d // 2), jnp.uint32),
        compiler_params=_cparams(("arbitrary",), VMEM_LIMIT),
        name="moe_experts",
    )(block_e, nxt_e, flags, n_used, x_sorted, w1, w3, w2)


def _combine_kernel(tm, n_steps, final, dest_ref, nxt_ref, y_ref, gate_ref, u_ref, ws1_ref, ws3_ref,
                    ws2_ref, x_ref, g2_ref, fg_ref, o_ref, buf_ref, routed_ref, sems):
    i = pl.program_id(0)
    slot = i % 2
    grp = 8

    def gather(d_ref, s, g):
        for tt in range(grp):
            t = g * grp + tt
            for k in range(TOP_K):
                _row_copy(y_ref, d_ref[0, 0, k * tm + t], buf_ref.at[s, k], t, sems.at[s]).start()

    def reduce(g):
        r0 = pl.multiple_of(g * grp, grp)
        gt = gate_ref[pl.ds(r0, grp), :]
        lo = hi = None
        for k in range(TOP_K):
            p = buf_ref[slot, k, pl.ds(r0, grp), :]
            gk = gt[:, k:k + 1]
            lo_k = pltpu.bitcast(p << 16, F32) * gk
            hi_k = pltpu.bitcast(p & jnp.uint32(0xFFFF0000), F32) * gk
            lo = lo_k if lo is None else lo + lo_k
            hi = hi_k if hi is None else hi + hi_k
        routed_ref[pl.ds(r0, grp), :] = jnp.concatenate([lo, hi], axis=1)

    @pl.when(i == 0)
    def _():
        lax.fori_loop(0, tm // grp, lambda g, c: (gather(dest_ref, slot, g), c)[1], 0)

    _wait_rows(y_ref, buf_ref.at[slot, 0], sems.at[slot], tm)

    @pl.when(i + 1 < n_steps)
    def _():
        def both(g, c):
            gather(nxt_ref, 1 - slot, g)
            reduce(g)
            return c
        lax.fori_loop(0, tm // grp, both, 0)

    @pl.when(i + 1 >= n_steps)
    def _():
        lax.fori_loop(0, tm // grp, lambda g, c: (reduce(g), c)[1], 0)

    ub = u_ref[...].astype(BF16)
    hs = _silu(_dot(ub, ws1_ref[...])) * _dot(ub, ws3_ref[...])
    shared = _dot(hs.astype(BF16), ws2_ref[...])
    x = x_ref[...] + g2_ref[0] * (routed_ref[...] + shared)
    o_ref[...] = _rms(x, fg_ref[...]) if final else x


def moe_combine(y_sorted, dest, gate_rep, u, ws1, ws3, ws2, xs, mod, final_g, final,
                tm, seq, n_lat, b, n_tok):
    d = u.shape[1]
    nlt = n_lat // tm
    row = functools.partial(_mod_row, tm=tm, seq=seq, n_lat_tiles=nlt, b=b)
    full = lambda a: pl.BlockSpec(a.shape, lambda i: (0,) * a.ndim)
    tile = pl.BlockSpec((tm, d), lambda i: (i, 0))
    n_steps = n_tok // tm
    return pl.pallas_call(
        functools.partial(_combine_kernel, tm, n_steps, final),
        grid=(n_steps,),
        in_specs=[pl.BlockSpec((1, 1, TOP_K * tm), lambda i: (i, 0, 0), memory_space=pltpu.SMEM),
                  pl.BlockSpec((1, 1, TOP_K * tm),
                               lambda i: (jnp.minimum(i + 1, n_steps - 1), 0, 0),
                               memory_space=pltpu.SMEM),
                  pl.BlockSpec(memory_space=pl.ANY),
                  pl.BlockSpec((tm, TOP_K), lambda i: (i, 0)),
                  tile, full(ws1), full(ws3), full(ws2), tile,
                  pl.BlockSpec((1, 1, d), lambda i: (row(i), 0, 5)),
                  full(final_g)],
        out_specs=tile,
        out_shape=jax.ShapeDtypeStruct((n_tok, d), F32),
        scratch_shapes=[pltpu.VMEM((2, TOP_K, tm, d // 2), jnp.uint32),
                        pltpu.VMEM((tm, d), F32), pltpu.SemaphoreType.DMA((2,))],
        compiler_params=_cparams(("arbitrary",), VMEM_LIMIT),
        name="moe_combine",
    )(dest, dest, y_sorted, gate_rep, u, ws1, ws3, ws2, xs, mod, final_g)


def moe_layer(u, xs, mod, router_w, router_b, w1, w3, w2, layer, ws1, ws3, ws2, final_g, final,
              tm, seq, n_lat, b, n_tok):
    n_exp = router_w.shape[1]
    wr_hi, wr_lo = _split(router_w.T)
    idx, gate, rank, cnt = moe_router(u, wr_hi, wr_lo, router_b, n_tok, tm)
    counts = cnt[:, 0].astype(jnp.int32)
    padded = (counts + MOE_ROWS - 1) // MOE_ROWS * MOE_ROWS
    pad_end = jnp.cumsum(padded)
    pad_start = pad_end - padded
    dest = moe_dest(idx, rank, pad_start, math.gcd(n_tok, 1024))
    n_blocks = -(-(n_tok * TOP_K + n_exp * (MOE_ROWS - 1)) // MOE_ROWS)
    block_start = jnp.arange(n_blocks, dtype=jnp.int32) * MOE_ROWS
    block_e = jnp.minimum(jnp.sum((pad_end[None, :] <= block_start[:, None]).astype(jnp.int32), axis=1),
                          n_exp - 1)
    n_used = (pad_end[-1:] // MOE_ROWS).astype(jnp.int32)
    dest = dest.reshape(TOP_K, n_tok // tm, tm).transpose(1, 0, 2).reshape(n_tok // tm, 1, TOP_K * tm)
    x_sorted = moe_dispatch(u, dest, pad_start + counts, pad_end, n_tok, n_blocks * MOE_ROWS, tm)
    y_sorted = moe_experts(x_sorted, block_e, n_used, w1, w3, w2, layer)
    return moe_combine(y_sorted, dest, gate.T, u, ws1.astype(BF16), ws3.astype(BF16),
                       ws2.astype(BF16), xs, mod, final_g.reshape(1, -1), final,
                       tm, seq, n_lat, b, n_tok)


def _grid_pos_emb(rows, d):
    r, col = jnp.meshgrid(jnp.arange(rows, dtype=F32), jnp.arange(GRID_W, dtype=F32), indexing='ij')
    quarter = d // 4
    omega = 1.0 / (10000.0 ** (jnp.arange(quarter, dtype=F32) / quarter))

    def emb(p):
        a = p.reshape(-1)[:, None] * omega[None, :]
        return jnp.concatenate([jnp.sin(a), jnp.cos(a)], axis=-1)

    return jnp.concatenate([emb(r), emb(col)], axis=-1)


def kernel(x, c, ctx, c_ctx, ada_w, ada_b, norm1_g, norm2_g, w_in, w_out, hy_conv_w, hy_conv_b,
           hy_f_w1, hy_f_b1, hy_f_w2, hy_f_b2, hy_f_w3, hy_skip, gla_wa2, gla_ba2, gla_norm_g,
           router_w, router_b, exp_w1, exp_w3, exp_w2, sh_w1, sh_w3, sh_w2, final_g):
    b, l, d = x.shape
    lc = ctx.shape[1]
    depth = ada_w.shape[0]
    n_lat, n_ctx = b * l, b * lc
    n_all = n_lat + n_ctx
    hy_width = hy_skip.shape[-1]
    hy_cols = hy_conv_w.shape[-1]
    qk = gla_wa2.shape[-1]
    wv = d - hy_width
    g_cols = 2 * qk + 2 * wv
    tm = math.gcd(math.gcd(l, lc), 256)
    tm_proj = math.gcd(math.gcd(l, n_ctx), 512)
    cb = min(hy_width, 256)

    cond = jnp.concatenate([c, c_ctx[None, :]], axis=0)
    n_rows = -(-(b + 1) // 8) * 8
    cond = jnp.concatenate([cond, jnp.zeros((n_rows - b - 1, d), F32)], axis=0)
    mods = ada_table(cond, ada_w, ada_b)

    xs = embed_tokens(x.reshape(n_lat, d), _grid_pos_emb(l // GRID_W, d), ctx.reshape(n_ctx, d), tm)

    tabs = {}
    for length in (l, lc):
        if length not in tabs:
            tabs[length] = _trig_tables(length, True) + _trig_tables(length, False)
    zero_state = jnp.zeros((b, 2, wv, qk), F32)

    for layer in range(depth):
        last = layer == depth - 1
        mod = mods[layer][:, None, :]
        g1 = norm1_g[layer].reshape(1, d)
        g2 = norm2_g[layer].reshape(1, d)
        w_l = w_in[layer].astype(BF16)
        wa2 = gla_wa2[layer]
        rank = wa2.shape[1]
        wa2_bd = jnp.zeros((2 * rank, 2 * qk), F32)
        wa2_bd = wa2_bd.at[:rank, :qk].set(wa2[0]).at[rank:, qk:].set(wa2[1])
        ba2_cat = gla_ba2[layer].reshape(1, 2 * qk)
        zh, zg, la = in_proj(xs, g1, mod, w_l[:, :hy_cols], w_l[:, hy_cols:hy_cols + g_cols],
                             w_l[:, hy_cols + g_cols:], wa2_bd, ba2_cat, tm_proj, l, n_lat, b)

        filt = (hy_f_w1[layer], hy_f_b1[layer], hy_f_w2[layer], hy_f_b2[layer], hy_f_w3[layer],
                hy_skip[layer])
        yc_gla, s_ctx = gla_mixer(zg, la, zero_state, gla_norm_g[layer], lc, b, n_lat)
        y_gla, _ = gla_mixer(zg, la, s_ctx, gla_norm_g[layer], l, b, 0)
        ct, st, c0, s0 = tabs[l]
        gr, gi = hyena_spectra(l, *filt, c0, s0)
        y_hy = hyena_mixer(zh, hy_conv_w[layer], hy_conv_b[layer], ct, st, gr, gi, l, b, 0, cb)
        if last:
            n_tok = n_lat
        else:
            n_tok = n_all
            ct, st, c0, s0 = tabs[lc]
            gr, gi = hyena_spectra(lc, *filt, c0, s0)
            yc_hy = hyena_mixer(zh, hy_conv_w[layer], hy_conv_b[layer], ct, st, gr, gi,
                                lc, b, n_lat, cb)
            y_hy = jnp.concatenate([y_hy, yc_hy], axis=0)
            y_gla = jnp.concatenate([y_gla, yc_gla], axis=0)
        w_o = w_out[layer].astype(BF16)
        xs_mid, u = out_proj(y_hy, y_gla, w_o[:hy_width], w_o[hy_width:], xs, mod, g2,
                             tm, l, n_lat, b, n_tok)
        xs = moe_layer(u, xs_mid, mod, router_w[layer], router_b[layer], exp_w1, exp_w3, exp_w2,
                       layer, sh_w1[layer], sh_w3[layer], sh_w2[layer],
                       final_g, last, tm, l, n_lat, b, n_tok)
    return xs[:n_lat].reshape(b, l, d)
```

```python
import functools
import math

import numpy as np
import jax
import jax.numpy as jnp
from jax import lax
from jax.experimental import pallas as pl
from jax.experimental.pallas import tpu as pltpu

GRID_W = 64
EPS = 1e-6

HY_ORDER = 2
HY_EMB = 33
HY_EMB_PAD = 40
HY_FREQ = 1.0
HY_TARGET = 1e-2
HY_FAST_PCT = 0.3
HY_SLOW_PCT = 1.5
HY_MIN_DECAY = math.log(1.0 / HY_TARGET) / HY_SLOW_PCT
HY_MAX_DECAY = math.log(1.0 / HY_TARGET) / HY_FAST_PCT

GLA_HEADS = 4
GLA_RANK = 16
GLA_TAU = 16.0
GLA_CHUNK = 64
GLA_GROUP = 4
GLA_STATE_UNROLL = 4

TOP_K = 8
N_GROUPS = 8
TOPK_GROUPS = 4
ROUTED_SCALE = 2.5

HY_FREQ_CHUNK = 1024
MOE_ROWS = 512
VMEM_LIMIT = 56 * 1024 * 1024

F32 = jnp.float32
BF16 = jnp.bfloat16


def _cparams(sem, vmem=None):
    return pltpu.CompilerParams(dimension_semantics=sem, vmem_limit_bytes=vmem)


def _split(a):
    hi = a.astype(BF16)
    lo = (a - hi.astype(F32)).astype(BF16)
    return hi, lo


def _dot(a, b):
    return jnp.dot(a, b, preferred_element_type=F32)


def _dot3(a, b):
    ah, al = _split(a)
    bh, bl = _split(b)
    return _dot(ah, bh) + _dot(ah, bl) + _dot(al, bh)


def _silu(x):
    return x * (1.0 / (1.0 + jnp.exp(-x)))


def _rms(x, g):
    return x * lax.rsqrt(jnp.mean(x * x, axis=-1, keepdims=True) + EPS) * g


def _ada_kernel(c_ref, w_ref, b_ref, o_ref):
    o_ref[0] = _dot3(_silu(c_ref[...]), w_ref[0]) + b_ref[0]


def ada_table(cond_rows, ada_w, ada_b):
    depth, d, six_d = ada_w.shape
    r = cond_rows.shape[0]
    tn = 1024
    return pl.pallas_call(
        _ada_kernel,
        grid=(depth, six_d // tn),
        in_specs=[pl.BlockSpec((r, d), lambda l, j: (0, 0)),
                  pl.BlockSpec((1, d, tn), lambda l, j: (l, 0, j)),
                  pl.BlockSpec((1, 1, tn), lambda l, j: (l, 0, j))],
        out_specs=pl.BlockSpec((1, r, tn), lambda l, j: (l, 0, j)),
        out_shape=jax.ShapeDtypeStruct((depth, r, six_d), F32),
        compiler_params=_cparams(("arbitrary", "arbitrary")),
        name="ada_table",
    )(cond_rows, ada_w, ada_b.reshape(depth, 1, six_d))


def _embed_kernel(n_lat_tiles, x_ref, p_ref, c_ref, o_ref):
    i = pl.program_id(0)

    @pl.when(i < n_lat_tiles)
    def _():
        o_ref[...] = x_ref[...] + p_ref[...]

    @pl.when(i >= n_lat_tiles)
    def _():
        o_ref[...] = c_ref[...]


def embed_tokens(x2, pos, ctx2, tm):
    nl, d = x2.shape
    nc = ctx2.shape[0]
    l = pos.shape[0]
    nlt, nct, lt = nl // tm, nc // tm, l // tm
    return pl.pallas_call(
        functools.partial(_embed_kernel, nlt),
        grid=(nlt + nct,),
        in_specs=[pl.BlockSpec((tm, d), lambda i: (jnp.minimum(i, nlt - 1), 0)),
                  pl.BlockSpec((tm, d), lambda i: (i % lt, 0)),
                  pl.BlockSpec((tm, d), lambda i: (jnp.maximum(i - nlt, 0), 0))],
        out_specs=pl.BlockSpec((tm, d), lambda i: (i, 0)),
        out_shape=jax.ShapeDtypeStruct((nl + nc, d), F32),
        compiler_params=_cparams(("arbitrary",)),
        name="embed_tokens",
    )(x2, pos, ctx2)


def _mod_row(i, tm, seq, n_lat_tiles, b):
    return jnp.where(i < n_lat_tiles, (i * tm) // seq, b)


def _inproj_kernel(x_ref, g_ref, sh_ref, sc_ref, wh_ref, wg_ref, wa_ref, wa2_ref, ba2_ref,
                   zh_ref, zg_ref, la_ref):
    x = x_ref[...]
    h = _rms(x, g_ref[...]) * (1.0 + sc_ref[0]) + sh_ref[0]
    hb = h.astype(BF16)
    zh_ref[...] = _dot(hb, wh_ref[...]).astype(BF16)
    zg_ref[...] = _dot(hb, wg_ref[...]).astype(BF16)
    za = _dot(hb, wa_ref[...])
    t = _dot3(za, wa2_ref[...]) + ba2_ref[...]
    la_ref[...] = (jnp.minimum(t, 0.0) - jnp.log(1.0 + jnp.exp(-jnp.abs(t)))) * (1.0 / GLA_TAU)


def in_proj(xs, norm_g, mod, w_hy, w_g, w_a, wa2_bd, ba2_cat, tm, seq, n_lat, b):
    n, d = xs.shape
    nlt = n_lat // tm
    row = functools.partial(_mod_row, tm=tm, seq=seq, n_lat_tiles=nlt, b=b)
    hy_cols, g_cols, la_cols = w_hy.shape[1], w_g.shape[1], wa2_bd.shape[1]
    full = lambda a: pl.BlockSpec(a.shape, lambda i: (0,) * a.ndim)
    return pl.pallas_call(
        _inproj_kernel,
        grid=(n // tm,),
        in_specs=[pl.BlockSpec((tm, d), lambda i: (i, 0)),
                  full(norm_g),
                  pl.BlockSpec((1, 1, d), lambda i: (row(i), 0, 0)),
                  pl.BlockSpec((1, 1, d), lambda i: (row(i), 0, 1)),
                  full(w_hy), full(w_g), full(w_a), full(wa2_bd), full(ba2_cat)],
        out_specs=[pl.BlockSpec((tm, hy_cols), lambda i: (i, 0)),
                   pl.BlockSpec((tm, g_cols), lambda i: (i, 0)),
                   pl.BlockSpec((tm, la_cols), lambda i: (i, 0))],
        out_shape=[jax.ShapeDtypeStruct((n, hy_cols), BF16),
                   jax.ShapeDtypeStruct((n, g_cols), BF16),
                   jax.ShapeDtypeStruct((n, la_cols), F32)],
        compiler_params=_cparams(("arbitrary",), VMEM_LIMIT),
        name="in_proj",
    )(xs, norm_g, mod, mod, w_hy, w_g, w_a, wa2_bd, ba2_cat)


def _trig_tables(l, half_shift):
    k = np.arange(l, dtype=np.int64)[:, None]
    n = np.arange(l, dtype=np.int64)[None, :]
    m = ((2 * k + 1) * (2 * n + (1 if half_shift else 0))) % (8 * l)
    ang = m.astype(np.float64) * (2.0 * math.pi / (8 * l))
    return (jnp.asarray(np.cos(ang).astype(np.float32).astype(BF16)),
            jnp.asarray(np.sin(ang).astype(np.float32).astype(BF16)))


def _filter_kernel(n_orders, width, l, tk,
                   pos_ref, t_ref, dl_ref, w1_ref, b1_ref, w2_ref, b2_ref, w3_ref, skip_ref,
                   c0_ref, s0_ref, gr_ref, gi_ref, hs_ref, hd_ref):
    j = pl.program_id(0)

    @pl.when(j == 0)
    def _():
        hid = jnp.sin(HY_FREQ * (_dot3(pos_ref[...], w1_ref[...]) + b1_ref[...]))
        hid = jnp.sin(HY_FREQ * (_dot3(hid, w2_ref[...]) + b2_ref[...]))
        h = _dot3(hid, w3_ref[...])
        decay = jnp.exp(-t_ref[...] * dl_ref[...])
        for o in range(n_orders):
            hf = h[:, (2 * o) * width:(2 * o + 1) * width] * decay
            hb = h[:, (2 * o + 1) * width:(2 * o + 2) * width] * decay
            hs_ref[:, o * width:(o + 1) * width] = (hf + hb).astype(BF16)
            hd_ref[:, o * width:(o + 1) * width] = (hb - hf).astype(BF16)

    scale = 1.0 / l
    gr = _dot(c0_ref[...], hs_ref[...])
    gi = _dot(s0_ref[...], hd_ref[...])
    for o in range(n_orders):
        gr_ref[o] = (gr[:, o * width:(o + 1) * width] + skip_ref[o]) * scale
        gi_ref[o] = gi[:, o * width:(o + 1) * width] * scale


def hyena_spectra(l, w1, b1, w2, b2, w3, skip, c0, s0):
    width = skip.shape[-1]
    n_orders = skip.shape[0]
    t = jnp.linspace(0.0, 1.0, l, dtype=F32)[:, None]
    bands = (HY_EMB - 1) // 2
    w = 2.0 * math.pi * jnp.arange(l, dtype=F32)[:, None] / l
    f = jnp.linspace(1e-4, bands - 1, bands, dtype=F32)[None, :]
    pos = jnp.concatenate([t, jnp.cos(f * w), -jnp.sin(f * w),
                           jnp.zeros((l, HY_EMB_PAD - HY_EMB), F32)], axis=-1)
    w1p = jnp.concatenate([w1, jnp.zeros((HY_EMB_PAD - HY_EMB, w1.shape[1]), F32)], axis=0)
    deltas = jnp.linspace(HY_MIN_DECAY, HY_MAX_DECAY, width, dtype=F32)[None, :]
    tk = min(l, 256)
    full = lambda a: pl.BlockSpec(a.shape, lambda j: (0,) * a.ndim)
    skip3 = skip.reshape(n_orders, 1, width)
    args = (pos, t, deltas, w1p, b1.reshape(1, -1), w2, b2.reshape(1, -1), w3, skip3)
    return pl.pallas_call(
        functools.partial(_filter_kernel, n_orders, width, l, tk),
        grid=(l // tk,),
        in_specs=[full(a) for a in args] + [pl.BlockSpec((tk, l), lambda j: (j, 0)),
                                            pl.BlockSpec((tk, l), lambda j: (j, 0))],
        out_specs=[pl.BlockSpec((n_orders, tk, width), lambda j: (0, j, 0)),
                   pl.BlockSpec((n_orders, tk, width), lambda j: (0, j, 0))],
        out_shape=[jax.ShapeDtypeStruct((n_orders, l, width), F32)] * 2,
        scratch_shapes=[pltpu.VMEM((l, n_orders * width), BF16),
                        pltpu.VMEM((l, n_orders * width), BF16)],
        compiler_params=_cparams(("arbitrary",), VMEM_LIMIT),
        name="hyena_spectra",
    )(*args, c0, s0)


def _hyena_kernel(l, fc, zv_ref, z1_ref, z2_ref, wv_ref, w1_ref, w2_ref, bv_ref, b1_ref, b2_ref,
                  c_ref, s_ref, gr_ref, gi_ref, y_ref, ub_ref, a_ref, b_ref, x1_ref, x2_ref):
    rows = lax.broadcasted_iota(jnp.int32, (l, 1), 0)

    def conv3(z_ref, w_ref, bias_ref):
        z = z_ref[...].astype(F32)
        zm = jnp.where(rows == 0, 0.0, pltpu.roll(z, 1, 0))
        zp = jnp.where(rows == l - 1, 0.0, pltpu.roll(z, l - 1, 0))
        return zm * w_ref[0:1, :] + z * w_ref[1:2, :] + zp * w_ref[2:3, :] + bias_ref[...]

    ub_ref[...] = conv3(zv_ref, wv_ref, bv_ref).astype(BF16)
    x1_ref[...] = conv3(z1_ref, w1_ref, b1_ref)
    x2_ref[...] = conv3(z2_ref, w2_ref, b2_ref)
    n = l // fc

    def forward(o):
        def body(c, carry):
            r0 = pl.multiple_of(c * fc, fc)
            ur = _dot(c_ref[pl.ds(r0, fc), :], ub_ref[...])
            us = _dot(s_ref[pl.ds(r0, fc), :], ub_ref[...])
            gr = gr_ref[o, pl.ds(r0, fc), :]
            gi = gi_ref[o, pl.ds(r0, fc), :]
            a_ref[pl.ds(r0, fc), :] = (ur * gr + us * gi).astype(BF16)
            b_ref[pl.ds(r0, fc), :] = (us * gr - ur * gi).astype(BF16)
            return carry
        lax.fori_loop(0, n, body, 0)

    def inverse(gate_ref, dst_ref):
        def body(c, carry):
            r0 = pl.multiple_of(c * fc, fc)
            lc = (_dot(c_ref[pl.ds(r0, fc), :], a_ref[...])
                  + _dot(s_ref[pl.ds(r0, fc), :], b_ref[...]))
            dst_ref[pl.ds(r0, fc), :] = (gate_ref[pl.ds(r0, fc), :] * lc).astype(BF16)
            return carry
        lax.fori_loop(0, n, body, 0)

    forward(0)
    inverse(x1_ref, ub_ref)
    forward(1)
    inverse(x2_ref, y_ref)


def hyena_mixer(zh, conv_w, conv_b, ctab, stab, gr, gi, l, n_seq, row0, cb):
    width = gr.shape[-1]
    n_orders = gr.shape[0]
    ncb = width // cb
    sb0 = row0 // l
    conv_b2 = conv_b.reshape(1, -1)
    zspec = lambda part: pl.BlockSpec((l, cb), lambda c, s: (sb0 + s, part * ncb + c))
    wspec = lambda part: pl.BlockSpec((3, cb), lambda c, s: (0, part * ncb + c))
    bspec = lambda part: pl.BlockSpec((1, cb), lambda c, s: (0, part * ncb + c))
    once = pl.Buffered(1)
    tspec = pl.BlockSpec((l, l), lambda c, s: (0, 0), pipeline_mode=once)
    gspec = pl.BlockSpec((n_orders, l, cb), lambda c, s: (0, 0, c), pipeline_mode=once)
    return pl.pallas_call(
        functools.partial(_hyena_kernel, l, min(l, HY_FREQ_CHUNK)),
        grid=(ncb, n_seq),
        in_specs=[zspec(0), zspec(1), zspec(2), wspec(0), wspec(1), wspec(2),
                  bspec(0), bspec(1), bspec(2), tspec, tspec, gspec, gspec],
        out_specs=pl.BlockSpec((l, cb), lambda c, s: (s, c)),
        out_shape=jax.ShapeDtypeStruct((n_seq * l, width), BF16),
        scratch_shapes=[pltpu.VMEM((l, cb), BF16), pltpu.VMEM((l, cb), BF16),
                        pltpu.VMEM((l, cb), BF16), pltpu.VMEM((l, cb), F32),
                        pltpu.VMEM((l, cb), F32)],
        compiler_params=_cparams(("arbitrary", "arbitrary"), VMEM_LIMIT),
        name="hyena_mixer",
    )(zh, zh, zh, conv_w, conv_w, conv_w, conv_b2, conv_b2, conv_b2, ctab, stab, gr, gi)


def _gla_kernel(l, qk, wv, dk, dv,
                zg_ref, la_ref, s0_ref, g_ref, y_ref, sfin_ref,
                of_ref, ob_ref, qe_ref, ks_ref, dec_ref, st_ref):
    ch = GLA_CHUNK
    n = l // ch
    heads = qk // dk
    grp = min(GLA_GROUP, n)
    rg = grp * ch
    r_i = lax.broadcasted_iota(jnp.int32, (rg, rg), 0)
    c_i = lax.broadcasted_iota(jnp.int32, (rg, rg), 1)
    same = (r_i // ch) == (c_i // ch)
    low = same & (r_i >= c_i)
    upp = same & (r_i <= c_i)
    masks = (low, upp)
    cum2 = tuple(jnp.concatenate([jnp.concatenate([m.astype(BF16)] * 2, axis=1),
                                  jnp.concatenate([same.astype(BF16)] * 2, axis=1)], axis=0)
                 for m in masks)
    lane_head = lax.broadcasted_iota(jnp.int32, (1, qk), 1) // dk
    scale = dk ** -0.5
    o_refs = (of_ref, ob_ref)

    def local(g, d):
        r0 = pl.multiple_of(g * rg, rg)
        q = zg_ref[pl.ds(r0, rg), 0:qk].astype(F32) * scale
        k = zg_ref[pl.ds(r0, rg), qk:2 * qk].astype(F32)
        v = zg_ref[pl.ds(r0, rg), 2 * qk:2 * qk + wv]
        la = la_ref[pl.ds(r0, rg), d * qk:(d + 1) * qk]
        lh, ll = _split(la)
        bt = _dot(cum2[d], jnp.concatenate([lh, ll], axis=0))
        b, tot = bt[:rg], bt[rg:]
        qe = (q * jnp.exp(b)).astype(BF16)
        ke = (k * jnp.exp(-b)).astype(BF16)
        qe_ref[d, pl.ds(r0, rg), :] = qe
        ks_ref[d, pl.ds(r0, rg), :] = (k * jnp.exp(tot - b)).astype(BF16)
        dec = jnp.exp(tot)
        for c in range(grp):
            dec_ref[d, pl.ds(pl.multiple_of((g * grp + c) * 8, 8), 8), :] = dec[c * ch:c * ch + 8]
        parts = []
        for h in range(heads):
            qh = jnp.where(lane_head == h, qe, jnp.zeros_like(qe))
            att = lax.dot_general(qh, ke, (((1,), (1,)), ((), ())), preferred_element_type=F32)
            att = jnp.where(masks[d], att, 0.0).astype(BF16)
            parts.append(_dot(att, v[:, h * dv:(h + 1) * dv]))
        o_refs[d][pl.ds(r0, rg), :] = jnp.concatenate(parts, axis=1)

    def local_body(g, carry):
        local(g, 0)
        local(g, 1)
        return carry

    lax.fori_loop(0, n // grp, local_body, 0)

    bd = (lax.broadcasted_iota(jnp.int32, (wv, qk), 0) // dv
          == lax.broadcasted_iota(jnp.int32, (wv, qk), 1) // dk).astype(F32)
    st_ref[...] = s0_ref[0]

    def carry_state(c, d):
        r0 = pl.multiple_of(c * ch, ch)
        st = st_ref[d]
        o = lax.dot_general(qe_ref[d, pl.ds(r0, ch), :], st.astype(BF16), (((1,), (1,)), ((), ())),
                            preferred_element_type=F32)
        o_refs[d][pl.ds(r0, ch), :] += o
        v = zg_ref[pl.ds(r0, ch), 2 * qk:2 * qk + wv]
        upd = lax.dot_general(v, ks_ref[d, pl.ds(r0, ch), :], (((0,), (0,)), ((), ())),
                              preferred_element_type=F32)
        dec = dec_ref[d, pl.ds(pl.multiple_of(c * 8, 8), 1), :]
        st_ref[d] = st * dec + upd * bd

    su = min(GLA_STATE_UNROLL, n)

    def state_body(i, carry):
        for u in range(su):
            carry_state(i * su + u, 0)
            carry_state(n - 1 - (i * su + u), 1)
        return carry

    lax.fori_loop(0, n // su, state_body, 0)
    sfin_ref[0] = st_ref[...]

    def finish(i, carry):
        r0 = pl.multiple_of(i * ch, ch)
        o = of_ref[pl.ds(r0, ch), :] + ob_ref[pl.ds(r0, ch), :]
        r = zg_ref[pl.ds(r0, ch), 2 * qk + wv:2 * qk + 2 * wv].astype(F32)
        parts = [_rms(o[:, h * dv:(h + 1) * dv], g_ref[...]) for h in range(heads)]
        y_ref[pl.ds(r0, ch), :] = (jnp.concatenate(parts, axis=1) * _silu(r)).astype(BF16)
        return carry

    lax.fori_loop(0, n, finish, 0)


def gla_mixer(zg, la, s0, norm_g, l, n_seq, row0):
    qk = la.shape[1] // 2
    wv = (zg.shape[1] - 2 * qk) // 2
    dk, dv = qk // GLA_HEADS, wv // GLA_HEADS
    sb0 = row0 // l
    return pl.pallas_call(
        functools.partial(_gla_kernel, l, qk, wv, dk, dv),
        grid=(n_seq,),
        in_specs=[pl.BlockSpec((l, zg.shape[1]), lambda s: (sb0 + s, 0)),
                  pl.BlockSpec((l, la.shape[1]), lambda s: (sb0 + s, 0)),
                  pl.BlockSpec((1, 2, wv, qk), lambda s: (s, 0, 0, 0)),
                  pl.BlockSpec((1, dv), lambda s: (0, 0))],
        out_specs=[pl.BlockSpec((l, wv), lambda s: (s, 0)),
                   pl.BlockSpec((1, 2, wv, qk), lambda s: (s, 0, 0, 0))],
        out_shape=[jax.ShapeDtypeStruct((n_seq * l, wv), BF16),
                   jax.ShapeDtypeStruct((n_seq, 2, wv, qk), F32)],
        scratch_shapes=[pltpu.VMEM((l, wv), F32), pltpu.VMEM((l, wv), F32),
                        pltpu.VMEM((2, l, qk), BF16), pltpu.VMEM((2, l, qk), BF16),
                        pltpu.VMEM((2, l // GLA_CHUNK * 8, qk), F32),
                        pltpu.VMEM((2, wv, qk), F32)],
        compiler_params=_cparams(("arbitrary",), VMEM_LIMIT),
        name="gla_mixer",
    )(zg, la, s0, norm_g.reshape(1, dv))


def _outproj_kernel(yh_ref, yg_ref, wh_ref, wg_ref, x_ref, g1_ref, g_ref, sh_ref, sc_ref,
                    xo_ref, u_ref):
    delta = _dot(yh_ref[...], wh_ref[...]) + _dot(yg_ref[...], wg_ref[...])
    x = x_ref[...] + g1_ref[0] * delta
    xo_ref[...] = x
    u_ref[...] = _rms(x, g_ref[...]) * (1.0 + sc_ref[0]) + sh_ref[0]


def out_proj(y_hy, y_gla, w_hy, w_gla, xs, mod, norm_g, tm, seq, n_lat, b, n_tok):
    d = xs.shape[1]
    nlt = n_lat // tm
    row = functools.partial(_mod_row, tm=tm, seq=seq, n_lat_tiles=nlt, b=b)
    full = lambda a: pl.BlockSpec(a.shape, lambda i: (0,) * a.ndim)
    modspec = lambda col: pl.BlockSpec((1, 1, d), lambda i: (row(i), 0, col))
    tile = lambda w: pl.BlockSpec((tm, w), lambda i: (i, 0))
    return pl.pallas_call(
        _outproj_kernel,
        grid=(n_tok // tm,),
        in_specs=[tile(y_hy.shape[1]), tile(y_gla.shape[1]), full(w_hy), full(w_gla), tile(d),
                  modspec(2), full(norm_g), modspec(3), modspec(4)],
        out_specs=[tile(d), tile(d)],
        out_shape=[jax.ShapeDtypeStruct((n_tok, d), F32)] * 2,
        compiler_params=_cparams(("arbitrary",), VMEM_LIMIT),
        name="out_proj",
    )(y_hy, y_gla, w_hy, w_gla, xs, mod, norm_g, mod, mod)


def _router_kernel(n_exp, tm, u_ref, wh_ref, wl_ref, b_ref, tri_ref,
                   idx_ref, gate_ref, rank_ref, cnt_ref):
    i = pl.program_id(0)

    @pl.when(i == 0)
    def _():
        cnt_ref[...] = jnp.zeros_like(cnt_ref)

    uh, ul = _split(u_ref[...])
    nt = (((1,), (1,)), ((), ()))
    logits = (lax.dot_general(wh_ref[...], uh, nt, preferred_element_type=F32)
              + lax.dot_general(wh_ref[...], ul, nt, preferred_element_type=F32)
              + lax.dot_general(wl_ref[...], uh, nt, preferred_element_type=F32))
    s = 1.0 / (1.0 + jnp.exp(-logits))
    sel = s + b_ref[...]
    gsz = n_exp // N_GROUPS
    neg = -jnp.inf
    rows_g = lax.broadcasted_iota(jnp.int32, (gsz, tm), 0)
    blocks, gscore = [], []
    for g in range(N_GROUPS):
        blk = sel[g * gsz:(g + 1) * gsz, :]
        m1 = jnp.max(blk, axis=0, keepdims=True)
        i1 = jnp.min(jnp.where(blk == m1, rows_g, gsz), axis=0, keepdims=True)
        m2 = jnp.max(jnp.where(rows_g == i1, neg, blk), axis=0, keepdims=True)
        blocks.append(blk)
        gscore.append(m1 + m2)
    masked = []
    for g in range(N_GROUPS):
        beaten = jnp.zeros((1, tm), jnp.int32)
        for h in range(N_GROUPS):
            if h == g:
                continue
            wins = (gscore[h] > gscore[g]) | ((gscore[h] == gscore[g]) & (h < g))
            beaten = beaten + wins.astype(jnp.int32)
        masked.append(jnp.where(beaten < TOPK_GROUPS, blocks[g], neg))
    sel = jnp.concatenate(masked, axis=0)

    rows = lax.broadcasted_iota(jnp.int32, (n_exp, tm), 0)
    chosen = jnp.zeros((n_exp, tm), F32)
    idxs, gates = [], []
    for _ in range(TOP_K):
        m = jnp.max(sel, axis=0, keepdims=True)
        ik = jnp.min(jnp.where(sel == m, rows, n_exp), axis=0, keepdims=True)
        hit = rows == ik
        gates.append(jnp.sum(jnp.where(hit, s, 0.0), axis=0, keepdims=True))
        sel = jnp.where(hit, neg, sel)
        chosen = jnp.where(hit, 1.0, chosen)
        idxs.append(ik)
    gsum = gates[0]
    for g in gates[1:]:
        gsum = gsum + g
    inv = ROUTED_SCALE / gsum
    before = _dot(chosen.astype(BF16), tri_ref[...]) + cnt_ref[...]
    for k in range(TOP_K):
        idx_ref[k:k + 1, :] = idxs[k]
        gate_ref[k:k + 1, :] = gates[k] * inv
        rk = jnp.sum(jnp.where(rows == idxs[k], before, 0.0), axis=0, keepdims=True)
        rank_ref[k:k + 1, :] = rk.astype(jnp.int32)
    cnt_ref[...] += jnp.sum(chosen, axis=1, keepdims=True)


def moe_router(u, wr_hi, wr_lo, b_r, n_tok, tm):
    n_exp, d = wr_hi.shape
    tri = (lax.broadcasted_iota(jnp.int32, (tm, tm), 0)
           < lax.broadcasted_iota(jnp.int32, (tm, tm), 1)).astype(BF16)
    full = lambda a: pl.BlockSpec(a.shape, lambda i: (0,) * a.ndim)
    tok = pl.BlockSpec((TOP_K, tm), lambda i: (0, i))
    return pl.pallas_call(
        functools.partial(_router_kernel, n_exp, tm),
        grid=(n_tok // tm,),
        in_specs=[pl.BlockSpec((tm, d), lambda i: (i, 0)), full(wr_hi), full(wr_lo),
                  pl.BlockSpec((n_exp, 1), lambda i: (0, 0)), full(tri)],
        out_specs=[tok, tok, tok, pl.BlockSpec((n_exp, 1), lambda i: (0, 0))],
        out_shape=[jax.ShapeDtypeStruct((TOP_K, n_tok), jnp.int32),
                   jax.ShapeDtypeStruct((TOP_K, n_tok), F32),
                   jax.ShapeDtypeStruct((TOP_K, n_tok), jnp.int32),
                   jax.ShapeDtypeStruct((n_exp, 1), F32)],
        compiler_params=_cparams(("arbitrary",), VMEM_LIMIT),
        name="moe_router",
    )(u, wr_hi, wr_lo, b_r.reshape(n_exp, 1), tri)


def _row_copy(src, src_row, dst, dst_row, sem):
    return pltpu.make_async_copy(src.at[pl.ds(src_row, 1)], dst.at[pl.ds(dst_row, 1)], sem)


def _pack_bf16_pairs(x):
    w = x.shape[1] // 2
    lo = pltpu.bitcast(x[:, :w].astype(BF16).astype(F32), jnp.uint32) >> 16
    hi = pltpu.bitcast(x[:, w:].astype(BF16).astype(F32), jnp.uint32)
    return hi | lo


def _unpack_bf16_pairs(p):
    lo = pltpu.bitcast(p << 16, F32)
    hi = pltpu.bitcast(p & jnp.uint32(0xFFFF0000), F32)
    return jnp.concatenate([lo, hi], axis=1).astype(BF16)


def _dest_kernel(n_exp, tm, idx_ref, rank_ref, ps_ref, dest_ref):
    rows = lax.broadcasted_iota(jnp.int32, (n_exp, tm), 0)
    ps = ps_ref[...]
    for k in range(TOP_K):
        base = jnp.sum(jnp.where(rows == idx_ref[k:k + 1, :], ps, 0), axis=0, keepdims=True)
        dest_ref[k:k + 1, :] = base + rank_ref[k:k + 1, :]


def moe_dest(idx, rank, pad_start, tm):
    n_tok = idx.shape[1]
    n_exp = pad_start.shape[0]
    tok = pl.BlockSpec((TOP_K, tm), lambda i: (0, i))
    return pl.pallas_call(
        functools.partial(_dest_kernel, n_exp, tm),
        grid=(n_tok // tm,),
        in_specs=[tok, tok, pl.BlockSpec((n_exp, 1), lambda i: (0, 0))],
        out_specs=tok,
        out_shape=jax.ShapeDtypeStruct((TOP_K, n_tok), jnp.int32),
        compiler_params=_cparams(("arbitrary",)),
        name="moe_dest",
    )(idx, rank, pad_start.reshape(n_exp, 1))


def _wait_rows(src, dst, sem, tm):
    for _ in range(TOP_K):
        pltpu.make_async_copy(src.at[pl.ds(0, tm)], dst.at[pl.ds(0, tm)], sem).wait()


ZERO_ROWS = MOE_ROWS // 2


def _zero_padding(n_exp, ps_ref, pe_ref, xs_ref, z_ref, sem, wait):
    def go(copy):
        copy.wait() if wait else copy.start()

    def per_expert(e, carry):
        s = ps_ref[e]
        head = (-s) & 7
        for j in range(7):
            @pl.when(j < head)
            def _():
                go(pltpu.make_async_copy(z_ref.at[pl.ds(0, 1)], xs_ref.at[pl.ds(s + j, 1)], sem))
        s8 = s + head
        m = (pe_ref[e] - s8) >> 3
        bit = ZERO_ROWS // 8
        while bit:
            rows = bit * 8
            start = pl.multiple_of(s8 + (m & ~(2 * bit - 1)) * 8, 8)

            @pl.when((m & bit) != 0)
            def _():
                go(pltpu.make_async_copy(z_ref.at[pl.ds(0, rows)], xs_ref.at[pl.ds(start, rows)], sem))
            bit //= 2
        return carry

    lax.fori_loop(0, n_exp, per_expert, 0)

    tail = pe_ref[n_exp - 1]

    def per_chunk(j, carry):
        start = pl.multiple_of(tail + j * ZERO_ROWS, ZERO_ROWS)
        go(pltpu.make_async_copy(z_ref, xs_ref.at[pl.ds(start, ZERO_ROWS)], sem))
        return carry

    lax.fori_loop(0, (xs_ref.shape[0] - tail) // ZERO_ROWS, per_chunk, 0)


def _dispatch_kernel(tm, n_steps, n_exp, ps_ref, pe_ref, dest_ref, u_ref, xs_ref, pk_ref, z_ref,
                     sems, zsem):
    i = pl.program_id(0)
    slot = i % 2
    pk = pk_ref.at[slot]
    sem = sems.at[slot]

    @pl.when(i == 0)
    def _():
        z_ref[...] = jnp.zeros_like(z_ref)
        _zero_padding(n_exp, ps_ref, pe_ref, xs_ref, z_ref, zsem, wait=False)

    @pl.when(i >= 2)
    def _():
        _wait_rows(pk, xs_ref, sem, tm)

    pk[...] = _pack_bf16_pairs(u_ref[...])

    def issue(t, carry):
        for k in range(TOP_K):
            _row_copy(pk, t, xs_ref, dest_ref[0, 0, k * tm + t], sem).start()
        return carry

    lax.fori_loop(0, tm, issue, 0)

    @pl.when(i == n_steps - 1)
    def _():
        if n_steps > 1:
            _wait_rows(pk_ref.at[1 - slot], xs_ref, sems.at[1 - slot], tm)
        _wait_rows(pk, xs_ref, sem, tm)
        _zero_padding(n_exp, ps_ref, pe_ref, xs_ref, z_ref, zsem, wait=True)


def moe_dispatch(u, dest, pad_from, pad_end, n_tok, n_slots, tm):
    d = u.shape[1]
    n_exp = pad_end.shape[0]
    grid_spec = pltpu.PrefetchScalarGridSpec(
        num_scalar_prefetch=2,
        grid=(n_tok // tm,),
        in_specs=[pl.BlockSpec((1, 1, TOP_K * tm), lambda i, *_: (i, 0, 0), memory_space=pltpu.SMEM),
                  pl.BlockSpec((tm, d), lambda i, *_: (i, 0))],
        out_specs=pl.BlockSpec(memory_space=pl.ANY),
        scratch_shapes=[pltpu.VMEM((2, tm, d // 2), jnp.uint32),
                        pltpu.VMEM((ZERO_ROWS, d // 2), jnp.uint32),
                        pltpu.SemaphoreType.DMA((2,)), pltpu.SemaphoreType.DMA])
    return pl.pallas_call(
        functools.partial(_dispatch_kernel, tm, n_tok // tm, n_exp),
        grid_spec=grid_spec,
        out_shape=jax.ShapeDtypeStruct((n_slots, d // 2), jnp.uint32),
        compiler_params=_cparams(("arbitrary",)),
        name="moe_dispatch",
    )(pad_from, pad_end, dest, u)


def _gmm_kernel(layer, be_ref, nx_ref, fl_ref, nu_ref, x_ref, w1_hbm, w3_hbm, w2_hbm, y_ref,
                w1f, w3f, w2f, w1b, w3b, w2b, sems):
    i = pl.program_id(0)
    fl = fl_ref[i]

    def fetch(e, s):
        return (pltpu.make_async_copy(w1_hbm.at[layer, e], w1f.at[s], sems.at[s]),
                pltpu.make_async_copy(w3_hbm.at[layer, e], w3f.at[s], sems.at[s]),
                pltpu.make_async_copy(w2_hbm.at[layer, e], w2f.at[s], sems.at[s]))

    @pl.when(i < nu_ref[0])
    def _():
        @pl.when((fl & 1) != 0)
        def _():
            s = (fl >> 1) & 1

            @pl.when(i == 0)
            def _():
                for c in fetch(be_ref[i], s):
                    c.start()

            for c in fetch(be_ref[i], s):
                c.wait()

            @pl.when((fl & 4) != 0)
            def _():
                for c in fetch(nx_ref[i], 1 - s):
                    c.start()

            w1b[...] = w1f[s].astype(BF16)
            w3b[...] = w3f[s].astype(BF16)
            w2b[...] = w2f[s].astype(BF16)

        x = _unpack_bf16_pairs(x_ref[...])
        h = _silu(_dot(x, w1b[...])) * _dot(x, w3b[...])
        y_ref[...] = _pack_bf16_pairs(_dot(h.astype(BF16), w2b[...]))

    @pl.when(i >= nu_ref[0])
    def _():
        y_ref[...] = jnp.zeros_like(y_ref)


def moe_experts(x_sorted, block_e, n_used, w1, w3, w2, layer):
    n_blocks = block_e.shape[0]
    _, _, d, f = w1.shape
    idx = jnp.arange(n_blocks, dtype=jnp.int32)
    first = (idx == 0) | (block_e != jnp.roll(block_e, 1))
    seg = jnp.cumsum(first.astype(jnp.int32)) - 1
    first_at_or_after = lax.cummin(jnp.where(first, idx, n_blocks)[::-1])[::-1]
    nxt_idx = jnp.concatenate([first_at_or_after[1:], jnp.full((1,), n_blocks, jnp.int32)])
    has_next = nxt_idx < n_used[0]
    nxt_e = block_e[jnp.minimum(nxt_idx, n_blocks - 1)]
    flags = (first.astype(jnp.int32) | ((seg & 1) << 1) | (has_next.astype(jnp.int32) << 2))
    grid_spec = pltpu.PrefetchScalarGridSpec(
        num_scalar_prefetch=4,
        grid=(n_blocks,),
        in_specs=[pl.BlockSpec((MOE_ROWS, d // 2),
                               lambda i, be, nx, fl, nu: (jnp.where(i < nu[0], i, 0), 0)),
                  pl.BlockSpec(memory_space=pl.ANY),
                  pl.BlockSpec(memory_space=pl.ANY),
                  pl.BlockSpec(memory_space=pl.ANY)],
        out_specs=pl.BlockSpec((MOE_ROWS, d // 2), lambda i, *_: (i, 0)),
        scratch_shapes=[pltpu.VMEM((2, d, f), F32), pltpu.VMEM((2, d, f), F32),
                        pltpu.VMEM((2, f, d), F32),
                        pltpu.VMEM((d, f), BF16), pltpu.VMEM((d, f), BF16),
                        pltpu.VMEM((f, d), BF16), pltpu.SemaphoreType.DMA((2,))])
    return pl.pallas_call(
        functools.partial(_gmm_kernel, layer),
        grid_spec=grid_spec,
        out_shape=jax.ShapeDtypeStruct((x_sorted.shape[0], d // 2), jnp.uint32),
        compiler_params=_cparams(("arbitrary",), VMEM_LIMIT),
        name="moe_experts",
    )(block_e, nxt_e, flags, n_used, x_sorted, w1, w3, w2)


def _combine_kernel(tm, n_steps, final, dest_ref, nxt_ref, y_ref, gate_ref, u_ref, ws1_ref, ws3_ref,
                    ws2_ref, x_ref, g2_ref, fg_ref, o_ref, buf_ref, routed_ref, sems):
    i = pl.program_id(0)
    slot = i % 2
    grp = 8

    def gather(d_ref, s, g):
        for tt in range(grp):
            t = g * grp + tt
            for k in range(TOP_K):
                _row_copy(y_ref, d_ref[0, 0, k * tm + t], buf_ref.at[s, k], t, sems.at[s]).start()

    def reduce(g):
        r0 = pl.multiple_of(g * grp, grp)
        gt = gate_ref[pl.ds(r0, grp), :]
        lo = hi = None
        for k in range(TOP_K):
            p = buf_ref[slot, k, pl.ds(r0, grp), :]
            gk = gt[:, k:k + 1]
            lo_k = pltpu.bitcast(p << 16, F32) * gk
            hi_k = pltpu.bitcast(p & jnp.uint32(0xFFFF0000), F32) * gk
            lo = lo_k if lo is None else lo + lo_k
            hi = hi_k if hi is None else hi + hi_k
        routed_ref[pl.ds(r0, grp), :] = jnp.concatenate([lo, hi], axis=1)

    @pl.when(i == 0)
    def _():
        lax.fori_loop(0, tm // grp, lambda g, c: (gather(dest_ref, slot, g), c)[1], 0)

    _wait_rows(y_ref, buf_ref.at[slot, 0], sems.at[slot], tm)

    @pl.when(i + 1 < n_steps)
    def _():
        def both(g, c):
            gather(nxt_ref, 1 - slot, g)
            reduce(g)
            return c
        lax.fori_loop(0, tm // grp, both, 0)

    @pl.when(i + 1 >= n_steps)
    def _():
        lax.fori_loop(0, tm // grp, lambda g, c: (reduce(g), c)[1], 0)

    ub = u_ref[...].astype(BF16)
    hs = _silu(_dot(ub, ws1_ref[...])) * _dot(ub, ws3_ref[...])
    shared = _dot(hs.astype(BF16), ws2_ref[...])
    x = x_ref[...] + g2_ref[0] * (routed_ref[...] + shared)
    o_ref[...] = _rms(x, fg_ref[...]) if final else x


def moe_combine(y_sorted, dest, gate_rep, u, ws1, ws3, ws2, xs, mod, final_g, final,
                tm, seq, n_lat, b, n_tok):
    d = u.shape[1]
    nlt = n_lat // tm
    row = functools.partial(_mod_row, tm=tm, seq=seq, n_lat_tiles=nlt, b=b)
    full = lambda a: pl.BlockSpec(a.shape, lambda i: (0,) * a.ndim)
    tile = pl.BlockSpec((tm, d), lambda i: (i, 0))
    n_steps = n_tok // tm
    return pl.pallas_call(
        functools.partial(_combine_kernel, tm, n_steps, final),
        grid=(n_steps,),
        in_specs=[pl.BlockSpec((1, 1, TOP_K * tm), lambda i: (i, 0, 0), memory_space=pltpu.SMEM),
                  pl.BlockSpec((1, 1, TOP_K * tm),
                               lambda i: (jnp.minimum(i + 1, n_steps - 1), 0, 0),
                               memory_space=pltpu.SMEM),
                  pl.BlockSpec(memory_space=pl.ANY),
                  pl.BlockSpec((tm, TOP_K), lambda i: (i, 0)),
                  tile, full(ws1), full(ws3), full(ws2), tile,
                  pl.BlockSpec((1, 1, d), lambda i: (row(i), 0, 5)),
                  full(final_g)],
        out_specs=tile,
        out_shape=jax.ShapeDtypeStruct((n_tok, d), F32),
        scratch_shapes=[pltpu.VMEM((2, TOP_K, tm, d // 2), jnp.uint32),
                        pltpu.VMEM((tm, d), F32), pltpu.SemaphoreType.DMA((2,))],
        compiler_params=_cparams(("arbitrary",), VMEM_LIMIT),
        name="moe_combine",
    )(dest, dest, y_sorted, gate_rep, u, ws1, ws3, ws2, xs, mod, final_g)


def moe_layer(u, xs, mod, router_w, router_b, w1, w3, w2, layer, ws1, ws3, ws2, final_g, final,
              tm, seq, n_lat, b, n_tok):
    n_exp = router_w.shape[1]
    wr_hi, wr_lo = _split(router_w.T)
    idx, gate, rank, cnt = moe_router(u, wr_hi, wr_lo, router_b, n_tok, tm)
    counts = cnt[:, 0].astype(jnp.int32)
    padded = (counts + MOE_ROWS - 1) // MOE_ROWS * MOE_ROWS
    pad_end = jnp.cumsum(padded)
    pad_start = pad_end - padded
    dest = moe_dest(idx, rank, pad_start, math.gcd(n_tok, 1024))
    n_blocks = -(-(n_tok * TOP_K + n_exp * (MOE_ROWS - 1)) // MOE_ROWS)
    block_start = jnp.arange(n_blocks, dtype=jnp.int32) * MOE_ROWS
    block_e = jnp.minimum(jnp.sum((pad_end[None, :] <= block_start[:, None]).astype(jnp.int32), axis=1),
                          n_exp - 1)
    n_used = (pad_end[-1:] // MOE_ROWS).astype(jnp.int32)
    dest = dest.reshape(TOP_K, n_tok // tm, tm).transpose(1, 0, 2).reshape(n_tok // tm, 1, TOP_K * tm)
    x_sorted = moe_dispatch(u, dest, pad_start + counts, pad_end, n_tok, n_blocks * MOE_ROWS, tm)
    y_sorted = moe_experts(x_sorted, block_e, n_used, w1, w3, w2, layer)
    return moe_combine(y_sorted, dest, gate.T, u, ws1.astype(BF16), ws3.astype(BF16),
                       ws2.astype(BF16), xs, mod, final_g.reshape(1, -1), final,
                       tm, seq, n_lat, b, n_tok)


def _grid_pos_emb(rows, d):
    r, col = jnp.meshgrid(jnp.arange(rows, dtype=F32), jnp.arange(GRID_W, dtype=F32), indexing='ij')
    quarter = d // 4
    omega = 1.0 / (10000.0 ** (jnp.arange(quarter, dtype=F32) / quarter))

    def emb(p):
        a = p.reshape(-1)[:, None] * omega[None, :]
        return jnp.concatenate([jnp.sin(a), jnp.cos(a)], axis=-1)

    return jnp.concatenate([emb(r), emb(col)], axis=-1)


def kernel(x, c, ctx, c_ctx, ada_w, ada_b, norm1_g, norm2_g, w_in, w_out, hy_conv_w, hy_conv_b,
           hy_f_w1, hy_f_b1, hy_f_w2, hy_f_b2, hy_f_w3, hy_skip, gla_wa2, gla_ba2, gla_norm_g,
           router_w, router_b, exp_w1, exp_w3, exp_w2, sh_w1, sh_w3, sh_w2, final_g):
    b, l, d = x.shape
    lc = ctx.shape[1]
    depth = ada_w.shape[0]
    n_lat, n_ctx = b * l, b * lc
    n_all = n_lat + n_ctx
    hy_width = hy_skip.shape[-1]
    hy_cols = hy_conv_w.shape[-1]
    qk = gla_wa2.shape[-1]
    wv = d - hy_width
    g_cols = 2 * qk + 2 * wv
    tm = math.gcd(math.gcd(l, lc), 256)
    tm_proj = math.gcd(math.gcd(l, n_ctx), 512)
    cb = min(hy_width, 256)

    cond = jnp.concatenate([c, c_ctx[None, :]], axis=0)
    n_rows = -(-(b + 1) // 8) * 8
    cond = jnp.concatenate([cond, jnp.zeros((n_rows - b - 1, d), F32)], axis=0)
    mods = ada_table(cond, ada_w, ada_b)

    xs = embed_tokens(x.reshape(n_lat, d), _grid_pos_emb(l // GRID_W, d), ctx.reshape(n_ctx, d), tm)

    tabs = {}
    for length in (l, lc):
        if length not in tabs:
            tabs[length] = _trig_tables(length, True) + _trig_tables(length, False)
    zero_state = jnp.zeros((b, 2, wv, qk), F32)

    for layer in range(depth):
        last = layer == depth - 1
        mod = mods[layer][:, None, :]
        g1 = norm1_g[layer].reshape(1, d)
        g2 = norm2_g[layer].reshape(1, d)
        w_l = w_in[layer].astype(BF16)
        wa2 = gla_wa2[layer]
        rank = wa2.shape[1]
        wa2_bd = jnp.zeros((2 * rank, 2 * qk), F32)
        wa2_bd = wa2_bd.at[:rank, :qk].set(wa2[0]).at[rank:, qk:].set(wa2[1])
        ba2_cat = gla_ba2[layer].reshape(1, 2 * qk)
        zh, zg, la = in_proj(xs, g1, mod, w_l[:, :hy_cols], w_l[:, hy_cols:hy_cols + g_cols],
                             w_l[:, hy_cols + g_cols:], wa2_bd, ba2_cat, tm_proj, l, n_lat, b)

        filt = (hy_f_w1[layer], hy_f_b1[layer], hy_f_w2[layer], hy_f_b2[layer], hy_f_w3[layer],
                hy_skip[layer])
        yc_gla, s_ctx = gla_mixer(zg, la, zero_state, gla_norm_g[layer], lc, b, n_lat)
        y_gla, _ = gla_mixer(zg, la, s_ctx, gla_norm_g[layer], l, b, 0)
        ct, st, c0, s0 = tabs[l]
        gr, gi = hyena_spectra(l, *filt, c0, s0)
        y_hy = hyena_mixer(zh, hy_conv_w[layer], hy_conv_b[layer], ct, st, gr, gi, l, b, 0, cb)
        if last:
            n_tok = n_lat
        else:
            n_tok = n_all
            ct, st, c0, s0 = tabs[lc]
            gr, gi = hyena_spectra(lc, *filt, c0, s0)
            yc_hy = hyena_mixer(zh, hy_conv_w[layer], hy_conv_b[layer], ct, st, gr, gi,
                                lc, b, n_lat, cb)
            y_hy = jnp.concatenate([y_hy, yc_hy], axis=0)
            y_gla = jnp.concatenate([y_gla, yc_gla], axis=0)
        w_o = w_out[layer].astype(BF16)
        xs_mid, u = out_proj(y_hy, y_gla, w_o[:hy_width], w_o[hy_width:], xs, mod, g2,
                             tm, l, n_lat, b, n_tok)
        xs = moe_layer(u, xs_mid, mod, router_w[layer], router_b[layer], exp_w1, exp_w3, exp_w2,
                       layer, sh_w1[layer], sh_w3[layer], sh_w2[layer],
                       final_g, last, tm, l, n_lat, b, n_tok)
    return xs[:n_lat].reshape(b, l, d)
```

```python
import functools
import math

import numpy as np
import jax
import jax.numpy as jnp
from jax import lax
from jax.experimental import pallas as pl
from jax.experimental.pallas import tpu as pltpu

GRID_W = 64
EPS = 1e-6

HY_ORDER = 2
HY_EMB = 33
HY_EMB_PAD = 40
HY_FREQ = 1.0
HY_TARGET = 1e-2
HY_FAST_PCT = 0.3
HY_SLOW_PCT = 1.5
HY_MIN_DECAY = math.log(1.0 / HY_TARGET) / HY_SLOW_PCT
HY_MAX_DECAY = math.log(1.0 / HY_TARGET) / HY_FAST_PCT

GLA_HEADS = 4
GLA_RANK = 16
GLA_TAU = 16.0
GLA_CHUNK = 64
GLA_GROUP = 4
GLA_STATE_UNROLL = 4

TOP_K = 8
N_GROUPS = 8
TOPK_GROUPS = 4
ROUTED_SCALE = 2.5

HY_FREQ_CHUNK = 1024
MOE_ROWS = 512
VMEM_LIMIT = 56 * 1024 * 1024

F32 = jnp.float32
BF16 = jnp.bfloat16


def _cparams(sem, vmem=None):
    return pltpu.CompilerParams(dimension_semantics=sem, vmem_limit_bytes=vmem)


def _split(a):
    hi = a.astype(BF16)
    lo = (a - hi.astype(F32)).astype(BF16)
    return hi, lo


def _dot(a, b):
    return jnp.dot(a, b, preferred_element_type=F32)


def _dot3(a, b):
    ah, al = _split(a)
    bh, bl = _split(b)
    return _dot(ah, bh) + _dot(ah, bl) + _dot(al, bh)


def _silu(x):
    return x * (1.0 / (1.0 + jnp.exp(-x)))


def _rms(x, g):
    return x * lax.rsqrt(jnp.mean(x * x, axis=-1, keepdims=True) + EPS) * g


def _ada_kernel(c_ref, w_ref, b_ref, o_ref):
    o_ref[0] = _dot3(_silu(c_ref[...]), w_ref[0]) + b_ref[0]


def ada_table(cond_rows, ada_w, ada_b):
    depth, d, six_d = ada_w.shape
    r = cond_rows.shape[0]
    tn = 1024
    return pl.pallas_call(
        _ada_kernel,
        grid=(depth, six_d // tn),
        in_specs=[pl.BlockSpec((r, d), lambda l, j: (0, 0)),
                  pl.BlockSpec((1, d, tn), lambda l, j: (l, 0, j)),
                  pl.BlockSpec((1, 1, tn), lambda l, j: (l, 0, j))],
        out_specs=pl.BlockSpec((1, r, tn), lambda l, j: (l, 0, j)),
        out_shape=jax.ShapeDtypeStruct((depth, r, six_d), F32),
        compiler_params=_cparams(("arbitrary", "arbitrary")),
        name="ada_table",
    )(cond_rows, ada_w, ada_b.reshape(depth, 1, six_d))


def _embed_kernel(n_lat_tiles, x_ref, p_ref, c_ref, o_ref):
    i = pl.program_id(0)

    @pl.when(i < n_lat_tiles)
    def _():
        o_ref[...] = x_ref[...] + p_ref[...]

    @pl.when(i >= n_lat_tiles)
    def _():
        o_ref[...] = c_ref[...]


def embed_tokens(x2, pos, ctx2, tm):
    nl, d = x2.shape
    nc = ctx2.shape[0]
    l = pos.shape[0]
    nlt, nct, lt = nl // tm, nc // tm, l // tm
    return pl.pallas_call(
        functools.partial(_embed_kernel, nlt),
        grid=(nlt + nct,),
        in_specs=[pl.BlockSpec((tm, d), lambda i: (jnp.minimum(i, nlt - 1), 0)),
                  pl.BlockSpec((tm, d), lambda i: (i % lt, 0)),
                  pl.BlockSpec((tm, d), lambda i: (jnp.maximum(i - nlt, 0), 0))],
        out_specs=pl.BlockSpec((tm, d), lambda i: (i, 0)),
        out_shape=jax.ShapeDtypeStruct((nl + nc, d), F32),
        compiler_params=_cparams(("arbitrary",)),
        name="embed_tokens",
    )(x2, pos, ctx2)


def _mod_row(i, tm, seq, n_lat_tiles, b):
    return jnp.where(i < n_lat_tiles, (i * tm) // seq, b)


def _inproj_kernel(x_ref, g_ref, sh_ref, sc_ref, wh_ref, wg_ref, wa_ref, wa2_ref, ba2_ref,
                   zh_ref, zg_ref, la_ref):
    x = x_ref[...]
    h = _rms(x, g_ref[...]) * (1.0 + sc_ref[0]) + sh_ref[0]
    hb = h.astype(BF16)
    zh_ref[...] = _dot(hb, wh_ref[...]).astype(BF16)
    zg_ref[...] = _dot(hb, wg_ref[...]).astype(BF16)
    za = _dot(hb, wa_ref[...])
    t = _dot3(za, wa2_ref[...]) + ba2_ref[...]
    la_ref[...] = (jnp.minimum(t, 0.0) - jnp.log(1.0 + jnp.exp(-jnp.abs(t)))) * (1.0 / GLA_TAU)


def in_proj(xs, norm_g, mod, w_hy, w_g, w_a, wa2_bd, ba2_cat, tm, seq, n_lat, b):
    n, d = xs.shape
    nlt = n_lat // tm
    row = functools.partial(_mod_row, tm=tm, seq=seq, n_lat_tiles=nlt, b=b)
    hy_cols, g_cols, la_cols = w_hy.shape[1], w_g.shape[1], wa2_bd.shape[1]
    full = lambda a: pl.BlockSpec(a.shape, lambda i: (0,) * a.ndim)
    return pl.pallas_call(
        _inproj_kernel,
        grid=(n // tm,),
        in_specs=[pl.BlockSpec((tm, d), lambda i: (i, 0)),
                  full(norm_g),
                  pl.BlockSpec((1, 1, d), lambda i: (row(i), 0, 0)),
                  pl.BlockSpec((1, 1, d), lambda i: (row(i), 0, 1)),
                  full(w_hy), full(w_g), full(w_a), full(wa2_bd), full(ba2_cat)],
        out_specs=[pl.BlockSpec((tm, hy_cols), lambda i: (i, 0)),
                   pl.BlockSpec((tm, g_cols), lambda i: (i, 0)),
                   pl.BlockSpec((tm, la_cols), lambda i: (i, 0))],
        out_shape=[jax.ShapeDtypeStruct((n, hy_cols), BF16),
                   jax.ShapeDtypeStruct((n, g_cols), BF16),
                   jax.ShapeDtypeStruct((n, la_cols), F32)],
        compiler_params=_cparams(("arbitrary",), VMEM_LIMIT),
        name="in_proj",
    )(xs, norm_g, mod, mod, w_hy, w_g, w_a, wa2_bd, ba2_cat)


def _trig_tables(l, half_shift):
    k = np.arange(l, dtype=np.int64)[:, None]
    n = np.arange(l, dtype=np.int64)[None, :]
    m = ((2 * k + 1) * (2 * n + (1 if half_shift else 0))) % (8 * l)
    ang = m.astype(np.float64) * (2.0 * math.pi / (8 * l))
    return (jnp.asarray(np.cos(ang).astype(np.float32).astype(BF16)),
            jnp.asarray(np.sin(ang).astype(np.float32).astype(BF16)))


def _filter_kernel(n_orders, width, l, tk,
                   pos_ref, t_ref, dl_ref, w1_ref, b1_ref, w2_ref, b2_ref, w3_ref, skip_ref,
                   c0_ref, s0_ref, gr_ref, gi_ref, hs_ref, hd_ref):
    j = pl.program_id(0)

    @pl.when(j == 0)
    def _():
        hid = jnp.sin(HY_FREQ * (_dot3(pos_ref[...], w1_ref[...]) + b1_ref[...]))
        hid = jnp.sin(HY_FREQ * (_dot3(hid, w2_ref[...]) + b2_ref[...]))
        h = _dot3(hid, w3_ref[...])
        decay = jnp.exp(-t_ref[...] * dl_ref[...])
        for o in range(n_orders):
            hf = h[:, (2 * o) * width:(2 * o + 1) * width] * decay
            hb = h[:, (2 * o + 1) * width:(2 * o + 2) * width] * decay
            hs_ref[:, o * width:(o + 1) * width] = (hf + hb).astype(BF16)
            hd_ref[:, o * width:(o + 1) * width] = (hb - hf).astype(BF16)

    scale = 1.0 / l
    gr = _dot(c0_ref[...], hs_ref[...])
    gi = _dot(s0_ref[...], hd_ref[...])
    for o in range(n_orders):
        gr_ref[o] = (gr[:, o * width:(o + 1) * width] + skip_ref[o]) * scale
        gi_ref[o] = gi[:, o * width:(o + 1) * width] * scale


def hyena_spectra(l, w1, b1, w2, b2, w3, skip, c0, s0):
    width = skip.shape[-1]
    n_orders = skip.shape[0]
    t = jnp.linspace(0.0, 1.0, l, dtype=F32)[:, None]
    bands = (HY_EMB - 1) // 2
    w = 2.0 * math.pi * jnp.arange(l, dtype=F32)[:, None] / l
    f = jnp.linspace(1e-4, bands - 1, bands, dtype=F32)[None, :]
    pos = jnp.concatenate([t, jnp.cos(f * w), -jnp.sin(f * w),
                           jnp.zeros((l, HY_EMB_PAD - HY_EMB), F32)], axis=-1)
    w1p = jnp.concatenate([w1, jnp.zeros((HY_EMB_PAD - HY_EMB, w1.shape[1]), F32)], axis=0)
    deltas = jnp.linspace(HY_MIN_DECAY, HY_MAX_DECAY, width, dtype=F32)[None, :]
    tk = min(l, 256)
    full = lambda a: pl.BlockSpec(a.shape, lambda j: (0,) * a.ndim)
    skip3 = skip.reshape(n_orders, 1, width)
    args = (pos, t, deltas, w1p, b1.reshape(1, -1), w2, b2.reshape(1, -1), w3, skip3)
    return pl.pallas_call(
        functools.partial(_filter_kernel, n_orders, width, l, tk),
        grid=(l // tk,),
        in_specs=[full(a) for a in args] + [pl.BlockSpec((tk, l), lambda j: (j, 0)),
                                            pl.BlockSpec((tk, l), lambda j: (j, 0))],
        out_specs=[pl.BlockSpec((n_orders, tk, width), lambda j: (0, j, 0)),
                   pl.BlockSpec((n_orders, tk, width), lambda j: (0, j, 0))],
        out_shape=[jax.ShapeDtypeStruct((n_orders, l, width), F32)] * 2,
        scratch_shapes=[pltpu.VMEM((l, n_orders * width), BF16),
                        pltpu.VMEM((l, n_orders * width), BF16)],
        compiler_params=_cparams(("arbitrary",), VMEM_LIMIT),
        name="hyena_spectra",
    )(*args, c0, s0)


def _hyena_kernel(l, fc, zv_ref, z1_ref, z2_ref, wv_ref, w1_ref, w2_ref, bv_ref, b1_ref, b2_ref,
                  c_ref, s_ref, gr_ref, gi_ref, y_ref, ub_ref, a_ref, b_ref, x1_ref, x2_ref):
    rows = lax.broadcasted_iota(jnp.int32, (l, 1), 0)

    def conv3(z_ref, w_ref, bias_ref):
        z = z_ref[...].astype(F32)
        zm = jnp.where(rows == 0, 0.0, pltpu.roll(z, 1, 0))
        zp = jnp.where(rows == l - 1, 0.0, pltpu.roll(z, l - 1, 0))
        return zm * w_ref[0:1, :] + z * w_ref[1:2, :] + zp * w_ref[2:3, :] + bias_ref[...]

    ub_ref[...] = conv3(zv_ref, wv_ref, bv_ref).astype(BF16)
    x1_ref[...] = conv3(z1_ref, w1_ref, b1_ref)
    x2_ref[...] = conv3(z2_ref, w2_ref, b2_ref)
    n = l // fc

    def forward(o):
        def body(c, carry):
            r0 = pl.multiple_of(c * fc, fc)
            ur = _dot(c_ref[pl.ds(r0, fc), :], ub_ref[...])
            us = _dot(s_ref[pl.ds(r0, fc), :], ub_ref[...])
            gr = gr_ref[o, pl.ds(r0, fc), :]
            gi = gi_ref[o, pl.ds(r0, fc), :]
            a_ref[pl.ds(r0, fc), :] = (ur * gr + us * gi).astype(BF16)
            b_ref[pl.ds(r0, fc), :] = (us * gr - ur * gi).astype(BF16)
            return carry
        lax.fori_loop(0, n, body, 0)

    def inverse(gate_ref, dst_ref):
        def body(c, carry):
            r0 = pl.multiple_of(c * fc, fc)
            lc = (_dot(c_ref[pl.ds(r0, fc), :], a_ref[...])
                  + _dot(s_ref[pl.ds(r0, fc), :], b_ref[...]))
            dst_ref[pl.ds(r0, fc), :] = (gate_ref[pl.ds(r0, fc), :] * lc).astype(BF16)
            return carry
        lax.fori_loop(0, n, body, 0)

    forward(0)
    inverse(x1_ref, ub_ref)
    forward(1)
    inverse(x2_ref, y_ref)


def hyena_mixer(zh, conv_w, conv_b, ctab, stab, gr, gi, l, n_seq, row0, cb):
    width = gr.shape[-1]
    n_orders = gr.shape[0]
    ncb = width // cb
    sb0 = row0 // l
    conv_b2 = conv_b.reshape(1, -1)
    zspec = lambda part: pl.BlockSpec((l, cb), lambda c, s: (sb0 + s, part * ncb + c))
    wspec = lambda part: pl.BlockSpec((3, cb), lambda c, s: (0, part * ncb + c))
    bspec = lambda part: pl.BlockSpec((1, cb), lambda c, s: (0, part * ncb + c))
    once = pl.Buffered(1)
    tspec = pl.BlockSpec((l, l), lambda c, s: (0, 0), pipeline_mode=once)
    gspec = pl.BlockSpec((n_orders, l, cb), lambda c, s: (0, 0, c), pipeline_mode=once)
    return pl.pallas_call(
        functools.partial(_hyena_kernel, l, min(l, HY_FREQ_CHUNK)),
        grid=(ncb, n_seq),
        in_specs=[zspec(0), zspec(1), zspec(2), wspec(0), wspec(1), wspec(2),
                  bspec(0), bspec(1), bspec(2), tspec, tspec, gspec, gspec],
        out_specs=pl.BlockSpec((l, cb), lambda c, s: (s, c)),
        out_shape=jax.ShapeDtypeStruct((n_seq * l, width), BF16),
        scratch_shapes=[pltpu.VMEM((l, cb), BF16), pltpu.VMEM((l, cb), BF16),
                        pltpu.VMEM((l, cb), BF16), pltpu.VMEM((l, cb), F32),
                        pltpu.VMEM((l, cb), F32)],
        compiler_params=_cparams(("arbitrary", "arbitrary"), VMEM_LIMIT),
        name="hyena_mixer",
    )(zh, zh, zh, conv_w, conv_w, conv_w, conv_b2, conv_b2, conv_b2, ctab, stab, gr, gi)


def _gla_kernel(l, qk, wv, dk, dv,
                zg_ref, la_ref, s0_ref, g_ref, y_ref, sfin_ref,
                of_ref, ob_ref, qe_ref, ks_ref, dec_ref, st_ref):
    ch = GLA_CHUNK
    n = l // ch
    heads = qk // dk
    grp = min(GLA_GROUP, n)
    rg = grp * ch
    r_i = lax.broadcasted_iota(jnp.int32, (rg, rg), 0)
    c_i = lax.broadcasted_iota(jnp.int32, (rg, rg), 1)
    same = (r_i // ch) == (c_i // ch)
    low = same & (r_i >= c_i)
    upp = same & (r_i <= c_i)
    masks = (low, upp)
    cum2 = tuple(jnp.concatenate([jnp.concatenate([m.astype(BF16)] * 2, axis=1),
                                  jnp.concatenate([same.astype(BF16)] * 2, axis=1)], axis=0)
                 for m in masks)
    lane_head = lax.broadcasted_iota(jnp.int32, (1, qk), 1) // dk
    scale = dk ** -0.5
    o_refs = (of_ref, ob_ref)

    def local(g, d):
        r0 = pl.multiple_of(g * rg, rg)
        q = zg_ref[pl.ds(r0, rg), 0:qk].astype(F32) * scale
        k = zg_ref[pl.ds(r0, rg), qk:2 * qk].astype(F32)
        v = zg_ref[pl.ds(r0, rg), 2 * qk:2 * qk + wv]
        la = la_ref[pl.ds(r0, rg), d * qk:(d + 1) * qk]
        lh, ll = _split(la)
        bt = _dot(cum2[d], jnp.concatenate([lh, ll], axis=0))
        b, tot = bt[:rg], bt[rg:]
        qe = (q * jnp.exp(b)).astype(BF16)
        ke = (k * jnp.exp(-b)).astype(BF16)
        qe_ref[d, pl.ds(r0, rg), :] = qe
        ks_ref[d, pl.ds(r0, rg), :] = (k * jnp.exp(tot - b)).astype(BF16)
        dec = jnp.exp(tot)
        for c in range(grp):
            dec_ref[d, pl.ds(pl.multiple_of((g * grp + c) * 8, 8), 8), :] = dec[c * ch:c * ch + 8]
        parts = []
        for h in range(heads):
            qh = jnp.where(lane_head == h, qe, jnp.zeros_like(qe))
            att = lax.dot_general(qh, ke, (((1,), (1,)), ((), ())), preferred_element_type=F32)
            att = jnp.where(masks[d], att, 0.0).astype(BF16)
            parts.append(_dot(att, v[:, h * dv:(h + 1) * dv]))
        o_refs[d][pl.ds(r0, rg), :] = jnp.concatenate(parts, axis=1)

    def local_body(g, carry):
        local(g, 0)
        local(g, 1)
        return carry

    lax.fori_loop(0, n // grp, local_body, 0)

    bd = (lax.broadcasted_iota(jnp.int32, (wv, qk), 0) // dv
          == lax.broadcasted_iota(jnp.int32, (wv, qk), 1) // dk).astype(F32)
    st_ref[...] = s0_ref[0]

    def carry_state(c, d):
        r0 = pl.multiple_of(c * ch, ch)
        st = st_ref[d]
        o = lax.dot_general(qe_ref[d, pl.ds(r0, ch), :], st.astype(BF16), (((1,), (1,)), ((), ())),
                            preferred_element_type=F32)
        o_refs[d][pl.ds(r0, ch), :] += o
        v = zg_ref[pl.ds(r0, ch), 2 * qk:2 * qk + wv]
        upd = lax.dot_general(v, ks_ref[d, pl.ds(r0, ch), :], (((0,), (0,)), ((), ())),
                              preferred_element_type=F32)
        dec = dec_ref[d, pl.ds(pl.multiple_of(c * 8, 8), 1), :]
        st_ref[d] = st * dec + upd * bd

    su = min(GLA_STATE_UNROLL, n)

    def state_body(i, carry):
        for u in range(su):
            carry_state(i * su + u, 0)
            carry_state(n - 1 - (i * su + u), 1)
        return carry

    lax.fori_loop(0, n // su, state_body, 0)
    sfin_ref[0] = st_ref[...]

    def finish(i, carry):
        r0 = pl.multiple_of(i * ch, ch)
        o = of_ref[pl.ds(r0, ch), :] + ob_ref[pl.ds(r0, ch), :]
        r = zg_ref[pl.ds(r0, ch), 2 * qk + wv:2 * qk + 2 * wv].astype(F32)
        parts = [_rms(o[:, h * dv:(h + 1) * dv], g_ref[...]) for h in range(heads)]
        y_ref[pl.ds(r0, ch), :] = (jnp.concatenate(parts, axis=1) * _silu(r)).astype(BF16)
        return carry

    lax.fori_loop(0, n, finish, 0)


def gla_mixer(zg, la, s0, norm_g, l, n_seq, row0):
    qk = la.shape[1] // 2
    wv = (zg.shape[1] - 2 * qk) // 2
    dk, dv = qk // GLA_HEADS, wv // GLA_HEADS
    sb0 = row0 // l
    return pl.pallas_call(
        functools.partial(_gla_kernel, l, qk, wv, dk, dv),
        grid=(n_seq,),
        in_specs=[pl.BlockSpec((l, zg.shape[1]), lambda s: (sb0 + s, 0)),
                  pl.BlockSpec((l, la.shape[1]), lambda s: (sb0 + s, 0)),
                  pl.BlockSpec((1, 2, wv, qk), lambda s: (s, 0, 0, 0)),
                  pl.BlockSpec((1, dv), lambda s: (0, 0))],
        out_specs=[pl.BlockSpec((l, wv), lambda s: (s, 0)),
                   pl.BlockSpec((1, 2, wv, qk), lambda s: (s, 0, 0, 0))],
        out_shape=[jax.ShapeDtypeStruct((n_seq * l, wv), BF16),
                   jax.ShapeDtypeStruct((n_seq, 2, wv, qk), F32)],
        scratch_shapes=[pltpu.VMEM((l, wv), F32), pltpu.VMEM((l, wv), F32),
                        pltpu.VMEM((2, l, qk), BF16), pltpu.VMEM((2, l, qk), BF16),
                        pltpu.VMEM((2, l // GLA_CHUNK * 8, qk), F32),
                        pltpu.VMEM((2, wv, qk), F32)],
        compiler_params=_cparams(("arbitrary",), VMEM_LIMIT),
        name="gla_mixer",
    )(zg, la, s0, norm_g.reshape(1, dv))


def _outproj_kernel(yh_ref, yg_ref, wh_ref, wg_ref, x_ref, g1_ref, g_ref, sh_ref, sc_ref,
                    xo_ref, u_ref):
    delta = _dot(yh_ref[...], wh_ref[...]) + _dot(yg_ref[...], wg_ref[...])
    x = x_ref[...] + g1_ref[0] * delta
    xo_ref[...] = x
    u_ref[...] = _rms(x, g_ref[...]) * (1.0 + sc_ref[0]) + sh_ref[0]


def out_proj(y_hy, y_gla, w_hy, w_gla, xs, mod, norm_g, tm, seq, n_lat, b, n_tok):
    d = xs.shape[1]
    nlt = n_lat // tm
    row = functools.partial(_mod_row, tm=tm, seq=seq, n_lat_tiles=nlt, b=b)
    full = lambda a: pl.BlockSpec(a.shape, lambda i: (0,) * a.ndim)
    modspec = lambda col: pl.BlockSpec((1, 1, d), lambda i: (row(i), 0, col))
    tile = lambda w: pl.BlockSpec((tm, w), lambda i: (i, 0))
    return pl.pallas_call(
        _outproj_kernel,
        grid=(n_tok // tm,),
        in_specs=[tile(y_hy.shape[1]), tile(y_gla.shape[1]), full(w_hy), full(w_gla), tile(d),
                  modspec(2), full(norm_g), modspec(3), modspec(4)],
        out_specs=[tile(d), tile(d)],
        out_shape=[jax.ShapeDtypeStruct((n_tok, d), F32)] * 2,
        compiler_params=_cparams(("arbitrary",), VMEM_LIMIT),
        name="out_proj",
    )(y_hy, y_gla, w_hy, w_gla, xs, mod, norm_g, mod, mod)


def _router_kernel(n_exp, tm, u_ref, wh_ref, wl_ref, b_ref, tri_ref,
                   idx_ref, gate_ref, rank_ref, cnt_ref):
    i = pl.program_id(0)

    @pl.when(i == 0)
    def _():
        cnt_ref[...] = jnp.zeros_like(cnt_ref)

    uh, ul = _split(u_ref[...])
    nt = (((1,), (1,)), ((), ()))
    logits = (lax.dot_general(wh_ref[...], uh, nt, preferred_element_type=F32)
              + lax.dot_general(wh_ref[...], ul, nt, preferred_element_type=F32)
              + lax.dot_general(wl_ref[...], uh, nt, preferred_element_type=F32))
    s = 1.0 / (1.0 + jnp.exp(-logits))
    sel = s + b_ref[...]
    gsz = n_exp // N_GROUPS
    neg = -jnp.inf
    rows_g = lax.broadcasted_iota(jnp.int32, (gsz, tm), 0)
    blocks, gscore = [], []
    for g in range(N_GROUPS):
        blk = sel[g * gsz:(g + 1) * gsz, :]
        m1 = jnp.max(blk, axis=0, keepdims=True)
        i1 = jnp.min(jnp.where(blk == m1, rows_g, gsz), axis=0, keepdims=True)
        m2 = jnp.max(jnp.where(rows_g == i1, neg, blk), axis=0, keepdims=True)
        blocks.append(blk)
        gscore.append(m1 + m2)
    masked = []
    for g in range(N_GROUPS):
        beaten = jnp.zeros((1, tm), jnp.int32)
        for h in range(N_GROUPS):
            if h == g:
                continue
            wins = (gscore[h] > gscore[g]) | ((gscore[h] == gscore[g]) & (h < g))
            beaten = beaten + wins.astype(jnp.int32)
        masked.append(jnp.where(beaten < TOPK_GROUPS, blocks[g], neg))
    sel = jnp.concatenate(masked, axis=0)

    rows = lax.broadcasted_iota(jnp.int32, (n_exp, tm), 0)
    chosen = jnp.zeros((n_exp, tm), F32)
    idxs, gates = [], []
    for _ in range(TOP_K):
        m = jnp.max(sel, axis=0, keepdims=True)
        ik = jnp.min(jnp.where(sel == m, rows, n_exp), axis=0, keepdims=True)
        hit = rows == ik
        gates.append(jnp.sum(jnp.where(hit, s, 0.0), axis=0, keepdims=True))
        sel = jnp.where(hit, neg, sel)
        chosen = jnp.where(hit, 1.0, chosen)
        idxs.append(ik)
    gsum = gates[0]
    for g in gates[1:]:
        gsum = gsum + g
    inv = ROUTED_SCALE / gsum
    before = _dot(chosen.astype(BF16), tri_ref[...]) + cnt_ref[...]
    for k in range(TOP_K):
        idx_ref[k:k + 1, :] = idxs[k]
        gate_ref[k:k + 1, :] = gates[k] * inv
        rk = jnp.sum(jnp.where(rows == idxs[k], before, 0.0), axis=0, keepdims=True)
        rank_ref[k:k + 1, :] = rk.astype(jnp.int32)
    cnt_ref[...] += jnp.sum(chosen, axis=1, keepdims=True)


def moe_router(u, wr_hi, wr_lo, b_r, n_tok, tm):
    n_exp, d = wr_hi.shape
    tri = (lax.broadcasted_iota(jnp.int32, (tm, tm), 0)
           < lax.broadcasted_iota(jnp.int32, (tm, tm), 1)).astype(BF16)
    full = lambda a: pl.BlockSpec(a.shape, lambda i: (0,) * a.ndim)
    tok = pl.BlockSpec((TOP_K, tm), lambda i: (0, i))
    return pl.pallas_call(
        functools.partial(_router_kernel, n_exp, tm),
        grid=(n_tok // tm,),
        in_specs=[pl.BlockSpec((tm, d), lambda i: (i, 0)), full(wr_hi), full(wr_lo),
                  pl.BlockSpec((n_exp, 1), lambda i: (0, 0)), full(tri)],
        out_specs=[tok, tok, tok, pl.BlockSpec((n_exp, 1), lambda i: (0, 0))],
        out_shape=[jax.ShapeDtypeStruct((TOP_K, n_tok), jnp.int32),
                   jax.ShapeDtypeStruct((TOP_K, n_tok), F32),
                   jax.ShapeDtypeStruct((TOP_K, n_tok), jnp.int32),
                   jax.ShapeDtypeStruct((n_exp, 1), F32)],
        compiler_params=_cparams(("arbitrary",), VMEM_LIMIT),
        name="moe_router",
    )(u, wr_hi, wr_lo, b_r.reshape(n_exp, 1), tri)


def _start_row_copy(src, src_row, dst, dst_row, sem, queue):
    pltpu.async_copy(src.at[pl.ds(src_row, 1)], dst.at[pl.ds(dst_row, 1)], sem, priority=queue)


def _pack_bf16_pairs(x):
    w = x.shape[1] // 2
    lo = pltpu.bitcast(x[:, :w].astype(BF16).astype(F32), jnp.uint32) >> 16
    hi = pltpu.bitcast(x[:, w:].astype(BF16).astype(F32), jnp.uint32)
    return hi | lo


def _unpack_bf16_pairs(p):
    lo = pltpu.bitcast(p << 16, F32)
    hi = pltpu.bitcast(p & jnp.uint32(0xFFFF0000), F32)
    return jnp.concatenate([lo, hi], axis=1).astype(BF16)


def _dest_kernel(n_exp, tm, idx_ref, rank_ref, ps_ref, dest_ref):
    rows = lax.broadcasted_iota(jnp.int32, (n_exp, tm), 0)
    ps = ps_ref[...]
    for k in range(TOP_K):
        base = jnp.sum(jnp.where(rows == idx_ref[k:k + 1, :], ps, 0), axis=0, keepdims=True)
        dest_ref[k:k + 1, :] = base + rank_ref[k:k + 1, :]


def moe_dest(idx, rank, pad_start, tm):
    n_tok = idx.shape[1]
    n_exp = pad_start.shape[0]
    tok = pl.BlockSpec((TOP_K, tm), lambda i: (0, i))
    return pl.pallas_call(
        functools.partial(_dest_kernel, n_exp, tm),
        grid=(n_tok // tm,),
        in_specs=[tok, tok, pl.BlockSpec((n_exp, 1), lambda i: (0, 0))],
        out_specs=tok,
        out_shape=jax.ShapeDtypeStruct((TOP_K, n_tok), jnp.int32),
        compiler_params=_cparams(("arbitrary",)),
        name="moe_dest",
    )(idx, rank, pad_start.reshape(n_exp, 1))


def _wait_rows(src, dst, sem, tm):
    for _ in range(TOP_K):
        pltpu.make_async_copy(src.at[pl.ds(0, tm)], dst.at[pl.ds(0, tm)], sem).wait()


ZERO_ROWS = MOE_ROWS // 2


def _zero_padding(n_exp, ps_ref, pe_ref, xs_ref, z_ref, sem, wait):
    def go(copy):
        copy.wait() if wait else copy.start()

    def per_expert(e, carry):
        s = ps_ref[e]
        head = (-s) & 7
        for j in range(7):
            @pl.when(j < head)
            def _():
                go(pltpu.make_async_copy(z_ref.at[pl.ds(0, 1)], xs_ref.at[pl.ds(s + j, 1)], sem))
        s8 = s + head
        m = (pe_ref[e] - s8) >> 3
        bit = ZERO_ROWS // 8
        while bit:
            rows = bit * 8
            start = pl.multiple_of(s8 + (m & ~(2 * bit - 1)) * 8, 8)

            @pl.when((m & bit) != 0)
            def _():
                go(pltpu.make_async_copy(z_ref.at[pl.ds(0, rows)], xs_ref.at[pl.ds(start, rows)], sem))
            bit //= 2
        return carry

    lax.fori_loop(0, n_exp, per_expert, 0)

    tail = pe_ref[n_exp - 1]

    def per_chunk(j, carry):
        start = pl.multiple_of(tail + j * ZERO_ROWS, ZERO_ROWS)
        go(pltpu.make_async_copy(z_ref, xs_ref.at[pl.ds(start, ZERO_ROWS)], sem))
        return carry

    lax.fori_loop(0, (xs_ref.shape[0] - tail) // ZERO_ROWS, per_chunk, 0)


def _dispatch_kernel(tm, n_steps, n_exp, ps_ref, pe_ref, dest_ref, u_ref, xs_ref, pk_ref, z_ref,
                     sems, zsem):
    i = pl.program_id(0)
    slot = i % 2
    pk = pk_ref.at[slot]
    sem = sems.at[slot]

    @pl.when(i == 0)
    def _():
        z_ref[...] = jnp.zeros_like(z_ref)
        _zero_padding(n_exp, ps_ref, pe_ref, xs_ref, z_ref, zsem, wait=False)

    @pl.when(i >= 2)
    def _():
        _wait_rows(pk, xs_ref, sem, tm)

    pk[...] = _pack_bf16_pairs(u_ref[...])

    def issue(t, carry):
        for k in range(TOP_K):
            _start_row_copy(pk, t, xs_ref, dest_ref[0, 0, k * tm + t], sem, k % 2)
        return carry

    lax.fori_loop(0, tm, issue, 0)

    @pl.when(i == n_steps - 1)
    def _():
        if n_steps > 1:
            _wait_rows(pk_ref.at[1 - slot], xs_ref, sems.at[1 - slot], tm)
        _wait_rows(pk, xs_ref, sem, tm)
        _zero_padding(n_exp, ps_ref, pe_ref, xs_ref, z_ref, zsem, wait=True)


def moe_dispatch(u, dest, pad_from, pad_end, n_tok, n_slots, tm):
    d = u.shape[1]
    n_exp = pad_end.shape[0]
    grid_spec = pltpu.PrefetchScalarGridSpec(
        num_scalar_prefetch=2,
        grid=(n_tok // tm,),
        in_specs=[pl.BlockSpec((1, 1, TOP_K * tm), lambda i, *_: (i, 0, 0), memory_space=pltpu.SMEM),
                  pl.BlockSpec((tm, d), lambda i, *_: (i, 0))],
        out_specs=pl.BlockSpec(memory_space=pl.ANY),
        scratch_shapes=[pltpu.VMEM((2, tm, d // 2), jnp.uint32),
                        pltpu.VMEM((ZERO_ROWS, d // 2), jnp.uint32),
                        pltpu.SemaphoreType.DMA((2,)), pltpu.SemaphoreType.DMA])
    return pl.pallas_call(
        functools.partial(_dispatch_kernel, tm, n_tok // tm, n_exp),
        grid_spec=grid_spec,
        out_shape=jax.ShapeDtypeStruct((n_slots, d // 2), jnp.uint32),
        compiler_params=_cparams(("arbitrary",)),
        name="moe_dispatch",
    )(pad_from, pad_end, dest, u)


def _gmm_kernel(layer, be_ref, nx_ref, fl_ref, nu_ref, x_ref, w1_hbm, w3_hbm, w2_hbm, y_ref,
                w1f, w3f, w2f, w1b, w3b, w2b, sems):
    i = pl.program_id(0)
    fl = fl_ref[i]

    def fetch(e, s):
        return (pltpu.make_async_copy(w1_hbm.at[layer, e], w1f.at[s], sems.at[s]),
                pltpu.make_async_copy(w3_hbm.at[layer, e], w3f.at[s], sems.at[s]),
                pltpu.make_async_copy(w2_hbm.at[layer, e], w2f.at[s], sems.at[s]))

    @pl.when(i < nu_ref[0])
    def _():
        @pl.when((fl & 1) != 0)
        def _():
            s = (fl >> 1) & 1

            @pl.when(i == 0)
            def _():
                for c in fetch(be_ref[i], s):
                    c.start()

            for c in fetch(be_ref[i], s):
                c.wait()

            @pl.when((fl & 4) != 0)
            def _():
                for c in fetch(nx_ref[i], 1 - s):
                    c.start()

            w1b[...] = w1f[s].astype(BF16)
            w3b[...] = w3f[s].astype(BF16)
            w2b[...] = w2f[s].astype(BF16)

        x = _unpack_bf16_pairs(x_ref[...])
        h = _silu(_dot(x, w1b[...])) * _dot(x, w3b[...])
        y_ref[...] = _pack_bf16_pairs(_dot(h.astype(BF16), w2b[...]))

    @pl.when(i >= nu_ref[0])
    def _():
        y_ref[...] = jnp.zeros_like(y_ref)


def moe_experts(x_sorted, block_e, n_used, w1, w3, w2, layer):
    n_blocks = block_e.shape[0]
    _, _, d, f = w1.shape
    idx = jnp.arange(n_blocks, dtype=jnp.int32)
    first = (idx == 0) | (block_e != jnp.roll(block_e, 1))
    seg = jnp.cumsum(first.astype(jnp.int32)) - 1
    first_at_or_after = lax.cummin(jnp.where(first, idx, n_blocks)[::-1])[::-1]
    nxt_idx = jnp.concatenate([first_at_or_after[1:], jnp.full((1,), n_blocks, jnp.int32)])
    has_next = nxt_idx < n_used[0]
    nxt_e = block_e[jnp.minimum(nxt_idx, n_blocks - 1)]
    flags = (first.astype(jnp.int32) | ((seg & 1) << 1) | (has_next.astype(jnp.int32) << 2))
    grid_spec = pltpu.PrefetchScalarGridSpec(
        num_scalar_prefetch=4,
        grid=(n_blocks,),
        in_specs=[pl.BlockSpec((MOE_ROWS, d // 2),
                               lambda i, be, nx, fl, nu: (jnp.where(i < nu[0], i, 0), 0)),
                  pl.BlockSpec(memory_space=pl.ANY),
                  pl.BlockSpec(memory_space=pl.ANY),
                  pl.BlockSpec(memory_space=pl.ANY)],
        out_specs=pl.BlockSpec((MOE_ROWS, d // 2), lambda i, *_: (i, 0)),
        scratch_shapes=[pltpu.VMEM((2, d, f), F32), pltpu.VMEM((2, d, f), F32),
                        pltpu.VMEM((2, f, d), F32),
                        pltpu.VMEM((d, f), BF16), pltpu.VMEM((d, f), BF16),
                        pltpu.VMEM((f, d), BF16), pltpu.SemaphoreType.DMA((2,))])
    return pl.pallas_call(
        functools.partial(_gmm_kernel, layer),
        grid_spec=grid_spec,
        out_shape=jax.ShapeDtypeStruct((x_sorted.shape[0], d // 2), jnp.uint32),
        compiler_params=_cparams(("arbitrary",), VMEM_LIMIT),
        name="moe_experts",
    )(block_e, nxt_e, flags, n_used, x_sorted, w1, w3, w2)


def _combine_kernel(tm, n_steps, final, dest_ref, nxt_ref, y_ref, gate_ref, u_ref, ws1_ref, ws3_ref,
                    ws2_ref, x_ref, g2_ref, fg_ref, o_ref, buf_ref, routed_ref, sems):
    i = pl.program_id(0)
    slot = i % 2
    grp = 8

    def gather(d_ref, s, g):
        for tt in range(grp):
            t = g * grp + tt
            for k in range(TOP_K):
                _start_row_copy(y_ref, d_ref[0, 0, k * tm + t], buf_ref.at[s, k], t, sems.at[s], k % 2)

    def reduce(g):
        r0 = pl.multiple_of(g * grp, grp)
        gt = gate_ref[pl.ds(r0, grp), :]
        lo = hi = None
        for k in range(TOP_K):
            p = buf_ref[slot, k, pl.ds(r0, grp), :]
            gk = gt[:, k:k + 1]
            lo_k = pltpu.bitcast(p << 16, F32) * gk
            hi_k = pltpu.bitcast(p & jnp.uint32(0xFFFF0000), F32) * gk
            lo = lo_k if lo is None else lo + lo_k
            hi = hi_k if hi is None else hi + hi_k
        routed_ref[pl.ds(r0, grp), :] = jnp.concatenate([lo, hi], axis=1)

    @pl.when(i == 0)
    def _():
        lax.fori_loop(0, tm // grp, lambda g, c: (gather(dest_ref, slot, g), c)[1], 0)

    _wait_rows(y_ref, buf_ref.at[slot, 0], sems.at[slot], tm)

    @pl.when(i + 1 < n_steps)
    def _():
        def both(g, c):
            gather(nxt_ref, 1 - slot, g)
            reduce(g)
            return c
        lax.fori_loop(0, tm // grp, both, 0)

    @pl.when(i + 1 >= n_steps)
    def _():
        lax.fori_loop(0, tm // grp, lambda g, c: (reduce(g), c)[1], 0)

    ub = u_ref[...].astype(BF16)
    hs = _silu(_dot(ub, ws1_ref[...])) * _dot(ub, ws3_ref[...])
    shared = _dot(hs.astype(BF16), ws2_ref[...])
    x = x_ref[...] + g2_ref[0] * (routed_ref[...] + shared)
    o_ref[...] = _rms(x, fg_ref[...]) if final else x


def moe_combine(y_sorted, dest, gate_rep, u, ws1, ws3, ws2, xs, mod, final_g, final,
                tm, seq, n_lat, b, n_tok):
    d = u.shape[1]
    nlt = n_lat // tm
    row = functools.partial(_mod_row, tm=tm, seq=seq, n_lat_tiles=nlt, b=b)
    full = lambda a: pl.BlockSpec(a.shape, lambda i: (0,) * a.ndim)
    tile = pl.BlockSpec((tm, d), lambda i: (i, 0))
    n_steps = n_tok // tm
    return pl.pallas_call(
        functools.partial(_combine_kernel, tm, n_steps, final),
        grid=(n_steps,),
        in_specs=[pl.BlockSpec((1, 1, TOP_K * tm), lambda i: (i, 0, 0), memory_space=pltpu.SMEM),
                  pl.BlockSpec((1, 1, TOP_K * tm),
                               lambda i: (jnp.minimum(i + 1, n_steps - 1), 0, 0),
                               memory_space=pltpu.SMEM),
                  pl.BlockSpec(memory_space=pl.ANY),
                  pl.BlockSpec((tm, TOP_K), lambda i: (i, 0)),
                  tile, full(ws1), full(ws3), full(ws2), tile,
                  pl.BlockSpec((1, 1, d), lambda i: (row(i), 0, 5)),
                  full(final_g)],
        out_specs=tile,
        out_shape=jax.ShapeDtypeStruct((n_tok, d), F32),
        scratch_shapes=[pltpu.VMEM((2, TOP_K, tm, d // 2), jnp.uint32),
                        pltpu.VMEM((tm, d), F32), pltpu.SemaphoreType.DMA((2,))],
        compiler_params=_cparams(("arbitrary",), VMEM_LIMIT),
        name="moe_combine",
    )(dest, dest, y_sorted, gate_rep, u, ws1, ws3, ws2, xs, mod, final_g)


def moe_layer(u, xs, mod, router_w, router_b, w1, w3, w2, layer, ws1, ws3, ws2, final_g, final,
              tm, seq, n_lat, b, n_tok):
    n_exp = router_w.shape[1]
    wr_hi, wr_lo = _split(router_w.T)
    idx, gate, rank, cnt = moe_router(u, wr_hi, wr_lo, router_b, n_tok, tm)
    counts = cnt[:, 0].astype(jnp.int32)
    padded = (counts + MOE_ROWS - 1) // MOE_ROWS * MOE_ROWS
    pad_end = jnp.cumsum(padded)
    pad_start = pad_end - padded
    dest = moe_dest(idx, rank, pad_start, math.gcd(n_tok, 1024))
    n_blocks = -(-(n_tok * TOP_K + n_exp * (MOE_ROWS - 1)) // MOE_ROWS)
    block_start = jnp.arange(n_blocks, dtype=jnp.int32) * MOE_ROWS
    block_e = jnp.minimum(jnp.sum((pad_end[None, :] <= block_start[:, None]).astype(jnp.int32), axis=1),
                          n_exp - 1)
    n_used = (pad_end[-1:] // MOE_ROWS).astype(jnp.int32)
    dest = dest.reshape(TOP_K, n_tok // tm, tm).transpose(1, 0, 2).reshape(n_tok // tm, 1, TOP_K * tm)
    x_sorted = moe_dispatch(u, dest, pad_start + counts, pad_end, n_tok, n_blocks * MOE_ROWS, tm)
    y_sorted = moe_experts(x_sorted, block_e, n_used, w1, w3, w2, layer)
    return moe_combine(y_sorted, dest, gate.T, u, ws1.astype(BF16), ws3.astype(BF16),
                       ws2.astype(BF16), xs, mod, final_g.reshape(1, -1), final,
                       tm, seq, n_lat, b, n_tok)


def _grid_pos_emb(rows, d):
    r, col = jnp.meshgrid(jnp.arange(rows, dtype=F32), jnp.arange(GRID_W, dtype=F32), indexing='ij')
    quarter = d // 4
    omega = 1.0 / (10000.0 ** (jnp.arange(quarter, dtype=F32) / quarter))

    def emb(p):
        a = p.reshape(-1)[:, None] * omega[None, :]
        return jnp.concatenate([jnp.sin(a), jnp.cos(a)], axis=-1)

    return jnp.concatenate([emb(r), emb(col)], axis=-1)


def kernel(x, c, ctx, c_ctx, ada_w, ada_b, norm1_g, norm2_g, w_in, w_out, hy_conv_w, hy_conv_b,
           hy_f_w1, hy_f_b1, hy_f_w2, hy_f_b2, hy_f_w3, hy_skip, gla_wa2, gla_ba2, gla_norm_g,
           router_w, router_b, exp_w1, exp_w3, exp_w2, sh_w1, sh_w3, sh_w2, final_g):
    b, l, d = x.shape
    lc = ctx.shape[1]
    depth = ada_w.shape[0]
    n_lat, n_ctx = b * l, b * lc
    n_all = n_lat + n_ctx
    hy_width = hy_skip.shape[-1]
    hy_cols = hy_conv_w.shape[-1]
    qk = gla_wa2.shape[-1]
    wv = d - hy_width
    g_cols = 2 * qk + 2 * wv
    tm = math.gcd(math.gcd(l, lc), 256)
    tm_proj = math.gcd(math.gcd(l, n_ctx), 512)
    cb = min(hy_width, 256)

    cond = jnp.concatenate([c, c_ctx[None, :]], axis=0)
    n_rows = -(-(b + 1) // 8) * 8
    cond = jnp.concatenate([cond, jnp.zeros((n_rows - b - 1, d), F32)], axis=0)
    mods = ada_table(cond, ada_w, ada_b)

    xs = embed_tokens(x.reshape(n_lat, d), _grid_pos_emb(l // GRID_W, d), ctx.reshape(n_ctx, d), tm)

    tabs = {}
    for length in (l, lc):
        if length not in tabs:
            tabs[length] = _trig_tables(length, True) + _trig_tables(length, False)
    zero_state = jnp.zeros((b, 2, wv, qk), F32)

    for layer in range(depth):
        last = layer == depth - 1
        mod = mods[layer][:, None, :]
        g1 = norm1_g[layer].reshape(1, d)
        g2 = norm2_g[layer].reshape(1, d)
        w_l = w_in[layer].astype(BF16)
        wa2 = gla_wa2[layer]
        rank = wa2.shape[1]
        wa2_bd = jnp.zeros((2 * rank, 2 * qk), F32)
        wa2_bd = wa2_bd.at[:rank, :qk].set(wa2[0]).at[rank:, qk:].set(wa2[1])
        ba2_cat = gla_ba2[layer].reshape(1, 2 * qk)
        zh, zg, la = in_proj(xs, g1, mod, w_l[:, :hy_cols], w_l[:, hy_cols:hy_cols + g_cols],
                             w_l[:, hy_cols + g_cols:], wa2_bd, ba2_cat, tm_proj, l, n_lat, b)

        filt = (hy_f_w1[layer], hy_f_b1[layer], hy_f_w2[layer], hy_f_b2[layer], hy_f_w3[layer],
                hy_skip[layer])
        yc_gla, s_ctx = gla_mixer(zg, la, zero_state, gla_norm_g[layer], lc, b, n_lat)
        y_gla, _ = gla_mixer(zg, la, s_ctx, gla_norm_g[layer], l, b, 0)
        ct, st, c0, s0 = tabs[l]
        gr, gi = hyena_spectra(l, *filt, c0, s0)
        y_hy = hyena_mixer(zh, hy_conv_w[layer], hy_conv_b[layer], ct, st, gr, gi, l, b, 0, cb)
        if last:
            n_tok = n_lat
        else:
            n_tok = n_all
            ct, st, c0, s0 = tabs[lc]
            gr, gi = hyena_spectra(lc, *filt, c0, s0)
            yc_hy = hyena_mixer(zh, hy_conv_w[layer], hy_conv_b[layer], ct, st, gr, gi,
                                lc, b, n_lat, cb)
            y_hy = jnp.concatenate([y_hy, yc_hy], axis=0)
            y_gla = jnp.concatenate([y_gla, yc_gla], axis=0)
        w_o = w_out[layer].astype(BF16)
        xs_mid, u = out_proj(y_hy, y_gla, w_o[:hy_width], w_o[hy_width:], xs, mod, g2,
                             tm, l, n_lat, b, n_tok)
        xs = moe_layer(u, xs_mid, mod, router_w[layer], router_b[layer], exp_w1, exp_w3, exp_w2,
                       layer, sh_w1[layer], sh_w3[layer], sh_w2[layer],
                       final_g, last, tm, l, n_lat, b, n_tok)
    return xs[:n_lat].reshape(b, l, d)
```

```python
import functools
import math

import numpy as np
import jax
import jax.numpy as jnp
from jax import lax
from jax.experimental import pallas as pl
from jax.experimental.pallas import tpu as pltpu

GRID_W = 64
EPS = 1e-6

HY_ORDER = 2
HY_EMB = 33
HY_EMB_PAD = 40
HY_FREQ = 1.0
HY_TARGET = 1e-2
HY_FAST_PCT = 0.3
HY_SLOW_PCT = 1.5
HY_MIN_DECAY = math.log(1.0 / HY_TARGET) / HY_SLOW_PCT
HY_MAX_DECAY = math.log(1.0 / HY_TARGET) / HY_FAST_PCT

GLA_HEADS = 4
GLA_RANK = 16
GLA_TAU = 16.0
GLA_CHUNK = 64
GLA_GROUP = 4
GLA_STATE_UNROLL = 4

TOP_K = 8
N_GROUPS = 8
TOPK_GROUPS = 4
ROUTED_SCALE = 2.5

HY_FREQ_CHUNK = 1024
MOE_ROWS = 512
VMEM_LIMIT = 56 * 1024 * 1024

F32 = jnp.float32
BF16 = jnp.bfloat16


def _cparams(sem, vmem=None):
    return pltpu.CompilerParams(dimension_semantics=sem, vmem_limit_bytes=vmem)


def _split(a):
    hi = a.astype(BF16)
    lo = (a - hi.astype(F32)).astype(BF16)
    return hi, lo


def _dot(a, b):
    return jnp.dot(a, b, preferred_element_type=F32)


def _dot3(a, b):
    ah, al = _split(a)
    bh, bl = _split(b)
    return _dot(ah, bh) + _dot(ah, bl) + _dot(al, bh)


def _silu(x):
    return x * (1.0 / (1.0 + jnp.exp(-x)))


def _rms(x, g):
    return x * lax.rsqrt(jnp.mean(x * x, axis=-1, keepdims=True) + EPS) * g


def _ada_kernel(c_ref, w_ref, b_ref, o_ref):
    o_ref[0] = _dot3(_silu(c_ref[...]), w_ref[0]) + b_ref[0]


def ada_table(cond_rows, ada_w, ada_b):
    depth, d, six_d = ada_w.shape
    r = cond_rows.shape[0]
    tn = 1024
    return pl.pallas_call(
        _ada_kernel,
        grid=(depth, six_d // tn),
        in_specs=[pl.BlockSpec((r, d), lambda l, j: (0, 0)),
                  pl.BlockSpec((1, d, tn), lambda l, j: (l, 0, j)),
                  pl.BlockSpec((1, 1, tn), lambda l, j: (l, 0, j))],
        out_specs=pl.BlockSpec((1, r, tn), lambda l, j: (l, 0, j)),
        out_shape=jax.ShapeDtypeStruct((depth, r, six_d), F32),
        compiler_params=_cparams(("arbitrary", "arbitrary")),
        name="ada_table",
    )(cond_rows, ada_w, ada_b.reshape(depth, 1, six_d))


def _embed_kernel(n_lat_tiles, x_ref, p_ref, c_ref, o_ref):
    i = pl.program_id(0)

    @pl.when(i < n_lat_tiles)
    def _():
        o_ref[...] = x_ref[...] + p_ref[...]

    @pl.when(i >= n_lat_tiles)
    def _():
        o_ref[...] = c_ref[...]


def embed_tokens(x2, pos, ctx2, tm):
    nl, d = x2.shape
    nc = ctx2.shape[0]
    l = pos.shape[0]
    nlt, nct, lt = nl // tm, nc // tm, l // tm
    return pl.pallas_call(
        functools.partial(_embed_kernel, nlt),
        grid=(nlt + nct,),
        in_specs=[pl.BlockSpec((tm, d), lambda i: (jnp.minimum(i, nlt - 1), 0)),
                  pl.BlockSpec((tm, d), lambda i: (i % lt, 0)),
                  pl.BlockSpec((tm, d), lambda i: (jnp.maximum(i - nlt, 0), 0))],
        out_specs=pl.BlockSpec((tm, d), lambda i: (i, 0)),
        out_shape=jax.ShapeDtypeStruct((nl + nc, d), F32),
        compiler_params=_cparams(("arbitrary",)),
        name="embed_tokens",
    )(x2, pos, ctx2)


def _mod_row(i, tm, seq, n_lat_tiles, b):
    return jnp.where(i < n_lat_tiles, (i * tm) // seq, b)


def _inproj_kernel(x_ref, g_ref, sh_ref, sc_ref, wh_ref, wg_ref, wa_ref, wa2_ref, ba2_ref,
                   zh_ref, zg_ref, la_ref):
    x = x_ref[...]
    h = _rms(x, g_ref[...]) * (1.0 + sc_ref[0]) + sh_ref[0]
    hb = h.astype(BF16)
    zh_ref[...] = _dot(hb, wh_ref[...]).astype(BF16)
    zg_ref[...] = _dot(hb, wg_ref[...]).astype(BF16)
    za = _dot(hb, wa_ref[...])
    t = _dot3(za, wa2_ref[...]) + ba2_ref[...]
    la_ref[...] = (jnp.minimum(t, 0.0) - jnp.log(1.0 + jnp.exp(-jnp.abs(t)))) * (1.0 / GLA_TAU)


def in_proj(xs, norm_g, mod, w_hy, w_g, w_a, wa2_bd, ba2_cat, tm, seq, n_lat, b):
    n, d = xs.shape
    nlt = n_lat // tm
    row = functools.partial(_mod_row, tm=tm, seq=seq, n_lat_tiles=nlt, b=b)
    hy_cols, g_cols, la_cols = w_hy.shape[1], w_g.shape[1], wa2_bd.shape[1]
    full = lambda a: pl.BlockSpec(a.shape, lambda i: (0,) * a.ndim)
    return pl.pallas_call(
        _inproj_kernel,
        grid=(n // tm,),
        in_specs=[pl.BlockSpec((tm, d), lambda i: (i, 0)),
                  full(norm_g),
                  pl.BlockSpec((1, 1, d), lambda i: (row(i), 0, 0)),
                  pl.BlockSpec((1, 1, d), lambda i: (row(i), 0, 1)),
                  full(w_hy), full(w_g), full(w_a), full(wa2_bd), full(ba2_cat)],
        out_specs=[pl.BlockSpec((tm, hy_cols), lambda i: (i, 0)),
                   pl.BlockSpec((tm, g_cols), lambda i: (i, 0)),
                   pl.BlockSpec((tm, la_cols), lambda i: (i, 0))],
        out_shape=[jax.ShapeDtypeStruct((n, hy_cols), BF16),
                   jax.ShapeDtypeStruct((n, g_cols), BF16),
                   jax.ShapeDtypeStruct((n, la_cols), F32)],
        compiler_params=_cparams(("arbitrary",), VMEM_LIMIT),
        name="in_proj",
    )(xs, norm_g, mod, mod, w_hy, w_g, w_a, wa2_bd, ba2_cat)


def _trig_tables(l, half_shift):
    k = np.arange(l, dtype=np.int64)[:, None]
    n = np.arange(l, dtype=np.int64)[None, :]
    m = ((2 * k + 1) * (2 * n + (1 if half_shift else 0))) % (8 * l)
    ang = m.astype(np.float64) * (2.0 * math.pi / (8 * l))
    return (jnp.asarray(np.cos(ang).astype(np.float32).astype(BF16)),
            jnp.asarray(np.sin(ang).astype(np.float32).astype(BF16)))


def _filter_kernel(n_orders, width, l, tk,
                   pos_ref, t_ref, dl_ref, w1_ref, b1_ref, w2_ref, b2_ref, w3_ref, skip_ref,
                   c0_ref, s0_ref, gr_ref, gi_ref, hs_ref, hd_ref):
    j = pl.program_id(0)

    @pl.when(j == 0)
    def _():
        hid = jnp.sin(HY_FREQ * (_dot3(pos_ref[...], w1_ref[...]) + b1_ref[...]))
        hid = jnp.sin(HY_FREQ * (_dot3(hid, w2_ref[...]) + b2_ref[...]))
        h = _dot3(hid, w3_ref[...])
        decay = jnp.exp(-t_ref[...] * dl_ref[...])
        for o in range(n_orders):
            hf = h[:, (2 * o) * width:(2 * o + 1) * width] * decay
            hb = h[:, (2 * o + 1) * width:(2 * o + 2) * width] * decay
            hs_ref[:, o * width:(o + 1) * width] = (hf + hb).astype(BF16)
            hd_ref[:, o * width:(o + 1) * width] = (hb - hf).astype(BF16)

    scale = 1.0 / l
    gr = _dot(c0_ref[...], hs_ref[...])
    gi = _dot(s0_ref[...], hd_ref[...])
    for o in range(n_orders):
        gr_ref[o] = (gr[:, o * width:(o + 1) * width] + skip_ref[o]) * scale
        gi_ref[o] = gi[:, o * width:(o + 1) * width] * scale


def hyena_spectra(l, w1, b1, w2, b2, w3, skip, c0, s0):
    width = skip.shape[-1]
    n_orders = skip.shape[0]
    t = jnp.linspace(0.0, 1.0, l, dtype=F32)[:, None]
    bands = (HY_EMB - 1) // 2
    w = 2.0 * math.pi * jnp.arange(l, dtype=F32)[:, None] / l
    f = jnp.linspace(1e-4, bands - 1, bands, dtype=F32)[None, :]
    pos = jnp.concatenate([t, jnp.cos(f * w), -jnp.sin(f * w),
                           jnp.zeros((l, HY_EMB_PAD - HY_EMB), F32)], axis=-1)
    w1p = jnp.concatenate([w1, jnp.zeros((HY_EMB_PAD - HY_EMB, w1.shape[1]), F32)], axis=0)
    deltas = jnp.linspace(HY_MIN_DECAY, HY_MAX_DECAY, width, dtype=F32)[None, :]
    tk = min(l, 256)
    full = lambda a: pl.BlockSpec(a.shape, lambda j: (0,) * a.ndim)
    skip3 = skip.reshape(n_orders, 1, width)
    args = (pos, t, deltas, w1p, b1.reshape(1, -1), w2, b2.reshape(1, -1), w3, skip3)
    return pl.pallas_call(
        functools.partial(_filter_kernel, n_orders, width, l, tk),
        grid=(l // tk,),
        in_specs=[full(a) for a in args] + [pl.BlockSpec((tk, l), lambda j: (j, 0)),
                                            pl.BlockSpec((tk, l), lambda j: (j, 0))],
        out_specs=[pl.BlockSpec((n_orders, tk, width), lambda j: (0, j, 0)),
                   pl.BlockSpec((n_orders, tk, width), lambda j: (0, j, 0))],
        out_shape=[jax.ShapeDtypeStruct((n_orders, l, width), F32)] * 2,
        scratch_shapes=[pltpu.VMEM((l, n_orders * width), BF16),
                        pltpu.VMEM((l, n_orders * width), BF16)],
        compiler_params=_cparams(("arbitrary",), VMEM_LIMIT),
        name="hyena_spectra",
    )(*args, c0, s0)


def _hyena_kernel(l, fc, zv_ref, z1_ref, z2_ref, wv_ref, w1_ref, w2_ref, bv_ref, b1_ref, b2_ref,
                  c_ref, s_ref, gr_ref, gi_ref, y_ref, ub_ref, a_ref, b_ref, x1_ref, x2_ref):
    rows = lax.broadcasted_iota(jnp.int32, (l, 1), 0)

    def conv3(z_ref, w_ref, bias_ref):
        z = z_ref[...].astype(F32)
        zm = jnp.where(rows == 0, 0.0, pltpu.roll(z, 1, 0))
        zp = jnp.where(rows == l - 1, 0.0, pltpu.roll(z, l - 1, 0))
        return zm * w_ref[0:1, :] + z * w_ref[1:2, :] + zp * w_ref[2:3, :] + bias_ref[...]

    ub_ref[...] = conv3(zv_ref, wv_ref, bv_ref).astype(BF16)
    x1_ref[...] = conv3(z1_ref, w1_ref, b1_ref)
    x2_ref[...] = conv3(z2_ref, w2_ref, b2_ref)
    n = l // fc

    def forward(o):
        def body(c, carry):
            r0 = pl.multiple_of(c * fc, fc)
            ur = _dot(c_ref[pl.ds(r0, fc), :], ub_ref[...])
            us = _dot(s_ref[pl.ds(r0, fc), :], ub_ref[...])
            gr = gr_ref[o, pl.ds(r0, fc), :]
            gi = gi_ref[o, pl.ds(r0, fc), :]
            a_ref[pl.ds(r0, fc), :] = (ur * gr + us * gi).astype(BF16)
            b_ref[pl.ds(r0, fc), :] = (us * gr - ur * gi).astype(BF16)
            return carry
        lax.fori_loop(0, n, body, 0)

    def inverse(gate_ref, dst_ref):
        def body(c, carry):
            r0 = pl.multiple_of(c * fc, fc)
            lc = (_dot(c_ref[pl.ds(r0, fc), :], a_ref[...])
                  + _dot(s_ref[pl.ds(r0, fc), :], b_ref[...]))
            dst_ref[pl.ds(r0, fc), :] = (gate_ref[pl.ds(r0, fc), :] * lc).astype(BF16)
            return carry
        lax.fori_loop(0, n, body, 0)

    forward(0)
    inverse(x1_ref, ub_ref)
    forward(1)
    inverse(x2_ref, y_ref)


def hyena_mixer(zh, conv_w, conv_b, ctab, stab, gr, gi, l, n_seq, row0, cb):
    width = gr.shape[-1]
    n_orders = gr.shape[0]
    ncb = width // cb
    sb0 = row0 // l
    conv_b2 = conv_b.reshape(1, -1)
    zspec = lambda part: pl.BlockSpec((l, cb), lambda c, s: (sb0 + s, part * ncb + c))
    wspec = lambda part: pl.BlockSpec((3, cb), lambda c, s: (0, part * ncb + c))
    bspec = lambda part: pl.BlockSpec((1, cb), lambda c, s: (0, part * ncb + c))
    once = pl.Buffered(1)
    tspec = pl.BlockSpec((l, l), lambda c, s: (0, 0), pipeline_mode=once)
    gspec = pl.BlockSpec((n_orders, l, cb), lambda c, s: (0, 0, c), pipeline_mode=once)
    return pl.pallas_call(
        functools.partial(_hyena_kernel, l, min(l, HY_FREQ_CHUNK)),
        grid=(ncb, n_seq),
        in_specs=[zspec(0), zspec(1), zspec(2), wspec(0), wspec(1), wspec(2),
                  bspec(0), bspec(1), bspec(2), tspec, tspec, gspec, gspec],
        out_specs=pl.BlockSpec((l, cb), lambda c, s: (s, c)),
        out_shape=jax.ShapeDtypeStruct((n_seq * l, width), BF16),
        scratch_shapes=[pltpu.VMEM((l, cb), BF16), pltpu.VMEM((l, cb), BF16),
                        pltpu.VMEM((l, cb), BF16), pltpu.VMEM((l, cb), F32),
                        pltpu.VMEM((l, cb), F32)],
        compiler_params=_cparams(("arbitrary", "arbitrary"), VMEM_LIMIT),
        name="hyena_mixer",
    )(zh, zh, zh, conv_w, conv_w, conv_w, conv_b2, conv_b2, conv_b2, ctab, stab, gr, gi)


def _gla_kernel(l, qk, wv, dk, dv,
                zg_ref, la_ref, s0_ref, g_ref, y_ref, sfin_ref,
                of_ref, ob_ref, qe_ref, ks_ref, dec_ref, st_ref):
    ch = GLA_CHUNK
    n = l // ch
    heads = qk // dk
    grp = min(GLA_GROUP, n)
    rg = grp * ch
    r_i = lax.broadcasted_iota(jnp.int32, (rg, rg), 0)
    c_i = lax.broadcasted_iota(jnp.int32, (rg, rg), 1)
    same = (r_i // ch) == (c_i // ch)
    low = same & (r_i >= c_i)
    upp = same & (r_i <= c_i)
    masks = (low, upp)
    cum2 = tuple(jnp.concatenate([jnp.concatenate([m.astype(BF16)] * 2, axis=1),
                                  jnp.concatenate([same.astype(BF16)] * 2, axis=1)], axis=0)
                 for m in masks)
    lane_head = lax.broadcasted_iota(jnp.int32, (1, qk), 1) // dk
    scale = dk ** -0.5
    o_refs = (of_ref, ob_ref)

    def local(g, d):
        r0 = pl.multiple_of(g * rg, rg)
        q = zg_ref[pl.ds(r0, rg), 0:qk].astype(F32) * scale
        k = zg_ref[pl.ds(r0, rg), qk:2 * qk].astype(F32)
        v = zg_ref[pl.ds(r0, rg), 2 * qk:2 * qk + wv]
        la = la_ref[pl.ds(r0, rg), d * qk:(d + 1) * qk]
        lh, ll = _split(la)
        bt = _dot(cum2[d], jnp.concatenate([lh, ll], axis=0))
        b, tot = bt[:rg], bt[rg:]
        qe = (q * jnp.exp(b)).astype(BF16)
        ke = (k * jnp.exp(-b)).astype(BF16)
        qe_ref[d, pl.ds(r0, rg), :] = qe
        ks_ref[d, pl.ds(r0, rg), :] = (k * jnp.exp(tot - b)).astype(BF16)
        dec = jnp.exp(tot)
        for c in range(grp):
            dec_ref[d, pl.ds(pl.multiple_of((g * grp + c) * 8, 8), 8), :] = dec[c * ch:c * ch + 8]
        parts = []
        for h in range(heads):
            qh = jnp.where(lane_head == h, qe, jnp.zeros_like(qe))
            att = lax.dot_general(qh, ke, (((1,), (1,)), ((), ())), preferred_element_type=F32)
            att = jnp.where(masks[d], att, 0.0).astype(BF16)
            parts.append(_dot(att, v[:, h * dv:(h + 1) * dv]))
        o_refs[d][pl.ds(r0, rg), :] = jnp.concatenate(parts, axis=1)

    def local_body(g, carry):
        local(g, 0)
        local(g, 1)
        return carry

    lax.fori_loop(0, n // grp, local_body, 0)

    bd = (lax.broadcasted_iota(jnp.int32, (wv, qk), 0) // dv
          == lax.broadcasted_iota(jnp.int32, (wv, qk), 1) // dk).astype(F32)
    st_ref[...] = s0_ref[0]

    def carry_state(c, d):
        r0 = pl.multiple_of(c * ch, ch)
        st = st_ref[d]
        o = lax.dot_general(qe_ref[d, pl.ds(r0, ch), :], st.astype(BF16), (((1,), (1,)), ((), ())),
                            preferred_element_type=F32)
        o_refs[d][pl.ds(r0, ch), :] += o
        v = zg_ref[pl.ds(r0, ch), 2 * qk:2 * qk + wv]
        upd = lax.dot_general(v, ks_ref[d, pl.ds(r0, ch), :], (((0,), (0,)), ((), ())),
                              preferred_element_type=F32)
        dec = dec_ref[d, pl.ds(pl.multiple_of(c * 8, 8), 1), :]
        st_ref[d] = st * dec + upd * bd

    su = min(GLA_STATE_UNROLL, n)

    def state_body(i, carry):
        for u in range(su):
            carry_state(i * su + u, 0)
            carry_state(n - 1 - (i * su + u), 1)
        return carry

    lax.fori_loop(0, n // su, state_body, 0)
    sfin_ref[0] = st_ref[...]

    def finish(i, carry):
        r0 = pl.multiple_of(i * ch, ch)
        o = of_ref[pl.ds(r0, ch), :] + ob_ref[pl.ds(r0, ch), :]
        r = zg_ref[pl.ds(r0, ch), 2 * qk + wv:2 * qk + 2 * wv].astype(F32)
        parts = [_rms(o[:, h * dv:(h + 1) * dv], g_ref[...]) for h in range(heads)]
        y_ref[pl.ds(r0, ch), :] = (jnp.concatenate(parts, axis=1) * _silu(r)).astype(BF16)
        return carry

    lax.fori_loop(0, n, finish, 0)


def gla_mixer(zg, la, s0, norm_g, l, n_seq, row0):
    qk = la.shape[1] // 2
    wv = (zg.shape[1] - 2 * qk) // 2
    dk, dv = qk // GLA_HEADS, wv // GLA_HEADS
    sb0 = row0 // l
    return pl.pallas_call(
        functools.partial(_gla_kernel, l, qk, wv, dk, dv),
        grid=(n_seq,),
        in_specs=[pl.BlockSpec((l, zg.shape[1]), lambda s: (sb0 + s, 0)),
                  pl.BlockSpec((l, la.shape[1]), lambda s: (sb0 + s, 0)),
                  pl.BlockSpec((1, 2, wv, qk), lambda s: (s, 0, 0, 0)),
                  pl.BlockSpec((1, dv), lambda s: (0, 0))],
        out_specs=[pl.BlockSpec((l, wv), lambda s: (s, 0)),
                   pl.BlockSpec((1, 2, wv, qk), lambda s: (s, 0, 0, 0))],
        out_shape=[jax.ShapeDtypeStruct((n_seq * l, wv), BF16),
                   jax.ShapeDtypeStruct((n_seq, 2, wv, qk), F32)],
        scratch_shapes=[pltpu.VMEM((l, wv), F32), pltpu.VMEM((l, wv), F32),
                        pltpu.VMEM((2, l, qk), BF16), pltpu.VMEM((2, l, qk), BF16),
                        pltpu.VMEM((2, l // GLA_CHUNK * 8, qk), F32),
                        pltpu.VMEM((2, wv, qk), F32)],
        compiler_params=_cparams(("arbitrary",), VMEM_LIMIT),
        name="gla_mixer",
    )(zg, la, s0, norm_g.reshape(1, dv))


def _outproj_kernel(yh_ref, yg_ref, wh_ref, wg_ref, x_ref, g1_ref, g_ref, sh_ref, sc_ref,
                    xo_ref, u_ref):
    delta = _dot(yh_ref[...], wh_ref[...]) + _dot(yg_ref[...], wg_ref[...])
    x = x_ref[...] + g1_ref[0] * delta
    xo_ref[...] = x
    u_ref[...] = _rms(x, g_ref[...]) * (1.0 + sc_ref[0]) + sh_ref[0]


def out_proj(y_hy, y_gla, w_hy, w_gla, xs, mod, norm_g, tm, seq, n_lat, b, n_tok):
    d = xs.shape[1]
    nlt = n_lat // tm
    row = functools.partial(_mod_row, tm=tm, seq=seq, n_lat_tiles=nlt, b=b)
    full = lambda a: pl.BlockSpec(a.shape, lambda i: (0,) * a.ndim)
    modspec = lambda col: pl.BlockSpec((1, 1, d), lambda i: (row(i), 0, col))
    tile = lambda w: pl.BlockSpec((tm, w), lambda i: (i, 0))
    return pl.pallas_call(
        _outproj_kernel,
        grid=(n_tok // tm,),
        in_specs=[tile(y_hy.shape[1]), tile(y_gla.shape[1]), full(w_hy), full(w_gla), tile(d),
                  modspec(2), full(norm_g), modspec(3), modspec(4)],
        out_specs=[tile(d), tile(d)],
        out_shape=[jax.ShapeDtypeStruct((n_tok, d), F32)] * 2,
        compiler_params=_cparams(("arbitrary",), VMEM_LIMIT),
        name="out_proj",
    )(y_hy, y_gla, w_hy, w_gla, xs, mod, norm_g, mod, mod)


def _router_kernel(n_exp, tm, u_ref, wh_ref, wl_ref, b_ref, tri_ref,
                   idx_ref, gate_ref, rank_ref, cnt_ref):
    i = pl.program_id(0)

    @pl.when(i == 0)
    def _():
        cnt_ref[...] = jnp.zeros_like(cnt_ref)

    uh, ul = _split(u_ref[...])
    nt = (((1,), (1,)), ((), ()))
    logits = (lax.dot_general(wh_ref[...], uh, nt, preferred_element_type=F32)
              + lax.dot_general(wh_ref[...], ul, nt, preferred_element_type=F32)
              + lax.dot_general(wl_ref[...], uh, nt, preferred_element_type=F32))
    s = 1.0 / (1.0 + jnp.exp(-logits))
    sel = s + b_ref[...]
    gsz = n_exp // N_GROUPS
    neg = -jnp.inf
    rows_g = lax.broadcasted_iota(jnp.int32, (gsz, tm), 0)
    blocks, gscore = [], []
    for g in range(N_GROUPS):
        blk = sel[g * gsz:(g + 1) * gsz, :]
        m1 = jnp.max(blk, axis=0, keepdims=True)
        i1 = jnp.min(jnp.where(blk == m1, rows_g, gsz), axis=0, keepdims=True)
        m2 = jnp.max(jnp.where(rows_g == i1, neg, blk), axis=0, keepdims=True)
        blocks.append(blk)
        gscore.append(m1 + m2)
    masked = []
    for g in range(N_GROUPS):
        beaten = jnp.zeros((1, tm), jnp.int32)
        for h in range(N_GROUPS):
            if h == g:
                continue
            wins = (gscore[h] > gscore[g]) | ((gscore[h] == gscore[g]) & (h < g))
            beaten = beaten + wins.astype(jnp.int32)
        masked.append(jnp.where(beaten < TOPK_GROUPS, blocks[g], neg))
    sel = jnp.concatenate(masked, axis=0)

    rows = lax.broadcasted_iota(jnp.int32, (n_exp, tm), 0)
    chosen = jnp.zeros((n_exp, tm), F32)
    idxs, gates = [], []
    for _ in range(TOP_K):
        m = jnp.max(sel, axis=0, keepdims=True)
        ik = jnp.min(jnp.where(sel == m, rows, n_exp), axis=0, keepdims=True)
        hit = rows == ik
        gates.append(jnp.sum(jnp.where(hit, s, 0.0), axis=0, keepdims=True))
        sel = jnp.where(hit, neg, sel)
        chosen = jnp.where(hit, 1.0, chosen)
        idxs.append(ik)
    gsum = gates[0]
    for g in gates[1:]:
        gsum = gsum + g
    inv = ROUTED_SCALE / gsum
    before = _dot(chosen.astype(BF16), tri_ref[...]) + cnt_ref[...]
    for k in range(TOP_K):
        idx_ref[k:k + 1, :] = idxs[k]
        gate_ref[k:k + 1, :] = gates[k] * inv
        rk = jnp.sum(jnp.where(rows == idxs[k], before, 0.0), axis=0, keepdims=True)
        rank_ref[k:k + 1, :] = rk.astype(jnp.int32)
    cnt_ref[...] += jnp.sum(chosen, axis=1, keepdims=True)


def moe_router(u, wr_hi, wr_lo, b_r, n_tok, tm):
    n_exp, d = wr_hi.shape
    tri = (lax.broadcasted_iota(jnp.int32, (tm, tm), 0)
           < lax.broadcasted_iota(jnp.int32, (tm, tm), 1)).astype(BF16)
    full = lambda a: pl.BlockSpec(a.shape, lambda i: (0,) * a.ndim)
    tok = pl.BlockSpec((TOP_K, tm), lambda i: (0, i))
    return pl.pallas_call(
        functools.partial(_router_kernel, n_exp, tm),
        grid=(n_tok // tm,),
        in_specs=[pl.BlockSpec((tm, d), lambda i: (i, 0)), full(wr_hi), full(wr_lo),
                  pl.BlockSpec((n_exp, 1), lambda i: (0, 0)), full(tri)],
        out_specs=[tok, tok, tok, pl.BlockSpec((n_exp, 1), lambda i: (0, 0))],
        out_shape=[jax.ShapeDtypeStruct((TOP_K, n_tok), jnp.int32),
                   jax.ShapeDtypeStruct((TOP_K, n_tok), F32),
                   jax.ShapeDtypeStruct((TOP_K, n_tok), jnp.int32),
                   jax.ShapeDtypeStruct((n_exp, 1), F32)],
        compiler_params=_cparams(("arbitrary",), VMEM_LIMIT),
        name="moe_router",
    )(u, wr_hi, wr_lo, b_r.reshape(n_exp, 1), tri)


LANES = 128


def _start_row_copy(src, src_row, dst, dst_row, sem, queue):
    pltpu.async_copy(src.at[src_row], dst.at[dst_row], sem, priority=queue)


def _store_rows_tiled(ref, x):
    for j in range(ref.shape[-2]):
        ref[:, j, :] = x[:, j * LANES:(j + 1) * LANES]


def _load_rows_tiled(ref):
    return jnp.concatenate([ref[:, j, :] for j in range(ref.shape[-2])], axis=1)


def _pack_bf16_pairs(x):
    w = x.shape[1] // 2
    lo = pltpu.bitcast(x[:, :w].astype(BF16).astype(F32), jnp.uint32) >> 16
    hi = pltpu.bitcast(x[:, w:].astype(BF16).astype(F32), jnp.uint32)
    return hi | lo


def _unpack_bf16_pairs(p):
    lo = pltpu.bitcast(p << 16, F32)
    hi = pltpu.bitcast(p & jnp.uint32(0xFFFF0000), F32)
    return jnp.concatenate([lo, hi], axis=1).astype(BF16)


def _dest_kernel(n_exp, tm, idx_ref, rank_ref, ps_ref, dest_ref):
    rows = lax.broadcasted_iota(jnp.int32, (n_exp, tm), 0)
    ps = ps_ref[...]
    for k in range(TOP_K):
        base = jnp.sum(jnp.where(rows == idx_ref[k:k + 1, :], ps, 0), axis=0, keepdims=True)
        dest_ref[k:k + 1, :] = base + rank_ref[k:k + 1, :]


def moe_dest(idx, rank, pad_start, tm):
    n_tok = idx.shape[1]
    n_exp = pad_start.shape[0]
    tok = pl.BlockSpec((TOP_K, tm), lambda i: (0, i))
    return pl.pallas_call(
        functools.partial(_dest_kernel, n_exp, tm),
        grid=(n_tok // tm,),
        in_specs=[tok, tok, pl.BlockSpec((n_exp, 1), lambda i: (0, 0))],
        out_specs=tok,
        out_shape=jax.ShapeDtypeStruct((TOP_K, n_tok), jnp.int32),
        compiler_params=_cparams(("arbitrary",)),
        name="moe_dest",
    )(idx, rank, pad_start.reshape(n_exp, 1))


def _wait_rows(src, dst, sem, tm):
    for _ in range(TOP_K):
        pltpu.make_async_copy(src.at[pl.ds(0, tm)], dst.at[pl.ds(0, tm)], sem).wait()


ZERO_ROWS = MOE_ROWS // 2


def _zero_padding(n_exp, ps_ref, pe_ref, xs_ref, z_ref, sem, wait):
    def go(copy):
        copy.wait() if wait else copy.start()

    def per_expert(e, carry):
        s = ps_ref[e]
        m = pe_ref[e] - s
        bit = ZERO_ROWS
        while bit:
            start = s + (m & ~(2 * bit - 1))

            @pl.when((m & bit) != 0)
            def _():
                go(pltpu.make_async_copy(z_ref.at[pl.ds(0, bit)], xs_ref.at[pl.ds(start, bit)], sem))
            bit //= 2
        return carry

    lax.fori_loop(0, n_exp, per_expert, 0)

    tail = pe_ref[n_exp - 1]

    def per_chunk(j, carry):
        go(pltpu.make_async_copy(z_ref, xs_ref.at[pl.ds(tail + j * ZERO_ROWS, ZERO_ROWS)], sem))
        return carry

    lax.fori_loop(0, (xs_ref.shape[0] - tail) // ZERO_ROWS, per_chunk, 0)


def _dispatch_kernel(tm, n_steps, n_exp, ps_ref, pe_ref, dest_ref, u_ref, xs_ref, pk_ref, z_ref,
                     sems, zsem):
    i = pl.program_id(0)
    slot = i % 2
    pk = pk_ref.at[slot]
    sem = sems.at[slot]

    @pl.when(i == 0)
    def _():
        z_ref[...] = jnp.zeros_like(z_ref)
        _zero_padding(n_exp, ps_ref, pe_ref, xs_ref, z_ref, zsem, wait=False)

    @pl.when(i >= 2)
    def _():
        _wait_rows(pk, xs_ref, sem, tm)

    _store_rows_tiled(pk, _pack_bf16_pairs(u_ref[...]))

    def issue(t, carry):
        for k in range(TOP_K):
            _start_row_copy(pk, t, xs_ref, dest_ref[0, 0, k * tm + t], sem, k % 2)
        return carry

    lax.fori_loop(0, tm, issue, 0)

    @pl.when(i == n_steps - 1)
    def _():
        if n_steps > 1:
            _wait_rows(pk_ref.at[1 - slot], xs_ref, sems.at[1 - slot], tm)
        _wait_rows(pk, xs_ref, sem, tm)
        _zero_padding(n_exp, ps_ref, pe_ref, xs_ref, z_ref, zsem, wait=True)


def moe_dispatch(u, dest, pad_from, pad_end, n_tok, n_slots, tm):
    d = u.shape[1]
    sub = d // 2 // LANES
    n_exp = pad_end.shape[0]
    grid_spec = pltpu.PrefetchScalarGridSpec(
        num_scalar_prefetch=2,
        grid=(n_tok // tm,),
        in_specs=[pl.BlockSpec((1, 1, TOP_K * tm), lambda i, *_: (i, 0, 0), memory_space=pltpu.SMEM),
                  pl.BlockSpec((tm, d), lambda i, *_: (i, 0))],
        out_specs=pl.BlockSpec(memory_space=pl.ANY),
        scratch_shapes=[pltpu.VMEM((2, tm, sub, LANES), jnp.uint32),
                        pltpu.VMEM((ZERO_ROWS, sub, LANES), jnp.uint32),
                        pltpu.SemaphoreType.DMA((2,)), pltpu.SemaphoreType.DMA])
    return pl.pallas_call(
        functools.partial(_dispatch_kernel, tm, n_tok // tm, n_exp),
        grid_spec=grid_spec,
        out_shape=jax.ShapeDtypeStruct((n_slots, sub, LANES), jnp.uint32),
        compiler_params=_cparams(("arbitrary",)),
        name="moe_dispatch",
    )(pad_from, pad_end, dest, u)


def _gmm_kernel(layer, be_ref, nx_ref, fl_ref, nu_ref, x_ref, w1_hbm, w3_hbm, w2_hbm, y_ref,
                w1f, w3f, w2f, w1b, w3b, w2b, sems):
    i = pl.program_id(0)
    fl = fl_ref[i]

    def fetch(e, s):
        return (pltpu.make_async_copy(w1_hbm.at[layer, e], w1f.at[s], sems.at[s]),
                pltpu.make_async_copy(w3_hbm.at[layer, e], w3f.at[s], sems.at[s]),
                pltpu.make_async_copy(w2_hbm.at[layer, e], w2f.at[s], sems.at[s]))

    @pl.when(i < nu_ref[0])
    def _():
        @pl.when((fl & 1) != 0)
        def _():
            s = (fl >> 1) & 1

            @pl.when(i == 0)
            def _():
                for c in fetch(be_ref[i], s):
                    c.start()

            for c in fetch(be_ref[i], s):
                c.wait()

            @pl.when((fl & 4) != 0)
            def _():
                for c in fetch(nx_ref[i], 1 - s):
                    c.start()

            w1b[...] = w1f[s].astype(BF16)
            w3b[...] = w3f[s].astype(BF16)
            w2b[...] = w2f[s].astype(BF16)

        x = _unpack_bf16_pairs(_load_rows_tiled(x_ref))
        h = _silu(_dot(x, w1b[...])) * _dot(x, w3b[...])
        _store_rows_tiled(y_ref, _pack_bf16_pairs(_dot(h.astype(BF16), w2b[...])))

    @pl.when(i >= nu_ref[0])
    def _():
        y_ref[...] = jnp.zeros_like(y_ref)


def moe_experts(x_sorted, block_e, n_used, w1, w3, w2, layer):
    n_blocks = block_e.shape[0]
    _, _, d, f = w1.shape
    idx = jnp.arange(n_blocks, dtype=jnp.int32)
    first = (idx == 0) | (block_e != jnp.roll(block_e, 1))
    seg = jnp.cumsum(first.astype(jnp.int32)) - 1
    first_at_or_after = lax.cummin(jnp.where(first, idx, n_blocks)[::-1])[::-1]
    nxt_idx = jnp.concatenate([first_at_or_after[1:], jnp.full((1,), n_blocks, jnp.int32)])
    has_next = nxt_idx < n_used[0]
    nxt_e = block_e[jnp.minimum(nxt_idx, n_blocks - 1)]
    flags = (first.astype(jnp.int32) | ((seg & 1) << 1) | (has_next.astype(jnp.int32) << 2))
    grid_spec = pltpu.PrefetchScalarGridSpec(
        num_scalar_prefetch=4,
        grid=(n_blocks,),
        in_specs=[pl.BlockSpec((MOE_ROWS,) + x_sorted.shape[1:], lambda i, *_: (i, 0, 0)),
                  pl.BlockSpec(memory_space=pl.ANY),
                  pl.BlockSpec(memory_space=pl.ANY),
                  pl.BlockSpec(memory_space=pl.ANY)],
        out_specs=pl.BlockSpec((MOE_ROWS,) + x_sorted.shape[1:], lambda i, *_: (i, 0, 0)),
        scratch_shapes=[pltpu.VMEM((2, d, f), F32), pltpu.VMEM((2, d, f), F32),
                        pltpu.VMEM((2, f, d), F32),
                        pltpu.VMEM((d, f), BF16), pltpu.VMEM((d, f), BF16),
                        pltpu.VMEM((f, d), BF16), pltpu.SemaphoreType.DMA((2,))])
    return pl.pallas_call(
        functools.partial(_gmm_kernel, layer),
        grid_spec=grid_spec,
        out_shape=jax.ShapeDtypeStruct(x_sorted.shape, jnp.uint32),
        compiler_params=_cparams(("arbitrary",), VMEM_LIMIT),
        name="moe_experts",
    )(block_e, nxt_e, flags, n_used, x_sorted, w1, w3, w2)


def _combine_kernel(tm, n_steps, final, dest_ref, nxt_ref, y_ref, gate_ref, u_ref, ws1_ref, ws3_ref,
                    ws2_ref, x_ref, g2_ref, fg_ref, o_ref, buf_ref, routed_ref, sems):
    i = pl.program_id(0)
    slot = i % 2
    grp = 8

    def gather(d_ref, s, g):
        for tt in range(grp):
            t = g * grp + tt
            for k in range(TOP_K):
                _start_row_copy(y_ref, d_ref[0, 0, k * tm + t], buf_ref.at[s, k], t, sems.at[s], k % 2)

    def reduce(g):
        r0 = pl.multiple_of(g * grp, grp)
        gt = gate_ref[pl.ds(r0, grp), :]
        lo = hi = None
        for k in range(TOP_K):
            p = buf_ref[slot, k, pl.ds(r0, grp)]
            gk = gt[:, k:k + 1][:, :, None]
            lo_k = pltpu.bitcast(p << 16, F32) * gk
            hi_k = pltpu.bitcast(p & jnp.uint32(0xFFFF0000), F32) * gk
            lo = lo_k if lo is None else lo + lo_k
            hi = hi_k if hi is None else hi + hi_k
        sub = lo.shape[1]
        routed_ref[pl.ds(r0, grp), 0:sub, :] = lo
        routed_ref[pl.ds(r0, grp), sub:2 * sub, :] = hi

    @pl.when(i == 0)
    def _():
        lax.fori_loop(0, tm // grp, lambda g, c: (gather(dest_ref, slot, g), c)[1], 0)

    _wait_rows(y_ref, buf_ref.at[slot, 0], sems.at[slot], tm)

    @pl.when(i + 1 < n_steps)
    def _():
        def both(g, c):
            gather(nxt_ref, 1 - slot, g)
            reduce(g)
            return c
        lax.fori_loop(0, tm // grp, both, 0)

    @pl.when(i + 1 >= n_steps)
    def _():
        lax.fori_loop(0, tm // grp, lambda g, c: (reduce(g), c)[1], 0)

    ub = u_ref[...].astype(BF16)
    hs = _silu(_dot(ub, ws1_ref[...])) * _dot(ub, ws3_ref[...])
    shared = _dot(hs.astype(BF16), ws2_ref[...])
    x = x_ref[...] + g2_ref[0] * (_load_rows_tiled(routed_ref) + shared)
    o_ref[...] = _rms(x, fg_ref[...]) if final else x


def moe_combine(y_sorted, dest, gate_rep, u, ws1, ws3, ws2, xs, mod, final_g, final,
                tm, seq, n_lat, b, n_tok):
    d = u.shape[1]
    nlt = n_lat // tm
    row = functools.partial(_mod_row, tm=tm, seq=seq, n_lat_tiles=nlt, b=b)
    full = lambda a: pl.BlockSpec(a.shape, lambda i: (0,) * a.ndim)
    tile = pl.BlockSpec((tm, d), lambda i: (i, 0))
    n_steps = n_tok // tm
    return pl.pallas_call(
        functools.partial(_combine_kernel, tm, n_steps, final),
        grid=(n_steps,),
        in_specs=[pl.BlockSpec((1, 1, TOP_K * tm), lambda i: (i, 0, 0), memory_space=pltpu.SMEM),
                  pl.BlockSpec((1, 1, TOP_K * tm),
                               lambda i: (jnp.minimum(i + 1, n_steps - 1), 0, 0),
                               memory_space=pltpu.SMEM),
                  pl.BlockSpec(memory_space=pl.ANY),
                  pl.BlockSpec((tm, TOP_K), lambda i: (i, 0)),
                  tile, full(ws1), full(ws3), full(ws2), tile,
                  pl.BlockSpec((1, 1, d), lambda i: (row(i), 0, 5)),
                  full(final_g)],
        out_specs=tile,
        out_shape=jax.ShapeDtypeStruct((n_tok, d), F32),
        scratch_shapes=[pltpu.VMEM((2, TOP_K, tm) + y_sorted.shape[1:], jnp.uint32),
                        pltpu.VMEM((tm, d // LANES, LANES), F32), pltpu.SemaphoreType.DMA((2,))],
        compiler_params=_cparams(("arbitrary",), VMEM_LIMIT),
        name="moe_combine",
    )(dest, dest, y_sorted, gate_rep, u, ws1, ws3, ws2, xs, mod, final_g)


def moe_layer(u, xs, mod, router_w, router_b, w1, w3, w2, layer, ws1, ws3, ws2, final_g, final,
              tm, seq, n_lat, b, n_tok):
    n_exp = router_w.shape[1]
    wr_hi, wr_lo = _split(router_w.T)
    idx, gate, rank, cnt = moe_router(u, wr_hi, wr_lo, router_b, n_tok, tm)
    counts = cnt[:, 0].astype(jnp.int32)
    padded = (counts + MOE_ROWS - 1) // MOE_ROWS * MOE_ROWS
    pad_end = jnp.cumsum(padded)
    pad_start = pad_end - padded
    dest = moe_dest(idx, rank, pad_start, math.gcd(n_tok, 1024))
    n_blocks = -(-(n_tok * TOP_K + n_exp * (MOE_ROWS - 1)) // MOE_ROWS)
    block_start = jnp.arange(n_blocks, dtype=jnp.int32) * MOE_ROWS
    block_e = jnp.minimum(jnp.sum((pad_end[None, :] <= block_start[:, None]).astype(jnp.int32), axis=1),
                          n_exp - 1)
    n_used = (pad_end[-1:] // MOE_ROWS).astype(jnp.int32)
    dest = dest.reshape(TOP_K, n_tok // tm, tm).transpose(1, 0, 2).reshape(n_tok // tm, 1, TOP_K * tm)
    x_sorted = moe_dispatch(u, dest, pad_start + counts, pad_end, n_tok, n_blocks * MOE_ROWS, tm)
    y_sorted = moe_experts(x_sorted, block_e, n_used, w1, w3, w2, layer)
    return moe_combine(y_sorted, dest, gate.T, u, ws1.astype(BF16), ws3.astype(BF16),
                       ws2.astype(BF16), xs, mod, final_g.reshape(1, -1), final,
                       tm, seq, n_lat, b, n_tok)


def _grid_pos_emb(rows, d):
    r, col = jnp.meshgrid(jnp.arange(rows, dtype=F32), jnp.arange(GRID_W, dtype=F32), indexing='ij')
    quarter = d // 4
    omega = 1.0 / (10000.0 ** (jnp.arange(quarter, dtype=F32) / quarter))

    def emb(p):
        a = p.reshape(-1)[:, None] * omega[None, :]
        return jnp.concatenate([jnp.sin(a), jnp.cos(a)], axis=-1)

    return jnp.concatenate([emb(r), emb(col)], axis=-1)


def kernel(x, c, ctx, c_ctx, ada_w, ada_b, norm1_g, norm2_g, w_in, w_out, hy_conv_w, hy_conv_b,
           hy_f_w1, hy_f_b1, hy_f_w2, hy_f_b2, hy_f_w3, hy_skip, gla_wa2, gla_ba2, gla_norm_g,
           router_w, router_b, exp_w1, exp_w3, exp_w2, sh_w1, sh_w3, sh_w2, final_g):
    b, l, d = x.shape
    lc = ctx.shape[1]
    depth = ada_w.shape[0]
    n_lat, n_ctx = b * l, b * lc
    n_all = n_lat + n_ctx
    hy_width = hy_skip.shape[-1]
    hy_cols = hy_conv_w.shape[-1]
    qk = gla_wa2.shape[-1]
    wv = d - hy_width
    g_cols = 2 * qk + 2 * wv
    tm = math.gcd(math.gcd(l, lc), 256)
    tm_proj = math.gcd(math.gcd(l, n_ctx), 512)
    cb = min(hy_width, 256)

    cond = jnp.concatenate([c, c_ctx[None, :]], axis=0)
    n_rows = -(-(b + 1) // 8) * 8
    cond = jnp.concatenate([cond, jnp.zeros((n_rows - b - 1, d), F32)], axis=0)
    mods = ada_table(cond, ada_w, ada_b)

    xs = embed_tokens(x.reshape(n_lat, d), _grid_pos_emb(l // GRID_W, d), ctx.reshape(n_ctx, d), tm)

    tabs = {}
    for length in (l, lc):
        if length not in tabs:
            tabs[length] = _trig_tables(length, True) + _trig_tables(length, False)
    zero_state = jnp.zeros((b, 2, wv, qk), F32)

    for layer in range(depth):
        last = layer == depth - 1
        mod = mods[layer][:, None, :]
        g1 = norm1_g[layer].reshape(1, d)
        g2 = norm2_g[layer].reshape(1, d)
        w_l = w_in[layer].astype(BF16)
        wa2 = gla_wa2[layer]
        rank = wa2.shape[1]
        wa2_bd = jnp.zeros((2 * rank, 2 * qk), F32)
        wa2_bd = wa2_bd.at[:rank, :qk].set(wa2[0]).at[rank:, qk:].set(wa2[1])
        ba2_cat = gla_ba2[layer].reshape(1, 2 * qk)
        zh, zg, la = in_proj(xs, g1, mod, w_l[:, :hy_cols], w_l[:, hy_cols:hy_cols + g_cols],
                             w_l[:, hy_cols + g_cols:], wa2_bd, ba2_cat, tm_proj, l, n_lat, b)

        filt = (hy_f_w1[layer], hy_f_b1[layer], hy_f_w2[layer], hy_f_b2[layer], hy_f_w3[layer],
                hy_skip[layer])
        yc_gla, s_ctx = gla_mixer(zg, la, zero_state, gla_norm_g[layer], lc, b, n_lat)
        y_gla, _ = gla_mixer(zg, la, s_ctx, gla_norm_g[layer], l, b, 0)
        ct, st, c0, s0 = tabs[l]
        gr, gi = hyena_spectra(l, *filt, c0, s0)
        y_hy = hyena_mixer(zh, hy_conv_w[layer], hy_conv_b[layer], ct, st, gr, gi, l, b, 0, cb)
        if last:
            n_tok = n_lat
        else:
            n_tok = n_all
            ct, st, c0, s0 = tabs[lc]
            gr, gi = hyena_spectra(lc, *filt, c0, s0)
            yc_hy = hyena_mixer(zh, hy_conv_w[layer], hy_conv_b[layer], ct, st, gr, gi,
                                lc, b, n_lat, cb)
            y_hy = jnp.concatenate([y_hy, yc_hy], axis=0)
            y_gla = jnp.concatenate([y_gla, yc_gla], axis=0)
        w_o = w_out[layer].astype(BF16)
        xs_mid, u = out_proj(y_hy, y_gla, w_o[:hy_width], w_o[hy_width:], xs, mod, g2,
                             tm, l, n_lat, b, n_tok)
        xs = moe_layer(u, xs_mid, mod, router_w[layer], router_b[layer], exp_w1, exp_w3, exp_w2,
                       layer, sh_w1[layer], sh_w3[layer], sh_w2[layer],
                       final_g, last, tm, l, n_lat, b, n_tok)
    return xs[:n_lat].reshape(b, l, d)
```

```python
import functools
import math

import numpy as np
import jax
import jax.numpy as jnp
from jax import lax
from jax.experimental import pallas as pl
from jax.experimental.pallas import tpu as pltpu

GRID_W = 64
EPS = 1e-6

HY_ORDER = 2
HY_EMB = 33
HY_EMB_PAD = 40
HY_FREQ = 1.0
HY_TARGET = 1e-2
HY_FAST_PCT = 0.3
HY_SLOW_PCT = 1.5
HY_MIN_DECAY = math.log(1.0 / HY_TARGET) / HY_SLOW_PCT
HY_MAX_DECAY = math.log(1.0 / HY_TARGET) / HY_FAST_PCT

GLA_HEADS = 4
GLA_RANK = 16
GLA_TAU = 16.0
GLA_CHUNK = 64
GLA_GROUP = 4
GLA_STATE_UNROLL = 4

TOP_K = 8
N_GROUPS = 8
TOPK_GROUPS = 4
ROUTED_SCALE = 2.5

HY_FREQ_CHUNK = 1024
MOE_ROWS = 512
VMEM_LIMIT = 56 * 1024 * 1024

F32 = jnp.float32
BF16 = jnp.bfloat16


def _cparams(sem, vmem=None):
    return pltpu.CompilerParams(dimension_semantics=sem, vmem_limit_bytes=vmem)


def _split(a):
    hi = a.astype(BF16)
    lo = (a - hi.astype(F32)).astype(BF16)
    return hi, lo


def _dot(a, b):
    return jnp.dot(a, b, preferred_element_type=F32)


def _dot3(a, b):
    ah, al = _split(a)
    bh, bl = _split(b)
    return _dot(ah, bh) + _dot(ah, bl) + _dot(al, bh)


def _silu(x):
    return x * (1.0 / (1.0 + jnp.exp(-x)))


def _rms(x, g):
    return x * lax.rsqrt(jnp.mean(x * x, axis=-1, keepdims=True) + EPS) * g


def _ada_kernel(c_ref, w_ref, b_ref, o_ref):
    o_ref[0] = _dot3(_silu(c_ref[...]), w_ref[0]) + b_ref[0]


def ada_table(cond_rows, ada_w, ada_b):
    depth, d, six_d = ada_w.shape
    r = cond_rows.shape[0]
    tn = 1024
    return pl.pallas_call(
        _ada_kernel,
        grid=(depth, six_d // tn),
        in_specs=[pl.BlockSpec((r, d), lambda l, j: (0, 0)),
                  pl.BlockSpec((1, d, tn), lambda l, j: (l, 0, j)),
                  pl.BlockSpec((1, 1, tn), lambda l, j: (l, 0, j))],
        out_specs=pl.BlockSpec((1, r, tn), lambda l, j: (l, 0, j)),
        out_shape=jax.ShapeDtypeStruct((depth, r, six_d), F32),
        compiler_params=_cparams(("arbitrary", "arbitrary")),
        name="ada_table",
    )(cond_rows, ada_w, ada_b.reshape(depth, 1, six_d))


def _embed_kernel(n_lat_tiles, x_ref, p_ref, c_ref, o_ref):
    i = pl.program_id(0)

    @pl.when(i < n_lat_tiles)
    def _():
        o_ref[...] = x_ref[...] + p_ref[...]

    @pl.when(i >= n_lat_tiles)
    def _():
        o_ref[...] = c_ref[...]


def embed_tokens(x2, pos, ctx2, tm):
    nl, d = x2.shape
    nc = ctx2.shape[0]
    l = pos.shape[0]
    nlt, nct, lt = nl // tm, nc // tm, l // tm
    return pl.pallas_call(
        functools.partial(_embed_kernel, nlt),
        grid=(nlt + nct,),
        in_specs=[pl.BlockSpec((tm, d), lambda i: (jnp.minimum(i, nlt - 1), 0)),
                  pl.BlockSpec((tm, d), lambda i: (i % lt, 0)),
                  pl.BlockSpec((tm, d), lambda i: (jnp.maximum(i - nlt, 0), 0))],
        out_specs=pl.BlockSpec((tm, d), lambda i: (i, 0)),
        out_shape=jax.ShapeDtypeStruct((nl + nc, d), F32),
        compiler_params=_cparams(("arbitrary",)),
        name="embed_tokens",
    )(x2, pos, ctx2)


def _mod_row(i, tm, seq, n_lat_tiles, b):
    return jnp.where(i < n_lat_tiles, (i * tm) // seq, b)


def _inproj_kernel(x_ref, g_ref, sh_ref, sc_ref, wh_ref, wg_ref, wa_ref, wa2_ref, ba2_ref,
                   zh_ref, zg_ref, la_ref):
    x = x_ref[...]
    h = _rms(x, g_ref[...]) * (1.0 + sc_ref[0]) + sh_ref[0]
    hb = h.astype(BF16)
    zh_ref[...] = _dot(hb, wh_ref[...]).astype(BF16)
    zg_ref[...] = _dot(hb, wg_ref[...]).astype(BF16)
    za = _dot(hb, wa_ref[...])
    t = _dot3(za, wa2_ref[...]) + ba2_ref[...]
    la_ref[...] = (jnp.minimum(t, 0.0) - jnp.log(1.0 + jnp.exp(-jnp.abs(t)))) * (1.0 / GLA_TAU)


def in_proj(xs, norm_g, mod, w_hy, w_g, w_a, wa2_bd, ba2_cat, tm, seq, n_lat, b):
    n, d = xs.shape
    nlt = n_lat // tm
    row = functools.partial(_mod_row, tm=tm, seq=seq, n_lat_tiles=nlt, b=b)
    hy_cols, g_cols, la_cols = w_hy.shape[1], w_g.shape[1], wa2_bd.shape[1]
    full = lambda a: pl.BlockSpec(a.shape, lambda i: (0,) * a.ndim)
    return pl.pallas_call(
        _inproj_kernel,
        grid=(n // tm,),
        in_specs=[pl.BlockSpec((tm, d), lambda i: (i, 0)),
                  full(norm_g),
                  pl.BlockSpec((1, 1, d), lambda i: (row(i), 0, 0)),
                  pl.BlockSpec((1, 1, d), lambda i: (row(i), 0, 1)),
                  full(w_hy), full(w_g), full(w_a), full(wa2_bd), full(ba2_cat)],
        out_specs=[pl.BlockSpec((tm, hy_cols), lambda i: (i, 0)),
                   pl.BlockSpec((tm, g_cols), lambda i: (i, 0)),
                   pl.BlockSpec((tm, la_cols), lambda i: (i, 0))],
        out_shape=[jax.ShapeDtypeStruct((n, hy_cols), BF16),
                   jax.ShapeDtypeStruct((n, g_cols), BF16),
                   jax.ShapeDtypeStruct((n, la_cols), F32)],
        compiler_params=_cparams(("arbitrary",), VMEM_LIMIT),
        name="in_proj",
    )(xs, norm_g, mod, mod, w_hy, w_g, w_a, wa2_bd, ba2_cat)


def _trig_tables(l, half_shift):
    k = np.arange(l, dtype=np.int64)[:, None]
    n = np.arange(l, dtype=np.int64)[None, :]
    m = ((2 * k + 1) * (2 * n + (1 if half_shift else 0))) % (8 * l)
    ang = m.astype(np.float64) * (2.0 * math.pi / (8 * l))
    return (jnp.asarray(np.cos(ang).astype(np.float32).astype(BF16)),
            jnp.asarray(np.sin(ang).astype(np.float32).astype(BF16)))


def _filter_kernel(n_orders, width, l, tk,
                   pos_ref, t_ref, dl_ref, w1_ref, b1_ref, w2_ref, b2_ref, w3_ref, skip_ref,
                   c0_ref, s0_ref, gr_ref, gi_ref, hs_ref, hd_ref):
    j = pl.program_id(0)

    @pl.when(j == 0)
    def _():
        hid = jnp.sin(HY_FREQ * (_dot3(pos_ref[...], w1_ref[...]) + b1_ref[...]))
        hid = jnp.sin(HY_FREQ * (_dot3(hid, w2_ref[...]) + b2_ref[...]))
        h = _dot3(hid, w3_ref[...])
        decay = jnp.exp(-t_ref[...] * dl_ref[...])
        for o in range(n_orders):
            hf = h[:, (2 * o) * width:(2 * o + 1) * width] * decay
            hb = h[:, (2 * o + 1) * width:(2 * o + 2) * width] * decay
            hs_ref[:, o * width:(o + 1) * width] = (hf + hb).astype(BF16)
            hd_ref[:, o * width:(o + 1) * width] = (hb - hf).astype(BF16)

    scale = 1.0 / l
    gr = _dot(c0_ref[...], hs_ref[...])
    gi = _dot(s0_ref[...], hd_ref[...])
    for o in range(n_orders):
        gr_ref[o] = (gr[:, o * width:(o + 1) * width] + skip_ref[o]) * scale
        gi_ref[o] = gi[:, o * width:(o + 1) * width] * scale


def hyena_spectra(l, w1, b1, w2, b2, w3, skip, c0, s0):
    width = skip.shape[-1]
    n_orders = skip.shape[0]
    t = jnp.linspace(0.0, 1.0, l, dtype=F32)[:, None]
    bands = (HY_EMB - 1) // 2
    w = 2.0 * math.pi * jnp.arange(l, dtype=F32)[:, None] / l
    f = jnp.linspace(1e-4, bands - 1, bands, dtype=F32)[None, :]
    pos = jnp.concatenate([t, jnp.cos(f * w), -jnp.sin(f * w),
                           jnp.zeros((l, HY_EMB_PAD - HY_EMB), F32)], axis=-1)
    w1p = jnp.concatenate([w1, jnp.zeros((HY_EMB_PAD - HY_EMB, w1.shape[1]), F32)], axis=0)
    deltas = jnp.linspace(HY_MIN_DECAY, HY_MAX_DECAY, width, dtype=F32)[None, :]
    tk = min(l, 256)
    full = lambda a: pl.BlockSpec(a.shape, lambda j: (0,) * a.ndim)
    skip3 = skip.reshape(n_orders, 1, width)
    args = (pos, t, deltas, w1p, b1.reshape(1, -1), w2, b2.reshape(1, -1), w3, skip3)
    return pl.pallas_call(
        functools.partial(_filter_kernel, n_orders, width, l, tk),
        grid=(l // tk,),
        in_specs=[full(a) for a in args] + [pl.BlockSpec((tk, l), lambda j: (j, 0)),
                                            pl.BlockSpec((tk, l), lambda j: (j, 0))],
        out_specs=[pl.BlockSpec((n_orders, tk, width), lambda j: (0, j, 0)),
                   pl.BlockSpec((n_orders, tk, width), lambda j: (0, j, 0))],
        out_shape=[jax.ShapeDtypeStruct((n_orders, l, width), F32)] * 2,
        scratch_shapes=[pltpu.VMEM((l, n_orders * width), BF16),
                        pltpu.VMEM((l, n_orders * width), BF16)],
        compiler_params=_cparams(("arbitrary",), VMEM_LIMIT),
        name="hyena_spectra",
    )(*args, c0, s0)


def _hyena_kernel(l, fc, zv_ref, z1_ref, z2_ref, wv_ref, w1_ref, w2_ref, bv_ref, b1_ref, b2_ref,
                  c_ref, s_ref, gr_ref, gi_ref, y_ref, ub_ref, a_ref, b_ref, x1_ref, x2_ref):
    rows = lax.broadcasted_iota(jnp.int32, (l, 1), 0)

    def conv3(z_ref, w_ref, bias_ref):
        z = z_ref[...].astype(F32)
        zm = jnp.where(rows == 0, 0.0, pltpu.roll(z, 1, 0))
        zp = jnp.where(rows == l - 1, 0.0, pltpu.roll(z, l - 1, 0))
        return zm * w_ref[0:1, :] + z * w_ref[1:2, :] + zp * w_ref[2:3, :] + bias_ref[...]

    ub_ref[...] = conv3(zv_ref, wv_ref, bv_ref).astype(BF16)
    x1_ref[...] = conv3(z1_ref, w1_ref, b1_ref)
    x2_ref[...] = conv3(z2_ref, w2_ref, b2_ref)
    n = l // fc

    def forward(o):
        def body(c, carry):
            r0 = pl.multiple_of(c * fc, fc)
            ur = _dot(c_ref[pl.ds(r0, fc), :], ub_ref[...])
            us = _dot(s_ref[pl.ds(r0, fc), :], ub_ref[...])
            gr = gr_ref[o, pl.ds(r0, fc), :]
            gi = gi_ref[o, pl.ds(r0, fc), :]
            a_ref[pl.ds(r0, fc), :] = (ur * gr + us * gi).astype(BF16)
            b_ref[pl.ds(r0, fc), :] = (us * gr - ur * gi).astype(BF16)
            return carry
        lax.fori_loop(0, n, body, 0)

    def inverse(gate_ref, dst_ref):
        def body(c, carry):
            r0 = pl.multiple_of(c * fc, fc)
            lc = (_dot(c_ref[pl.ds(r0, fc), :], a_ref[...])
                  + _dot(s_ref[pl.ds(r0, fc), :], b_ref[...]))
            dst_ref[pl.ds(r0, fc), :] = (gate_ref[pl.ds(r0, fc), :] * lc).astype(BF16)
            return carry
        lax.fori_loop(0, n, body, 0)

    forward(0)
    inverse(x1_ref, ub_ref)
    forward(1)
    inverse(x2_ref, y_ref)


def hyena_mixer(zh, conv_w, conv_b, ctab, stab, gr, gi, l, n_seq, row0, cb):
    width = gr.shape[-1]
    n_orders = gr.shape[0]
    ncb = width // cb
    sb0 = row0 // l
    conv_b2 = conv_b.reshape(1, -1)
    zspec = lambda part: pl.BlockSpec((l, cb), lambda c, s: (sb0 + s, part * ncb + c))
    wspec = lambda part: pl.BlockSpec((3, cb), lambda c, s: (0, part * ncb + c))
    bspec = lambda part: pl.BlockSpec((1, cb), lambda c, s: (0, part * ncb + c))
    once = pl.Buffered(1)
    tspec = pl.BlockSpec((l, l), lambda c, s: (0, 0), pipeline_mode=once)
    gspec = pl.BlockSpec((n_orders, l, cb), lambda c, s: (0, 0, c), pipeline_mode=once)
    return pl.pallas_call(
        functools.partial(_hyena_kernel, l, min(l, HY_FREQ_CHUNK)),
        grid=(ncb, n_seq),
        in_specs=[zspec(0), zspec(1), zspec(2), wspec(0), wspec(1), wspec(2),
                  bspec(0), bspec(1), bspec(2), tspec, tspec, gspec, gspec],
        out_specs=pl.BlockSpec((l, cb), lambda c, s: (s, c)),
        out_shape=jax.ShapeDtypeStruct((n_seq * l, width), BF16),
        scratch_shapes=[pltpu.VMEM((l, cb), BF16), pltpu.VMEM((l, cb), BF16),
                        pltpu.VMEM((l, cb), BF16), pltpu.VMEM((l, cb), F32),
                        pltpu.VMEM((l, cb), F32)],
        compiler_params=_cparams(("arbitrary", "arbitrary"), VMEM_LIMIT),
        name="hyena_mixer",
    )(zh, zh, zh, conv_w, conv_w, conv_w, conv_b2, conv_b2, conv_b2, ctab, stab, gr, gi)


def _gla_kernel(l, qk, wv, dk, dv,
                zg_ref, la_ref, s0_ref, g_ref, y_ref, sfin_ref,
                of_ref, ob_ref, qe_ref, ks_ref, dec_ref, st_ref):
    ch = GLA_CHUNK
    n = l // ch
    heads = qk // dk
    grp = min(GLA_GROUP, n)
    rg = grp * ch
    r_i = lax.broadcasted_iota(jnp.int32, (rg, rg), 0)
    c_i = lax.broadcasted_iota(jnp.int32, (rg, rg), 1)
    same = (r_i // ch) == (c_i // ch)
    low = same & (r_i >= c_i)
    upp = same & (r_i <= c_i)
    masks = (low, upp)
    cum2 = tuple(jnp.concatenate([jnp.concatenate([m.astype(BF16)] * 2, axis=1),
                                  jnp.concatenate([same.astype(BF16)] * 2, axis=1)], axis=0)
                 for m in masks)
    lane_head = lax.broadcasted_iota(jnp.int32, (1, qk), 1) // dk
    scale = dk ** -0.5
    o_refs = (of_ref, ob_ref)

    def local(g, d):
        r0 = pl.multiple_of(g * rg, rg)
        q = zg_ref[pl.ds(r0, rg), 0:qk].astype(F32) * scale
        k = zg_ref[pl.ds(r0, rg), qk:2 * qk].astype(F32)
        v = zg_ref[pl.ds(r0, rg), 2 * qk:2 * qk + wv]
        la = la_ref[pl.ds(r0, rg), d * qk:(d + 1) * qk]
        lh, ll = _split(la)
        bt = _dot(cum2[d], jnp.concatenate([lh, ll], axis=0))
        b, tot = bt[:rg], bt[rg:]
        qe = (q * jnp.exp(b)).astype(BF16)
        ke = (k * jnp.exp(-b)).astype(BF16)
        qe_ref[d, pl.ds(r0, rg), :] = qe
        ks_ref[d, pl.ds(r0, rg), :] = (k * jnp.exp(tot - b)).astype(BF16)
        dec = jnp.exp(tot)
        for c in range(grp):
            dec_ref[d, pl.ds(pl.multiple_of((g * grp + c) * 8, 8), 8), :] = dec[c * ch:c * ch + 8]
        parts = []
        for h in range(heads):
            qh = jnp.where(lane_head == h, qe, jnp.zeros_like(qe))
            att = lax.dot_general(qh, ke, (((1,), (1,)), ((), ())), preferred_element_type=F32)
            att = jnp.where(masks[d], att, 0.0).astype(BF16)
            parts.append(_dot(att, v[:, h * dv:(h + 1) * dv]))
        o_refs[d][pl.ds(r0, rg), :] = jnp.concatenate(parts, axis=1)

    def local_body(g, carry):
        local(g, 0)
        local(g, 1)
        return carry

    lax.fori_loop(0, n // grp, local_body, 0)

    bd = (lax.broadcasted_iota(jnp.int32, (wv, qk), 0) // dv
          == lax.broadcasted_iota(jnp.int32, (wv, qk), 1) // dk).astype(F32)
    st_ref[...] = s0_ref[0]

    def carry_state(c, d):
        r0 = pl.multiple_of(c * ch, ch)
        st = st_ref[d]
        o = lax.dot_general(qe_ref[d, pl.ds(r0, ch), :], st.astype(BF16), (((1,), (1,)), ((), ())),
                            preferred_element_type=F32)
        o_refs[d][pl.ds(r0, ch), :] += o
        v = zg_ref[pl.ds(r0, ch), 2 * qk:2 * qk + wv]
        upd = lax.dot_general(v, ks_ref[d, pl.ds(r0, ch), :], (((0,), (0,)), ((), ())),
                              preferred_element_type=F32)
        dec = dec_ref[d, pl.ds(pl.multiple_of(c * 8, 8), 1), :]
        st_ref[d] = st * dec + upd * bd

    su = min(GLA_STATE_UNROLL, n)

    def state_body(i, carry):
        for u in range(su):
            carry_state(i * su + u, 0)
            carry_state(n - 1 - (i * su + u), 1)
        return carry

    lax.fori_loop(0, n // su, state_body, 0)
    sfin_ref[0] = st_ref[...]

    def finish(i, carry):
        r0 = pl.multiple_of(i * ch, ch)
        o = of_ref[pl.ds(r0, ch), :] + ob_ref[pl.ds(r0, ch), :]
        r = zg_ref[pl.ds(r0, ch), 2 * qk + wv:2 * qk + 2 * wv].astype(F32)
        parts = [_rms(o[:, h * dv:(h + 1) * dv], g_ref[...]) for h in range(heads)]
        y_ref[pl.ds(r0, ch), :] = (jnp.concatenate(parts, axis=1) * _silu(r)).astype(BF16)
        return carry

    lax.fori_loop(0, n, finish, 0)


def gla_mixer(zg, la, s0, norm_g, l, n_seq, row0):
    qk = la.shape[1] // 2
    wv = (zg.shape[1] - 2 * qk) // 2
    dk, dv = qk // GLA_HEADS, wv // GLA_HEADS
    sb0 = row0 // l
    return pl.pallas_call(
        functools.partial(_gla_kernel, l, qk, wv, dk, dv),
        grid=(n_seq,),
        in_specs=[pl.BlockSpec((l, zg.shape[1]), lambda s: (sb0 + s, 0)),
                  pl.BlockSpec((l, la.shape[1]), lambda s: (sb0 + s, 0)),
                  pl.BlockSpec((1, 2, wv, qk), lambda s: (s, 0, 0, 0)),
                  pl.BlockSpec((1, dv), lambda s: (0, 0))],
        out_specs=[pl.BlockSpec((l, wv), lambda s: (s, 0)),
                   pl.BlockSpec((1, 2, wv, qk), lambda s: (s, 0, 0, 0))],
        out_shape=[jax.ShapeDtypeStruct((n_seq * l, wv), BF16),
                   jax.ShapeDtypeStruct((n_seq, 2, wv, qk), F32)],
        scratch_shapes=[pltpu.VMEM((l, wv), F32), pltpu.VMEM((l, wv), F32),
                        pltpu.VMEM((2, l, qk), BF16), pltpu.VMEM((2, l, qk), BF16),
                        pltpu.VMEM((2, l // GLA_CHUNK * 8, qk), F32),
                        pltpu.VMEM((2, wv, qk), F32)],
        compiler_params=_cparams(("arbitrary",), VMEM_LIMIT),
        name="gla_mixer",
    )(zg, la, s0, norm_g.reshape(1, dv))


def _outproj_kernel(yh_ref, yg_ref, wh_ref, wg_ref, x_ref, g1_ref, g_ref, sh_ref, sc_ref,
                    xo_ref, u_ref):
    delta = _dot(yh_ref[...], wh_ref[...]) + _dot(yg_ref[...], wg_ref[...])
    x = x_ref[...] + g1_ref[0] * delta
    xo_ref[...] = x
    u_ref[...] = _rms(x, g_ref[...]) * (1.0 + sc_ref[0]) + sh_ref[0]


def out_proj(y_hy, y_gla, w_hy, w_gla, xs, mod, norm_g, tm, seq, n_lat, b, n_tok):
    d = xs.shape[1]
    nlt = n_lat // tm
    row = functools.partial(_mod_row, tm=tm, seq=seq, n_lat_tiles=nlt, b=b)
    full = lambda a: pl.BlockSpec(a.shape, lambda i: (0,) * a.ndim)
    modspec = lambda col: pl.BlockSpec((1, 1, d), lambda i: (row(i), 0, col))
    tile = lambda w: pl.BlockSpec((tm, w), lambda i: (i, 0))
    return pl.pallas_call(
        _outproj_kernel,
        grid=(n_tok // tm,),
        in_specs=[tile(y_hy.shape[1]), tile(y_gla.shape[1]), full(w_hy), full(w_gla), tile(d),
                  modspec(2), full(norm_g), modspec(3), modspec(4)],
        out_specs=[tile(d), tile(d)],
        out_shape=[jax.ShapeDtypeStruct((n_tok, d), F32)] * 2,
        compiler_params=_cparams(("arbitrary",), VMEM_LIMIT),
        name="out_proj",
    )(y_hy, y_gla, w_hy, w_gla, xs, mod, norm_g, mod, mod)


def _router_kernel(n_exp, tm, u_ref, wh_ref, wl_ref, b_ref, tri_ref,
                   idx_ref, gate_ref, rank_ref, cnt_ref):
    i = pl.program_id(0)

    @pl.when(i == 0)
    def _():
        cnt_ref[...] = jnp.zeros_like(cnt_ref)

    uh, ul = _split(u_ref[...])
    nt = (((1,), (1,)), ((), ()))
    logits = (lax.dot_general(wh_ref[...], uh, nt, preferred_element_type=F32)
              + lax.dot_general(wh_ref[...], ul, nt, preferred_element_type=F32)
              + lax.dot_general(wl_ref[...], uh, nt, preferred_element_type=F32))
    s = 1.0 / (1.0 + jnp.exp(-logits))
    sel = s + b_ref[...]
    gsz = n_exp // N_GROUPS
    neg = -jnp.inf
    rows_g = lax.broadcasted_iota(jnp.int32, (gsz, tm), 0)
    blocks, gscore = [], []
    for g in range(N_GROUPS):
        blk = sel[g * gsz:(g + 1) * gsz, :]
        m1 = jnp.max(blk, axis=0, keepdims=True)
        i1 = jnp.min(jnp.where(blk == m1, rows_g, gsz), axis=0, keepdims=True)
        m2 = jnp.max(jnp.where(rows_g == i1, neg, blk), axis=0, keepdims=True)
        blocks.append(blk)
        gscore.append(m1 + m2)
    masked = []
    for g in range(N_GROUPS):
        beaten = jnp.zeros((1, tm), jnp.int32)
        for h in range(N_GROUPS):
            if h == g:
                continue
            wins = (gscore[h] > gscore[g]) | ((gscore[h] == gscore[g]) & (h < g))
            beaten = beaten + wins.astype(jnp.int32)
        masked.append(jnp.where(beaten < TOPK_GROUPS, blocks[g], neg))
    sel = jnp.concatenate(masked, axis=0)

    rows = lax.broadcasted_iota(jnp.int32, (n_exp, tm), 0)
    chosen = jnp.zeros((n_exp, tm), F32)
    idxs, gates = [], []
    for _ in range(TOP_K):
        m = jnp.max(sel, axis=0, keepdims=True)
        ik = jnp.min(jnp.where(sel == m, rows, n_exp), axis=0, keepdims=True)
        hit = rows == ik
        gates.append(jnp.sum(jnp.where(hit, s, 0.0), axis=0, keepdims=True))
        sel = jnp.where(hit, neg, sel)
        chosen = jnp.where(hit, 1.0, chosen)
        idxs.append(ik)
    gsum = gates[0]
    for g in gates[1:]:
        gsum = gsum + g
    inv = ROUTED_SCALE / gsum
    before = _dot(chosen.astype(BF16), tri_ref[...]) + cnt_ref[...]
    for k in range(TOP_K):
        idx_ref[k:k + 1, :] = idxs[k]
        gate_ref[k:k + 1, :] = gates[k] * inv
        rk = jnp.sum(jnp.where(rows == idxs[k], before, 0.0), axis=0, keepdims=True)
        rank_ref[k:k + 1, :] = rk.astype(jnp.int32)
    cnt_ref[...] += jnp.sum(chosen, axis=1, keepdims=True)


def moe_router(u, wr_hi, wr_lo, b_r, n_tok, tm):
    n_exp, d = wr_hi.shape
    tri = (lax.broadcasted_iota(jnp.int32, (tm, tm), 0)
           < lax.broadcasted_iota(jnp.int32, (tm, tm), 1)).astype(BF16)
    full = lambda a: pl.BlockSpec(a.shape, lambda i: (0,) * a.ndim)
    tok = pl.BlockSpec((TOP_K, tm), lambda i: (0, i))
    return pl.pallas_call(
        functools.partial(_router_kernel, n_exp, tm),
        grid=(n_tok // tm,),
        in_specs=[pl.BlockSpec((tm, d), lambda i: (i, 0)), full(wr_hi), full(wr_lo),
                  pl.BlockSpec((n_exp, 1), lambda i: (0, 0)), full(tri)],
        out_specs=[tok, tok, tok, pl.BlockSpec((n_exp, 1), lambda i: (0, 0))],
        out_shape=[jax.ShapeDtypeStruct((TOP_K, n_tok), jnp.int32),
                   jax.ShapeDtypeStruct((TOP_K, n_tok), F32),
                   jax.ShapeDtypeStruct((TOP_K, n_tok), jnp.int32),
                   jax.ShapeDtypeStruct((n_exp, 1), F32)],
        compiler_params=_cparams(("arbitrary",), VMEM_LIMIT),
        name="moe_router",
    )(u, wr_hi, wr_lo, b_r.reshape(n_exp, 1), tri)


LANES = 128


def _start_row_copy(src, src_row, dst, dst_row, sem, queue):
    pltpu.async_copy(src.at[src_row], dst.at[dst_row], sem, priority=queue)


def _store_rows_tiled(ref, x):
    for j in range(ref.shape[-2]):
        ref[:, j, :] = x[:, j * LANES:(j + 1) * LANES]


def _load_rows_tiled(ref):
    return jnp.concatenate([ref[:, j, :] for j in range(ref.shape[-2])], axis=1)


def _pack_bf16_pairs(x):
    w = x.shape[1] // 2
    lo = pltpu.bitcast(x[:, :w].astype(BF16).astype(F32), jnp.uint32) >> 16
    hi = pltpu.bitcast(x[:, w:].astype(BF16).astype(F32), jnp.uint32)
    return hi | lo


def _unpack_bf16_pairs(p):
    lo = pltpu.bitcast(p << 16, F32)
    hi = pltpu.bitcast(p & jnp.uint32(0xFFFF0000), F32)
    return jnp.concatenate([lo, hi], axis=1).astype(BF16)


def _dest_kernel(n_exp, tm, idx_ref, rank_ref, ps_ref, dest_ref):
    rows = lax.broadcasted_iota(jnp.int32, (n_exp, tm), 0)
    ps = ps_ref[...]
    for k in range(TOP_K):
        base = jnp.sum(jnp.where(rows == idx_ref[k:k + 1, :], ps, 0), axis=0, keepdims=True)
        dest_ref[k:k + 1, :] = base + rank_ref[k:k + 1, :]


def moe_dest(idx, rank, pad_start, tm):
    n_tok = idx.shape[1]
    n_exp = pad_start.shape[0]
    tok = pl.BlockSpec((TOP_K, tm), lambda i: (0, i))
    return pl.pallas_call(
        functools.partial(_dest_kernel, n_exp, tm),
        grid=(n_tok // tm,),
        in_specs=[tok, tok, pl.BlockSpec((n_exp, 1), lambda i: (0, 0))],
        out_specs=tok,
        out_shape=jax.ShapeDtypeStruct((TOP_K, n_tok), jnp.int32),
        compiler_params=_cparams(("arbitrary",)),
        name="moe_dest",
    )(idx, rank, pad_start.reshape(n_exp, 1))


def _wait_rows(src, dst, sem, tm):
    for _ in range(TOP_K):
        pltpu.make_async_copy(src.at[pl.ds(0, tm)], dst.at[pl.ds(0, tm)], sem).wait()


ZERO_ROWS = MOE_ROWS // 2


def _zero_padding(n_exp, ps_ref, pe_ref, xs_ref, z_ref, sem, wait):
    def go(copy):
        copy.wait() if wait else copy.start()

    def per_expert(e, carry):
        s = ps_ref[e]
        m = pe_ref[e] - s
        bit = ZERO_ROWS
        while bit:
            start = s + (m & ~(2 * bit - 1))

            @pl.when((m & bit) != 0)
            def _():
                go(pltpu.make_async_copy(z_ref.at[pl.ds(0, bit)], xs_ref.at[pl.ds(start, bit)], sem))
            bit //= 2
        return carry

    lax.fori_loop(0, n_exp, per_expert, 0)

    tail = pe_ref[n_exp - 1]

    def per_chunk(j, carry):
        go(pltpu.make_async_copy(z_ref, xs_ref.at[pl.ds(tail + j * ZERO_ROWS, ZERO_ROWS)], sem))
        return carry

    lax.fori_loop(0, (xs_ref.shape[0] - tail) // ZERO_ROWS, per_chunk, 0)


def _dispatch_kernel(tm, n_steps, n_exp, ps_ref, pe_ref, dest_ref, u_ref, xs_ref, pk_ref, z_ref,
                     sems, zsem):
    i = pl.program_id(0)
    slot = i % 2
    pk = pk_ref.at[slot]
    sem = sems.at[slot]

    @pl.when(i == 0)
    def _():
        z_ref[...] = jnp.zeros_like(z_ref)
        _zero_padding(n_exp, ps_ref, pe_ref, xs_ref, z_ref, zsem, wait=False)

    @pl.when(i >= 2)
    def _():
        _wait_rows(pk, xs_ref, sem, tm)

    _store_rows_tiled(pk, _pack_bf16_pairs(u_ref[...]))

    def issue(t, carry):
        for k in range(TOP_K):
            _start_row_copy(pk, t, xs_ref, dest_ref[0, 0, k * tm + t], sem, k % 2)
        return carry

    lax.fori_loop(0, tm, issue, 0)

    @pl.when(i == n_steps - 1)
    def _():
        if n_steps > 1:
            _wait_rows(pk_ref.at[1 - slot], xs_ref, sems.at[1 - slot], tm)
        _wait_rows(pk, xs_ref, sem, tm)
        _zero_padding(n_exp, ps_ref, pe_ref, xs_ref, z_ref, zsem, wait=True)


def moe_dispatch(u, dest, pad_from, pad_end, n_tok, n_slots, tm):
    d = u.shape[1]
    sub = d // 2 // LANES
    n_exp = pad_end.shape[0]
    grid_spec = pltpu.PrefetchScalarGridSpec(
        num_scalar_prefetch=2,
        grid=(n_tok // tm,),
        in_specs=[pl.BlockSpec((1, 1, TOP_K * tm), lambda i, *_: (i, 0, 0), memory_space=pltpu.SMEM),
                  pl.BlockSpec((tm, d), lambda i, *_: (i, 0))],
        out_specs=pl.BlockSpec(memory_space=pl.ANY),
        scratch_shapes=[pltpu.VMEM((2, tm, sub, LANES), jnp.uint32),
                        pltpu.VMEM((ZERO_ROWS, sub, LANES), jnp.uint32),
                        pltpu.SemaphoreType.DMA((2,)), pltpu.SemaphoreType.DMA])
    return pl.pallas_call(
        functools.partial(_dispatch_kernel, tm, n_tok // tm, n_exp),
        grid_spec=grid_spec,
        out_shape=jax.ShapeDtypeStruct((n_slots, sub, LANES), jnp.uint32),
        compiler_params=_cparams(("arbitrary",)),
        name="moe_dispatch",
    )(pad_from, pad_end, dest, u)


def _gmm_kernel(layer, be_ref, nx_ref, fl_ref, nu_ref, x_ref, w1_hbm, w3_hbm, w2_hbm, y_ref,
                w1f, w3f, w2f, w1b, w3b, w2b, sems):
    i = pl.program_id(0)
    fl = fl_ref[i]
    sub = x_ref.shape[0] // MOE_ROWS

    def fetch(e, s):
        return (pltpu.make_async_copy(w1_hbm.at[layer, e], w1f.at[s], sems.at[s]),
                pltpu.make_async_copy(w3_hbm.at[layer, e], w3f.at[s], sems.at[s]),
                pltpu.make_async_copy(w2_hbm.at[layer, e], w2f.at[s], sems.at[s]))

    @pl.when(i < nu_ref[0])
    def _():
        @pl.when((fl & 1) != 0)
        def _():
            s = (fl >> 1) & 1

            @pl.when(i == 0)
            def _():
                for c in fetch(be_ref[i], s):
                    c.start()

            for c in fetch(be_ref[i], s):
                c.wait()

            @pl.when((fl & 4) != 0)
            def _():
                for c in fetch(nx_ref[i], 1 - s):
                    c.start()

            w1b[...] = w1f[s].astype(BF16)
            w3b[...] = w3f[s].astype(BF16)
            w2b[...] = w2f[s].astype(BF16)

        x = _unpack_bf16_pairs(jnp.concatenate(
            [x_ref[pl.ds(j, MOE_ROWS, stride=sub), :] for j in range(sub)], axis=1))
        h = _silu(_dot(x, w1b[...])) * _dot(x, w3b[...])
        y = _pack_bf16_pairs(_dot(h.astype(BF16), w2b[...]))
        for j in range(sub):
            y_ref[pl.ds(j, MOE_ROWS, stride=sub), :] = y[:, j * LANES:(j + 1) * LANES]

    @pl.when(i >= nu_ref[0])
    def _():
        y_ref[...] = jnp.zeros_like(y_ref)


def moe_experts(x_sorted, block_e, n_used, w1, w3, w2, layer):
    n_blocks = block_e.shape[0]
    _, _, d, f = w1.shape
    n_slots, sub, _ = x_sorted.shape
    idx = jnp.arange(n_blocks, dtype=jnp.int32)
    first = (idx == 0) | (block_e != jnp.roll(block_e, 1))
    seg = jnp.cumsum(first.astype(jnp.int32)) - 1
    first_at_or_after = lax.cummin(jnp.where(first, idx, n_blocks)[::-1])[::-1]
    nxt_idx = jnp.concatenate([first_at_or_after[1:], jnp.full((1,), n_blocks, jnp.int32)])
    has_next = nxt_idx < n_used[0]
    nxt_e = block_e[jnp.minimum(nxt_idx, n_blocks - 1)]
    flags = (first.astype(jnp.int32) | ((seg & 1) << 1) | (has_next.astype(jnp.int32) << 2))
    grid_spec = pltpu.PrefetchScalarGridSpec(
        num_scalar_prefetch=4,
        grid=(n_blocks,),
        in_specs=[pl.BlockSpec((MOE_ROWS * sub, LANES), lambda i, *_: (i, 0)),
                  pl.BlockSpec(memory_space=pl.ANY),
                  pl.BlockSpec(memory_space=pl.ANY),
                  pl.BlockSpec(memory_space=pl.ANY)],
        out_specs=pl.BlockSpec((MOE_ROWS * sub, LANES), lambda i, *_: (i, 0)),
        scratch_shapes=[pltpu.VMEM((2, d, f), F32), pltpu.VMEM((2, d, f), F32),
                        pltpu.VMEM((2, f, d), F32),
                        pltpu.VMEM((d, f), BF16), pltpu.VMEM((d, f), BF16),
                        pltpu.VMEM((f, d), BF16), pltpu.SemaphoreType.DMA((2,))])
    y = pl.pallas_call(
        functools.partial(_gmm_kernel, layer),
        grid_spec=grid_spec,
        out_shape=jax.ShapeDtypeStruct((n_slots * sub, LANES), jnp.uint32),
        compiler_params=_cparams(("arbitrary",), VMEM_LIMIT),
        name="moe_experts",
    )(block_e, nxt_e, flags, n_used, x_sorted.reshape(n_slots * sub, LANES), w1, w3, w2)
    return y.reshape(n_slots, sub, LANES)


def _combine_kernel(tm, n_steps, final, dest_ref, nxt_ref, y_ref, gate_ref, u_ref, ws1_ref, ws3_ref,
                    ws2_ref, x_ref, g2_ref, fg_ref, o_ref, buf_ref, routed_ref, sems):
    i = pl.program_id(0)
    slot = i % 2
    grp = 8

    def gather(d_ref, s, g):
        for tt in range(grp):
            t = g * grp + tt
            for k in range(TOP_K):
                _start_row_copy(y_ref, d_ref[0, 0, k * tm + t], buf_ref.at[s, k], t, sems.at[s], k % 2)

    def reduce(g):
        r0 = pl.multiple_of(g * grp, grp)
        gt = gate_ref[pl.ds(r0, grp), :]
        lo = hi = None
        for k in range(TOP_K):
            p = buf_ref[slot, k, pl.ds(r0, grp)]
            gk = gt[:, k:k + 1][:, :, None]
            lo_k = pltpu.bitcast(p << 16, F32) * gk
            hi_k = pltpu.bitcast(p & jnp.uint32(0xFFFF0000), F32) * gk
            lo = lo_k if lo is None else lo + lo_k
            hi = hi_k if hi is None else hi + hi_k
        sub = lo.shape[1]
        routed_ref[pl.ds(r0, grp), 0:sub, :] = lo
        routed_ref[pl.ds(r0, grp), sub:2 * sub, :] = hi

    @pl.when(i == 0)
    def _():
        lax.fori_loop(0, tm // grp, lambda g, c: (gather(dest_ref, slot, g), c)[1], 0)

    _wait_rows(y_ref, buf_ref.at[slot, 0], sems.at[slot], tm)

    @pl.when(i + 1 < n_steps)
    def _():
        def both(g, c):
            gather(nxt_ref, 1 - slot, g)
            reduce(g)
            return c
        lax.fori_loop(0, tm // grp, both, 0)

    @pl.when(i + 1 >= n_steps)
    def _():
        lax.fori_loop(0, tm // grp, lambda g, c: (reduce(g), c)[1], 0)

    ub = u_ref[...].astype(BF16)
    hs = _silu(_dot(ub, ws1_ref[...])) * _dot(ub, ws3_ref[...])
    shared = _dot(hs.astype(BF16), ws2_ref[...])
    x = x_ref[...] + g2_ref[0] * (_load_rows_tiled(routed_ref) + shared)
    o_ref[...] = _rms(x, fg_ref[...]) if final else x


def moe_combine(y_sorted, dest, gate_rep, u, ws1, ws3, ws2, xs, mod, final_g, final,
                tm, seq, n_lat, b, n_tok):
    d = u.shape[1]
    nlt = n_lat // tm
    row = functools.partial(_mod_row, tm=tm, seq=seq, n_lat_tiles=nlt, b=b)
    full = lambda a: pl.BlockSpec(a.shape, lambda i: (0,) * a.ndim)
    tile = pl.BlockSpec((tm, d), lambda i: (i, 0))
    n_steps = n_tok // tm
    return pl.pallas_call(
        functools.partial(_combine_kernel, tm, n_steps, final),
        grid=(n_steps,),
        in_specs=[pl.BlockSpec((1, 1, TOP_K * tm), lambda i: (i, 0, 0), memory_space=pltpu.SMEM),
                  pl.BlockSpec((1, 1, TOP_K * tm),
                               lambda i: (jnp.minimum(i + 1, n_steps - 1), 0, 0),
                               memory_space=pltpu.SMEM),
                  pl.BlockSpec(memory_space=pl.ANY),
                  pl.BlockSpec((tm, TOP_K), lambda i: (i, 0)),
                  tile, full(ws1), full(ws3), full(ws2), tile,
                  pl.BlockSpec((1, 1, d), lambda i: (row(i), 0, 5)),
                  full(final_g)],
        out_specs=tile,
        out_shape=jax.ShapeDtypeStruct((n_tok, d), F32),
        scratch_shapes=[pltpu.VMEM((2, TOP_K, tm) + y_sorted.shape[1:], jnp.uint32),
                        pltpu.VMEM((tm, d // LANES, LANES), F32), pltpu.SemaphoreType.DMA((2,))],
        compiler_params=_cparams(("arbitrary",), VMEM_LIMIT),
        name="moe_combine",
    )(dest, dest, y_sorted, gate_rep, u, ws1, ws3, ws2, xs, mod, final_g)


def moe_layer(u, xs, mod, router_w, router_b, w1, w3, w2, layer, ws1, ws3, ws2, final_g, final,
              tm, seq, n_lat, b, n_tok):
    n_exp = router_w.shape[1]
    wr_hi, wr_lo = _split(router_w.T)
    idx, gate, rank, cnt = moe_router(u, wr_hi, wr_lo, router_b, n_tok, tm)
    counts = cnt[:, 0].astype(jnp.int32)
    padded = (counts + MOE_ROWS - 1) // MOE_ROWS * MOE_ROWS
    pad_end = jnp.cumsum(padded)
    pad_start = pad_end - padded
    dest = moe_dest(idx, rank, pad_start, math.gcd(n_tok, 1024))
    n_blocks = -(-(n_tok * TOP_K + n_exp * (MOE_ROWS - 1)) // MOE_ROWS)
    block_start = jnp.arange(n_blocks, dtype=jnp.int32) * MOE_ROWS
    block_e = jnp.minimum(jnp.sum((pad_end[None, :] <= block_start[:, None]).astype(jnp.int32), axis=1),
                          n_exp - 1)
    n_used = (pad_end[-1:] // MOE_ROWS).astype(jnp.int32)
    dest = dest.reshape(TOP_K, n_tok // tm, tm).transpose(1, 0, 2).reshape(n_tok // tm, 1, TOP_K * tm)
    x_sorted = moe_dispatch(u, dest, pad_start + counts, pad_end, n_tok, n_blocks * MOE_ROWS, tm)
    y_sorted = moe_experts(x_sorted, block_e, n_used, w1, w3, w2, layer)
    return moe_combine(y_sorted, dest, gate.T, u, ws1.astype(BF16), ws3.astype(BF16),
                       ws2.astype(BF16), xs, mod, final_g.reshape(1, -1), final,
                       tm, seq, n_lat, b, n_tok)


def _grid_pos_emb(rows, d):
    r, col = jnp.meshgrid(jnp.arange(rows, dtype=F32), jnp.arange(GRID_W, dtype=F32), indexing='ij')
    quarter = d // 4
    omega = 1.0 / (10000.0 ** (jnp.arange(quarter, dtype=F32) / quarter))

    def emb(p):
        a = p.reshape(-1)[:, None] * omega[None, :]
        return jnp.concatenate([jnp.sin(a), jnp.cos(a)], axis=-1)

    return jnp.concatenate([emb(r), emb(col)], axis=-1)


def kernel(x, c, ctx, c_ctx, ada_w, ada_b, norm1_g, norm2_g, w_in, w_out, hy_conv_w, hy_conv_b,
           hy_f_w1, hy_f_b1, hy_f_w2, hy_f_b2, hy_f_w3, hy_skip, gla_wa2, gla_ba2, gla_norm_g,
           router_w, router_b, exp_w1, exp_w3, exp_w2, sh_w1, sh_w3, sh_w2, final_g):
    b, l, d = x.shape
    lc = ctx.shape[1]
    depth = ada_w.shape[0]
    n_lat, n_ctx = b * l, b * lc
    n_all = n_lat + n_ctx
    hy_width = hy_skip.shape[-1]
    hy_cols = hy_conv_w.shape[-1]
    qk = gla_wa2.shape[-1]
    wv = d - hy_width
    g_cols = 2 * qk + 2 * wv
    tm = math.gcd(math.gcd(l, lc), 256)
    tm_proj = math.gcd(math.gcd(l, n_ctx), 512)
    cb = min(hy_width, 256)

    cond = jnp.concatenate([c, c_ctx[None, :]], axis=0)
    n_rows = -(-(b + 1) // 8) * 8
    cond = jnp.concatenate([cond, jnp.zeros((n_rows - b - 1, d), F32)], axis=0)
    mods = ada_table(cond, ada_w, ada_b)

    xs = embed_tokens(x.reshape(n_lat, d), _grid_pos_emb(l // GRID_W, d), ctx.reshape(n_ctx, d), tm)

    tabs = {}
    for length in (l, lc):
        if length not in tabs:
            tabs[length] = _trig_tables(length, True) + _trig_tables(length, False)
    zero_state = jnp.zeros((b, 2, wv, qk), F32)

    for layer in range(depth):
        last = layer == depth - 1
        mod = mods[layer][:, None, :]
        g1 = norm1_g[layer].reshape(1, d)
        g2 = norm2_g[layer].reshape(1, d)
        w_l = w_in[layer].astype(BF16)
        wa2 = gla_wa2[layer]
        rank = wa2.shape[1]
        wa2_bd = jnp.zeros((2 * rank, 2 * qk), F32)
        wa2_bd = wa2_bd.at[:rank, :qk].set(wa2[0]).at[rank:, qk:].set(wa2[1])
        ba2_cat = gla_ba2[layer].reshape(1, 2 * qk)
        zh, zg, la = in_proj(xs, g1, mod, w_l[:, :hy_cols], w_l[:, hy_cols:hy_cols + g_cols],
                             w_l[:, hy_cols + g_cols:], wa2_bd, ba2_cat, tm_proj, l, n_lat, b)

        filt = (hy_f_w1[layer], hy_f_b1[layer], hy_f_w2[layer], hy_f_b2[layer], hy_f_w3[layer],
                hy_skip[layer])
        yc_gla, s_ctx = gla_mixer(zg, la, zero_state, gla_norm_g[layer], lc, b, n_lat)
        y_gla, _ = gla_mixer(zg, la, s_ctx, gla_norm_g[layer], l, b, 0)
        ct, st, c0, s0 = tabs[l]
        gr, gi = hyena_spectra(l, *filt, c0, s0)
        y_hy = hyena_mixer(zh, hy_conv_w[layer], hy_conv_b[layer], ct, st, gr, gi, l, b, 0, cb)
        if last:
            n_tok = n_lat
        else:
            n_tok = n_all
            ct, st, c0, s0 = tabs[lc]
            gr, gi = hyena_spectra(lc, *filt, c0, s0)
            yc_hy = hyena_mixer(zh, hy_conv_w[layer], hy_conv_b[layer], ct, st, gr, gi,
                                lc, b, n_lat, cb)
            y_hy = jnp.concatenate([y_hy, yc_hy], axis=0)
            y_gla = jnp.concatenate([y_gla, yc_gla], axis=0)
        w_o = w_out[layer].astype(BF16)
        xs_mid, u = out_proj(y_hy, y_gla, w_o[:hy_width], w_o[hy_width:], xs, mod, g2,
                             tm, l, n_lat, b, n_tok)
        xs = moe_layer(u, xs_mid, mod, router_w[layer], router_b[layer], exp_w1, exp_w3, exp_w2,
                       layer, sh_w1[layer], sh_w3[layer], sh_w2[layer],
                       final_g, last, tm, l, n_lat, b, n_tok)
    return xs[:n_lat].reshape(b, l, d)
```

```python
import functools
import math

import numpy as np
import jax
import jax.numpy as jnp
from jax import lax
from jax.experimental import pallas as pl
from jax.experimental.pallas import tpu as pltpu

GRID_W = 64
EPS = 1e-6

HY_EMB = 33
HY_EMB_PAD = 40
HY_FREQ = 1.0
HY_TARGET = 1e-2
HY_FAST_PCT = 0.3
HY_SLOW_PCT = 1.5
HY_MIN_DECAY = math.log(1.0 / HY_TARGET) / HY_SLOW_PCT
HY_MAX_DECAY = math.log(1.0 / HY_TARGET) / HY_FAST_PCT

GLA_HEADS = 4
GLA_TAU = 16.0
GLA_CHUNK = 64
GLA_GROUP = 4
GLA_STATE_UNROLL = 4

TOP_K = 8
N_GROUPS = 8
TOPK_GROUPS = 4
ROUTED_SCALE = 2.5

HY_FREQ_CHUNK = 1024
MOE_ROWS = 512
VMEM_LIMIT = 56 * 1024 * 1024

F32 = jnp.float32
BF16 = jnp.bfloat16


def _cparams(sem, vmem=None):
    return pltpu.CompilerParams(dimension_semantics=sem, vmem_limit_bytes=vmem)


def _split(a):
    hi = a.astype(BF16)
    lo = (a - hi.astype(F32)).astype(BF16)
    return hi, lo


def _dot(a, b):
    return jnp.dot(a, b, preferred_element_type=F32)


def _dot3(a, b):
    ah, al = _split(a)
    bh, bl = _split(b)
    return _dot(ah, bh) + _dot(ah, bl) + _dot(al, bh)


def _silu(x):
    return x * (1.0 / (1.0 + jnp.exp(-x)))


def _rms(x, g):
    return x * lax.rsqrt(jnp.mean(x * x, axis=-1, keepdims=True) + EPS) * g


def _ada_kernel(c_ref, w_ref, b_ref, o_ref):
    o_ref[0] = _dot3(_silu(c_ref[...]), w_ref[0]) + b_ref[0]


def ada_table(cond_rows, ada_w, ada_b):
    depth, d, six_d = ada_w.shape
    r = cond_rows.shape[0]
    tn = 1024
    return pl.pallas_call(
        _ada_kernel,
        grid=(depth, six_d // tn),
        in_specs=[pl.BlockSpec((r, d), lambda l, j: (0, 0)),
                  pl.BlockSpec((1, d, tn), lambda l, j: (l, 0, j)),
                  pl.BlockSpec((1, 1, tn), lambda l, j: (l, 0, j))],
        out_specs=pl.BlockSpec((1, r, tn), lambda l, j: (l, 0, j)),
        out_shape=jax.ShapeDtypeStruct((depth, r, six_d), F32),
        compiler_params=_cparams(("arbitrary", "arbitrary")),
        name="ada_table",
    )(cond_rows, ada_w, ada_b.reshape(depth, 1, six_d))


def _mod_row(i, tm, seq, n_lat_tiles, b):
    return jnp.where(i < n_lat_tiles, (i * tm) // seq, b)


def _inproj_embed_kernel(n_lat_tiles, x_ref, p_ref, c_ref, *rest):
    xs_ref = rest[-1]
    i = pl.program_id(0)

    @pl.when(i < n_lat_tiles)
    def _():
        xs_ref[...] = x_ref[...] + p_ref[...]

    @pl.when(i >= n_lat_tiles)
    def _():
        xs_ref[...] = c_ref[...]

    _inproj_kernel(xs_ref, *rest[:-1])


def _inproj_kernel(x_ref, g_ref, sh_ref, sc_ref, wh_ref, wg_ref, wa_ref, wa2_ref, ba2_ref,
                   zh_ref, zg_ref, la_ref):
    x = x_ref[...]
    h = _rms(x, g_ref[...]) * (1.0 + sc_ref[0]) + sh_ref[0]
    hb = h.astype(BF16)
    zh_ref[...] = _dot(hb, wh_ref[...]).astype(BF16)
    zg_ref[...] = _dot(hb, wg_ref[...]).astype(BF16)
    za = _dot(hb, wa_ref[...])
    t = _dot3(za, wa2_ref[...]) + ba2_ref[...]
    la_ref[...] = (jnp.minimum(t, 0.0) - jnp.log(1.0 + jnp.exp(-jnp.abs(t)))) * (1.0 / GLA_TAU)


def in_proj(tokens, norm_g, mod, w_hy, w_g, w_a, wa2_bd, ba2_cat, tm, seq, n_lat, b):
    embed = isinstance(tokens, tuple)
    nlt = n_lat // tm
    if embed:
        x2, pos, ctx2 = tokens
        d = x2.shape[1]
        n = x2.shape[0] + ctx2.shape[0]
        lt = pos.shape[0] // tm
        tok_specs = [pl.BlockSpec((tm, d), lambda i: (jnp.minimum(i, nlt - 1), 0)),
                     pl.BlockSpec((tm, d), lambda i: (i % lt, 0)),
                     pl.BlockSpec((tm, d), lambda i: (jnp.maximum(i - nlt, 0), 0))]
        body = functools.partial(_inproj_embed_kernel, nlt)
    else:
        n, d = tokens.shape
        tokens = (tokens,)
        tok_specs = [pl.BlockSpec((tm, d), lambda i: (i, 0))]
        body = _inproj_kernel
    row = functools.partial(_mod_row, tm=tm, seq=seq, n_lat_tiles=nlt, b=b)
    cols = (w_hy.shape[1], w_g.shape[1], wa2_bd.shape[1]) + ((d,) if embed else ())
    dtypes = (BF16, BF16, F32) + ((F32,) if embed else ())
    full = lambda a: pl.BlockSpec(a.shape, lambda i: (0,) * a.ndim)
    return pl.pallas_call(
        body,
        grid=(n // tm,),
        in_specs=tok_specs + [full(norm_g),
                              pl.BlockSpec((1, 1, d), lambda i: (row(i), 0, 0)),
                              pl.BlockSpec((1, 1, d), lambda i: (row(i), 0, 1)),
                              full(w_hy), full(w_g), full(w_a), full(wa2_bd), full(ba2_cat)],
        out_specs=[pl.BlockSpec((tm, c), lambda i: (i, 0)) for c in cols],
        out_shape=[jax.ShapeDtypeStruct((n, c), t) for c, t in zip(cols, dtypes)],
        compiler_params=_cparams(("arbitrary",), VMEM_LIMIT),
        name="in_proj",
    )(*tokens, norm_g, mod, mod, w_hy, w_g, w_a, wa2_bd, ba2_cat)


def _trig_tables(l, half_shift):
    k = np.arange(l, dtype=np.int64)[:, None]
    n = np.arange(l, dtype=np.int64)[None, :]
    m = ((2 * k + 1) * (2 * n + (1 if half_shift else 0))) % (8 * l)
    ang = m.astype(np.float64) * (2.0 * math.pi / (8 * l))
    return (jnp.asarray(np.cos(ang).astype(np.float32).astype(BF16)),
            jnp.asarray(np.sin(ang).astype(np.float32).astype(BF16)))


def _filter_kernel(n_orders, width, l, tk,
                   pos_ref, t_ref, dl_ref, w1_ref, b1_ref, w2_ref, b2_ref, w3_ref, skip_ref,
                   c0_ref, s0_ref, gr_ref, gi_ref, hs_ref, hd_ref):
    j = pl.program_id(0)

    @pl.when(j == 0)
    def _():
        hid = jnp.sin(HY_FREQ * (_dot3(pos_ref[...], w1_ref[...]) + b1_ref[...]))
        hid = jnp.sin(HY_FREQ * (_dot3(hid, w2_ref[...]) + b2_ref[...]))
        h = _dot3(hid, w3_ref[...])
        decay = jnp.exp(-t_ref[...] * dl_ref[...])
        for o in range(n_orders):
            hf = h[:, (2 * o) * width:(2 * o + 1) * width] * decay
            hb = h[:, (2 * o + 1) * width:(2 * o + 2) * width] * decay
            hs_ref[:, o * width:(o + 1) * width] = (hf + hb).astype(BF16)
            hd_ref[:, o * width:(o + 1) * width] = (hb - hf).astype(BF16)

    scale = 1.0 / l
    gr = _dot(c0_ref[...], hs_ref[...])
    gi = _dot(s0_ref[...], hd_ref[...])
    for o in range(n_orders):
        gr_ref[o] = (gr[:, o * width:(o + 1) * width] + skip_ref[o]) * scale
        gi_ref[o] = gi[:, o * width:(o + 1) * width] * scale


def hyena_spectra(l, w1, b1, w2, b2, w3, skip, c0, s0):
    width = skip.shape[-1]
    n_orders = skip.shape[0]
    t64 = np.linspace(0.0, 1.0, l)[:, None]
    bands = (HY_EMB - 1) // 2
    w = 2.0 * math.pi * np.arange(l, dtype=np.float64)[:, None] / l
    f = np.linspace(1e-4, bands - 1, bands)[None, :]
    pos = jnp.asarray(np.concatenate([t64, np.cos(f * w), -np.sin(f * w),
                                      np.zeros((l, HY_EMB_PAD - HY_EMB))], axis=-1).astype(np.float32))
    t = jnp.asarray(t64.astype(np.float32))
    w1p = jnp.concatenate([w1, jnp.zeros((HY_EMB_PAD - HY_EMB, w1.shape[1]), F32)], axis=0)
    deltas = jnp.asarray(np.linspace(HY_MIN_DECAY, HY_MAX_DECAY, width)[None, :].astype(np.float32))
    tk = min(l, 256)
    full = lambda a: pl.BlockSpec(a.shape, lambda j: (0,) * a.ndim)
    skip3 = skip.reshape(n_orders, 1, width)
    args = (pos, t, deltas, w1p, b1.reshape(1, -1), w2, b2.reshape(1, -1), w3, skip3)
    return pl.pallas_call(
        functools.partial(_filter_kernel, n_orders, width, l, tk),
        grid=(l // tk,),
        in_specs=[full(a) for a in args] + [pl.BlockSpec((tk, l), lambda j: (j, 0)),
                                            pl.BlockSpec((tk, l), lambda j: (j, 0))],
        out_specs=[pl.BlockSpec((n_orders, tk, width), lambda j: (0, j, 0)),
                   pl.BlockSpec((n_orders, tk, width), lambda j: (0, j, 0))],
        out_shape=[jax.ShapeDtypeStruct((n_orders, l, width), F32)] * 2,
        scratch_shapes=[pltpu.VMEM((l, n_orders * width), BF16),
                        pltpu.VMEM((l, n_orders * width), BF16)],
        compiler_params=_cparams(("arbitrary",), VMEM_LIMIT),
        name="hyena_spectra",
    )(*args, c0, s0)


def _hyena_kernel(l, fc, zv_ref, z1_ref, z2_ref, wv_ref, w1_ref, w2_ref, bv_ref, b1_ref, b2_ref,
                  c_ref, s_ref, gr_ref, gi_ref, y_ref, ub_ref, a_ref, b_ref, x1_ref, x2_ref):
    rows = lax.broadcasted_iota(jnp.int32, (l, 1), 0)

    def conv3(z_ref, w_ref, bias_ref):
        z = z_ref[...].astype(F32)
        zm = jnp.where(rows == 0, 0.0, pltpu.roll(z, 1, 0))
        zp = jnp.where(rows == l - 1, 0.0, pltpu.roll(z, l - 1, 0))
        return zm * w_ref[0:1, :] + z * w_ref[1:2, :] + zp * w_ref[2:3, :] + bias_ref[...]

    ub_ref[...] = conv3(zv_ref, wv_ref, bv_ref).astype(BF16)
    x1_ref[...] = conv3(z1_ref, w1_ref, b1_ref)
    x2_ref[...] = conv3(z2_ref, w2_ref, b2_ref)
    n = l // fc

    def forward(o):
        def body(c, carry):
            r0 = pl.multiple_of(c * fc, fc)
            ur = _dot(c_ref[pl.ds(r0, fc), :], ub_ref[...])
            us = _dot(s_ref[pl.ds(r0, fc), :], ub_ref[...])
            gr = gr_ref[o, pl.ds(r0, fc), :]
            gi = gi_ref[o, pl.ds(r0, fc), :]
            a_ref[pl.ds(r0, fc), :] = (ur * gr + us * gi).astype(BF16)
            b_ref[pl.ds(r0, fc), :] = (us * gr - ur * gi).astype(BF16)
            return carry
        lax.fori_loop(0, n, body, 0)

    def inverse(gate_ref, dst_ref):
        def body(c, carry):
            r0 = pl.multiple_of(c * fc, fc)
            lc = (_dot(c_ref[pl.ds(r0, fc), :], a_ref[...])
                  + _dot(s_ref[pl.ds(r0, fc), :], b_ref[...]))
            dst_ref[pl.ds(r0, fc), :] = (gate_ref[pl.ds(r0, fc), :] * lc).astype(BF16)
            return carry
        lax.fori_loop(0, n, body, 0)

    forward(0)
    inverse(x1_ref, ub_ref)
    forward(1)
    inverse(x2_ref, y_ref)


def hyena_mixer(zh, conv_w, conv_b, ctab, stab, gr, gi, l, n_seq, row0, cb):
    width = gr.shape[-1]
    n_orders = gr.shape[0]
    ncb = width // cb
    sb0 = row0 // l
    conv_b2 = conv_b.reshape(1, -1)
    zspec = lambda part: pl.BlockSpec((l, cb), lambda c, s: (sb0 + s, part * ncb + c))
    wspec = lambda part: pl.BlockSpec((3, cb), lambda c, s: (0, part * ncb + c))
    bspec = lambda part: pl.BlockSpec((1, cb), lambda c, s: (0, part * ncb + c))
    once = pl.Buffered(1)
    tspec = pl.BlockSpec((l, l), lambda c, s: (0, 0), pipeline_mode=once)
    gspec = pl.BlockSpec((n_orders, l, cb), lambda c, s: (0, 0, c), pipeline_mode=once)
    return pl.pallas_call(
        functools.partial(_hyena_kernel, l, min(l, HY_FREQ_CHUNK)),
        grid=(ncb, n_seq),
        in_specs=[zspec(0), zspec(1), zspec(2), wspec(0), wspec(1), wspec(2),
                  bspec(0), bspec(1), bspec(2), tspec, tspec, gspec, gspec],
        out_specs=pl.BlockSpec((l, cb), lambda c, s: (s, c)),
        out_shape=jax.ShapeDtypeStruct((n_seq * l, width), BF16),
        scratch_shapes=[pltpu.VMEM((l, cb), BF16), pltpu.VMEM((l, cb), BF16),
                        pltpu.VMEM((l, cb), BF16), pltpu.VMEM((l, cb), F32),
                        pltpu.VMEM((l, cb), F32)],
        compiler_params=_cparams(("arbitrary", "arbitrary"), VMEM_LIMIT),
        name="hyena_mixer",
    )(zh, zh, zh, conv_w, conv_w, conv_w, conv_b2, conv_b2, conv_b2, ctab, stab, gr, gi)


def _gla_kernel(l, qk, wv, dk, dv,
                zg_ref, la_ref, s0_ref, g_ref, y_ref, sfin_ref,
                of_ref, ob_ref, qe_ref, ks_ref, dec_ref, st_ref):
    ch = GLA_CHUNK
    n = l // ch
    heads = qk // dk
    grp = min(GLA_GROUP, n)
    rg = grp * ch
    r_i = lax.broadcasted_iota(jnp.int32, (rg, rg), 0)
    c_i = lax.broadcasted_iota(jnp.int32, (rg, rg), 1)
    same = (r_i // ch) == (c_i // ch)
    low = same & (r_i >= c_i)
    upp = same & (r_i <= c_i)
    masks = (low, upp)
    cum2 = tuple(jnp.concatenate([jnp.concatenate([m.astype(BF16)] * 2, axis=1),
                                  jnp.concatenate([same.astype(BF16)] * 2, axis=1)], axis=0)
                 for m in masks)
    lane_head = lax.broadcasted_iota(jnp.int32, (1, qk), 1) // dk
    scale = dk ** -0.5
    o_refs = (of_ref, ob_ref)

    def local(g, d):
        r0 = pl.multiple_of(g * rg, rg)
        q = zg_ref[pl.ds(r0, rg), 0:qk].astype(F32) * scale
        k = zg_ref[pl.ds(r0, rg), qk:2 * qk].astype(F32)
        v = zg_ref[pl.ds(r0, rg), 2 * qk:2 * qk + wv]
        la = la_ref[pl.ds(r0, rg), d * qk:(d + 1) * qk]
        lh, ll = _split(la)
        bt = _dot(cum2[d], jnp.concatenate([lh, ll], axis=0))
        b, tot = bt[:rg], bt[rg:]
        qe = (q * jnp.exp(b)).astype(BF16)
        ke = (k * jnp.exp(-b)).astype(BF16)
        qe_ref[d, pl.ds(r0, rg), :] = qe
        ks_ref[d, pl.ds(r0, rg), :] = (k * jnp.exp(tot - b)).astype(BF16)
        dec = jnp.exp(tot)
        for c in range(grp):
            dec_ref[d, pl.ds(pl.multiple_of((g * grp + c) * 8, 8), 8), :] = dec[c * ch:c * ch + 8]
        parts = []
        for h in range(heads):
            qh = jnp.where(lane_head == h, qe, jnp.zeros_like(qe))
            att = lax.dot_general(qh, ke, (((1,), (1,)), ((), ())), preferred_element_type=F32)
            att = jnp.where(masks[d], att, 0.0).astype(BF16)
            parts.append(_dot(att, v[:, h * dv:(h + 1) * dv]))
        o_refs[d][pl.ds(r0, rg), :] = jnp.concatenate(parts, axis=1)

    def local_body(g, carry):
        local(g, 0)
        local(g, 1)
        return carry

    lax.fori_loop(0, n // grp, local_body, 0)

    bd = (lax.broadcasted_iota(jnp.int32, (wv, qk), 0) // dv
          == lax.broadcasted_iota(jnp.int32, (wv, qk), 1) // dk).astype(F32)
    st_ref[...] = s0_ref[0]

    def carry_state(c, d):
        r0 = pl.multiple_of(c * ch, ch)
        st = st_ref[d]
        o = lax.dot_general(qe_ref[d, pl.ds(r0, ch), :], st.astype(BF16), (((1,), (1,)), ((), ())),
                            preferred_element_type=F32)
        o_refs[d][pl.ds(r0, ch), :] += o
        v = zg_ref[pl.ds(r0, ch), 2 * qk:2 * qk + wv]
        upd = lax.dot_general(v, ks_ref[d, pl.ds(r0, ch), :], (((0,), (0,)), ((), ())),
                              preferred_element_type=F32)
        dec = dec_ref[d, pl.ds(pl.multiple_of(c * 8, 8), 1), :]
        st_ref[d] = st * dec + upd * bd

    su = min(GLA_STATE_UNROLL, n)

    def state_body(i, carry):
        for u in range(su):
            carry_state(i * su + u, 0)
            carry_state(n - 1 - (i * su + u), 1)
        return carry

    lax.fori_loop(0, n // su, state_body, 0)
    sfin_ref[0] = st_ref[...]

    def finish(i, carry):
        r0 = pl.multiple_of(i * ch, ch)
        o = of_ref[pl.ds(r0, ch), :] + ob_ref[pl.ds(r0, ch), :]
        r = zg_ref[pl.ds(r0, ch), 2 * qk + wv:2 * qk + 2 * wv].astype(F32)
        parts = [_rms(o[:, h * dv:(h + 1) * dv], g_ref[...]) for h in range(heads)]
        y_ref[pl.ds(r0, ch), :] = (jnp.concatenate(parts, axis=1) * _silu(r)).astype(BF16)
        return carry

    lax.fori_loop(0, n, finish, 0)


def gla_mixer(zg, la, s0, norm_g, l, n_seq, row0):
    qk = la.shape[1] // 2
    wv = (zg.shape[1] - 2 * qk) // 2
    dk, dv = qk // GLA_HEADS, wv // GLA_HEADS
    sb0 = row0 // l
    return pl.pallas_call(
        functools.partial(_gla_kernel, l, qk, wv, dk, dv),
        grid=(n_seq,),
        in_specs=[pl.BlockSpec((l, zg.shape[1]), lambda s: (sb0 + s, 0)),
                  pl.BlockSpec((l, la.shape[1]), lambda s: (sb0 + s, 0)),
                  pl.BlockSpec((1, 2, wv, qk), lambda s: (s, 0, 0, 0)),
                  pl.BlockSpec((1, dv), lambda s: (0, 0))],
        out_specs=[pl.BlockSpec((l, wv), lambda s: (s, 0)),
                   pl.BlockSpec((1, 2, wv, qk), lambda s: (s, 0, 0, 0))],
        out_shape=[jax.ShapeDtypeStruct((n_seq * l, wv), BF16),
                   jax.ShapeDtypeStruct((n_seq, 2, wv, qk), F32)],
        scratch_shapes=[pltpu.VMEM((l, wv), F32), pltpu.VMEM((l, wv), F32),
                        pltpu.VMEM((2, l, qk), BF16), pltpu.VMEM((2, l, qk), BF16),
                        pltpu.VMEM((2, l // GLA_CHUNK * 8, qk), F32),
                        pltpu.VMEM((2, wv, qk), F32)],
        compiler_params=_cparams(("arbitrary",), VMEM_LIMIT),
        name="gla_mixer",
    )(zg, la, s0, norm_g.reshape(1, dv))


def _outproj_kernel(yh_ref, yg_ref, wh_ref, wg_ref, x_ref, g1_ref, g_ref, sh_ref, sc_ref,
                    xo_ref, u_ref):
    delta = _dot(yh_ref[...], wh_ref[...]) + _dot(yg_ref[...], wg_ref[...])
    x = x_ref[...] + g1_ref[0] * delta
    xo_ref[...] = x
    u_ref[...] = _rms(x, g_ref[...]) * (1.0 + sc_ref[0]) + sh_ref[0]


def out_proj(y_hy, y_gla, w_hy, w_gla, xs, mod, norm_g, tm, seq, n_lat, b, n_tok):
    d = xs.shape[1]
    nlt = n_lat // tm
    row = functools.partial(_mod_row, tm=tm, seq=seq, n_lat_tiles=nlt, b=b)
    full = lambda a: pl.BlockSpec(a.shape, lambda i: (0,) * a.ndim)
    modspec = lambda col: pl.BlockSpec((1, 1, d), lambda i: (row(i), 0, col))
    tile = lambda w: pl.BlockSpec((tm, w), lambda i: (i, 0))
    return pl.pallas_call(
        _outproj_kernel,
        grid=(n_tok // tm,),
        in_specs=[tile(y_hy.shape[1]), tile(y_gla.shape[1]), full(w_hy), full(w_gla), tile(d),
                  modspec(2), full(norm_g), modspec(3), modspec(4)],
        out_specs=[tile(d), tile(d)],
        out_shape=[jax.ShapeDtypeStruct((n_tok, d), F32)] * 2,
        compiler_params=_cparams(("arbitrary",), VMEM_LIMIT),
        name="out_proj",
    )(y_hy, y_gla, w_hy, w_gla, xs, mod, norm_g, mod, mod)


def _router_kernel(n_exp, tm, u_ref, wh_ref, wl_ref, b_ref, tri_ref,
                   idx_ref, gate_ref, rank_ref, cnt_ref):
    i = pl.program_id(0)

    @pl.when(i == 0)
    def _():
        cnt_ref[...] = jnp.zeros_like(cnt_ref)

    uh, ul = _split(u_ref[...])
    nt = (((1,), (1,)), ((), ()))
    logits = (lax.dot_general(wh_ref[...], uh, nt, preferred_element_type=F32)
              + lax.dot_general(wh_ref[...], ul, nt, preferred_element_type=F32)
              + lax.dot_general(wl_ref[...], uh, nt, preferred_element_type=F32))
    s = 1.0 / (1.0 + jnp.exp(-logits))
    sel = s + b_ref[...]
    gsz = n_exp // N_GROUPS
    neg = -jnp.inf
    rows_g = lax.broadcasted_iota(jnp.int32, (gsz, tm), 0)
    blocks, gscore = [], []
    for g in range(N_GROUPS):
        blk = sel[g * gsz:(g + 1) * gsz, :]
        m1 = jnp.max(blk, axis=0, keepdims=True)
        i1 = jnp.min(jnp.where(blk == m1, rows_g, gsz), axis=0, keepdims=True)
        m2 = jnp.max(jnp.where(rows_g == i1, neg, blk), axis=0, keepdims=True)
        blocks.append(blk)
        gscore.append(m1 + m2)
    masked = []
    for g in range(N_GROUPS):
        beaten = jnp.zeros((1, tm), jnp.int32)
        for h in range(N_GROUPS):
            if h == g:
                continue
            wins = (gscore[h] > gscore[g]) | ((gscore[h] == gscore[g]) & (h < g))
            beaten = beaten + wins.astype(jnp.int32)
        masked.append(jnp.where(beaten < TOPK_GROUPS, blocks[g], neg))
    sel = jnp.concatenate(masked, axis=0)

    rows = lax.broadcasted_iota(jnp.int32, (n_exp, tm), 0)
    chosen = jnp.zeros((n_exp, tm), F32)
    idxs, gates = [], []
    for _ in range(TOP_K):
        m = jnp.max(sel, axis=0, keepdims=True)
        ik = jnp.min(jnp.where(sel == m, rows, n_exp), axis=0, keepdims=True)
        hit = rows == ik
        gates.append(jnp.sum(jnp.where(hit, s, 0.0), axis=0, keepdims=True))
        sel = jnp.where(hit, neg, sel)
        chosen = jnp.where(hit, 1.0, chosen)
        idxs.append(ik)
    gsum = gates[0]
    for g in gates[1:]:
        gsum = gsum + g
    inv = ROUTED_SCALE / gsum
    before = _dot(chosen.astype(BF16), tri_ref[...]) + cnt_ref[...]
    for k in range(TOP_K):
        idx_ref[k:k + 1, :] = idxs[k]
        gate_ref[k:k + 1, :] = gates[k] * inv
        rk = jnp.sum(jnp.where(rows == idxs[k], before, 0.0), axis=0, keepdims=True)
        rank_ref[k:k + 1, :] = rk.astype(jnp.int32)
    cnt_ref[...] += jnp.sum(chosen, axis=1, keepdims=True)


def moe_router(u, wr_hi, wr_lo, b_r, n_tok, tm):
    n_exp, d = wr_hi.shape
    tri = (lax.broadcasted_iota(jnp.int32, (tm, tm), 0)
           < lax.broadcasted_iota(jnp.int32, (tm, tm), 1)).astype(BF16)
    full = lambda a: pl.BlockSpec(a.shape, lambda i: (0,) * a.ndim)
    tok = pl.BlockSpec((TOP_K, tm), lambda i: (0, i))
    return pl.pallas_call(
        functools.partial(_router_kernel, n_exp, tm),
        grid=(n_tok // tm,),
        in_specs=[pl.BlockSpec((tm, d), lambda i: (i, 0)), full(wr_hi), full(wr_lo),
                  pl.BlockSpec((n_exp, 1), lambda i: (0, 0)), full(tri)],
        out_specs=[tok, tok, tok, pl.BlockSpec((n_exp, 1), lambda i: (0, 0))],
        out_shape=[jax.ShapeDtypeStruct((TOP_K, n_tok), jnp.int32),
                   jax.ShapeDtypeStruct((TOP_K, n_tok), F32),
                   jax.ShapeDtypeStruct((TOP_K, n_tok), jnp.int32),
                   jax.ShapeDtypeStruct((n_exp, 1), F32)],
        compiler_params=_cparams(("arbitrary",), VMEM_LIMIT),
        name="moe_router",
    )(u, wr_hi, wr_lo, b_r.reshape(n_exp, 1), tri)


LANES = 128


def _start_row_copy(src, src_row, dst, dst_row, sem, queue):
    pltpu.async_copy(src.at[src_row], dst.at[dst_row], sem, priority=queue)


def _store_rows_tiled(ref, x):
    for j in range(ref.shape[-2]):
        ref[:, j, :] = x[:, j * LANES:(j + 1) * LANES]


def _load_rows_tiled(ref):
    return jnp.concatenate([ref[:, j, :] for j in range(ref.shape[-2])], axis=1)


def _pack_bf16_pairs(x):
    w = x.shape[1] // 2
    lo = pltpu.bitcast(x[:, :w].astype(BF16).astype(F32), jnp.uint32) >> 16
    hi = pltpu.bitcast(x[:, w:].astype(BF16).astype(F32), jnp.uint32)
    return hi | lo


def _unpack_bf16_pairs(p):
    lo = pltpu.bitcast(p << 16, F32)
    hi = pltpu.bitcast(p & jnp.uint32(0xFFFF0000), F32)
    return jnp.concatenate([lo, hi], axis=1).astype(BF16)


def _dest_kernel(n_exp, tm, idx_ref, rank_ref, ps_ref, dest_ref):
    rows = lax.broadcasted_iota(jnp.int32, (n_exp, tm), 0)
    ps = ps_ref[...]
    for k in range(TOP_K):
        base = jnp.sum(jnp.where(rows == idx_ref[k:k + 1, :], ps, 0), axis=0, keepdims=True)
        dest_ref[k:k + 1, :] = base + rank_ref[k:k + 1, :]


def moe_dest(idx, rank, pad_start, tm):
    n_tok = idx.shape[1]
    n_exp = pad_start.shape[0]
    tok = pl.BlockSpec((TOP_K, tm), lambda i: (0, i))
    return pl.pallas_call(
        functools.partial(_dest_kernel, n_exp, tm),
        grid=(n_tok // tm,),
        in_specs=[tok, tok, pl.BlockSpec((n_exp, 1), lambda i: (0, 0))],
        out_specs=tok,
        out_shape=jax.ShapeDtypeStruct((TOP_K, n_tok), jnp.int32),
        compiler_params=_cparams(("arbitrary",)),
        name="moe_dest",
    )(idx, rank, pad_start.reshape(n_exp, 1))


def _wait_rows(src, dst, sem, tm):
    for _ in range(TOP_K):
        pltpu.make_async_copy(src.at[pl.ds(0, tm)], dst.at[pl.ds(0, tm)], sem).wait()


ZERO_ROWS = MOE_ROWS // 2


def _zero_padding(n_exp, ps_ref, pe_ref, xs_ref, z_ref, sem, wait):
    def go(copy):
        copy.wait() if wait else copy.start()

    def per_expert(e, carry):
        s = ps_ref[e]
        m = pe_ref[e] - s
        bit = ZERO_ROWS
        while bit:
            start = s + (m & ~(2 * bit - 1))

            @pl.when((m & bit) != 0)
            def _():
                go(pltpu.make_async_copy(z_ref.at[pl.ds(0, bit)], xs_ref.at[pl.ds(start, bit)], sem))
            bit //= 2
        return carry

    lax.fori_loop(0, n_exp, per_expert, 0)

    tail = pe_ref[n_exp - 1]

    def per_chunk(j, carry):
        go(pltpu.make_async_copy(z_ref, xs_ref.at[pl.ds(tail + j * ZERO_ROWS, ZERO_ROWS)], sem))
        return carry

    lax.fori_loop(0, (xs_ref.shape[0] - tail) // ZERO_ROWS, per_chunk, 0)


def _dispatch_kernel(tm, n_steps, n_exp, ps_ref, pe_ref, dest_ref, u_ref, xs_ref, pk_ref, z_ref,
                     sems, zsem):
    i = pl.program_id(0)
    slot = i % 2
    pk = pk_ref.at[slot]
    sem = sems.at[slot]

    @pl.when(i == 0)
    def _():
        z_ref[...] = jnp.zeros_like(z_ref)
        _zero_padding(n_exp, ps_ref, pe_ref, xs_ref, z_ref, zsem, wait=False)

    @pl.when(i >= 2)
    def _():
        _wait_rows(pk, xs_ref, sem, tm)

    _store_rows_tiled(pk, _pack_bf16_pairs(u_ref[...]))

    def issue(t, carry):
        for k in range(TOP_K):
            _start_row_copy(pk, t, xs_ref, dest_ref[0, 0, k * tm + t], sem, k % 2)
        return carry

    lax.fori_loop(0, tm, issue, 0)

    @pl.when(i == n_steps - 1)
    def _():
        if n_steps > 1:
            _wait_rows(pk_ref.at[1 - slot], xs_ref, sems.at[1 - slot], tm)
        _wait_rows(pk, xs_ref, sem, tm)
        _zero_padding(n_exp, ps_ref, pe_ref, xs_ref, z_ref, zsem, wait=True)


def moe_dispatch(u, dest, pad_from, pad_end, n_tok, n_slots, tm):
    d = u.shape[1]
    sub = d // 2 // LANES
    n_exp = pad_end.shape[0]
    grid_spec = pltpu.PrefetchScalarGridSpec(
        num_scalar_prefetch=2,
        grid=(n_tok // tm,),
        in_specs=[pl.BlockSpec((1, 1, TOP_K * tm), lambda i, *_: (i, 0, 0), memory_space=pltpu.SMEM),
                  pl.BlockSpec((tm, d), lambda i, *_: (i, 0))],
        out_specs=pl.BlockSpec(memory_space=pl.ANY),
        scratch_shapes=[pltpu.VMEM((2, tm, sub, LANES), jnp.uint32),
                        pltpu.VMEM((ZERO_ROWS, sub, LANES), jnp.uint32),
                        pltpu.SemaphoreType.DMA((2,)), pltpu.SemaphoreType.DMA])
    return pl.pallas_call(
        functools.partial(_dispatch_kernel, tm, n_tok // tm, n_exp),
        grid_spec=grid_spec,
        out_shape=jax.ShapeDtypeStruct((n_slots, sub, LANES), jnp.uint32),
        compiler_params=_cparams(("arbitrary",)),
        name="moe_dispatch",
    )(pad_from, pad_end, dest, u)


def _gmm_kernel(layer, be_ref, nx_ref, fl_ref, nu_ref, x_ref, w1_hbm, w3_hbm, w2_hbm, y_ref,
                w1f, w3f, w2f, w1b, w3b, w2b, sems):
    i = pl.program_id(0)
    fl = fl_ref[i]
    sub = x_ref.shape[0] // MOE_ROWS

    def fetch(e, s):
        return (pltpu.make_async_copy(w1_hbm.at[layer, e], w1f.at[s], sems.at[s]),
                pltpu.make_async_copy(w3_hbm.at[layer, e], w3f.at[s], sems.at[s]),
                pltpu.make_async_copy(w2_hbm.at[layer, e], w2f.at[s], sems.at[s]))

    @pl.when(i < nu_ref[0])
    def _():
        @pl.when((fl & 1) != 0)
        def _():
            s = (fl >> 1) & 1

            @pl.when(i == 0)
            def _():
                for c in fetch(be_ref[i], s):
                    c.start()

            for c in fetch(be_ref[i], s):
                c.wait()

            @pl.when((fl & 4) != 0)
            def _():
                for c in fetch(nx_ref[i], 1 - s):
                    c.start()

            w1b[...] = w1f[s].astype(BF16)
            w3b[...] = w3f[s].astype(BF16)
            w2b[...] = w2f[s].astype(BF16)

        x = _unpack_bf16_pairs(jnp.concatenate(
            [x_ref[pl.ds(j, MOE_ROWS, stride=sub), :] for j in range(sub)], axis=1))
        h = _silu(_dot(x, w1b[...])) * _dot(x, w3b[...])
        y = _pack_bf16_pairs(_dot(h.astype(BF16), w2b[...]))
        for j in range(sub):
            y_ref[pl.ds(j, MOE_ROWS, stride=sub), :] = y[:, j * LANES:(j + 1) * LANES]

    @pl.when(i >= nu_ref[0])
    def _():
        y_ref[...] = jnp.zeros_like(y_ref)


def moe_experts(x_sorted, block_e, n_used, w1, w3, w2, layer):
    n_blocks = block_e.shape[0]
    _, _, d, f = w1.shape
    n_slots, sub, _ = x_sorted.shape
    idx = jnp.arange(n_blocks, dtype=jnp.int32)
    first = (idx == 0) | (block_e != jnp.roll(block_e, 1))
    seg = jnp.cumsum(first.astype(jnp.int32)) - 1
    first_at_or_after = lax.cummin(jnp.where(first, idx, n_blocks)[::-1])[::-1]
    nxt_idx = jnp.concatenate([first_at_or_after[1:], jnp.full((1,), n_blocks, jnp.int32)])
    has_next = nxt_idx < n_used[0]
    nxt_e = block_e[jnp.minimum(nxt_idx, n_blocks - 1)]
    flags = (first.astype(jnp.int32) | ((seg & 1) << 1) | (has_next.astype(jnp.int32) << 2))
    grid_spec = pltpu.PrefetchScalarGridSpec(
        num_scalar_prefetch=4,
        grid=(n_blocks,),
        in_specs=[pl.BlockSpec((MOE_ROWS * sub, LANES), lambda i, *_: (i, 0)),
                  pl.BlockSpec(memory_space=pl.ANY),
                  pl.BlockSpec(memory_space=pl.ANY),
                  pl.BlockSpec(memory_space=pl.ANY)],
        out_specs=pl.BlockSpec((MOE_ROWS * sub, LANES), lambda i, *_: (i, 0)),
        scratch_shapes=[pltpu.VMEM((2, d, f), F32), pltpu.VMEM((2, d, f), F32),
                        pltpu.VMEM((2, f, d), F32),
                        pltpu.VMEM((d, f), BF16), pltpu.VMEM((d, f), BF16),
                        pltpu.VMEM((f, d), BF16), pltpu.SemaphoreType.DMA((2,))])
    y = pl.pallas_call(
        functools.partial(_gmm_kernel, layer),
        grid_spec=grid_spec,
        out_shape=jax.ShapeDtypeStruct((n_slots * sub, LANES), jnp.uint32),
        compiler_params=_cparams(("arbitrary",), VMEM_LIMIT),
        name="moe_experts",
    )(block_e, nxt_e, flags, n_used, x_sorted.reshape(n_slots * sub, LANES), w1, w3, w2)
    return y.reshape(n_slots, sub, LANES)


def _combine_kernel(tm, n_steps, final, dest_ref, nxt_ref, y_ref, gate_ref, u_ref, ws1_ref, ws3_ref,
                    ws2_ref, x_ref, g2_ref, fg_ref, o_ref, buf_ref, routed_ref, sems):
    i = pl.program_id(0)
    slot = i % 2
    grp = 8

    def gather(d_ref, s, g):
        for tt in range(grp):
            t = g * grp + tt
            for k in range(TOP_K):
                _start_row_copy(y_ref, d_ref[0, 0, k * tm + t], buf_ref.at[s, k], t, sems.at[s], k % 2)

    def reduce(g, s):
        r0 = pl.multiple_of(g * grp, grp)
        gt = gate_ref[pl.ds(r0, grp), :]
        lo = hi = None
        for k in range(TOP_K):
            p = buf_ref[s, k, pl.ds(r0, grp)]
            gk = gt[:, k:k + 1][:, :, None]
            lo_k = pltpu.bitcast(p << 16, F32) * gk
            hi_k = pltpu.bitcast(p & jnp.uint32(0xFFFF0000), F32) * gk
            lo = lo_k if lo is None else lo + lo_k
            hi = hi_k if hi is None else hi + hi_k
        sub = lo.shape[1]
        routed_ref[pl.ds(r0, grp), 0:sub, :] = lo
        routed_ref[pl.ds(r0, grp), sub:2 * sub, :] = hi

    @pl.when(i == 0)
    def _():
        lax.fori_loop(0, tm // grp, lambda g, c: (gather(dest_ref, slot, g), c)[1], 0)

    _wait_rows(y_ref, buf_ref.at[slot, 0], sems.at[slot], tm)

    @pl.when(i + 1 < n_steps)
    def _():
        def both(g, c):
            gather(nxt_ref, 1 - slot, g)
            reduce(g, slot)
            return c
        lax.fori_loop(0, tm // grp, both, 0)

    @pl.when(i + 1 >= n_steps)
    def _():
        lax.fori_loop(0, tm // grp, lambda g, c: (reduce(g, slot), c)[1], 0)

    ub = u_ref[...].astype(BF16)
    hs = _silu(_dot(ub, ws1_ref[...])) * _dot(ub, ws3_ref[...])
    shared = _dot(hs.astype(BF16), ws2_ref[...])
    x = x_ref[...] + g2_ref[0] * (_load_rows_tiled(routed_ref) + shared)
    o_ref[...] = _rms(x, fg_ref[...]) if final else x


def moe_combine(y_sorted, dest, gate_rep, u, ws1, ws3, ws2, xs, mod, final_g, final,
                tm, seq, n_lat, b, n_tok):
    d = u.shape[1]
    nlt = n_lat // tm
    row = functools.partial(_mod_row, tm=tm, seq=seq, n_lat_tiles=nlt, b=b)
    full = lambda a: pl.BlockSpec(a.shape, lambda i: (0,) * a.ndim)
    tile = pl.BlockSpec((tm, d), lambda i: (i, 0))
    n_steps = n_tok // tm
    return pl.pallas_call(
        functools.partial(_combine_kernel, tm, n_steps, final),
        grid=(n_steps,),
        in_specs=[pl.BlockSpec((1, 1, TOP_K * tm), lambda i: (i, 0, 0), memory_space=pltpu.SMEM),
                  pl.BlockSpec((1, 1, TOP_K * tm),
                               lambda i: (jnp.minimum(i + 1, n_steps - 1), 0, 0),
                               memory_space=pltpu.SMEM),
                  pl.BlockSpec(memory_space=pl.ANY),
                  pl.BlockSpec((tm, TOP_K), lambda i: (i, 0)),
                  tile, full(ws1), full(ws3), full(ws2), tile,
                  pl.BlockSpec((1, 1, d), lambda i: (row(i), 0, 5)),
                  full(final_g)],
        out_specs=tile,
        out_shape=jax.ShapeDtypeStruct((n_tok, d), F32),
        scratch_shapes=[pltpu.VMEM((2, TOP_K, tm) + y_sorted.shape[1:], jnp.uint32),
                        pltpu.VMEM((tm, d // LANES, LANES), F32), pltpu.SemaphoreType.DMA((2,))],
        compiler_params=_cparams(("arbitrary",), VMEM_LIMIT),
        name="moe_combine",
    )(dest, dest, y_sorted, gate_rep, u, ws1, ws3, ws2, xs, mod, final_g)


def moe_layer(u, xs, mod, router_w, router_b, w1, w3, w2, layer, ws1, ws3, ws2, final_g, final,
              tm, seq, n_lat, b, n_tok):
    n_exp = router_w.shape[1]
    wr_hi, wr_lo = _split(router_w.T)
    idx, gate, rank, cnt = moe_router(u, wr_hi, wr_lo, router_b, n_tok, tm)
    counts = cnt[:, 0].astype(jnp.int32)
    padded = (counts + MOE_ROWS - 1) // MOE_ROWS * MOE_ROWS
    pad_end = jnp.cumsum(padded)
    pad_start = pad_end - padded
    dest = moe_dest(idx, rank, pad_start, math.gcd(n_tok, 1024))
    n_blocks = -(-(n_tok * TOP_K + n_exp * (MOE_ROWS - 1)) // MOE_ROWS)
    block_start = jnp.arange(n_blocks, dtype=jnp.int32) * MOE_ROWS
    block_e = jnp.minimum(jnp.sum((pad_end[None, :] <= block_start[:, None]).astype(jnp.int32), axis=1),
                          n_exp - 1)
    n_used = (pad_end[-1:] // MOE_ROWS).astype(jnp.int32)
    dest = dest.reshape(TOP_K, n_tok // tm, tm).transpose(1, 0, 2).reshape(n_tok // tm, 1, TOP_K * tm)
    x_sorted = moe_dispatch(u, dest, pad_start + counts, pad_end, n_tok, n_blocks * MOE_ROWS, tm)
    y_sorted = moe_experts(x_sorted, block_e, n_used, w1, w3, w2, layer)
    return moe_combine(y_sorted, dest, gate.T, u, ws1.astype(BF16), ws3.astype(BF16),
                       ws2.astype(BF16), xs, mod, final_g.reshape(1, -1), final,
                       tm, seq, n_lat, b, n_tok)


def _grid_pos_emb(rows, d):
    r, col = np.meshgrid(np.arange(rows, dtype=np.float64), np.arange(GRID_W, dtype=np.float64),
                         indexing='ij')
    quarter = d // 4
    omega = 1.0 / (10000.0 ** (np.arange(quarter, dtype=np.float64) / quarter))

    def emb(p):
        a = p.reshape(-1)[:, None] * omega[None, :]
        return np.concatenate([np.sin(a), np.cos(a)], axis=-1)

    return jnp.asarray(np.concatenate([emb(r), emb(col)], axis=-1).astype(np.float32))


def kernel(x, c, ctx, c_ctx, ada_w, ada_b, norm1_g, norm2_g, w_in, w_out, hy_conv_w, hy_conv_b,
           hy_f_w1, hy_f_b1, hy_f_w2, hy_f_b2, hy_f_w3, hy_skip, gla_wa2, gla_ba2, gla_norm_g,
           router_w, router_b, exp_w1, exp_w3, exp_w2, sh_w1, sh_w3, sh_w2, final_g):
    b, l, d = x.shape
    lc = ctx.shape[1]
    depth = ada_w.shape[0]
    n_lat, n_ctx = b * l, b * lc
    n_all = n_lat + n_ctx
    hy_width = hy_skip.shape[-1]
    hy_cols = hy_conv_w.shape[-1]
    qk = gla_wa2.shape[-1]
    wv = d - hy_width
    g_cols = 2 * qk + 2 * wv
    tm = math.gcd(math.gcd(l, lc), 256)
    tm_proj = math.gcd(math.gcd(l, n_ctx), 512)
    cb = min(hy_width, 256)

    cond = jnp.concatenate([c, c_ctx[None, :]], axis=0)
    n_rows = -(-(b + 1) // 8) * 8
    cond = jnp.concatenate([cond, jnp.zeros((n_rows - b - 1, d), F32)], axis=0)
    mods = ada_table(cond, ada_w, ada_b)

    xs = (x.reshape(n_lat, d), _grid_pos_emb(l // GRID_W, d), ctx.reshape(n_ctx, d))

    tabs = {}
    for length in (l, lc):
        if length not in tabs:
            tabs[length] = _trig_tables(length, True) + _trig_tables(length, False)
    zero_state = jnp.zeros((b, 2, wv, qk), F32)

    for layer in range(depth):
        last = layer == depth - 1
        mod = mods[layer][:, None, :]
        g1 = norm1_g[layer].reshape(1, d)
        g2 = norm2_g[layer].reshape(1, d)
        w_l = w_in[layer].astype(BF16)
        wa2 = gla_wa2[layer]
        rank = wa2.shape[1]
        wa2_bd = jnp.zeros((2 * rank, 2 * qk), F32)
        wa2_bd = wa2_bd.at[:rank, :qk].set(wa2[0]).at[rank:, qk:].set(wa2[1])
        ba2_cat = gla_ba2[layer].reshape(1, 2 * qk)
        outs = in_proj(xs, g1, mod, w_l[:, :hy_cols], w_l[:, hy_cols:hy_cols + g_cols],
                       w_l[:, hy_cols + g_cols:], wa2_bd, ba2_cat, tm_proj, l, n_lat, b)
        zh, zg, la = outs[:3]
        if layer == 0:
            xs = outs[3]

        filt = (hy_f_w1[layer], hy_f_b1[layer], hy_f_w2[layer], hy_f_b2[layer], hy_f_w3[layer],
                hy_skip[layer])
        yc_gla, s_ctx = gla_mixer(zg, la, zero_state, gla_norm_g[layer], lc, b, n_lat)
        y_gla, _ = gla_mixer(zg, la, s_ctx, gla_norm_g[layer], l, b, 0)
        ct, st, c0, s0 = tabs[l]
        gr, gi = hyena_spectra(l, *filt, c0, s0)
        y_hy = hyena_mixer(zh, hy_conv_w[layer], hy_conv_b[layer], ct, st, gr, gi, l, b, 0, cb)
        if last:
            n_tok = n_lat
        else:
            n_tok = n_all
            ct, st, c0, s0 = tabs[lc]
            gr, gi = hyena_spectra(lc, *filt, c0, s0)
            yc_hy = hyena_mixer(zh, hy_conv_w[layer], hy_conv_b[layer], ct, st, gr, gi,
                                lc, b, n_lat, cb)
            y_hy = jnp.concatenate([y_hy, yc_hy], axis=0)
            y_gla = jnp.concatenate([y_gla, yc_gla], axis=0)
        w_o = w_out[layer].astype(BF16)
        xs_mid, u = out_proj(y_hy, y_gla, w_o[:hy_width], w_o[hy_width:], xs, mod, g2,
                             tm, l, n_lat, b, n_tok)
        xs = moe_layer(u, xs_mid, mod, router_w[layer], router_b[layer], exp_w1, exp_w3, exp_w2,
                       layer, sh_w1[layer], sh_w3[layer], sh_w2[layer],
                       final_g, last, tm, l, n_lat, b, n_tok)
    return xs[:n_lat].reshape(b, l, d)
```

```python
import functools
import math

import numpy as np
import jax
import jax.numpy as jnp
from jax import lax
from jax.experimental import pallas as pl
from jax.experimental.pallas import tpu as pltpu

GRID_W = 64
EPS = 1e-6

HY_EMB = 33
HY_EMB_PAD = 40
HY_FREQ = 1.0
HY_TARGET = 1e-2
HY_FAST_PCT = 0.3
HY_SLOW_PCT = 1.5
HY_MIN_DECAY = math.log(1.0 / HY_TARGET) / HY_SLOW_PCT
HY_MAX_DECAY = math.log(1.0 / HY_TARGET) / HY_FAST_PCT

GLA_HEADS = 4
GLA_TAU = 16.0
GLA_CHUNK = 64
GLA_GROUP = 4
GLA_STATE_UNROLL = 4

TOP_K = 8
N_GROUPS = 8
TOPK_GROUPS = 4
ROUTED_SCALE = 2.5

HY_FREQ_CHUNK = 1024
MOE_ROWS = 512
VMEM_LIMIT = 56 * 1024 * 1024

F32 = jnp.float32
BF16 = jnp.bfloat16


def _cparams(sem, vmem=None):
    return pltpu.CompilerParams(dimension_semantics=sem, vmem_limit_bytes=vmem)


def _split(a):
    hi = a.astype(BF16)
    lo = (a - hi.astype(F32)).astype(BF16)
    return hi, lo


def _dot(a, b):
    return jnp.dot(a, b, preferred_element_type=F32)


def _dot3(a, b):
    ah, al = _split(a)
    bh, bl = _split(b)
    return _dot(ah, bh) + _dot(ah, bl) + _dot(al, bh)


def _silu(x):
    return x * (1.0 / (1.0 + jnp.exp(-x)))


def _rms(x, g):
    return x * lax.rsqrt(jnp.mean(x * x, axis=-1, keepdims=True) + EPS) * g


def _ada_kernel(c_ref, w_ref, b_ref, o_ref):
    o_ref[0] = _dot3(_silu(c_ref[...]), w_ref[0]) + b_ref[0]


def ada_table(cond_rows, ada_w, ada_b):
    depth, d, six_d = ada_w.shape
    r = cond_rows.shape[0]
    tn = 1024
    return pl.pallas_call(
        _ada_kernel,
        grid=(depth, six_d // tn),
        in_specs=[pl.BlockSpec((r, d), lambda l, j: (0, 0)),
                  pl.BlockSpec((1, d, tn), lambda l, j: (l, 0, j)),
                  pl.BlockSpec((1, 1, tn), lambda l, j: (l, 0, j))],
        out_specs=pl.BlockSpec((1, r, tn), lambda l, j: (l, 0, j)),
        out_shape=jax.ShapeDtypeStruct((depth, r, six_d), F32),
        compiler_params=_cparams(("arbitrary", "arbitrary")),
        name="ada_table",
    )(cond_rows, ada_w, ada_b.reshape(depth, 1, six_d))


def _mod_row(i, tm, seq, n_lat_tiles, b):
    return jnp.where(i < n_lat_tiles, (i * tm) // seq, b)


def _inproj_embed_kernel(n_lat_tiles, x_ref, p_ref, c_ref, *rest):
    xs_ref = rest[-1]
    i = pl.program_id(0)

    @pl.when(i < n_lat_tiles)
    def _():
        xs_ref[...] = x_ref[...] + p_ref[...]

    @pl.when(i >= n_lat_tiles)
    def _():
        xs_ref[...] = c_ref[...]

    _inproj_kernel(xs_ref, *rest[:-1])


def _inproj_kernel(x_ref, g_ref, sh_ref, sc_ref, wh_ref, wg_ref, wa_ref, wa2_ref, ba2_ref,
                   zh_ref, zg_ref, la_ref):
    x = x_ref[...]
    h = _rms(x, g_ref[...]) * (1.0 + sc_ref[0]) + sh_ref[0]
    hb = h.astype(BF16)
    zh_ref[...] = _dot(hb, wh_ref[...]).astype(BF16)
    zg_ref[...] = _dot(hb, wg_ref[...]).astype(BF16)
    za = _dot(hb, wa_ref[...])
    t = _dot3(za, wa2_ref[...]) + ba2_ref[...]
    la_ref[...] = (jnp.minimum(t, 0.0) - jnp.log(1.0 + jnp.exp(-jnp.abs(t)))) * (1.0 / GLA_TAU)


def in_proj(tokens, norm_g, mod, w_hy, w_g, w_a, wa2_bd, ba2_cat, tm, seq, n_lat, b):
    embed = isinstance(tokens, tuple)
    nlt = n_lat // tm
    if embed:
        x2, pos, ctx2 = tokens
        d = x2.shape[1]
        n = x2.shape[0] + ctx2.shape[0]
        lt = pos.shape[0] // tm
        tok_specs = [pl.BlockSpec((tm, d), lambda i: (jnp.minimum(i, nlt - 1), 0)),
                     pl.BlockSpec((tm, d), lambda i: (i % lt, 0)),
                     pl.BlockSpec((tm, d), lambda i: (jnp.maximum(i - nlt, 0), 0))]
        body = functools.partial(_inproj_embed_kernel, nlt)
    else:
        n, d = tokens.shape
        tokens = (tokens,)
        tok_specs = [pl.BlockSpec((tm, d), lambda i: (i, 0))]
        body = _inproj_kernel
    row = functools.partial(_mod_row, tm=tm, seq=seq, n_lat_tiles=nlt, b=b)
    cols = (w_hy.shape[1], w_g.shape[1], wa2_bd.shape[1]) + ((d,) if embed else ())
    dtypes = (BF16, BF16, F32) + ((F32,) if embed else ())
    full = lambda a: pl.BlockSpec(a.shape, lambda i: (0,) * a.ndim)
    return pl.pallas_call(
        body,
        grid=(n // tm,),
        in_specs=tok_specs + [full(norm_g),
                              pl.BlockSpec((1, 1, d), lambda i: (row(i), 0, 0)),
                              pl.BlockSpec((1, 1, d), lambda i: (row(i), 0, 1)),
                              full(w_hy), full(w_g), full(w_a), full(wa2_bd), full(ba2_cat)],
        out_specs=[pl.BlockSpec((tm, c), lambda i: (i, 0)) for c in cols],
        out_shape=[jax.ShapeDtypeStruct((n, c), t) for c, t in zip(cols, dtypes)],
        compiler_params=_cparams(("arbitrary",), VMEM_LIMIT),
        name="in_proj",
    )(*tokens, norm_g, mod, mod, w_hy, w_g, w_a, wa2_bd, ba2_cat)


def _trig_tables(l, half_shift):
    k = np.arange(l, dtype=np.int64)[:, None]
    n = np.arange(l, dtype=np.int64)[None, :]
    m = ((2 * k + 1) * (2 * n + (1 if half_shift else 0))) % (8 * l)
    ang = m.astype(np.float64) * (2.0 * math.pi / (8 * l))
    return (jnp.asarray(np.cos(ang).astype(np.float32).astype(BF16)),
            jnp.asarray(np.sin(ang).astype(np.float32).astype(BF16)))


def _filter_kernel(n_orders, width, l, tk,
                   pos_ref, t_ref, dl_ref, w1_ref, b1_ref, w2_ref, b2_ref, w3_ref, skip_ref,
                   c0_ref, s0_ref, gr_ref, gi_ref, hs_ref, hd_ref):
    j = pl.program_id(0)

    @pl.when(j == 0)
    def _():
        hid = jnp.sin(HY_FREQ * (_dot3(pos_ref[...], w1_ref[...]) + b1_ref[...]))
        hid = jnp.sin(HY_FREQ * (_dot3(hid, w2_ref[...]) + b2_ref[...]))
        h = _dot3(hid, w3_ref[...])
        decay = jnp.exp(-t_ref[...] * dl_ref[...])
        for o in range(n_orders):
            hf = h[:, (2 * o) * width:(2 * o + 1) * width] * decay
            hb = h[:, (2 * o + 1) * width:(2 * o + 2) * width] * decay
            hs_ref[:, o * width:(o + 1) * width] = (hf + hb).astype(BF16)
            hd_ref[:, o * width:(o + 1) * width] = (hb - hf).astype(BF16)

    scale = 1.0 / l
    gr = _dot(c0_ref[...], hs_ref[...])
    gi = _dot(s0_ref[...], hd_ref[...])
    for o in range(n_orders):
        gr_ref[o] = (gr[:, o * width:(o + 1) * width] + skip_ref[o]) * scale
        gi_ref[o] = gi[:, o * width:(o + 1) * width] * scale


def hyena_spectra(l, w1, b1, w2, b2, w3, skip, c0, s0):
    width = skip.shape[-1]
    n_orders = skip.shape[0]
    t64 = np.linspace(0.0, 1.0, l)[:, None]
    bands = (HY_EMB - 1) // 2
    w = 2.0 * math.pi * np.arange(l, dtype=np.float64)[:, None] / l
    f = np.linspace(1e-4, bands - 1, bands)[None, :]
    pos = jnp.asarray(np.concatenate([t64, np.cos(f * w), -np.sin(f * w),
                                      np.zeros((l, HY_EMB_PAD - HY_EMB))], axis=-1).astype(np.float32))
    t = jnp.asarray(t64.astype(np.float32))
    w1p = jnp.concatenate([w1, jnp.zeros((HY_EMB_PAD - HY_EMB, w1.shape[1]), F32)], axis=0)
    deltas = jnp.asarray(np.linspace(HY_MIN_DECAY, HY_MAX_DECAY, width)[None, :].astype(np.float32))
    tk = min(l, 256)
    full = lambda a: pl.BlockSpec(a.shape, lambda j: (0,) * a.ndim)
    skip3 = skip.reshape(n_orders, 1, width)
    args = (pos, t, deltas, w1p, b1.reshape(1, -1), w2, b2.reshape(1, -1), w3, skip3)
    return pl.pallas_call(
        functools.partial(_filter_kernel, n_orders, width, l, tk),
        grid=(l // tk,),
        in_specs=[full(a) for a in args] + [pl.BlockSpec((tk, l), lambda j: (j, 0)),
                                            pl.BlockSpec((tk, l), lambda j: (j, 0))],
        out_specs=[pl.BlockSpec((n_orders, tk, width), lambda j: (0, j, 0)),
                   pl.BlockSpec((n_orders, tk, width), lambda j: (0, j, 0))],
        out_shape=[jax.ShapeDtypeStruct((n_orders, l, width), F32)] * 2,
        scratch_shapes=[pltpu.VMEM((l, n_orders * width), BF16),
                        pltpu.VMEM((l, n_orders * width), BF16)],
        compiler_params=_cparams(("arbitrary",), VMEM_LIMIT),
        name="hyena_spectra",
    )(*args, c0, s0)


def _hyena_kernel(l, fc, zv_ref, z1_ref, z2_ref, wv_ref, w1_ref, w2_ref, bv_ref, b1_ref, b2_ref,
                  c_ref, s_ref, gr_ref, gi_ref, y_ref, ub_ref, a_ref, b_ref, x1_ref, x2_ref):
    rows = lax.broadcasted_iota(jnp.int32, (l, 1), 0)

    def conv3(z_ref, w_ref, bias_ref):
        z = z_ref[...].astype(F32)
        zm = jnp.where(rows == 0, 0.0, pltpu.roll(z, 1, 0))
        zp = jnp.where(rows == l - 1, 0.0, pltpu.roll(z, l - 1, 0))
        return zm * w_ref[0:1, :] + z * w_ref[1:2, :] + zp * w_ref[2:3, :] + bias_ref[...]

    ub_ref[...] = conv3(zv_ref, wv_ref, bv_ref).astype(BF16)
    x1_ref[...] = conv3(z1_ref, w1_ref, b1_ref)
    x2_ref[...] = conv3(z2_ref, w2_ref, b2_ref)
    n = l // fc

    def forward(o):
        def body(c, carry):
            r0 = pl.multiple_of(c * fc, fc)
            ur = _dot(c_ref[pl.ds(r0, fc), :], ub_ref[...])
            us = _dot(s_ref[pl.ds(r0, fc), :], ub_ref[...])
            gr = gr_ref[o, pl.ds(r0, fc), :]
            gi = gi_ref[o, pl.ds(r0, fc), :]
            a_ref[pl.ds(r0, fc), :] = (ur * gr + us * gi).astype(BF16)
            b_ref[pl.ds(r0, fc), :] = (us * gr - ur * gi).astype(BF16)
            return carry
        lax.fori_loop(0, n, body, 0)

    def inverse(gate_ref, dst_ref):
        def body(c, carry):
            r0 = pl.multiple_of(c * fc, fc)
            lc = (_dot(c_ref[pl.ds(r0, fc), :], a_ref[...])
                  + _dot(s_ref[pl.ds(r0, fc), :], b_ref[...]))
            dst_ref[pl.ds(r0, fc), :] = (gate_ref[pl.ds(r0, fc), :] * lc).astype(BF16)
            return carry
        lax.fori_loop(0, n, body, 0)

    forward(0)
    inverse(x1_ref, ub_ref)
    forward(1)
    inverse(x2_ref, y_ref)


def hyena_mixer(zh, conv_w, conv_b, ctab, stab, gr, gi, l, n_seq, row0, cb):
    width = gr.shape[-1]
    n_orders = gr.shape[0]
    ncb = width // cb
    sb0 = row0 // l
    conv_b2 = conv_b.reshape(1, -1)
    zspec = lambda part: pl.BlockSpec((l, cb), lambda c, s: (sb0 + s, part * ncb + c))
    wspec = lambda part: pl.BlockSpec((3, cb), lambda c, s: (0, part * ncb + c))
    bspec = lambda part: pl.BlockSpec((1, cb), lambda c, s: (0, part * ncb + c))
    once = pl.Buffered(1)
    tspec = pl.BlockSpec((l, l), lambda c, s: (0, 0), pipeline_mode=once)
    gspec = pl.BlockSpec((n_orders, l, cb), lambda c, s: (0, 0, c), pipeline_mode=once)
    return pl.pallas_call(
        functools.partial(_hyena_kernel, l, min(l, HY_FREQ_CHUNK)),
        grid=(ncb, n_seq),
        in_specs=[zspec(0), zspec(1), zspec(2), wspec(0), wspec(1), wspec(2),
                  bspec(0), bspec(1), bspec(2), tspec, tspec, gspec, gspec],
        out_specs=pl.BlockSpec((l, cb), lambda c, s: (s, c)),
        out_shape=jax.ShapeDtypeStruct((n_seq * l, width), BF16),
        scratch_shapes=[pltpu.VMEM((l, cb), BF16), pltpu.VMEM((l, cb), BF16),
                        pltpu.VMEM((l, cb), BF16), pltpu.VMEM((l, cb), F32),
                        pltpu.VMEM((l, cb), F32)],
        compiler_params=_cparams(("arbitrary", "arbitrary"), VMEM_LIMIT),
        name="hyena_mixer",
    )(zh, zh, zh, conv_w, conv_w, conv_w, conv_b2, conv_b2, conv_b2, ctab, stab, gr, gi)


def _gla_kernel(l, qk, wv, dk, dv,
                zg_ref, la_ref, s0_ref, g_ref, y_ref, sfin_ref,
                of_ref, ob_ref, qe_ref, ks_ref, dec_ref, st_ref):
    ch = GLA_CHUNK
    n = l // ch
    heads = qk // dk
    grp = min(GLA_GROUP, n)
    rg = grp * ch
    r_i = lax.broadcasted_iota(jnp.int32, (rg, rg), 0)
    c_i = lax.broadcasted_iota(jnp.int32, (rg, rg), 1)
    same = (r_i // ch) == (c_i // ch)
    low = same & (r_i >= c_i)
    upp = same & (r_i <= c_i)
    masks = (low, upp)
    cum2 = tuple(jnp.concatenate([jnp.concatenate([m.astype(BF16)] * 2, axis=1),
                                  jnp.concatenate([same.astype(BF16)] * 2, axis=1)], axis=0)
                 for m in masks)
    lane_head = lax.broadcasted_iota(jnp.int32, (1, qk), 1) // dk
    scale = dk ** -0.5
    o_refs = (of_ref, ob_ref)

    def local(g, d):
        r0 = pl.multiple_of(g * rg, rg)
        q = zg_ref[pl.ds(r0, rg), 0:qk].astype(F32) * scale
        k = zg_ref[pl.ds(r0, rg), qk:2 * qk].astype(F32)
        v = zg_ref[pl.ds(r0, rg), 2 * qk:2 * qk + wv]
        la = la_ref[pl.ds(r0, rg), d * qk:(d + 1) * qk]
        lh, ll = _split(la)
        bt = _dot(cum2[d], jnp.concatenate([lh, ll], axis=0))
        b, tot = bt[:rg], bt[rg:]
        qe = (q * jnp.exp(b)).astype(BF16)
        ke = (k * jnp.exp(-b)).astype(BF16)
        qe_ref[d, pl.ds(r0, rg), :] = qe
        ks_ref[d, pl.ds(r0, rg), :] = (k * jnp.exp(tot - b)).astype(BF16)
        dec = jnp.exp(tot)
        for c in range(grp):
            dec_ref[d, pl.ds(pl.multiple_of((g * grp + c) * 8, 8), 8), :] = dec[c * ch:c * ch + 8]
        parts = []
        for h in range(heads):
            qh = jnp.where(lane_head == h, qe, jnp.zeros_like(qe))
            att = lax.dot_general(qh, ke, (((1,), (1,)), ((), ())), preferred_element_type=F32)
            att = jnp.where(masks[d], att, 0.0).astype(BF16)
            parts.append(_dot(att, v[:, h * dv:(h + 1) * dv]))
        o_refs[d][pl.ds(r0, rg), :] = jnp.concatenate(parts, axis=1)

    def local_body(g, carry):
        local(g, 0)
        local(g, 1)
        return carry

    lax.fori_loop(0, n // grp, local_body, 0)

    bd = (lax.broadcasted_iota(jnp.int32, (wv, qk), 0) // dv
          == lax.broadcasted_iota(jnp.int32, (wv, qk), 1) // dk).astype(F32)
    st_ref[...] = s0_ref[0]

    def carry_state(c, d):
        r0 = pl.multiple_of(c * ch, ch)
        st = st_ref[d]
        o = lax.dot_general(qe_ref[d, pl.ds(r0, ch), :], st.astype(BF16), (((1,), (1,)), ((), ())),
                            preferred_element_type=F32)
        o_refs[d][pl.ds(r0, ch), :] += o
        v = zg_ref[pl.ds(r0, ch), 2 * qk:2 * qk + wv]
        upd = lax.dot_general(v, ks_ref[d, pl.ds(r0, ch), :], (((0,), (0,)), ((), ())),
                              preferred_element_type=F32)
        dec = dec_ref[d, pl.ds(pl.multiple_of(c * 8, 8), 1), :]
        st_ref[d] = st * dec + upd * bd

    su = min(GLA_STATE_UNROLL, n)

    def state_body(i, carry):
        for u in range(su):
            carry_state(i * su + u, 0)
            carry_state(n - 1 - (i * su + u), 1)
        return carry

    lax.fori_loop(0, n // su, state_body, 0)
    sfin_ref[0] = st_ref[...]

    def finish(i, carry):
        r0 = pl.multiple_of(i * ch, ch)
        o = of_ref[pl.ds(r0, ch), :] + ob_ref[pl.ds(r0, ch), :]
        r = zg_ref[pl.ds(r0, ch), 2 * qk + wv:2 * qk + 2 * wv].astype(F32)
        parts = [_rms(o[:, h * dv:(h + 1) * dv], g_ref[...]) for h in range(heads)]
        y_ref[pl.ds(r0, ch), :] = (jnp.concatenate(parts, axis=1) * _silu(r)).astype(BF16)
        return carry

    lax.fori_loop(0, n, finish, 0)


def gla_mixer(zg, la, s0, norm_g, l, n_seq, row0):
    qk = la.shape[1] // 2
    wv = (zg.shape[1] - 2 * qk) // 2
    dk, dv = qk // GLA_HEADS, wv // GLA_HEADS
    sb0 = row0 // l
    return pl.pallas_call(
        functools.partial(_gla_kernel, l, qk, wv, dk, dv),
        grid=(n_seq,),
        in_specs=[pl.BlockSpec((l, zg.shape[1]), lambda s: (sb0 + s, 0)),
                  pl.BlockSpec((l, la.shape[1]), lambda s: (sb0 + s, 0)),
                  pl.BlockSpec((1, 2, wv, qk), lambda s: (s, 0, 0, 0)),
                  pl.BlockSpec((1, dv), lambda s: (0, 0))],
        out_specs=[pl.BlockSpec((l, wv), lambda s: (s, 0)),
                   pl.BlockSpec((1, 2, wv, qk), lambda s: (s, 0, 0, 0))],
        out_shape=[jax.ShapeDtypeStruct((n_seq * l, wv), BF16),
                   jax.ShapeDtypeStruct((n_seq, 2, wv, qk), F32)],
        scratch_shapes=[pltpu.VMEM((l, wv), F32), pltpu.VMEM((l, wv), F32),
                        pltpu.VMEM((2, l, qk), BF16), pltpu.VMEM((2, l, qk), BF16),
                        pltpu.VMEM((2, l // GLA_CHUNK * 8, qk), F32),
                        pltpu.VMEM((2, wv, qk), F32)],
        compiler_params=_cparams(("arbitrary",), VMEM_LIMIT),
        name="gla_mixer",
    )(zg, la, s0, norm_g.reshape(1, dv))


def _outproj_kernel(n_lat_tiles, yh_ref, yg_ref, yhc_ref, ygc_ref, wh_ref, wg_ref, x_ref, g1_ref,
                    g_ref, sh_ref, sc_ref, xo_ref, u_ref):
    i = pl.program_id(0)

    def finish(yh, yg):
        delta = _dot(yh[...], wh_ref[...]) + _dot(yg[...], wg_ref[...])
        x = x_ref[...] + g1_ref[0] * delta
        xo_ref[...] = x
        u_ref[...] = _rms(x, g_ref[...]) * (1.0 + sc_ref[0]) + sh_ref[0]

    @pl.when(i < n_lat_tiles)
    def _():
        finish(yh_ref, yg_ref)

    @pl.when(i >= n_lat_tiles)
    def _():
        finish(yhc_ref, ygc_ref)


def out_proj(y_lat, y_ctx, w_hy, w_gla, xs, mod, norm_g, tm, seq, n_lat, b, n_tok):
    d = xs.shape[1]
    nlt = n_lat // tm
    row = functools.partial(_mod_row, tm=tm, seq=seq, n_lat_tiles=nlt, b=b)
    full = lambda a: pl.BlockSpec(a.shape, lambda i: (0,) * a.ndim)
    modspec = lambda col: pl.BlockSpec((1, 1, d), lambda i: (row(i), 0, col))
    tile = lambda w: pl.BlockSpec((tm, w), lambda i: (i, 0))
    lat = lambda a: pl.BlockSpec((tm, a.shape[1]), lambda i: (jnp.minimum(i, nlt - 1), 0))
    ctx = lambda a: pl.BlockSpec((tm, a.shape[1]), lambda i: (jnp.maximum(i - nlt, 0), 0))
    return pl.pallas_call(
        functools.partial(_outproj_kernel, nlt),
        grid=(n_tok // tm,),
        in_specs=[lat(y_lat[0]), lat(y_lat[1]), ctx(y_ctx[0]), ctx(y_ctx[1]),
                  full(w_hy), full(w_gla), tile(d),
                  modspec(2), full(norm_g), modspec(3), modspec(4)],
        out_specs=[tile(d), tile(d)],
        out_shape=[jax.ShapeDtypeStruct((n_tok, d), F32)] * 2,
        compiler_params=_cparams(("arbitrary",), VMEM_LIMIT),
        name="out_proj",
    )(*y_lat, *y_ctx, w_hy, w_gla, xs, mod, norm_g, mod, mod)


def _router_kernel(n_exp, tm, u_ref, wh_ref, wl_ref, b_ref, tri_ref,
                   idx_ref, gate_ref, rank_ref, cnt_ref):
    i = pl.program_id(0)

    @pl.when(i == 0)
    def _():
        cnt_ref[...] = jnp.zeros_like(cnt_ref)

    uh, ul = _split(u_ref[...])
    nt = (((1,), (1,)), ((), ()))
    logits = (lax.dot_general(wh_ref[...], uh, nt, preferred_element_type=F32)
              + lax.dot_general(wh_ref[...], ul, nt, preferred_element_type=F32)
              + lax.dot_general(wl_ref[...], uh, nt, preferred_element_type=F32))
    s = 1.0 / (1.0 + jnp.exp(-logits))
    sel = s + b_ref[...]
    gsz = n_exp // N_GROUPS
    neg = -jnp.inf
    rows_g = lax.broadcasted_iota(jnp.int32, (gsz, tm), 0)
    blocks, gscore = [], []
    for g in range(N_GROUPS):
        blk = sel[g * gsz:(g + 1) * gsz, :]
        m1 = jnp.max(blk, axis=0, keepdims=True)
        i1 = jnp.min(jnp.where(blk == m1, rows_g, gsz), axis=0, keepdims=True)
        m2 = jnp.max(jnp.where(rows_g == i1, neg, blk), axis=0, keepdims=True)
        blocks.append(blk)
        gscore.append(m1 + m2)
    masked = []
    for g in range(N_GROUPS):
        beaten = jnp.zeros((1, tm), jnp.int32)
        for h in range(N_GROUPS):
            if h == g:
                continue
            wins = (gscore[h] > gscore[g]) | ((gscore[h] == gscore[g]) & (h < g))
            beaten = beaten + wins.astype(jnp.int32)
        masked.append(jnp.where(beaten < TOPK_GROUPS, blocks[g], neg))
    sel = jnp.concatenate(masked, axis=0)

    rows = lax.broadcasted_iota(jnp.int32, (n_exp, tm), 0)
    chosen = jnp.zeros((n_exp, tm), F32)
    idxs, gates = [], []
    for _ in range(TOP_K):
        m = jnp.max(sel, axis=0, keepdims=True)
        ik = jnp.min(jnp.where(sel == m, rows, n_exp), axis=0, keepdims=True)
        hit = rows == ik
        gates.append(jnp.sum(jnp.where(hit, s, 0.0), axis=0, keepdims=True))
        sel = jnp.where(hit, neg, sel)
        chosen = jnp.where(hit, 1.0, chosen)
        idxs.append(ik)
    gsum = gates[0]
    for g in gates[1:]:
        gsum = gsum + g
    inv = ROUTED_SCALE / gsum
    before = _dot(chosen.astype(BF16), tri_ref[...]) + cnt_ref[...]
    for k in range(TOP_K):
        idx_ref[k:k + 1, :] = idxs[k]
        gate_ref[k:k + 1, :] = gates[k] * inv
        rk = jnp.sum(jnp.where(rows == idxs[k], before, 0.0), axis=0, keepdims=True)
        rank_ref[k:k + 1, :] = rk.astype(jnp.int32)
    cnt_ref[...] += jnp.sum(chosen, axis=1, keepdims=True)


def moe_router(u, wr_hi, wr_lo, b_r, n_tok, tm):
    n_exp, d = wr_hi.shape
    tri = (lax.broadcasted_iota(jnp.int32, (tm, tm), 0)
           < lax.broadcasted_iota(jnp.int32, (tm, tm), 1)).astype(BF16)
    full = lambda a: pl.BlockSpec(a.shape, lambda i: (0,) * a.ndim)
    tok = pl.BlockSpec((TOP_K, tm), lambda i: (0, i))
    return pl.pallas_call(
        functools.partial(_router_kernel, n_exp, tm),
        grid=(n_tok // tm,),
        in_specs=[pl.BlockSpec((tm, d), lambda i: (i, 0)), full(wr_hi), full(wr_lo),
                  pl.BlockSpec((n_exp, 1), lambda i: (0, 0)), full(tri)],
        out_specs=[tok, tok, tok, pl.BlockSpec((n_exp, 1), lambda i: (0, 0))],
        out_shape=[jax.ShapeDtypeStruct((TOP_K, n_tok), jnp.int32),
                   jax.ShapeDtypeStruct((TOP_K, n_tok), F32),
                   jax.ShapeDtypeStruct((TOP_K, n_tok), jnp.int32),
                   jax.ShapeDtypeStruct((n_exp, 1), F32)],
        compiler_params=_cparams(("arbitrary",), VMEM_LIMIT),
        name="moe_router",
    )(u, wr_hi, wr_lo, b_r.reshape(n_exp, 1), tri)


LANES = 128


def _start_row_copy(src, src_row, dst, dst_row, sem, queue):
    pltpu.async_copy(src.at[src_row], dst.at[dst_row], sem, priority=queue)


def _store_rows_tiled(ref, x):
    for j in range(ref.shape[-2]):
        ref[:, j, :] = x[:, j * LANES:(j + 1) * LANES]


def _load_rows_tiled(ref):
    return jnp.concatenate([ref[:, j, :] for j in range(ref.shape[-2])], axis=1)


def _pack_bf16_pairs(x):
    w = x.shape[1] // 2
    lo = pltpu.bitcast(x[:, :w].astype(BF16).astype(F32), jnp.uint32) >> 16
    hi = pltpu.bitcast(x[:, w:].astype(BF16).astype(F32), jnp.uint32)
    return hi | lo


def _unpack_bf16_pairs(p):
    lo = pltpu.bitcast(p << 16, F32)
    hi = pltpu.bitcast(p & jnp.uint32(0xFFFF0000), F32)
    return jnp.concatenate([lo, hi], axis=1).astype(BF16)


def _dest_kernel(n_exp, tm, idx_ref, rank_ref, ps_ref, dest_ref):
    rows = lax.broadcasted_iota(jnp.int32, (n_exp, tm), 0)
    ps = ps_ref[...]
    for k in range(TOP_K):
        base = jnp.sum(jnp.where(rows == idx_ref[k:k + 1, :], ps, 0), axis=0, keepdims=True)
        dest_ref[k:k + 1, :] = base + rank_ref[k:k + 1, :]


def moe_dest(idx, rank, pad_start, tm):
    n_tok = idx.shape[1]
    n_exp = pad_start.shape[0]
    tok = pl.BlockSpec((TOP_K, tm), lambda i: (0, i))
    return pl.pallas_call(
        functools.partial(_dest_kernel, n_exp, tm),
        grid=(n_tok // tm,),
        in_specs=[tok, tok, pl.BlockSpec((n_exp, 1), lambda i: (0, 0))],
        out_specs=tok,
        out_shape=jax.ShapeDtypeStruct((TOP_K, n_tok), jnp.int32),
        compiler_params=_cparams(("arbitrary",)),
        name="moe_dest",
    )(idx, rank, pad_start.reshape(n_exp, 1))


def _wait_rows(src, dst, sem, tm):
    for _ in range(TOP_K):
        pltpu.make_async_copy(src.at[pl.ds(0, tm)], dst.at[pl.ds(0, tm)], sem).wait()


ZERO_ROWS = MOE_ROWS // 2


def _zero_padding(n_exp, ps_ref, pe_ref, xs_ref, z_ref, sem, wait):
    def go(copy):
        copy.wait() if wait else copy.start()

    def per_expert(e, carry):
        s = ps_ref[e]
        m = pe_ref[e] - s
        bit = ZERO_ROWS
        while bit:
            start = s + (m & ~(2 * bit - 1))

            @pl.when((m & bit) != 0)
            def _():
                go(pltpu.make_async_copy(z_ref.at[pl.ds(0, bit)], xs_ref.at[pl.ds(start, bit)], sem))
            bit //= 2
        return carry

    lax.fori_loop(0, n_exp, per_expert, 0)

    tail = pe_ref[n_exp - 1]

    def per_chunk(j, carry):
        go(pltpu.make_async_copy(z_ref, xs_ref.at[pl.ds(tail + j * ZERO_ROWS, ZERO_ROWS)], sem))
        return carry

    lax.fori_loop(0, (xs_ref.shape[0] - tail) // ZERO_ROWS, per_chunk, 0)


def _dispatch_kernel(tm, n_steps, n_exp, ps_ref, pe_ref, dest_ref, u_ref, xs_ref, pk_ref, z_ref,
                     sems, zsem):
    i = pl.program_id(0)
    slot = i % 2
    pk = pk_ref.at[slot]
    sem = sems.at[slot]

    @pl.when(i == 0)
    def _():
        z_ref[...] = jnp.zeros_like(z_ref)
        _zero_padding(n_exp, ps_ref, pe_ref, xs_ref, z_ref, zsem, wait=False)

    @pl.when(i >= 2)
    def _():
        _wait_rows(pk, xs_ref, sem, tm)

    _store_rows_tiled(pk, _pack_bf16_pairs(u_ref[...]))

    def issue(t, carry):
        for k in range(TOP_K):
            _start_row_copy(pk, t, xs_ref, dest_ref[0, 0, k * tm + t], sem, k % 2)
        return carry

    lax.fori_loop(0, tm, issue, 0)

    @pl.when(i == n_steps - 1)
    def _():
        if n_steps > 1:
            _wait_rows(pk_ref.at[1 - slot], xs_ref, sems.at[1 - slot], tm)
        _wait_rows(pk, xs_ref, sem, tm)
        _zero_padding(n_exp, ps_ref, pe_ref, xs_ref, z_ref, zsem, wait=True)


def moe_dispatch(u, dest, pad_from, pad_end, n_tok, n_slots, tm):
    d = u.shape[1]
    sub = d // 2 // LANES
    n_exp = pad_end.shape[0]
    grid_spec = pltpu.PrefetchScalarGridSpec(
        num_scalar_prefetch=2,
        grid=(n_tok // tm,),
        in_specs=[pl.BlockSpec((1, 1, TOP_K * tm), lambda i, *_: (i, 0, 0), memory_space=pltpu.SMEM),
                  pl.BlockSpec((tm, d), lambda i, *_: (i, 0))],
        out_specs=pl.BlockSpec(memory_space=pl.ANY),
        scratch_shapes=[pltpu.VMEM((2, tm, sub, LANES), jnp.uint32),
                        pltpu.VMEM((ZERO_ROWS, sub, LANES), jnp.uint32),
                        pltpu.SemaphoreType.DMA((2,)), pltpu.SemaphoreType.DMA])
    return pl.pallas_call(
        functools.partial(_dispatch_kernel, tm, n_tok // tm, n_exp),
        grid_spec=grid_spec,
        out_shape=jax.ShapeDtypeStruct((n_slots, sub, LANES), jnp.uint32),
        compiler_params=_cparams(("arbitrary",)),
        name="moe_dispatch",
    )(pad_from, pad_end, dest, u)


def _gmm_kernel(layer, be_ref, nx_ref, fl_ref, nu_ref, x_ref, w1_hbm, w3_hbm, w2_hbm, y_ref,
                w1f, w3f, w2f, w1b, w3b, w2b, sems):
    i = pl.program_id(0)
    fl = fl_ref[i]
    sub = x_ref.shape[0] // MOE_ROWS

    def fetch(e, s):
        return (pltpu.make_async_copy(w1_hbm.at[layer, e], w1f.at[s], sems.at[s]),
                pltpu.make_async_copy(w3_hbm.at[layer, e], w3f.at[s], sems.at[s]),
                pltpu.make_async_copy(w2_hbm.at[layer, e], w2f.at[s], sems.at[s]))

    @pl.when(i < nu_ref[0])
    def _():
        @pl.when((fl & 1) != 0)
        def _():
            s = (fl >> 1) & 1

            @pl.when(i == 0)
            def _():
                for c in fetch(be_ref[i], s):
                    c.start()

            for c in fetch(be_ref[i], s):
                c.wait()

            @pl.when((fl & 4) != 0)
            def _():
                for c in fetch(nx_ref[i], 1 - s):
                    c.start()

            w1b[...] = w1f[s].astype(BF16)
            w3b[...] = w3f[s].astype(BF16)
            w2b[...] = w2f[s].astype(BF16)

        x = _unpack_bf16_pairs(jnp.concatenate(
            [x_ref[pl.ds(j, MOE_ROWS, stride=sub), :] for j in range(sub)], axis=1))
        h = _silu(_dot(x, w1b[...])) * _dot(x, w3b[...])
        y = _pack_bf16_pairs(_dot(h.astype(BF16), w2b[...]))
        for j in range(sub):
            y_ref[pl.ds(j, MOE_ROWS, stride=sub), :] = y[:, j * LANES:(j + 1) * LANES]

    @pl.when(i >= nu_ref[0])
    def _():
        y_ref[...] = jnp.zeros_like(y_ref)


def moe_experts(x_sorted, block_e, n_used, w1, w3, w2, layer):
    n_blocks = block_e.shape[0]
    _, _, d, f = w1.shape
    n_slots, sub, _ = x_sorted.shape
    idx = jnp.arange(n_blocks, dtype=jnp.int32)
    first = (idx == 0) | (block_e != jnp.roll(block_e, 1))
    seg = jnp.cumsum(first.astype(jnp.int32)) - 1
    first_at_or_after = lax.cummin(jnp.where(first, idx, n_blocks)[::-1])[::-1]
    nxt_idx = jnp.concatenate([first_at_or_after[1:], jnp.full((1,), n_blocks, jnp.int32)])
    has_next = nxt_idx < n_used[0]
    nxt_e = block_e[jnp.minimum(nxt_idx, n_blocks - 1)]
    flags = (first.astype(jnp.int32) | ((seg & 1) << 1) | (has_next.astype(jnp.int32) << 2))
    grid_spec = pltpu.PrefetchScalarGridSpec(
        num_scalar_prefetch=4,
        grid=(n_blocks,),
        in_specs=[pl.BlockSpec((MOE_ROWS * sub, LANES), lambda i, *_: (i, 0)),
                  pl.BlockSpec(memory_space=pl.ANY),
                  pl.BlockSpec(memory_space=pl.ANY),
                  pl.BlockSpec(memory_space=pl.ANY)],
        out_specs=pl.BlockSpec((MOE_ROWS * sub, LANES), lambda i, *_: (i, 0)),
        scratch_shapes=[pltpu.VMEM((2, d, f), F32), pltpu.VMEM((2, d, f), F32),
                        pltpu.VMEM((2, f, d), F32),
                        pltpu.VMEM((d, f), BF16), pltpu.VMEM((d, f), BF16),
                        pltpu.VMEM((f, d), BF16), pltpu.SemaphoreType.DMA((2,))])
    y = pl.pallas_call(
        functools.partial(_gmm_kernel, layer),
        grid_spec=grid_spec,
        out_shape=jax.ShapeDtypeStruct((n_slots * sub, LANES), jnp.uint32),
        compiler_params=_cparams(("arbitrary",), VMEM_LIMIT),
        name="moe_experts",
    )(block_e, nxt_e, flags, n_used, x_sorted.reshape(n_slots * sub, LANES), w1, w3, w2)
    return y.reshape(n_slots, sub, LANES)


def _combine_kernel(tm, n_steps, final, dest_ref, nxt_ref, y_ref, gate_ref, u_ref, ws1_ref, ws3_ref,
                    ws2_ref, x_ref, g2_ref, fg_ref, o_ref, buf_ref, routed_ref, sems):
    i = pl.program_id(0)
    slot = i % 2
    grp = 8

    def gather(d_ref, s, g):
        for tt in range(grp):
            t = g * grp + tt
            for k in range(TOP_K):
                _start_row_copy(y_ref, d_ref[0, 0, k * tm + t], buf_ref.at[s, k], t, sems.at[s], k % 2)

    def reduce(g, s):
        r0 = pl.multiple_of(g * grp, grp)
        gt = gate_ref[pl.ds(r0, grp), :]
        lo = hi = None
        for k in range(TOP_K):
            p = buf_ref[s, k, pl.ds(r0, grp)]
            gk = gt[:, k:k + 1][:, :, None]
            lo_k = pltpu.bitcast(p << 16, F32) * gk
            hi_k = pltpu.bitcast(p & jnp.uint32(0xFFFF0000), F32) * gk
            lo = lo_k if lo is None else lo + lo_k
            hi = hi_k if hi is None else hi + hi_k
        sub = lo.shape[1]
        routed_ref[pl.ds(r0, grp), 0:sub, :] = lo
        routed_ref[pl.ds(r0, grp), sub:2 * sub, :] = hi

    @pl.when(i == 0)
    def _():
        lax.fori_loop(0, tm // grp, lambda g, c: (gather(dest_ref, slot, g), c)[1], 0)

    _wait_rows(y_ref, buf_ref.at[slot, 0], sems.at[slot], tm)

    @pl.when(i + 1 < n_steps)
    def _():
        def both(g, c):
            gather(nxt_ref, 1 - slot, g)
            reduce(g, slot)
            return c
        lax.fori_loop(0, tm // grp, both, 0)

    @pl.when(i + 1 >= n_steps)
    def _():
        lax.fori_loop(0, tm // grp, lambda g, c: (reduce(g, slot), c)[1], 0)

    ub = u_ref[...].astype(BF16)
    hs = _silu(_dot(ub, ws1_ref[...])) * _dot(ub, ws3_ref[...])
    shared = _dot(hs.astype(BF16), ws2_ref[...])
    x = x_ref[...] + g2_ref[0] * (_load_rows_tiled(routed_ref) + shared)
    o_ref[...] = _rms(x, fg_ref[...]) if final else x


def moe_combine(y_sorted, dest, gate_rep, u, ws1, ws3, ws2, xs, mod, final_g, final,
                tm, seq, n_lat, b, n_tok):
    d = u.shape[1]
    nlt = n_lat // tm
    row = functools.partial(_mod_row, tm=tm, seq=seq, n_lat_tiles=nlt, b=b)
    full = lambda a: pl.BlockSpec(a.shape, lambda i: (0,) * a.ndim)
    tile = pl.BlockSpec((tm, d), lambda i: (i, 0))
    n_steps = n_tok // tm
    return pl.pallas_call(
        functools.partial(_combine_kernel, tm, n_steps, final),
        grid=(n_steps,),
        in_specs=[pl.BlockSpec((1, 1, TOP_K * tm), lambda i: (i, 0, 0), memory_space=pltpu.SMEM),
                  pl.BlockSpec((1, 1, TOP_K * tm),
                               lambda i: (jnp.minimum(i + 1, n_steps - 1), 0, 0),
                               memory_space=pltpu.SMEM),
                  pl.BlockSpec(memory_space=pl.ANY),
                  pl.BlockSpec((tm, TOP_K), lambda i: (i, 0)),
                  tile, full(ws1), full(ws3), full(ws2), tile,
                  pl.BlockSpec((1, 1, d), lambda i: (row(i), 0, 5)),
                  full(final_g)],
        out_specs=tile,
        out_shape=jax.ShapeDtypeStruct((n_tok, d), F32),
        scratch_shapes=[pltpu.VMEM((2, TOP_K, tm) + y_sorted.shape[1:], jnp.uint32),
                        pltpu.VMEM((tm, d // LANES, LANES), F32), pltpu.SemaphoreType.DMA((2,))],
        compiler_params=_cparams(("arbitrary",), VMEM_LIMIT),
        name="moe_combine",
    )(dest, dest, y_sorted, gate_rep, u, ws1, ws3, ws2, xs, mod, final_g)


def moe_layer(u, xs, mod, router_w, router_b, w1, w3, w2, layer, ws1, ws3, ws2, final_g, final,
              tm, seq, n_lat, b, n_tok):
    n_exp = router_w.shape[1]
    wr_hi, wr_lo = _split(router_w.T)
    idx, gate, rank, cnt = moe_router(u, wr_hi, wr_lo, router_b, n_tok, tm)
    counts = cnt[:, 0].astype(jnp.int32)
    padded = (counts + MOE_ROWS - 1) // MOE_ROWS * MOE_ROWS
    pad_end = jnp.cumsum(padded)
    pad_start = pad_end - padded
    dest = moe_dest(idx, rank, pad_start, math.gcd(n_tok, 1024))
    n_blocks = -(-(n_tok * TOP_K + n_exp * (MOE_ROWS - 1)) // MOE_ROWS)
    block_start = jnp.arange(n_blocks, dtype=jnp.int32) * MOE_ROWS
    block_e = jnp.minimum(jnp.sum((pad_end[None, :] <= block_start[:, None]).astype(jnp.int32), axis=1),
                          n_exp - 1)
    n_used = (pad_end[-1:] // MOE_ROWS).astype(jnp.int32)
    dest = dest.reshape(TOP_K, n_tok // tm, tm).transpose(1, 0, 2).reshape(n_tok // tm, 1, TOP_K * tm)
    x_sorted = moe_dispatch(u, dest, pad_start + counts, pad_end, n_tok, n_blocks * MOE_ROWS, tm)
    y_sorted = moe_experts(x_sorted, block_e, n_used, w1, w3, w2, layer)
    return moe_combine(y_sorted, dest, gate.T, u, ws1.astype(BF16), ws3.astype(BF16),
                       ws2.astype(BF16), xs, mod, final_g.reshape(1, -1), final,
                       tm, seq, n_lat, b, n_tok)


def _grid_pos_emb(rows, d):
    r, col = np.meshgrid(np.arange(rows, dtype=np.float64), np.arange(GRID_W, dtype=np.float64),
                         indexing='ij')
    quarter = d // 4
    omega = 1.0 / (10000.0 ** (np.arange(quarter, dtype=np.float64) / quarter))

    def emb(p):
        a = p.reshape(-1)[:, None] * omega[None, :]
        return np.concatenate([np.sin(a), np.cos(a)], axis=-1)

    return jnp.asarray(np.concatenate([emb(r), emb(col)], axis=-1).astype(np.float32))


def kernel(x, c, ctx, c_ctx, ada_w, ada_b, norm1_g, norm2_g, w_in, w_out, hy_conv_w, hy_conv_b,
           hy_f_w1, hy_f_b1, hy_f_w2, hy_f_b2, hy_f_w3, hy_skip, gla_wa2, gla_ba2, gla_norm_g,
           router_w, router_b, exp_w1, exp_w3, exp_w2, sh_w1, sh_w3, sh_w2, final_g):
    b, l, d = x.shape
    lc = ctx.shape[1]
    depth = ada_w.shape[0]
    n_lat, n_ctx = b * l, b * lc
    n_all = n_lat + n_ctx
    hy_width = hy_skip.shape[-1]
    hy_cols = hy_conv_w.shape[-1]
    qk = gla_wa2.shape[-1]
    wv = d - hy_width
    g_cols = 2 * qk + 2 * wv
    tm = math.gcd(math.gcd(l, lc), 256)
    tm_proj = math.gcd(math.gcd(l, n_ctx), 512)
    cb = min(hy_width, 256)

    cond = jnp.concatenate([c, c_ctx[None, :]], axis=0)
    n_rows = -(-(b + 1) // 8) * 8
    cond = jnp.concatenate([cond, jnp.zeros((n_rows - b - 1, d), F32)], axis=0)
    mods = ada_table(cond, ada_w, ada_b)

    xs = (x.reshape(n_lat, d), _grid_pos_emb(l // GRID_W, d), ctx.reshape(n_ctx, d))

    tabs = {}
    for length in (l, lc):
        if length not in tabs:
            tabs[length] = _trig_tables(length, True) + _trig_tables(length, False)
    zero_state = jnp.zeros((b, 2, wv, qk), F32)

    for layer in range(depth):
        last = layer == depth - 1
        mod = mods[layer][:, None, :]
        g1 = norm1_g[layer].reshape(1, d)
        g2 = norm2_g[layer].reshape(1, d)
        w_l = w_in[layer].astype(BF16)
        wa2 = gla_wa2[layer]
        rank = wa2.shape[1]
        wa2_bd = jnp.zeros((2 * rank, 2 * qk), F32)
        wa2_bd = wa2_bd.at[:rank, :qk].set(wa2[0]).at[rank:, qk:].set(wa2[1])
        ba2_cat = gla_ba2[layer].reshape(1, 2 * qk)
        outs = in_proj(xs, g1, mod, w_l[:, :hy_cols], w_l[:, hy_cols:hy_cols + g_cols],
                       w_l[:, hy_cols + g_cols:], wa2_bd, ba2_cat, tm_proj, l, n_lat, b)
        zh, zg, la = outs[:3]
        if layer == 0:
            xs = outs[3]

        filt = (hy_f_w1[layer], hy_f_b1[layer], hy_f_w2[layer], hy_f_b2[layer], hy_f_w3[layer],
                hy_skip[layer])
        yc_gla, s_ctx = gla_mixer(zg, la, zero_state, gla_norm_g[layer], lc, b, n_lat)
        y_gla, _ = gla_mixer(zg, la, s_ctx, gla_norm_g[layer], l, b, 0)
        ct, st, c0, s0 = tabs[l]
        gr, gi = hyena_spectra(l, *filt, c0, s0)
        y_hy = hyena_mixer(zh, hy_conv_w[layer], hy_conv_b[layer], ct, st, gr, gi, l, b, 0, cb)
        if last:
            n_tok = n_lat
            y_ctx = (y_hy, y_gla)
        else:
            n_tok = n_all
            ct, st, c0, s0 = tabs[lc]
            gr, gi = hyena_spectra(lc, *filt, c0, s0)
            yc_hy = hyena_mixer(zh, hy_conv_w[layer], hy_conv_b[layer], ct, st, gr, gi,
                                lc, b, n_lat, cb)
            y_ctx = (yc_hy, yc_gla)
        w_o = w_out[layer].astype(BF16)
        xs_mid, u = out_proj((y_hy, y_gla), y_ctx, w_o[:hy_width], w_o[hy_width:], xs, mod, g2,
                             tm, l, n_lat, b, n_tok)
        xs = moe_layer(u, xs_mid, mod, router_w[layer], router_b[layer], exp_w1, exp_w3, exp_w2,
                       layer, sh_w1[layer], sh_w3[layer], sh_w2[layer],
                       final_g, last, tm, l, n_lat, b, n_tok)
    return xs[:n_lat].reshape(b, l, d)
```

```python
import functools
import math

import numpy as np
import jax
import jax.numpy as jnp
from jax import lax
from jax.experimental import pallas as pl
from jax.experimental.pallas import tpu as pltpu

GRID_W = 64
EPS = 1e-6

HY_EMB = 33
HY_EMB_PAD = 40
HY_FREQ = 1.0
HY_TARGET = 1e-2
HY_FAST_PCT = 0.3
HY_SLOW_PCT = 1.5
HY_MIN_DECAY = math.log(1.0 / HY_TARGET) / HY_SLOW_PCT
HY_MAX_DECAY = math.log(1.0 / HY_TARGET) / HY_FAST_PCT

GLA_HEADS = 4
GLA_TAU = 16.0
GLA_CHUNK = 64
GLA_GROUP = 4
GLA_STATE_UNROLL = 4

TOP_K = 8
N_GROUPS = 8
TOPK_GROUPS = 4
ROUTED_SCALE = 2.5

SUBLANES = 8
LANES = 128
MXU_DIM = 256
VMEM_LIMIT = 56 * 1024 * 1024

TOKEN_TILE = 256
PROJ_TILE = 512
DEST_TILE = 1024
ADA_COLS = 1024
SPECTRA_ROWS = 256
HY_FREQ_CHUNK = 1024
MOE_ROWS = 512

F32 = jnp.float32
BF16 = jnp.bfloat16


def _cparams(sem, vmem=None):
    return pltpu.CompilerParams(dimension_semantics=sem, vmem_limit_bytes=vmem)


def _split(a):
    hi = a.astype(BF16)
    lo = (a - hi.astype(F32)).astype(BF16)
    return hi, lo


def _dot(a, b):
    return jnp.dot(a, b, preferred_element_type=F32)


def _dot3(a, b):
    ah, al = _split(a)
    bh, bl = _split(b)
    return _dot(ah, bh) + _dot(ah, bl) + _dot(al, bh)


def _silu(x):
    return x * (1.0 / (1.0 + jnp.exp(-x)))


def _rms(x, g):
    return x * lax.rsqrt(jnp.mean(x * x, axis=-1, keepdims=True) + EPS) * g


def _ada_kernel(c_ref, w_ref, b_ref, o_ref):
    o_ref[0] = _dot3(_silu(c_ref[...]), w_ref[0]) + b_ref[0]


def ada_table(cond_rows, ada_w, ada_b):
    depth, d, six_d = ada_w.shape
    r = cond_rows.shape[0]
    tn = ADA_COLS
    return pl.pallas_call(
        _ada_kernel,
        grid=(depth, six_d // tn),
        in_specs=[pl.BlockSpec((r, d), lambda l, j: (0, 0)),
                  pl.BlockSpec((1, d, tn), lambda l, j: (l, 0, j)),
                  pl.BlockSpec((1, 1, tn), lambda l, j: (l, 0, j))],
        out_specs=pl.BlockSpec((1, r, tn), lambda l, j: (l, 0, j)),
        out_shape=jax.ShapeDtypeStruct((depth, r, six_d), F32),
        compiler_params=_cparams(("arbitrary", "arbitrary")),
        name="ada_table",
    )(cond_rows, ada_w, ada_b.reshape(depth, 1, six_d))


def _mod_row(i, tm, seq, n_lat_tiles, b):
    return jnp.where(i < n_lat_tiles, (i * tm) // seq, b)


def _inproj_embed_kernel(n_lat_tiles, x_ref, p_ref, c_ref, *rest):
    xs_ref = rest[-1]
    i = pl.program_id(0)

    @pl.when(i < n_lat_tiles)
    def _():
        xs_ref[...] = x_ref[...] + p_ref[...]

    @pl.when(i >= n_lat_tiles)
    def _():
        xs_ref[...] = c_ref[...]

    _inproj_kernel(xs_ref, *rest[:-1])


def _inproj_kernel(x_ref, g_ref, sh_ref, sc_ref, wh_ref, wg_ref, wa_ref, wa2_ref, ba2_ref,
                   zh_ref, zg_ref, la_ref):
    x = x_ref[...]
    h = _rms(x, g_ref[...]) * (1.0 + sc_ref[0]) + sh_ref[0]
    hb = h.astype(BF16)
    zh_ref[...] = _dot(hb, wh_ref[...]).astype(BF16)
    zg_ref[...] = _dot(hb, wg_ref[...]).astype(BF16)
    za = _dot(hb, wa_ref[...])
    t = _dot3(za, wa2_ref[...]) + ba2_ref[...]
    la_ref[...] = (jnp.minimum(t, 0.0) - jnp.log(1.0 + jnp.exp(-jnp.abs(t)))) * (1.0 / GLA_TAU)


def in_proj(tokens, norm_g, mod, w_hy, w_g, w_a, wa2_bd, ba2_cat, tm, seq, n_lat, b):
    embed = isinstance(tokens, tuple)
    nlt = n_lat // tm
    if embed:
        x2, pos, ctx2 = tokens
        d = x2.shape[1]
        n = x2.shape[0] + ctx2.shape[0]
        lt = pos.shape[0] // tm
        tok_specs = [pl.BlockSpec((tm, d), lambda i: (jnp.minimum(i, nlt - 1), 0)),
                     pl.BlockSpec((tm, d), lambda i: (i % lt, 0)),
                     pl.BlockSpec((tm, d), lambda i: (jnp.maximum(i - nlt, 0), 0))]
        body = functools.partial(_inproj_embed_kernel, nlt)
    else:
        n, d = tokens.shape
        tokens = (tokens,)
        tok_specs = [pl.BlockSpec((tm, d), lambda i: (i, 0))]
        body = _inproj_kernel
    row = functools.partial(_mod_row, tm=tm, seq=seq, n_lat_tiles=nlt, b=b)
    cols = (w_hy.shape[1], w_g.shape[1], wa2_bd.shape[1]) + ((d,) if embed else ())
    dtypes = (BF16, BF16, F32) + ((F32,) if embed else ())
    full = lambda a: pl.BlockSpec(a.shape, lambda i: (0,) * a.ndim)
    return pl.pallas_call(
        body,
        grid=(n // tm,),
        in_specs=tok_specs + [full(norm_g),
                              pl.BlockSpec((1, 1, d), lambda i: (row(i), 0, 0)),
                              pl.BlockSpec((1, 1, d), lambda i: (row(i), 0, 1)),
                              full(w_hy), full(w_g), full(w_a), full(wa2_bd), full(ba2_cat)],
        out_specs=[pl.BlockSpec((tm, c), lambda i: (i, 0)) for c in cols],
        out_shape=[jax.ShapeDtypeStruct((n, c), t) for c, t in zip(cols, dtypes)],
        compiler_params=_cparams(("arbitrary",), VMEM_LIMIT),
        name="in_proj",
    )(*tokens, norm_g, mod, mod, w_hy, w_g, w_a, wa2_bd, ba2_cat)


def _trig_tables(l, half_shift):
    k = np.arange(l, dtype=np.int64)[:, None]
    n = np.arange(l, dtype=np.int64)[None, :]
    m = ((2 * k + 1) * (2 * n + (1 if half_shift else 0))) % (8 * l)
    ang = m.astype(np.float64) * (2.0 * math.pi / (8 * l))
    return (jnp.asarray(np.cos(ang).astype(np.float32).astype(BF16)),
            jnp.asarray(np.sin(ang).astype(np.float32).astype(BF16)))


def _filter_kernel(n_orders, width, l,
                   pos_ref, t_ref, dl_ref, w1_ref, b1_ref, w2_ref, b2_ref, w3_ref, skip_ref,
                   c0_ref, s0_ref, gr_ref, gi_ref, hs_ref, hd_ref):
    j = pl.program_id(0)

    @pl.when(j == 0)
    def _():
        hid = jnp.sin(HY_FREQ * (_dot3(pos_ref[...], w1_ref[...]) + b1_ref[...]))
        hid = jnp.sin(HY_FREQ * (_dot3(hid, w2_ref[...]) + b2_ref[...]))
        h = _dot3(hid, w3_ref[...])
        decay = jnp.exp(-t_ref[...] * dl_ref[...])
        for o in range(n_orders):
            hf = h[:, (2 * o) * width:(2 * o + 1) * width] * decay
            hb = h[:, (2 * o + 1) * width:(2 * o + 2) * width] * decay
            hs_ref[:, o * width:(o + 1) * width] = (hf + hb).astype(BF16)
            hd_ref[:, o * width:(o + 1) * width] = (hb - hf).astype(BF16)

    scale = 1.0 / l
    gr = _dot(c0_ref[...], hs_ref[...])
    gi = _dot(s0_ref[...], hd_ref[...])
    for o in range(n_orders):
        gr_ref[o] = (gr[:, o * width:(o + 1) * width] + skip_ref[o]) * scale
        gi_ref[o] = gi[:, o * width:(o + 1) * width] * scale


def hyena_spectra(l, w1, b1, w2, b2, w3, skip, c0, s0):
    width = skip.shape[-1]
    n_orders = skip.shape[0]
    t64 = np.linspace(0.0, 1.0, l)[:, None]
    bands = (HY_EMB - 1) // 2
    w = 2.0 * math.pi * np.arange(l, dtype=np.float64)[:, None] / l
    f = np.linspace(1e-4, bands - 1, bands)[None, :]
    pos = jnp.asarray(np.concatenate([t64, np.cos(f * w), -np.sin(f * w),
                                      np.zeros((l, HY_EMB_PAD - HY_EMB))], axis=-1).astype(np.float32))
    t = jnp.asarray(t64.astype(np.float32))
    w1p = jnp.concatenate([w1, jnp.zeros((HY_EMB_PAD - HY_EMB, w1.shape[1]), F32)], axis=0)
    deltas = jnp.asarray(np.linspace(HY_MIN_DECAY, HY_MAX_DECAY, width)[None, :].astype(np.float32))
    tk = min(l, SPECTRA_ROWS)
    full = lambda a: pl.BlockSpec(a.shape, lambda j: (0,) * a.ndim)
    skip3 = skip.reshape(n_orders, 1, width)
    args = (pos, t, deltas, w1p, b1.reshape(1, -1), w2, b2.reshape(1, -1), w3, skip3)
    return pl.pallas_call(
        functools.partial(_filter_kernel, n_orders, width, l),
        grid=(l // tk,),
        in_specs=[full(a) for a in args] + [pl.BlockSpec((tk, l), lambda j: (j, 0)),
                                            pl.BlockSpec((tk, l), lambda j: (j, 0))],
        out_specs=[pl.BlockSpec((n_orders, tk, width), lambda j: (0, j, 0)),
                   pl.BlockSpec((n_orders, tk, width), lambda j: (0, j, 0))],
        out_shape=[jax.ShapeDtypeStruct((n_orders, l, width), F32)] * 2,
        scratch_shapes=[pltpu.VMEM((l, n_orders * width), BF16),
                        pltpu.VMEM((l, n_orders * width), BF16)],
        compiler_params=_cparams(("arbitrary",), VMEM_LIMIT),
        name="hyena_spectra",
    )(*args, c0, s0)


def _hyena_kernel(l, fc, zv_ref, z1_ref, z2_ref, wv_ref, w1_ref, w2_ref, bv_ref, b1_ref, b2_ref,
                  c_ref, s_ref, gr_ref, gi_ref, y_ref, ub_ref, a_ref, b_ref, x1_ref, x2_ref):
    rows = lax.broadcasted_iota(jnp.int32, (l, 1), 0)

    def conv3(z_ref, w_ref, bias_ref):
        z = z_ref[...].astype(F32)
        zm = jnp.where(rows == 0, 0.0, pltpu.roll(z, 1, 0))
        zp = jnp.where(rows == l - 1, 0.0, pltpu.roll(z, l - 1, 0))
        return zm * w_ref[0:1, :] + z * w_ref[1:2, :] + zp * w_ref[2:3, :] + bias_ref[...]

    ub_ref[...] = conv3(zv_ref, wv_ref, bv_ref).astype(BF16)
    x1_ref[...] = conv3(z1_ref, w1_ref, b1_ref)
    x2_ref[...] = conv3(z2_ref, w2_ref, b2_ref)
    n = l // fc

    def forward(o):
        def body(c, carry):
            r0 = pl.multiple_of(c * fc, fc)
            ur = _dot(c_ref[pl.ds(r0, fc), :], ub_ref[...])
            us = _dot(s_ref[pl.ds(r0, fc), :], ub_ref[...])
            gr = gr_ref[o, pl.ds(r0, fc), :]
            gi = gi_ref[o, pl.ds(r0, fc), :]
            a_ref[pl.ds(r0, fc), :] = (ur * gr + us * gi).astype(BF16)
            b_ref[pl.ds(r0, fc), :] = (us * gr - ur * gi).astype(BF16)
            return carry
        lax.fori_loop(0, n, body, 0)

    def inverse(gate_ref, dst_ref):
        def body(c, carry):
            r0 = pl.multiple_of(c * fc, fc)
            lc = (_dot(c_ref[pl.ds(r0, fc), :], a_ref[...])
                  + _dot(s_ref[pl.ds(r0, fc), :], b_ref[...]))
            dst_ref[pl.ds(r0, fc), :] = (gate_ref[pl.ds(r0, fc), :] * lc).astype(BF16)
            return carry
        lax.fori_loop(0, n, body, 0)

    forward(0)
    inverse(x1_ref, ub_ref)
    forward(1)
    inverse(x2_ref, y_ref)


def hyena_mixer(zh, conv_w, conv_b, ctab, stab, gr, gi, l, n_seq, row0, cb):
    width = gr.shape[-1]
    n_orders = gr.shape[0]
    ncb = width // cb
    sb0 = row0 // l
    conv_b2 = conv_b.reshape(1, -1)
    zspec = lambda part: pl.BlockSpec((l, cb), lambda c, s: (sb0 + s, part * ncb + c))
    wspec = lambda part: pl.BlockSpec((3, cb), lambda c, s: (0, part * ncb + c))
    bspec = lambda part: pl.BlockSpec((1, cb), lambda c, s: (0, part * ncb + c))
    once = pl.Buffered(1)
    tspec = pl.BlockSpec((l, l), lambda c, s: (0, 0), pipeline_mode=once)
    gspec = pl.BlockSpec((n_orders, l, cb), lambda c, s: (0, 0, c), pipeline_mode=once)
    return pl.pallas_call(
        functools.partial(_hyena_kernel, l, min(l, HY_FREQ_CHUNK)),
        grid=(ncb, n_seq),
        in_specs=[zspec(0), zspec(1), zspec(2), wspec(0), wspec(1), wspec(2),
                  bspec(0), bspec(1), bspec(2), tspec, tspec, gspec, gspec],
        out_specs=pl.BlockSpec((l, cb), lambda c, s: (s, c)),
        out_shape=jax.ShapeDtypeStruct((n_seq * l, width), BF16),
        scratch_shapes=[pltpu.VMEM((l, cb), BF16), pltpu.VMEM((l, cb), BF16),
                        pltpu.VMEM((l, cb), BF16), pltpu.VMEM((l, cb), F32),
                        pltpu.VMEM((l, cb), F32)],
        compiler_params=_cparams(("arbitrary", "arbitrary"), VMEM_LIMIT),
        name="hyena_mixer",
    )(zh, zh, zh, conv_w, conv_w, conv_w, conv_b2, conv_b2, conv_b2, ctab, stab, gr, gi)


def _gla_kernel(l, qk, wv, dk, dv,
                zg_ref, la_ref, s0_ref, g_ref, y_ref, sfin_ref,
                of_ref, ob_ref, qe_ref, ks_ref, dec_ref, st_ref):
    ch = GLA_CHUNK
    n = l // ch
    heads = qk // dk
    grp = min(GLA_GROUP, n)
    rg = grp * ch
    r_i = lax.broadcasted_iota(jnp.int32, (rg, rg), 0)
    c_i = lax.broadcasted_iota(jnp.int32, (rg, rg), 1)
    same = (r_i // ch) == (c_i // ch)
    low = same & (r_i >= c_i)
    upp = same & (r_i <= c_i)
    masks = (low, upp)
    cum2 = tuple(jnp.concatenate([jnp.concatenate([m.astype(BF16)] * 2, axis=1),
                                  jnp.concatenate([same.astype(BF16)] * 2, axis=1)], axis=0)
                 for m in masks)
    lane_head = lax.broadcasted_iota(jnp.int32, (1, qk), 1) // dk
    scale = dk ** -0.5
    o_refs = (of_ref, ob_ref)

    def local(g, d):
        r0 = pl.multiple_of(g * rg, rg)
        q = zg_ref[pl.ds(r0, rg), 0:qk].astype(F32) * scale
        k = zg_ref[pl.ds(r0, rg), qk:2 * qk].astype(F32)
        v = zg_ref[pl.ds(r0, rg), 2 * qk:2 * qk + wv]
        la = la_ref[pl.ds(r0, rg), d * qk:(d + 1) * qk]
        lh, ll = _split(la)
        bt = _dot(cum2[d], jnp.concatenate([lh, ll], axis=0))
        b, tot = bt[:rg], bt[rg:]
        qe = (q * jnp.exp(b)).astype(BF16)
        ke = (k * jnp.exp(-b)).astype(BF16)
        qe_ref[d, pl.ds(r0, rg), :] = qe
        ks_ref[d, pl.ds(r0, rg), :] = (k * jnp.exp(tot - b)).astype(BF16)
        dec = jnp.exp(tot)
        for c in range(grp):
            dec_ref[d, pl.ds(pl.multiple_of((g * grp + c) * 8, 8), 8), :] = dec[c * ch:c * ch + 8]
        parts = []
        for h in range(heads):
            qh = jnp.where(lane_head == h, qe, jnp.zeros_like(qe))
            att = lax.dot_general(qh, ke, (((1,), (1,)), ((), ())), preferred_element_type=F32)
            att = jnp.where(masks[d], att, 0.0).astype(BF16)
            parts.append(_dot(att, v[:, h * dv:(h + 1) * dv]))
        o_refs[d][pl.ds(r0, rg), :] = jnp.concatenate(parts, axis=1)

    def local_body(g, carry):
        local(g, 0)
        local(g, 1)
        return carry

    lax.fori_loop(0, n // grp, local_body, 0)

    bd = (lax.broadcasted_iota(jnp.int32, (wv, qk), 0) // dv
          == lax.broadcasted_iota(jnp.int32, (wv, qk), 1) // dk).astype(F32)
    st_ref[...] = s0_ref[0]

    def carry_state(c, d):
        r0 = pl.multiple_of(c * ch, ch)
        st = st_ref[d]
        o = lax.dot_general(qe_ref[d, pl.ds(r0, ch), :], st.astype(BF16), (((1,), (1,)), ((), ())),
                            preferred_element_type=F32)
        o_refs[d][pl.ds(r0, ch), :] += o
        v = zg_ref[pl.ds(r0, ch), 2 * qk:2 * qk + wv]
        upd = lax.dot_general(v, ks_ref[d, pl.ds(r0, ch), :], (((0,), (0,)), ((), ())),
                              preferred_element_type=F32)
        dec = dec_ref[d, pl.ds(pl.multiple_of(c * 8, 8), 1), :]
        st_ref[d] = st * dec + upd * bd

    su = min(GLA_STATE_UNROLL, n)

    def state_body(i, carry):
        for u in range(su):
            carry_state(i * su + u, 0)
            carry_state(n - 1 - (i * su + u), 1)
        return carry

    lax.fori_loop(0, n // su, state_body, 0)
    sfin_ref[0] = st_ref[...]

    def finish(i, carry):
        r0 = pl.multiple_of(i * ch, ch)
        o = of_ref[pl.ds(r0, ch), :] + ob_ref[pl.ds(r0, ch), :]
        r = zg_ref[pl.ds(r0, ch), 2 * qk + wv:2 * qk + 2 * wv].astype(F32)
        parts = [_rms(o[:, h * dv:(h + 1) * dv], g_ref[...]) for h in range(heads)]
        y_ref[pl.ds(r0, ch), :] = (jnp.concatenate(parts, axis=1) * _silu(r)).astype(BF16)
        return carry

    lax.fori_loop(0, n, finish, 0)


def gla_mixer(zg, la, s0, norm_g, l, n_seq, row0):
    qk = la.shape[1] // 2
    wv = (zg.shape[1] - 2 * qk) // 2
    dk, dv = qk // GLA_HEADS, wv // GLA_HEADS
    sb0 = row0 // l
    return pl.pallas_call(
        functools.partial(_gla_kernel, l, qk, wv, dk, dv),
        grid=(n_seq,),
        in_specs=[pl.BlockSpec((l, zg.shape[1]), lambda s: (sb0 + s, 0)),
                  pl.BlockSpec((l, la.shape[1]), lambda s: (sb0 + s, 0)),
                  pl.BlockSpec((1, 2, wv, qk), lambda s: (s, 0, 0, 0)),
                  pl.BlockSpec((1, dv), lambda s: (0, 0))],
        out_specs=[pl.BlockSpec((l, wv), lambda s: (s, 0)),
                   pl.BlockSpec((1, 2, wv, qk), lambda s: (s, 0, 0, 0))],
        out_shape=[jax.ShapeDtypeStruct((n_seq * l, wv), BF16),
                   jax.ShapeDtypeStruct((n_seq, 2, wv, qk), F32)],
        scratch_shapes=[pltpu.VMEM((l, wv), F32), pltpu.VMEM((l, wv), F32),
                        pltpu.VMEM((2, l, qk), BF16), pltpu.VMEM((2, l, qk), BF16),
                        pltpu.VMEM((2, l // GLA_CHUNK * 8, qk), F32),
                        pltpu.VMEM((2, wv, qk), F32)],
        compiler_params=_cparams(("arbitrary",), VMEM_LIMIT),
        name="gla_mixer",
    )(zg, la, s0, norm_g.reshape(1, dv))


def _outproj_kernel(n_lat_tiles, yh_ref, yg_ref, yhc_ref, ygc_ref, wh_ref, wg_ref, x_ref, g1_ref,
                    g_ref, sh_ref, sc_ref, xo_ref, u_ref):
    i = pl.program_id(0)

    def finish(yh, yg):
        delta = _dot(yh[...], wh_ref[...]) + _dot(yg[...], wg_ref[...])
        x = x_ref[...] + g1_ref[0] * delta
        xo_ref[...] = x
        u_ref[...] = _rms(x, g_ref[...]) * (1.0 + sc_ref[0]) + sh_ref[0]

    @pl.when(i < n_lat_tiles)
    def _():
        finish(yh_ref, yg_ref)

    @pl.when(i >= n_lat_tiles)
    def _():
        finish(yhc_ref, ygc_ref)


def out_proj(y_lat, y_ctx, w_hy, w_gla, xs, mod, norm_g, tm, seq, n_lat, b, n_tok):
    d = xs.shape[1]
    nlt = n_lat // tm
    row = functools.partial(_mod_row, tm=tm, seq=seq, n_lat_tiles=nlt, b=b)
    full = lambda a: pl.BlockSpec(a.shape, lambda i: (0,) * a.ndim)
    modspec = lambda col: pl.BlockSpec((1, 1, d), lambda i: (row(i), 0, col))
    tile = lambda w: pl.BlockSpec((tm, w), lambda i: (i, 0))
    lat = lambda a: pl.BlockSpec((tm, a.shape[1]), lambda i: (jnp.minimum(i, nlt - 1), 0))
    ctx = lambda a: pl.BlockSpec((tm, a.shape[1]), lambda i: (jnp.maximum(i - nlt, 0), 0))
    return pl.pallas_call(
        functools.partial(_outproj_kernel, nlt),
        grid=(n_tok // tm,),
        in_specs=[lat(y_lat[0]), lat(y_lat[1]), ctx(y_ctx[0]), ctx(y_ctx[1]),
                  full(w_hy), full(w_gla), tile(d),
                  modspec(2), full(norm_g), modspec(3), modspec(4)],
        out_specs=[tile(d), tile(d)],
        out_shape=[jax.ShapeDtypeStruct((n_tok, d), F32)] * 2,
        compiler_params=_cparams(("arbitrary",), VMEM_LIMIT),
        name="out_proj",
    )(*y_lat, *y_ctx, w_hy, w_gla, xs, mod, norm_g, mod, mod)


def _router_kernel(n_exp, tm, u_ref, wh_ref, wl_ref, b_ref, tri_ref,
                   idx_ref, gate_ref, rank_ref, cnt_ref):
    i = pl.program_id(0)

    @pl.when(i == 0)
    def _():
        cnt_ref[...] = jnp.zeros_like(cnt_ref)

    uh, ul = _split(u_ref[...])
    nt = (((1,), (1,)), ((), ()))
    logits = (lax.dot_general(wh_ref[...], uh, nt, preferred_element_type=F32)
              + lax.dot_general(wh_ref[...], ul, nt, preferred_element_type=F32)
              + lax.dot_general(wl_ref[...], uh, nt, preferred_element_type=F32))
    s = 1.0 / (1.0 + jnp.exp(-logits))
    sel = s + b_ref[...]
    gsz = n_exp // N_GROUPS
    neg = -jnp.inf
    rows_g = lax.broadcasted_iota(jnp.int32, (gsz, tm), 0)
    blocks, gscore = [], []
    for g in range(N_GROUPS):
        blk = sel[g * gsz:(g + 1) * gsz, :]
        m1 = jnp.max(blk, axis=0, keepdims=True)
        i1 = jnp.min(jnp.where(blk == m1, rows_g, gsz), axis=0, keepdims=True)
        m2 = jnp.max(jnp.where(rows_g == i1, neg, blk), axis=0, keepdims=True)
        blocks.append(blk)
        gscore.append(m1 + m2)
    masked = []
    for g in range(N_GROUPS):
        beaten = jnp.zeros((1, tm), jnp.int32)
        for h in range(N_GROUPS):
            if h == g:
                continue
            wins = (gscore[h] > gscore[g]) | ((gscore[h] == gscore[g]) & (h < g))
            beaten = beaten + wins.astype(jnp.int32)
        masked.append(jnp.where(beaten < TOPK_GROUPS, blocks[g], neg))
    sel = jnp.concatenate(masked, axis=0)

    rows = lax.broadcasted_iota(jnp.int32, (n_exp, tm), 0)
    chosen = jnp.zeros((n_exp, tm), F32)
    idxs, gates = [], []
    for _ in range(TOP_K):
        m = jnp.max(sel, axis=0, keepdims=True)
        ik = jnp.min(jnp.where(sel == m, rows, n_exp), axis=0, keepdims=True)
        hit = rows == ik
        gates.append(jnp.sum(jnp.where(hit, s, 0.0), axis=0, keepdims=True))
        sel = jnp.where(hit, neg, sel)
        chosen = jnp.where(hit, 1.0, chosen)
        idxs.append(ik)
    gsum = gates[0]
    for g in gates[1:]:
        gsum = gsum + g
    inv = ROUTED_SCALE / gsum
    before = _dot(chosen.astype(BF16), tri_ref[...]) + cnt_ref[...]
    for k in range(TOP_K):
        idx_ref[k:k + 1, :] = idxs[k]
        gate_ref[k:k + 1, :] = gates[k] * inv
        rk = jnp.sum(jnp.where(rows == idxs[k], before, 0.0), axis=0, keepdims=True)
        rank_ref[k:k + 1, :] = rk.astype(jnp.int32)
    cnt_ref[...] += jnp.sum(chosen, axis=1, keepdims=True)


def moe_router(u, wr_hi, wr_lo, b_r, n_tok, tm):
    n_exp, d = wr_hi.shape
    tri = (lax.broadcasted_iota(jnp.int32, (tm, tm), 0)
           < lax.broadcasted_iota(jnp.int32, (tm, tm), 1)).astype(BF16)
    full = lambda a: pl.BlockSpec(a.shape, lambda i: (0,) * a.ndim)
    tok = pl.BlockSpec((TOP_K, tm), lambda i: (0, i))
    return pl.pallas_call(
        functools.partial(_router_kernel, n_exp, tm),
        grid=(n_tok // tm,),
        in_specs=[pl.BlockSpec((tm, d), lambda i: (i, 0)), full(wr_hi), full(wr_lo),
                  pl.BlockSpec((n_exp, 1), lambda i: (0, 0)), full(tri)],
        out_specs=[tok, tok, tok, pl.BlockSpec((n_exp, 1), lambda i: (0, 0))],
        out_shape=[jax.ShapeDtypeStruct((TOP_K, n_tok), jnp.int32),
                   jax.ShapeDtypeStruct((TOP_K, n_tok), F32),
                   jax.ShapeDtypeStruct((TOP_K, n_tok), jnp.int32),
                   jax.ShapeDtypeStruct((n_exp, 1), F32)],
        compiler_params=_cparams(("arbitrary",), VMEM_LIMIT),
        name="moe_router",
    )(u, wr_hi, wr_lo, b_r.reshape(n_exp, 1), tri)


def _start_row_copy(src, src_row, dst, dst_row, sem, queue):
    pltpu.async_copy(src.at[src_row], dst.at[dst_row], sem, priority=queue)


def _store_rows_tiled(ref, x):
    for j in range(ref.shape[-2]):
        ref[:, j, :] = x[:, j * LANES:(j + 1) * LANES]


def _load_rows_tiled(ref):
    return jnp.concatenate([ref[:, j, :] for j in range(ref.shape[-2])], axis=1)


def _pack_bf16_pairs(x):
    w = x.shape[1] // 2
    lo = pltpu.bitcast(x[:, :w].astype(BF16).astype(F32), jnp.uint32) >> 16
    hi = pltpu.bitcast(x[:, w:].astype(BF16).astype(F32), jnp.uint32)
    return hi | lo


def _unpack_bf16_pairs(p):
    lo = pltpu.bitcast(p << 16, F32)
    hi = pltpu.bitcast(p & jnp.uint32(0xFFFF0000), F32)
    return jnp.concatenate([lo, hi], axis=1).astype(BF16)


def _dest_kernel(n_exp, tm, idx_ref, rank_ref, ps_ref, dest_ref):
    rows = lax.broadcasted_iota(jnp.int32, (n_exp, tm), 0)
    ps = ps_ref[...]
    for k in range(TOP_K):
        base = jnp.sum(jnp.where(rows == idx_ref[k:k + 1, :], ps, 0), axis=0, keepdims=True)
        dest_ref[k:k + 1, :] = base + rank_ref[k:k + 1, :]


def moe_dest(idx, rank, pad_start, tm):
    n_tok = idx.shape[1]
    n_exp = pad_start.shape[0]
    tok = pl.BlockSpec((TOP_K, tm), lambda i: (0, i))
    return pl.pallas_call(
        functools.partial(_dest_kernel, n_exp, tm),
        grid=(n_tok // tm,),
        in_specs=[tok, tok, pl.BlockSpec((n_exp, 1), lambda i: (0, 0))],
        out_specs=tok,
        out_shape=jax.ShapeDtypeStruct((TOP_K, n_tok), jnp.int32),
        compiler_params=_cparams(("arbitrary",)),
        name="moe_dest",
    )(idx, rank, pad_start.reshape(n_exp, 1))


def _wait_rows(src, dst, sem, tm):
    for _ in range(TOP_K):
        pltpu.make_async_copy(src.at[pl.ds(0, tm)], dst.at[pl.ds(0, tm)], sem).wait()


ZERO_ROWS = MOE_ROWS // 2


def _zero_padding(n_exp, ps_ref, pe_ref, xs_ref, z_ref, sem, wait):
    def go(copy):
        copy.wait() if wait else copy.start()

    def per_expert(e, carry):
        s = ps_ref[e]
        m = pe_ref[e] - s
        bit = ZERO_ROWS
        while bit:
            start = s + (m & ~(2 * bit - 1))

            @pl.when((m & bit) != 0)
            def _():
                go(pltpu.make_async_copy(z_ref.at[pl.ds(0, bit)], xs_ref.at[pl.ds(start, bit)], sem))
            bit //= 2
        return carry

    lax.fori_loop(0, n_exp, per_expert, 0)

    tail = pe_ref[n_exp - 1]

    def per_chunk(j, carry):
        go(pltpu.make_async_copy(z_ref, xs_ref.at[pl.ds(tail + j * ZERO_ROWS, ZERO_ROWS)], sem))
        return carry

    lax.fori_loop(0, (xs_ref.shape[0] - tail) // ZERO_ROWS, per_chunk, 0)


def _dispatch_kernel(tm, n_steps, n_exp, ps_ref, pe_ref, dest_ref, u_ref, xs_ref, pk_ref, z_ref,
                     sems, zsem):
    i = pl.program_id(0)
    slot = i % 2
    pk = pk_ref.at[slot]
    sem = sems.at[slot]

    @pl.when(i == 0)
    def _():
        z_ref[...] = jnp.zeros_like(z_ref)
        _zero_padding(n_exp, ps_ref, pe_ref, xs_ref, z_ref, zsem, wait=False)

    @pl.when(i >= 2)
    def _():
        _wait_rows(pk, xs_ref, sem, tm)

    _store_rows_tiled(pk, _pack_bf16_pairs(u_ref[...]))

    def issue(t, carry):
        for k in range(TOP_K):
            _start_row_copy(pk, t, xs_ref, dest_ref[0, 0, k * tm + t], sem, k % 2)
        return carry

    lax.fori_loop(0, tm, issue, 0)

    @pl.when(i == n_steps - 1)
    def _():
        if n_steps > 1:
            _wait_rows(pk_ref.at[1 - slot], xs_ref, sems.at[1 - slot], tm)
        _wait_rows(pk, xs_ref, sem, tm)
        _zero_padding(n_exp, ps_ref, pe_ref, xs_ref, z_ref, zsem, wait=True)


def moe_dispatch(u, dest, pad_from, pad_end, n_tok, n_slots, tm):
    d = u.shape[1]
    sub = d // 2 // LANES
    n_exp = pad_end.shape[0]
    grid_spec = pltpu.PrefetchScalarGridSpec(
        num_scalar_prefetch=2,
        grid=(n_tok // tm,),
        in_specs=[pl.BlockSpec((1, 1, TOP_K * tm), lambda i, *_: (i, 0, 0), memory_space=pltpu.SMEM),
                  pl.BlockSpec((tm, d), lambda i, *_: (i, 0))],
        out_specs=pl.BlockSpec(memory_space=pl.ANY),
        scratch_shapes=[pltpu.VMEM((2, tm, sub, LANES), jnp.uint32),
                        pltpu.VMEM((ZERO_ROWS, sub, LANES), jnp.uint32),
                        pltpu.SemaphoreType.DMA((2,)), pltpu.SemaphoreType.DMA])
    return pl.pallas_call(
        functools.partial(_dispatch_kernel, tm, n_tok // tm, n_exp),
        grid_spec=grid_spec,
        out_shape=jax.ShapeDtypeStruct((n_slots, sub, LANES), jnp.uint32),
        compiler_params=_cparams(("arbitrary",)),
        name="moe_dispatch",
    )(pad_from, pad_end, dest, u)


def _gmm_kernel(layer, be_ref, nx_ref, fl_ref, nu_ref, x_ref, w1_hbm, w3_hbm, w2_hbm, y_ref,
                w1f, w3f, w2f, w1b, w3b, w2b, sems):
    i = pl.program_id(0)
    fl = fl_ref[i]
    sub = x_ref.shape[0] // MOE_ROWS

    def fetch(e, s):
        return (pltpu.make_async_copy(w1_hbm.at[layer, e], w1f.at[s], sems.at[s]),
                pltpu.make_async_copy(w3_hbm.at[layer, e], w3f.at[s], sems.at[s]),
                pltpu.make_async_copy(w2_hbm.at[layer, e], w2f.at[s], sems.at[s]))

    @pl.when(i < nu_ref[0])
    def _():
        @pl.when((fl & 1) != 0)
        def _():
            s = (fl >> 1) & 1

            @pl.when(i == 0)
            def _():
                for c in fetch(be_ref[i], s):
                    c.start()

            for c in fetch(be_ref[i], s):
                c.wait()

            @pl.when((fl & 4) != 0)
            def _():
                for c in fetch(nx_ref[i], 1 - s):
                    c.start()

            w1b[...] = w1f[s].astype(BF16)
            w3b[...] = w3f[s].astype(BF16)
            w2b[...] = w2f[s].astype(BF16)

        x = _unpack_bf16_pairs(jnp.concatenate(
            [x_ref[pl.ds(j, MOE_ROWS, stride=sub), :] for j in range(sub)], axis=1))
        h = _silu(_dot(x, w1b[...])) * _dot(x, w3b[...])
        y = _pack_bf16_pairs(_dot(h.astype(BF16), w2b[...]))
        for j in range(sub):
            y_ref[pl.ds(j, MOE_ROWS, stride=sub), :] = y[:, j * LANES:(j + 1) * LANES]

    @pl.when(i >= nu_ref[0])
    def _():
        y_ref[...] = jnp.zeros_like(y_ref)


def moe_experts(x_sorted, block_e, n_used, w1, w3, w2, layer):
    n_blocks = block_e.shape[0]
    _, _, d, f = w1.shape
    n_slots, sub, _ = x_sorted.shape
    idx = jnp.arange(n_blocks, dtype=jnp.int32)
    first = (idx == 0) | (block_e != jnp.roll(block_e, 1))
    seg = jnp.cumsum(first.astype(jnp.int32)) - 1
    first_at_or_after = lax.cummin(jnp.where(first, idx, n_blocks)[::-1])[::-1]
    nxt_idx = jnp.concatenate([first_at_or_after[1:], jnp.full((1,), n_blocks, jnp.int32)])
    has_next = nxt_idx < n_used[0]
    nxt_e = block_e[jnp.minimum(nxt_idx, n_blocks - 1)]
    flags = (first.astype(jnp.int32) | ((seg & 1) << 1) | (has_next.astype(jnp.int32) << 2))
    grid_spec = pltpu.PrefetchScalarGridSpec(
        num_scalar_prefetch=4,
        grid=(n_blocks,),
        in_specs=[pl.BlockSpec((MOE_ROWS * sub, LANES), lambda i, *_: (i, 0)),
                  pl.BlockSpec(memory_space=pl.ANY),
                  pl.BlockSpec(memory_space=pl.ANY),
                  pl.BlockSpec(memory_space=pl.ANY)],
        out_specs=pl.BlockSpec((MOE_ROWS * sub, LANES), lambda i, *_: (i, 0)),
        scratch_shapes=[pltpu.VMEM((2, d, f), F32), pltpu.VMEM((2, d, f), F32),
                        pltpu.VMEM((2, f, d), F32),
                        pltpu.VMEM((d, f), BF16), pltpu.VMEM((d, f), BF16),
                        pltpu.VMEM((f, d), BF16), pltpu.SemaphoreType.DMA((2,))])
    y = pl.pallas_call(
        functools.partial(_gmm_kernel, layer),
        grid_spec=grid_spec,
        out_shape=jax.ShapeDtypeStruct((n_slots * sub, LANES), jnp.uint32),
        compiler_params=_cparams(("arbitrary",), VMEM_LIMIT),
        name="moe_experts",
    )(block_e, nxt_e, flags, n_used, x_sorted.reshape(n_slots * sub, LANES), w1, w3, w2)
    return y.reshape(n_slots, sub, LANES)


def _combine_kernel(tm, n_steps, final, dest_ref, nxt_ref, y_ref, gate_ref, u_ref, ws1_ref, ws3_ref,
                    ws2_ref, x_ref, g2_ref, fg_ref, o_ref, buf_ref, routed_ref, sems):
    i = pl.program_id(0)
    slot = i % 2
    grp = SUBLANES

    def gather(d_ref, s, g):
        for tt in range(grp):
            t = g * grp + tt
            for k in range(TOP_K):
                _start_row_copy(y_ref, d_ref[0, 0, k * tm + t], buf_ref.at[s, k], t, sems.at[s], k % 2)

    def reduce(g, s):
        r0 = pl.multiple_of(g * grp, grp)
        gt = gate_ref[pl.ds(r0, grp), :]
        lo = hi = None
        for k in range(TOP_K):
            p = buf_ref[s, k, pl.ds(r0, grp)]
            gk = gt[:, k:k + 1][:, :, None]
            lo_k = pltpu.bitcast(p << 16, F32) * gk
            hi_k = pltpu.bitcast(p & jnp.uint32(0xFFFF0000), F32) * gk
            lo = lo_k if lo is None else lo + lo_k
            hi = hi_k if hi is None else hi + hi_k
        sub = lo.shape[1]
        routed_ref[pl.ds(r0, grp), 0:sub, :] = lo
        routed_ref[pl.ds(r0, grp), sub:2 * sub, :] = hi

    @pl.when(i == 0)
    def _():
        lax.fori_loop(0, tm // grp, lambda g, c: (gather(dest_ref, slot, g), c)[1], 0)

    _wait_rows(y_ref, buf_ref.at[slot, 0], sems.at[slot], tm)

    @pl.when(i + 1 < n_steps)
    def _():
        def both(g, c):
            gather(nxt_ref, 1 - slot, g)
            reduce(g, slot)
            return c
        lax.fori_loop(0, tm // grp, both, 0)

    @pl.when(i + 1 >= n_steps)
    def _():
        lax.fori_loop(0, tm // grp, lambda g, c: (reduce(g, slot), c)[1], 0)

    ub = u_ref[...].astype(BF16)
    hs = _silu(_dot(ub, ws1_ref[...])) * _dot(ub, ws3_ref[...])
    shared = _dot(hs.astype(BF16), ws2_ref[...])
    x = x_ref[...] + g2_ref[0] * (_load_rows_tiled(routed_ref) + shared)
    o_ref[...] = _rms(x, fg_ref[...]) if final else x


def moe_combine(y_sorted, dest, gate_rep, u, ws1, ws3, ws2, xs, mod, final_g, final,
                tm, seq, n_lat, b, n_tok):
    d = u.shape[1]
    nlt = n_lat // tm
    row = functools.partial(_mod_row, tm=tm, seq=seq, n_lat_tiles=nlt, b=b)
    full = lambda a: pl.BlockSpec(a.shape, lambda i: (0,) * a.ndim)
    tile = pl.BlockSpec((tm, d), lambda i: (i, 0))
    n_steps = n_tok // tm
    return pl.pallas_call(
        functools.partial(_combine_kernel, tm, n_steps, final),
        grid=(n_steps,),
        in_specs=[pl.BlockSpec((1, 1, TOP_K * tm), lambda i: (i, 0, 0), memory_space=pltpu.SMEM),
                  pl.BlockSpec((1, 1, TOP_K * tm),
                               lambda i: (jnp.minimum(i + 1, n_steps - 1), 0, 0),
                               memory_space=pltpu.SMEM),
                  pl.BlockSpec(memory_space=pl.ANY),
                  pl.BlockSpec((tm, TOP_K), lambda i: (i, 0)),
                  tile, full(ws1), full(ws3), full(ws2), tile,
                  pl.BlockSpec((1, 1, d), lambda i: (row(i), 0, 5)),
                  full(final_g)],
        out_specs=tile,
        out_shape=jax.ShapeDtypeStruct((n_tok, d), F32),
        scratch_shapes=[pltpu.VMEM((2, TOP_K, tm) + y_sorted.shape[1:], jnp.uint32),
                        pltpu.VMEM((tm, d // LANES, LANES), F32), pltpu.SemaphoreType.DMA((2,))],
        compiler_params=_cparams(("arbitrary",), VMEM_LIMIT),
        name="moe_combine",
    )(dest, dest, y_sorted, gate_rep, u, ws1, ws3, ws2, xs, mod, final_g)


def moe_layer(u, xs, mod, router_w, router_b, w1, w3, w2, layer, ws1, ws3, ws2, final_g, final,
              tm, seq, n_lat, b, n_tok):
    n_exp = router_w.shape[1]
    wr_hi, wr_lo = _split(router_w.T)
    idx, gate, rank, cnt = moe_router(u, wr_hi, wr_lo, router_b, n_tok, tm)
    counts = cnt[:, 0].astype(jnp.int32)
    padded = (counts + MOE_ROWS - 1) // MOE_ROWS * MOE_ROWS
    pad_end = jnp.cumsum(padded)
    pad_start = pad_end - padded
    dest = moe_dest(idx, rank, pad_start, math.gcd(n_tok, DEST_TILE))
    n_blocks = -(-(n_tok * TOP_K + n_exp * (MOE_ROWS - 1)) // MOE_ROWS)
    block_start = jnp.arange(n_blocks, dtype=jnp.int32) * MOE_ROWS
    block_e = jnp.minimum(jnp.sum((pad_end[None, :] <= block_start[:, None]).astype(jnp.int32), axis=1),
                          n_exp - 1)
    n_used = (pad_end[-1:] // MOE_ROWS).astype(jnp.int32)
    dest = dest.reshape(TOP_K, n_tok // tm, tm).transpose(1, 0, 2).reshape(n_tok // tm, 1, TOP_K * tm)
    x_sorted = moe_dispatch(u, dest, pad_start + counts, pad_end, n_tok, n_blocks * MOE_ROWS, tm)
    y_sorted = moe_experts(x_sorted, block_e, n_used, w1, w3, w2, layer)
    return moe_combine(y_sorted, dest, gate.T, u, ws1.astype(BF16), ws3.astype(BF16),
                       ws2.astype(BF16), xs, mod, final_g.reshape(1, -1), final,
                       tm, seq, n_lat, b, n_tok)


def _grid_pos_emb(rows, d):
    r, col = np.meshgrid(np.arange(rows, dtype=np.float64), np.arange(GRID_W, dtype=np.float64),
                         indexing='ij')
    quarter = d // 4
    omega = 1.0 / (10000.0 ** (np.arange(quarter, dtype=np.float64) / quarter))

    def emb(p):
        a = p.reshape(-1)[:, None] * omega[None, :]
        return np.concatenate([np.sin(a), np.cos(a)], axis=-1)

    return jnp.asarray(np.concatenate([emb(r), emb(col)], axis=-1).astype(np.float32))


def kernel(x, c, ctx, c_ctx, ada_w, ada_b, norm1_g, norm2_g, w_in, w_out, hy_conv_w, hy_conv_b,
           hy_f_w1, hy_f_b1, hy_f_w2, hy_f_b2, hy_f_w3, hy_skip, gla_wa2, gla_ba2, gla_norm_g,
           router_w, router_b, exp_w1, exp_w3, exp_w2, sh_w1, sh_w3, sh_w2, final_g):
    b, l, d = x.shape
    lc = ctx.shape[1]
    depth = ada_w.shape[0]
    n_lat, n_ctx = b * l, b * lc
    n_all = n_lat + n_ctx
    hy_width = hy_skip.shape[-1]
    hy_cols = hy_conv_w.shape[-1]
    qk = gla_wa2.shape[-1]
    wv = d - hy_width
    g_cols = 2 * qk + 2 * wv
    tm = math.gcd(math.gcd(l, lc), TOKEN_TILE)
    tm_proj = math.gcd(math.gcd(l, n_ctx), PROJ_TILE)
    cb = min(hy_width, MXU_DIM)

    cond = jnp.concatenate([c, c_ctx[None, :]], axis=0)
    n_rows = -(-(b + 1) // SUBLANES) * SUBLANES
    cond = jnp.concatenate([cond, jnp.zeros((n_rows - b - 1, d), F32)], axis=0)
    mods = ada_table(cond, ada_w, ada_b)

    xs = (x.reshape(n_lat, d), _grid_pos_emb(l // GRID_W, d), ctx.reshape(n_ctx, d))

    tabs = {}
    for length in (l, lc):
        if length not in tabs:
            tabs[length] = _trig_tables(length, True) + _trig_tables(length, False)
    zero_state = jnp.zeros((b, 2, wv, qk), F32)

    for layer in range(depth):
        last = layer == depth - 1
        mod = mods[layer][:, None, :]
        g1 = norm1_g[layer].reshape(1, d)
        g2 = norm2_g[layer].reshape(1, d)
        w_l = w_in[layer].astype(BF16)
        wa2 = gla_wa2[layer]
        rank = wa2.shape[1]
        wa2_bd = jnp.zeros((2 * rank, 2 * qk), F32)
        wa2_bd = wa2_bd.at[:rank, :qk].set(wa2[0]).at[rank:, qk:].set(wa2[1])
        ba2_cat = gla_ba2[layer].reshape(1, 2 * qk)
        outs = in_proj(xs, g1, mod, w_l[:, :hy_cols], w_l[:, hy_cols:hy_cols + g_cols],
                       w_l[:, hy_cols + g_cols:], wa2_bd, ba2_cat, tm_proj, l, n_lat, b)
        zh, zg, la = outs[:3]
        if layer == 0:
            xs = outs[3]

        filt = (hy_f_w1[layer], hy_f_b1[layer], hy_f_w2[layer], hy_f_b2[layer], hy_f_w3[layer],
                hy_skip[layer])
        yc_gla, s_ctx = gla_mixer(zg, la, zero_state, gla_norm_g[layer], lc, b, n_lat)
        y_gla, _ = gla_mixer(zg, la, s_ctx, gla_norm_g[layer], l, b, 0)
        ct, st, c0, s0 = tabs[l]
        gr, gi = hyena_spectra(l, *filt, c0, s0)
        y_hy = hyena_mixer(zh, hy_conv_w[layer], hy_conv_b[layer], ct, st, gr, gi, l, b, 0, cb)
        if last:
            n_tok = n_lat
            y_ctx = (y_hy, y_gla)
        else:
            n_tok = n_all
            ct, st, c0, s0 = tabs[lc]
            gr, gi = hyena_spectra(lc, *filt, c0, s0)
            yc_hy = hyena_mixer(zh, hy_conv_w[layer], hy_conv_b[layer], ct, st, gr, gi,
                                lc, b, n_lat, cb)
            y_ctx = (yc_hy, yc_gla)
        w_o = w_out[layer].astype(BF16)
        xs_mid, u = out_proj((y_hy, y_gla), y_ctx, w_o[:hy_width], w_o[hy_width:], xs, mod, g2,
                             tm, l, n_lat, b, n_tok)
        xs = moe_layer(u, xs_mid, mod, router_w[layer], router_b[layer], exp_w1, exp_w3, exp_w2,
                       layer, sh_w1[layer], sh_w3[layer], sh_w2[layer],
                       final_g, last, tm, l, n_lat, b, n_tok)
    return xs[:n_lat].reshape(b, l, d)
```

```python
import functools
import math

import numpy as np
import jax
import jax.numpy as jnp
from jax import lax
from jax.experimental import pallas as pl
from jax.experimental.pallas import tpu as pltpu

GRID_W = 64
EPS = 1e-6

HY_EMB = 33
HY_EMB_PAD = 40
HY_FREQ = 1.0
HY_TARGET = 1e-2
HY_FAST_PCT = 0.3
HY_SLOW_PCT = 1.5
HY_MIN_DECAY = math.log(1.0 / HY_TARGET) / HY_SLOW_PCT
HY_MAX_DECAY = math.log(1.0 / HY_TARGET) / HY_FAST_PCT

GLA_HEADS = 4
GLA_TAU = 16.0
GLA_CHUNK = 64
GLA_GROUP = 4
GLA_STATE_UNROLL = 4

TOP_K = 8
N_GROUPS = 8
TOPK_GROUPS = 4
ROUTED_SCALE = 2.5

SUBLANES = 8
LANES = 128
MXU_DIM = 256
VMEM_LIMIT = 56 * 1024 * 1024

TOKEN_TILE = 256
PROJ_TILE = 512
DEST_TILE = 1024
ADA_COLS = 1024
SPECTRA_ROWS = 256
HY_FREQ_CHUNK = 1024
MOE_ROWS = 512

F32 = jnp.float32
BF16 = jnp.bfloat16


def _cparams(sem, vmem=None):
    return pltpu.CompilerParams(dimension_semantics=sem, vmem_limit_bytes=vmem)


def _split(a):
    hi = a.astype(BF16)
    lo = (a - hi.astype(F32)).astype(BF16)
    return hi, lo


def _dot(a, b):
    return jnp.dot(a, b, preferred_element_type=F32)


def _dot3(a, b):
    ah, al = _split(a)
    bh, bl = _split(b)
    return _dot(ah, bh) + _dot(ah, bl) + _dot(al, bh)


def _silu(x):
    return x * (1.0 / (1.0 + jnp.exp(-x)))


def _rms(x, g):
    return x * lax.rsqrt(jnp.mean(x * x, axis=-1, keepdims=True) + EPS) * g


def _ada_kernel(c_ref, w_ref, b_ref, o_ref):
    o_ref[0] = _dot3(_silu(c_ref[...]), w_ref[0]) + b_ref[0]


def ada_table(cond_rows, ada_w, ada_b):
    depth, d, six_d = ada_w.shape
    r = cond_rows.shape[0]
    tn = ADA_COLS
    return pl.pallas_call(
        _ada_kernel,
        grid=(depth, six_d // tn),
        in_specs=[pl.BlockSpec((r, d), lambda l, j: (0, 0)),
                  pl.BlockSpec((1, d, tn), lambda l, j: (l, 0, j)),
                  pl.BlockSpec((1, 1, tn), lambda l, j: (l, 0, j))],
        out_specs=pl.BlockSpec((1, r, tn), lambda l, j: (l, 0, j)),
        out_shape=jax.ShapeDtypeStruct((depth, r, six_d), F32),
        compiler_params=_cparams(("arbitrary", "arbitrary")),
        name="ada_table",
    )(cond_rows, ada_w, ada_b.reshape(depth, 1, six_d))


def _mod_row(i, tm, seq, n_lat_tiles, b):
    return jnp.where(i < n_lat_tiles, (i * tm) // seq, b)


def _inproj_embed_kernel(n_lat_tiles, x_ref, p_ref, c_ref, *rest):
    xs_ref = rest[-1]
    i = pl.program_id(0)

    @pl.when(i < n_lat_tiles)
    def _():
        xs_ref[...] = x_ref[...] + p_ref[...]

    @pl.when(i >= n_lat_tiles)
    def _():
        xs_ref[...] = c_ref[...]

    _inproj_kernel(xs_ref, *rest[:-1])


def _inproj_kernel(x_ref, g_ref, sh_ref, sc_ref, wh_ref, wg_ref, wa_ref, wa2_ref, ba2_ref,
                   zh_ref, zg_ref, la_ref):
    x = x_ref[...]
    h = _rms(x, g_ref[...]) * (1.0 + sc_ref[0]) + sh_ref[0]
    hb = h.astype(BF16)
    zh_ref[...] = _dot(hb, wh_ref[...]).astype(BF16)
    zg_ref[...] = _dot(hb, wg_ref[...]).astype(BF16)
    za = _dot(hb, wa_ref[...])
    t = _dot3(za, wa2_ref[...]) + ba2_ref[...]
    la_ref[...] = (jnp.minimum(t, 0.0) - jnp.log(1.0 + jnp.exp(-jnp.abs(t)))) * (1.0 / GLA_TAU)


def in_proj(tokens, norm_g, mod, w_hy, w_g, w_a, wa2_bd, ba2_cat, tm, seq, n_lat, b):
    embed = isinstance(tokens, tuple)
    nlt = n_lat // tm
    if embed:
        x2, pos, ctx2 = tokens
        d = x2.shape[1]
        n = x2.shape[0] + ctx2.shape[0]
        lt = pos.shape[0] // tm
        tok_specs = [pl.BlockSpec((tm, d), lambda i: (jnp.minimum(i, nlt - 1), 0)),
                     pl.BlockSpec((tm, d), lambda i: (i % lt, 0)),
                     pl.BlockSpec((tm, d), lambda i: (jnp.maximum(i - nlt, 0), 0))]
        body = functools.partial(_inproj_embed_kernel, nlt)
    else:
        n, d = tokens.shape
        tokens = (tokens,)
        tok_specs = [pl.BlockSpec((tm, d), lambda i: (i, 0))]
        body = _inproj_kernel
    row = functools.partial(_mod_row, tm=tm, seq=seq, n_lat_tiles=nlt, b=b)
    cols = (w_hy.shape[1], w_g.shape[1], wa2_bd.shape[1]) + ((d,) if embed else ())
    dtypes = (BF16, BF16, F32) + ((F32,) if embed else ())
    full = lambda a: pl.BlockSpec(a.shape, lambda i: (0,) * a.ndim)
    return pl.pallas_call(
        body,
        grid=(n // tm,),
        in_specs=tok_specs + [full(norm_g),
                              pl.BlockSpec((1, 1, d), lambda i: (row(i), 0, 0)),
                              pl.BlockSpec((1, 1, d), lambda i: (row(i), 0, 1)),
                              full(w_hy), full(w_g), full(w_a), full(wa2_bd), full(ba2_cat)],
        out_specs=[pl.BlockSpec((tm, c), lambda i: (i, 0)) for c in cols],
        out_shape=[jax.ShapeDtypeStruct((n, c), t) for c, t in zip(cols, dtypes)],
        compiler_params=_cparams(("arbitrary",), VMEM_LIMIT),
        name="in_proj",
    )(*tokens, norm_g, mod, mod, w_hy, w_g, w_a, wa2_bd, ba2_cat)


def _trig_tables(l, half_shift):
    k = np.arange(l, dtype=np.int64)[:, None]
    n = np.arange(l, dtype=np.int64)[None, :]
    m = ((2 * k + 1) * (2 * n + (1 if half_shift else 0))) % (8 * l)
    ang = m.astype(np.float64) * (2.0 * math.pi / (8 * l))
    return (jnp.asarray(np.cos(ang).astype(np.float32).astype(BF16)),
            jnp.asarray(np.sin(ang).astype(np.float32).astype(BF16)))


def _filter_kernel(n_orders, width, l,
                   pos_ref, t_ref, dl_ref, w1_ref, b1_ref, w2_ref, b2_ref, w3_ref, skip_ref,
                   c0_ref, s0_ref, gr_ref, gi_ref, hs_ref, hd_ref):
    j = pl.program_id(0)

    @pl.when(j == 0)
    def _():
        hid = jnp.sin(HY_FREQ * (_dot3(pos_ref[...], w1_ref[...]) + b1_ref[...]))
        hid = jnp.sin(HY_FREQ * (_dot3(hid, w2_ref[...]) + b2_ref[...]))
        h = _dot3(hid, w3_ref[...])
        decay = jnp.exp(-t_ref[...] * dl_ref[...])
        for o in range(n_orders):
            hf = h[:, (2 * o) * width:(2 * o + 1) * width] * decay
            hb = h[:, (2 * o + 1) * width:(2 * o + 2) * width] * decay
            hs_ref[:, o * width:(o + 1) * width] = (hf + hb).astype(BF16)
            hd_ref[:, o * width:(o + 1) * width] = (hb - hf).astype(BF16)

    scale = 1.0 / l
    gr = _dot(c0_ref[...], hs_ref[...])
    gi = _dot(s0_ref[...], hd_ref[...])
    for o in range(n_orders):
        gr_ref[o] = (gr[:, o * width:(o + 1) * width] + skip_ref[o]) * scale
        gi_ref[o] = gi[:, o * width:(o + 1) * width] * scale


def hyena_spectra(l, w1, b1, w2, b2, w3, skip, c0, s0):
    width = skip.shape[-1]
    n_orders = skip.shape[0]
    t64 = np.linspace(0.0, 1.0, l)[:, None]
    bands = (HY_EMB - 1) // 2
    w = 2.0 * math.pi * np.arange(l, dtype=np.float64)[:, None] / l
    f = np.linspace(1e-4, bands - 1, bands)[None, :]
    pos = jnp.asarray(np.concatenate([t64, np.cos(f * w), -np.sin(f * w),
                                      np.zeros((l, HY_EMB_PAD - HY_EMB))], axis=-1).astype(np.float32))
    t = jnp.asarray(t64.astype(np.float32))
    w1p = jnp.concatenate([w1, jnp.zeros((HY_EMB_PAD - HY_EMB, w1.shape[1]), F32)], axis=0)
    deltas = jnp.asarray(np.linspace(HY_MIN_DECAY, HY_MAX_DECAY, width)[None, :].astype(np.float32))
    tk = min(l, SPECTRA_ROWS)
    full = lambda a: pl.BlockSpec(a.shape, lambda j: (0,) * a.ndim)
    skip3 = skip.reshape(n_orders, 1, width)
    args = (pos, t, deltas, w1p, b1.reshape(1, -1), w2, b2.reshape(1, -1), w3, skip3)
    return pl.pallas_call(
        functools.partial(_filter_kernel, n_orders, width, l),
        grid=(l // tk,),
        in_specs=[full(a) for a in args] + [pl.BlockSpec((tk, l), lambda j: (j, 0)),
                                            pl.BlockSpec((tk, l), lambda j: (j, 0))],
        out_specs=[pl.BlockSpec((n_orders, tk, width), lambda j: (0, j, 0)),
                   pl.BlockSpec((n_orders, tk, width), lambda j: (0, j, 0))],
        out_shape=[jax.ShapeDtypeStruct((n_orders, l, width), F32)] * 2,
        scratch_shapes=[pltpu.VMEM((l, n_orders * width), BF16),
                        pltpu.VMEM((l, n_orders * width), BF16)],
        compiler_params=_cparams(("arbitrary",), VMEM_LIMIT),
        name="hyena_spectra",
    )(*args, c0, s0)


def _hyena_kernel(l, fc, zv_ref, z1_ref, z2_ref, wv_ref, w1_ref, w2_ref, bv_ref, b1_ref, b2_ref,
                  c_ref, s_ref, gr_ref, gi_ref, y_ref, ub_ref, a_ref, b_ref, x1_ref, x2_ref):
    rows = lax.broadcasted_iota(jnp.int32, (l, 1), 0)

    def conv3(z_ref, w_ref, bias_ref):
        z = z_ref[...].astype(F32)
        zm = jnp.where(rows == 0, 0.0, pltpu.roll(z, 1, 0))
        zp = jnp.where(rows == l - 1, 0.0, pltpu.roll(z, l - 1, 0))
        return zm * w_ref[0:1, :] + z * w_ref[1:2, :] + zp * w_ref[2:3, :] + bias_ref[...]

    ub_ref[...] = conv3(zv_ref, wv_ref, bv_ref).astype(BF16)
    x1_ref[...] = conv3(z1_ref, w1_ref, b1_ref)
    x2_ref[...] = conv3(z2_ref, w2_ref, b2_ref)
    n = l // fc

    def forward(o):
        def body(c, carry):
            r0 = pl.multiple_of(c * fc, fc)
            ur = _dot(c_ref[pl.ds(r0, fc), :], ub_ref[...])
            us = _dot(s_ref[pl.ds(r0, fc), :], ub_ref[...])
            gr = gr_ref[o, pl.ds(r0, fc), :]
            gi = gi_ref[o, pl.ds(r0, fc), :]
            a_ref[pl.ds(r0, fc), :] = (ur * gr + us * gi).astype(BF16)
            b_ref[pl.ds(r0, fc), :] = (us * gr - ur * gi).astype(BF16)
            return carry
        lax.fori_loop(0, n, body, 0)

    def inverse(gate_ref, dst_ref):
        def body(c, carry):
            r0 = pl.multiple_of(c * fc, fc)
            lc = (_dot(c_ref[pl.ds(r0, fc), :], a_ref[...])
                  + _dot(s_ref[pl.ds(r0, fc), :], b_ref[...]))
            dst_ref[pl.ds(r0, fc), :] = (gate_ref[pl.ds(r0, fc), :] * lc).astype(BF16)
            return carry
        lax.fori_loop(0, n, body, 0)

    forward(0)
    inverse(x1_ref, ub_ref)
    forward(1)
    inverse(x2_ref, y_ref)


def hyena_mixer(zh, conv_w, conv_b, ctab, stab, gr, gi, l, n_seq, row0, cb):
    width = gr.shape[-1]
    n_orders = gr.shape[0]
    ncb = width // cb
    sb0 = row0 // l
    conv_b2 = conv_b.reshape(1, -1)
    zspec = lambda part: pl.BlockSpec((l, cb), lambda c, s: (sb0 + s, part * ncb + c))
    wspec = lambda part: pl.BlockSpec((3, cb), lambda c, s: (0, part * ncb + c))
    bspec = lambda part: pl.BlockSpec((1, cb), lambda c, s: (0, part * ncb + c))
    once = pl.Buffered(1)
    tspec = pl.BlockSpec((l, l), lambda c, s: (0, 0), pipeline_mode=once)
    gspec = pl.BlockSpec((n_orders, l, cb), lambda c, s: (0, 0, c), pipeline_mode=once)
    return pl.pallas_call(
        functools.partial(_hyena_kernel, l, min(l, HY_FREQ_CHUNK)),
        grid=(ncb, n_seq),
        in_specs=[zspec(0), zspec(1), zspec(2), wspec(0), wspec(1), wspec(2),
                  bspec(0), bspec(1), bspec(2), tspec, tspec, gspec, gspec],
        out_specs=pl.BlockSpec((l, cb), lambda c, s: (s, c)),
        out_shape=jax.ShapeDtypeStruct((n_seq * l, width), BF16),
        scratch_shapes=[pltpu.VMEM((l, cb), BF16), pltpu.VMEM((l, cb), BF16),
                        pltpu.VMEM((l, cb), BF16), pltpu.VMEM((l, cb), F32),
                        pltpu.VMEM((l, cb), F32)],
        compiler_params=_cparams(("arbitrary", "arbitrary"), VMEM_LIMIT),
        name="hyena_mixer",
    )(zh, zh, zh, conv_w, conv_w, conv_w, conv_b2, conv_b2, conv_b2, ctab, stab, gr, gi)


def _gla_kernel(l, qk, wv, dk, dv,
                zg_ref, la_ref, s0_ref, g_ref, y_ref, sfin_ref,
                of_ref, ob_ref, qe_ref, ks_ref, dec_ref, st_ref):
    ch = GLA_CHUNK
    n = l // ch
    heads = qk // dk
    grp = min(GLA_GROUP, n)
    rg = grp * ch
    r_i = lax.broadcasted_iota(jnp.int32, (rg, rg), 0)
    c_i = lax.broadcasted_iota(jnp.int32, (rg, rg), 1)
    same = (r_i // ch) == (c_i // ch)
    low = same & (r_i >= c_i)
    upp = same & (r_i <= c_i)
    masks = (low, upp)
    cum2 = tuple(jnp.concatenate([jnp.concatenate([m.astype(BF16)] * 2, axis=1),
                                  jnp.concatenate([same.astype(BF16)] * 2, axis=1)], axis=0)
                 for m in masks)
    lane_head = lax.broadcasted_iota(jnp.int32, (1, qk), 1) // dk
    scale = dk ** -0.5
    o_refs = (of_ref, ob_ref)

    def local(g, d):
        r0 = pl.multiple_of(g * rg, rg)
        q = zg_ref[pl.ds(r0, rg), 0:qk].astype(F32) * scale
        k = zg_ref[pl.ds(r0, rg), qk:2 * qk].astype(F32)
        v = zg_ref[pl.ds(r0, rg), 2 * qk:2 * qk + wv]
        la = la_ref[pl.ds(r0, rg), d * qk:(d + 1) * qk]
        lh, ll = _split(la)
        bt = _dot(cum2[d], jnp.concatenate([lh, ll], axis=0))
        b, tot = bt[:rg], bt[rg:]
        qe = (q * jnp.exp(b)).astype(BF16)
        ke = (k * jnp.exp(-b)).astype(BF16)
        qe_ref[d, pl.ds(r0, rg), :] = qe
        ks_ref[d, pl.ds(r0, rg), :] = (k * jnp.exp(tot - b)).astype(BF16)
        dec = jnp.exp(tot)
        for c in range(grp):
            dec_ref[d, pl.ds(pl.multiple_of((g * grp + c) * 8, 8), 8), :] = dec[c * ch:c * ch + 8]
        parts = []
        for h in range(heads):
            qh = jnp.where(lane_head == h, qe, jnp.zeros_like(qe))
            att = lax.dot_general(qh, ke, (((1,), (1,)), ((), ())), preferred_element_type=F32)
            att = jnp.where(masks[d], att, 0.0).astype(BF16)
            parts.append(_dot(att, v[:, h * dv:(h + 1) * dv]))
        o_refs[d][pl.ds(r0, rg), :] = jnp.concatenate(parts, axis=1)

    def local_body(g, carry):
        local(g, 0)
        local(g, 1)
        return carry

    lax.fori_loop(0, n // grp, local_body, 0)

    bd = (lax.broadcasted_iota(jnp.int32, (wv, qk), 0) // dv
          == lax.broadcasted_iota(jnp.int32, (wv, qk), 1) // dk).astype(F32)
    st_ref[...] = s0_ref[0]

    def carry_state(c, d):
        r0 = pl.multiple_of(c * ch, ch)
        st = st_ref[d]
        o = lax.dot_general(qe_ref[d, pl.ds(r0, ch), :], st.astype(BF16), (((1,), (1,)), ((), ())),
                            preferred_element_type=F32)
        o_refs[d][pl.ds(r0, ch), :] += o
        v = zg_ref[pl.ds(r0, ch), 2 * qk:2 * qk + wv]
        upd = lax.dot_general(v, ks_ref[d, pl.ds(r0, ch), :], (((0,), (0,)), ((), ())),
                              preferred_element_type=F32)
        dec = dec_ref[d, pl.ds(pl.multiple_of(c * 8, 8), 1), :]
        st_ref[d] = st * dec + upd * bd

    su = min(GLA_STATE_UNROLL, n)

    def finish(c):
        r0 = pl.multiple_of(c * ch, ch)
        o = of_ref[pl.ds(r0, ch), :] + ob_ref[pl.ds(r0, ch), :]
        r = zg_ref[pl.ds(r0, ch), 2 * qk + wv:2 * qk + 2 * wv].astype(F32)
        parts = [_rms(o[:, h * dv:(h + 1) * dv], g_ref[...]) for h in range(heads)]
        y_ref[pl.ds(r0, ch), :] = (jnp.concatenate(parts, axis=1) * _silu(r)).astype(BF16)

    def state_body(done, i, carry):
        for u in range(su):
            carry_state(i * su + u, 0)
            carry_state(n - 1 - (i * su + u), 1)
        if done:
            for u in range(su):
                finish(i * su + u)
                finish(n - 1 - (i * su + u))
        return carry

    trips = n // su
    if trips % 2 == 0:
        lax.fori_loop(0, trips // 2, functools.partial(state_body, False), 0)
        lax.fori_loop(trips // 2, trips, functools.partial(state_body, True), 0)
    else:
        lax.fori_loop(0, trips, functools.partial(state_body, False), 0)
        lax.fori_loop(0, n, lambda c, carry: (finish(c), carry)[1], 0)
    sfin_ref[0] = st_ref[...]


def gla_mixer(zg, la, s0, norm_g, l, n_seq, row0):
    qk = la.shape[1] // 2
    wv = (zg.shape[1] - 2 * qk) // 2
    dk, dv = qk // GLA_HEADS, wv // GLA_HEADS
    sb0 = row0 // l
    return pl.pallas_call(
        functools.partial(_gla_kernel, l, qk, wv, dk, dv),
        grid=(n_seq,),
        in_specs=[pl.BlockSpec((l, zg.shape[1]), lambda s: (sb0 + s, 0)),
                  pl.BlockSpec((l, la.shape[1]), lambda s: (sb0 + s, 0)),
                  pl.BlockSpec((1, 2, wv, qk), lambda s: (s, 0, 0, 0)),
                  pl.BlockSpec((1, dv), lambda s: (0, 0))],
        out_specs=[pl.BlockSpec((l, wv), lambda s: (s, 0)),
                   pl.BlockSpec((1, 2, wv, qk), lambda s: (s, 0, 0, 0))],
        out_shape=[jax.ShapeDtypeStruct((n_seq * l, wv), BF16),
                   jax.ShapeDtypeStruct((n_seq, 2, wv, qk), F32)],
        scratch_shapes=[pltpu.VMEM((l, wv), F32), pltpu.VMEM((l, wv), F32),
                        pltpu.VMEM((2, l, qk), BF16), pltpu.VMEM((2, l, qk), BF16),
                        pltpu.VMEM((2, l // GLA_CHUNK * 8, qk), F32),
                        pltpu.VMEM((2, wv, qk), F32)],
        compiler_params=_cparams(("arbitrary",), VMEM_LIMIT),
        name="gla_mixer",
    )(zg, la, s0, norm_g.reshape(1, dv))


def _outproj_kernel(n_lat_tiles, yh_ref, yg_ref, yhc_ref, ygc_ref, wh_ref, wg_ref, x_ref, g1_ref,
                    g_ref, sh_ref, sc_ref, xo_ref, u_ref):
    i = pl.program_id(0)

    def finish(yh, yg):
        delta = _dot(yh[...], wh_ref[...]) + _dot(yg[...], wg_ref[...])
        x = x_ref[...] + g1_ref[0] * delta
        xo_ref[...] = x
        u_ref[...] = _rms(x, g_ref[...]) * (1.0 + sc_ref[0]) + sh_ref[0]

    @pl.when(i < n_lat_tiles)
    def _():
        finish(yh_ref, yg_ref)

    @pl.when(i >= n_lat_tiles)
    def _():
        finish(yhc_ref, ygc_ref)


def out_proj(y_lat, y_ctx, w_hy, w_gla, xs, mod, norm_g, tm, seq, n_lat, b, n_tok):
    d = xs.shape[1]
    nlt = n_lat // tm
    row = functools.partial(_mod_row, tm=tm, seq=seq, n_lat_tiles=nlt, b=b)
    full = lambda a: pl.BlockSpec(a.shape, lambda i: (0,) * a.ndim)
    modspec = lambda col: pl.BlockSpec((1, 1, d), lambda i: (row(i), 0, col))
    tile = lambda w: pl.BlockSpec((tm, w), lambda i: (i, 0))
    lat = lambda a: pl.BlockSpec((tm, a.shape[1]), lambda i: (jnp.minimum(i, nlt - 1), 0))
    ctx = lambda a: pl.BlockSpec((tm, a.shape[1]), lambda i: (jnp.maximum(i - nlt, 0), 0))
    return pl.pallas_call(
        functools.partial(_outproj_kernel, nlt),
        grid=(n_tok // tm,),
        in_specs=[lat(y_lat[0]), lat(y_lat[1]), ctx(y_ctx[0]), ctx(y_ctx[1]),
                  full(w_hy), full(w_gla), tile(d),
                  modspec(2), full(norm_g), modspec(3), modspec(4)],
        out_specs=[tile(d), tile(d)],
        out_shape=[jax.ShapeDtypeStruct((n_tok, d), F32)] * 2,
        compiler_params=_cparams(("arbitrary",), VMEM_LIMIT),
        name="out_proj",
    )(*y_lat, *y_ctx, w_hy, w_gla, xs, mod, norm_g, mod, mod)


def _router_kernel(n_exp, tm, u_ref, wh_ref, wl_ref, b_ref, tri_ref,
                   idx_ref, gate_ref, rank_ref, cnt_ref):
    i = pl.program_id(0)

    @pl.when(i == 0)
    def _():
        cnt_ref[...] = jnp.zeros_like(cnt_ref)

    uh, ul = _split(u_ref[...])
    nt = (((1,), (1,)), ((), ()))
    logits = (lax.dot_general(wh_ref[...], uh, nt, preferred_element_type=F32)
              + lax.dot_general(wh_ref[...], ul, nt, preferred_element_type=F32)
              + lax.dot_general(wl_ref[...], uh, nt, preferred_element_type=F32))
    s = 1.0 / (1.0 + jnp.exp(-logits))
    sel = s + b_ref[...]
    gsz = n_exp // N_GROUPS
    neg = -jnp.inf
    rows_g = lax.broadcasted_iota(jnp.int32, (gsz, tm), 0)
    blocks, gscore = [], []
    for g in range(N_GROUPS):
        blk = sel[g * gsz:(g + 1) * gsz, :]
        m1 = jnp.max(blk, axis=0, keepdims=True)
        i1 = jnp.min(jnp.where(blk == m1, rows_g, gsz), axis=0, keepdims=True)
        m2 = jnp.max(jnp.where(rows_g == i1, neg, blk), axis=0, keepdims=True)
        blocks.append(blk)
        gscore.append(m1 + m2)
    masked = []
    for g in range(N_GROUPS):
        beaten = jnp.zeros((1, tm), jnp.int32)
        for h in range(N_GROUPS):
            if h == g:
                continue
            wins = (gscore[h] > gscore[g]) | ((gscore[h] == gscore[g]) & (h < g))
            beaten = beaten + wins.astype(jnp.int32)
        masked.append(jnp.where(beaten < TOPK_GROUPS, blocks[g], neg))
    sel = jnp.concatenate(masked, axis=0)

    rows = lax.broadcasted_iota(jnp.int32, (n_exp, tm), 0)
    chosen = jnp.zeros((n_exp, tm), F32)
    idxs, gates = [], []
    for _ in range(TOP_K):
        m = jnp.max(sel, axis=0, keepdims=True)
        ik = jnp.min(jnp.where(sel == m, rows, n_exp), axis=0, keepdims=True)
        hit = rows == ik
        gates.append(jnp.sum(jnp.where(hit, s, 0.0), axis=0, keepdims=True))
        sel = jnp.where(hit, neg, sel)
        chosen = jnp.where(hit, 1.0, chosen)
        idxs.append(ik)
    gsum = gates[0]
    for g in gates[1:]:
        gsum = gsum + g
    inv = ROUTED_SCALE / gsum
    before = _dot(chosen.astype(BF16), tri_ref[...]) + cnt_ref[...]
    for k in range(TOP_K):
        idx_ref[k:k + 1, :] = idxs[k]
        gate_ref[k:k + 1, :] = gates[k] * inv
        rk = jnp.sum(jnp.where(rows == idxs[k], before, 0.0), axis=0, keepdims=True)
        rank_ref[k:k + 1, :] = rk.astype(jnp.int32)
    cnt_ref[...] += jnp.sum(chosen, axis=1, keepdims=True)


def moe_router(u, wr_hi, wr_lo, b_r, n_tok, tm):
    n_exp, d = wr_hi.shape
    tri = (lax.broadcasted_iota(jnp.int32, (tm, tm), 0)
           < lax.broadcasted_iota(jnp.int32, (tm, tm), 1)).astype(BF16)
    full = lambda a: pl.BlockSpec(a.shape, lambda i: (0,) * a.ndim)
    tok = pl.BlockSpec((TOP_K, tm), lambda i: (0, i))
    return pl.pallas_call(
        functools.partial(_router_kernel, n_exp, tm),
        grid=(n_tok // tm,),
        in_specs=[pl.BlockSpec((tm, d), lambda i: (i, 0)), full(wr_hi), full(wr_lo),
                  pl.BlockSpec((n_exp, 1), lambda i: (0, 0)), full(tri)],
        out_specs=[tok, tok, tok, pl.BlockSpec((n_exp, 1), lambda i: (0, 0))],
        out_shape=[jax.ShapeDtypeStruct((TOP_K, n_tok), jnp.int32),
                   jax.ShapeDtypeStruct((TOP_K, n_tok), F32),
                   jax.ShapeDtypeStruct((TOP_K, n_tok), jnp.int32),
                   jax.ShapeDtypeStruct((n_exp, 1), F32)],
        compiler_params=_cparams(("arbitrary",), VMEM_LIMIT),
        name="moe_router",
    )(u, wr_hi, wr_lo, b_r.reshape(n_exp, 1), tri)


def _start_row_copy(src, src_row, dst, dst_row, sem, queue):
    pltpu.async_copy(src.at[src_row], dst.at[dst_row], sem, priority=queue)


def _store_rows_tiled(ref, x):
    for j in range(ref.shape[-2]):
        ref[:, j, :] = x[:, j * LANES:(j + 1) * LANES]


def _load_rows_tiled(ref):
    return jnp.concatenate([ref[:, j, :] for j in range(ref.shape[-2])], axis=1)


def _pack_bf16_pairs(x):
    w = x.shape[1] // 2
    lo = pltpu.bitcast(x[:, :w].astype(BF16).astype(F32), jnp.uint32) >> 16
    hi = pltpu.bitcast(x[:, w:].astype(BF16).astype(F32), jnp.uint32)
    return hi | lo


def _unpack_bf16_pairs(p):
    lo = pltpu.bitcast(p << 16, F32)
    hi = pltpu.bitcast(p & jnp.uint32(0xFFFF0000), F32)
    return jnp.concatenate([lo, hi], axis=1).astype(BF16)


def _dest_kernel(n_exp, tm, idx_ref, rank_ref, ps_ref, dest_ref):
    rows = lax.broadcasted_iota(jnp.int32, (n_exp, tm), 0)
    ps = ps_ref[...]
    for k in range(TOP_K):
        base = jnp.sum(jnp.where(rows == idx_ref[k:k + 1, :], ps, 0), axis=0, keepdims=True)
        dest_ref[k:k + 1, :] = base + rank_ref[k:k + 1, :]


def moe_dest(idx, rank, pad_start, tm):
    n_tok = idx.shape[1]
    n_exp = pad_start.shape[0]
    tok = pl.BlockSpec((TOP_K, tm), lambda i: (0, i))
    return pl.pallas_call(
        functools.partial(_dest_kernel, n_exp, tm),
        grid=(n_tok // tm,),
        in_specs=[tok, tok, pl.BlockSpec((n_exp, 1), lambda i: (0, 0))],
        out_specs=tok,
        out_shape=jax.ShapeDtypeStruct((TOP_K, n_tok), jnp.int32),
        compiler_params=_cparams(("arbitrary",)),
        name="moe_dest",
    )(idx, rank, pad_start.reshape(n_exp, 1))


def _wait_rows(src, dst, sem, tm):
    for _ in range(TOP_K):
        pltpu.make_async_copy(src.at[pl.ds(0, tm)], dst.at[pl.ds(0, tm)], sem).wait()


ZERO_ROWS = MOE_ROWS // 2


def _zero_padding(n_exp, ps_ref, pe_ref, xs_ref, z_ref, sem, wait):
    def go(copy):
        copy.wait() if wait else copy.start()

    def per_expert(e, carry):
        s = ps_ref[e]
        m = pe_ref[e] - s
        bit = ZERO_ROWS
        while bit:
            start = s + (m & ~(2 * bit - 1))

            @pl.when((m & bit) != 0)
            def _():
                go(pltpu.make_async_copy(z_ref.at[pl.ds(0, bit)], xs_ref.at[pl.ds(start, bit)], sem))
            bit //= 2
        return carry

    lax.fori_loop(0, n_exp, per_expert, 0)

    tail = pe_ref[n_exp - 1]

    def per_chunk(j, carry):
        go(pltpu.make_async_copy(z_ref, xs_ref.at[pl.ds(tail + j * ZERO_ROWS, ZERO_ROWS)], sem))
        return carry

    lax.fori_loop(0, (xs_ref.shape[0] - tail) // ZERO_ROWS, per_chunk, 0)


def _dispatch_kernel(tm, n_steps, n_exp, ps_ref, pe_ref, dest_ref, u_ref, xs_ref, pk_ref, z_ref,
                     sems, zsem):
    i = pl.program_id(0)
    slot = i % 2
    pk = pk_ref.at[slot]
    sem = sems.at[slot]

    @pl.when(i == 0)
    def _():
        z_ref[...] = jnp.zeros_like(z_ref)
        _zero_padding(n_exp, ps_ref, pe_ref, xs_ref, z_ref, zsem, wait=False)

    @pl.when(i >= 2)
    def _():
        _wait_rows(pk, xs_ref, sem, tm)

    _store_rows_tiled(pk, _pack_bf16_pairs(u_ref[...]))

    def issue(t, carry):
        for k in range(TOP_K):
            _start_row_copy(pk, t, xs_ref, dest_ref[0, 0, k * tm + t], sem, k % 2)
        return carry

    lax.fori_loop(0, tm, issue, 0)

    @pl.when(i == n_steps - 1)
    def _():
        if n_steps > 1:
            _wait_rows(pk_ref.at[1 - slot], xs_ref, sems.at[1 - slot], tm)
        _wait_rows(pk, xs_ref, sem, tm)
        _zero_padding(n_exp, ps_ref, pe_ref, xs_ref, z_ref, zsem, wait=True)


def moe_dispatch(u, dest, pad_from, pad_end, n_tok, n_slots, tm):
    d = u.shape[1]
    sub = d // 2 // LANES
    n_exp = pad_end.shape[0]
    grid_spec = pltpu.PrefetchScalarGridSpec(
        num_scalar_prefetch=2,
        grid=(n_tok // tm,),
        in_specs=[pl.BlockSpec((1, 1, TOP_K * tm), lambda i, *_: (i, 0, 0), memory_space=pltpu.SMEM),
                  pl.BlockSpec((tm, d), lambda i, *_: (i, 0))],
        out_specs=pl.BlockSpec(memory_space=pl.ANY),
        scratch_shapes=[pltpu.VMEM((2, tm, sub, LANES), jnp.uint32),
                        pltpu.VMEM((ZERO_ROWS, sub, LANES), jnp.uint32),
                        pltpu.SemaphoreType.DMA((2,)), pltpu.SemaphoreType.DMA])
    return pl.pallas_call(
        functools.partial(_dispatch_kernel, tm, n_tok // tm, n_exp),
        grid_spec=grid_spec,
        out_shape=jax.ShapeDtypeStruct((n_slots, sub, LANES), jnp.uint32),
        compiler_params=_cparams(("arbitrary",)),
        name="moe_dispatch",
    )(pad_from, pad_end, dest, u)


def _gmm_kernel(layer, be_ref, nx_ref, fl_ref, nu_ref, x_ref, w1_hbm, w3_hbm, w2_hbm, y_ref,
                w1f, w3f, w2f, w1b, w3b, w2b, sems):
    i = pl.program_id(0)
    fl = fl_ref[i]
    sub = x_ref.shape[0] // MOE_ROWS

    def fetch(e, s):
        return (pltpu.make_async_copy(w1_hbm.at[layer, e], w1f.at[s], sems.at[s]),
                pltpu.make_async_copy(w3_hbm.at[layer, e], w3f.at[s], sems.at[s]),
                pltpu.make_async_copy(w2_hbm.at[layer, e], w2f.at[s], sems.at[s]))

    @pl.when(i < nu_ref[0])
    def _():
        @pl.when((fl & 1) != 0)
        def _():
            s = (fl >> 1) & 1

            @pl.when(i == 0)
            def _():
                for c in fetch(be_ref[i], s):
                    c.start()

            for c in fetch(be_ref[i], s):
                c.wait()

            @pl.when((fl & 4) != 0)
            def _():
                for c in fetch(nx_ref[i], 1 - s):
                    c.start()

            w1b[...] = w1f[s].astype(BF16)
            w3b[...] = w3f[s].astype(BF16)
            w2b[...] = w2f[s].astype(BF16)

        x = _unpack_bf16_pairs(jnp.concatenate(
            [x_ref[pl.ds(j, MOE_ROWS, stride=sub), :] for j in range(sub)], axis=1))
        h = _silu(_dot(x, w1b[...])) * _dot(x, w3b[...])
        y = _pack_bf16_pairs(_dot(h.astype(BF16), w2b[...]))
        for j in range(sub):
            y_ref[pl.ds(j, MOE_ROWS, stride=sub), :] = y[:, j * LANES:(j + 1) * LANES]

    @pl.when(i >= nu_ref[0])
    def _():
        y_ref[...] = jnp.zeros_like(y_ref)


def moe_experts(x_sorted, block_e, n_used, w1, w3, w2, layer):
    n_blocks = block_e.shape[0]
    _, _, d, f = w1.shape
    n_slots, sub, _ = x_sorted.shape
    idx = jnp.arange(n_blocks, dtype=jnp.int32)
    first = (idx == 0) | (block_e != jnp.roll(block_e, 1))
    seg = jnp.cumsum(first.astype(jnp.int32)) - 1
    first_at_or_after = lax.cummin(jnp.where(first, idx, n_blocks)[::-1])[::-1]
    nxt_idx = jnp.concatenate([first_at_or_after[1:], jnp.full((1,), n_blocks, jnp.int32)])
    has_next = nxt_idx < n_used[0]
    nxt_e = block_e[jnp.minimum(nxt_idx, n_blocks - 1)]
    flags = (first.astype(jnp.int32) | ((seg & 1) << 1) | (has_next.astype(jnp.int32) << 2))
    grid_spec = pltpu.PrefetchScalarGridSpec(
        num_scalar_prefetch=4,
        grid=(n_blocks,),
        in_specs=[pl.BlockSpec((MOE_ROWS * sub, LANES), lambda i, *_: (i, 0)),
                  pl.BlockSpec(memory_space=pl.ANY),
                  pl.BlockSpec(memory_space=pl.ANY),
                  pl.BlockSpec(memory_space=pl.ANY)],
        out_specs=pl.BlockSpec((MOE_ROWS * sub, LANES), lambda i, *_: (i, 0)),
        scratch_shapes=[pltpu.VMEM((2, d, f), F32), pltpu.VMEM((2, d, f), F32),
                        pltpu.VMEM((2, f, d), F32),
                        pltpu.VMEM((d, f), BF16), pltpu.VMEM((d, f), BF16),
                        pltpu.VMEM((f, d), BF16), pltpu.SemaphoreType.DMA((2,))])
    y = pl.pallas_call(
        functools.partial(_gmm_kernel, layer),
        grid_spec=grid_spec,
        out_shape=jax.ShapeDtypeStruct((n_slots * sub, LANES), jnp.uint32),
        compiler_params=_cparams(("arbitrary",), VMEM_LIMIT),
        name="moe_experts",
    )(block_e, nxt_e, flags, n_used, x_sorted.reshape(n_slots * sub, LANES), w1, w3, w2)
    return y.reshape(n_slots, sub, LANES)


def _combine_kernel(tm, n_steps, final, dest_ref, nxt_ref, y_ref, gate_ref, u_ref, ws1_ref, ws3_ref,
                    ws2_ref, x_ref, g2_ref, fg_ref, o_ref, buf_ref, routed_ref, sems):
    i = pl.program_id(0)
    slot = i % 2
    grp = SUBLANES

    def gather(d_ref, s, g):
        for tt in range(grp):
            t = g * grp + tt
            for k in range(TOP_K):
                _start_row_copy(y_ref, d_ref[0, 0, k * tm + t], buf_ref.at[s, k], t, sems.at[s], k % 2)

    def reduce(g, s):
        r0 = pl.multiple_of(g * grp, grp)
        gt = gate_ref[pl.ds(r0, grp), :]
        lo = hi = None
        for k in range(TOP_K):
            p = buf_ref[s, k, pl.ds(r0, grp)]
            gk = gt[:, k:k + 1][:, :, None]
            lo_k = pltpu.bitcast(p << 16, F32) * gk
            hi_k = pltpu.bitcast(p & jnp.uint32(0xFFFF0000), F32) * gk
            lo = lo_k if lo is None else lo + lo_k
            hi = hi_k if hi is None else hi + hi_k
        sub = lo.shape[1]
        routed_ref[pl.ds(r0, grp), 0:sub, :] = lo
        routed_ref[pl.ds(r0, grp), sub:2 * sub, :] = hi

    @pl.when(i == 0)
    def _():
        lax.fori_loop(0, tm // grp, lambda g, c: (gather(dest_ref, slot, g), c)[1], 0)

    _wait_rows(y_ref, buf_ref.at[slot, 0], sems.at[slot], tm)

    @pl.when(i + 1 < n_steps)
    def _():
        def both(g, c):
            gather(nxt_ref, 1 - slot, g)
            reduce(g, slot)
            return c
        lax.fori_loop(0, tm // grp, both, 0)

    @pl.when(i + 1 >= n_steps)
    def _():
        lax.fori_loop(0, tm // grp, lambda g, c: (reduce(g, slot), c)[1], 0)

    ub = u_ref[...].astype(BF16)
    hs = _silu(_dot(ub, ws1_ref[...])) * _dot(ub, ws3_ref[...])
    shared = _dot(hs.astype(BF16), ws2_ref[...])
    x = x_ref[...] + g2_ref[0] * (_load_rows_tiled(routed_ref) + shared)
    o_ref[...] = _rms(x, fg_ref[...]) if final else x


def moe_combine(y_sorted, dest, gate_rep, u, ws1, ws3, ws2, xs, mod, final_g, final,
                tm, seq, n_lat, b, n_tok):
    d = u.shape[1]
    nlt = n_lat // tm
    row = functools.partial(_mod_row, tm=tm, seq=seq, n_lat_tiles=nlt, b=b)
    full = lambda a: pl.BlockSpec(a.shape, lambda i: (0,) * a.ndim)
    tile = pl.BlockSpec((tm, d), lambda i: (i, 0))
    n_steps = n_tok // tm
    return pl.pallas_call(
        functools.partial(_combine_kernel, tm, n_steps, final),
        grid=(n_steps,),
        in_specs=[pl.BlockSpec((1, 1, TOP_K * tm), lambda i: (i, 0, 0), memory_space=pltpu.SMEM),
                  pl.BlockSpec((1, 1, TOP_K * tm),
                               lambda i: (jnp.minimum(i + 1, n_steps - 1), 0, 0),
                               memory_space=pltpu.SMEM),
                  pl.BlockSpec(memory_space=pl.ANY),
                  pl.BlockSpec((tm, TOP_K), lambda i: (i, 0)),
                  tile, full(ws1), full(ws3), full(ws2), tile,
                  pl.BlockSpec((1, 1, d), lambda i: (row(i), 0, 5)),
                  full(final_g)],
        out_specs=tile,
        out_shape=jax.ShapeDtypeStruct((n_tok, d), F32),
        scratch_shapes=[pltpu.VMEM((2, TOP_K, tm) + y_sorted.shape[1:], jnp.uint32),
                        pltpu.VMEM((tm, d // LANES, LANES), F32), pltpu.SemaphoreType.DMA((2,))],
        compiler_params=_cparams(("arbitrary",), VMEM_LIMIT),
        name="moe_combine",
    )(dest, dest, y_sorted, gate_rep, u, ws1, ws3, ws2, xs, mod, final_g)


def moe_layer(u, xs, mod, router_w, router_b, w1, w3, w2, layer, ws1, ws3, ws2, final_g, final,
              tm, seq, n_lat, b, n_tok):
    n_exp = router_w.shape[1]
    wr_hi, wr_lo = _split(router_w.T)
    idx, gate, rank, cnt = moe_router(u, wr_hi, wr_lo, router_b, n_tok, tm)
    counts = cnt[:, 0].astype(jnp.int32)
    padded = (counts + MOE_ROWS - 1) // MOE_ROWS * MOE_ROWS
    pad_end = jnp.cumsum(padded)
    pad_start = pad_end - padded
    dest = moe_dest(idx, rank, pad_start, math.gcd(n_tok, DEST_TILE))
    n_blocks = -(-(n_tok * TOP_K + n_exp * (MOE_ROWS - 1)) // MOE_ROWS)
    block_start = jnp.arange(n_blocks, dtype=jnp.int32) * MOE_ROWS
    block_e = jnp.minimum(jnp.sum((pad_end[None, :] <= block_start[:, None]).astype(jnp.int32), axis=1),
                          n_exp - 1)
    n_used = (pad_end[-1:] // MOE_ROWS).astype(jnp.int32)
    dest = dest.reshape(TOP_K, n_tok // tm, tm).transpose(1, 0, 2).reshape(n_tok // tm, 1, TOP_K * tm)
    x_sorted = moe_dispatch(u, dest, pad_start + counts, pad_end, n_tok, n_blocks * MOE_ROWS, tm)
    y_sorted = moe_experts(x_sorted, block_e, n_used, w1, w3, w2, layer)
    return moe_combine(y_sorted, dest, gate.T, u, ws1.astype(BF16), ws3.astype(BF16),
                       ws2.astype(BF16), xs, mod, final_g.reshape(1, -1), final,
                       tm, seq, n_lat, b, n_tok)


def _grid_pos_emb(rows, d):
    r, col = np.meshgrid(np.arange(rows, dtype=np.float64), np.arange(GRID_W, dtype=np.float64),
                         indexing='ij')
    quarter = d // 4
    omega = 1.0 / (10000.0 ** (np.arange(quarter, dtype=np.float64) / quarter))

    def emb(p):
        a = p.reshape(-1)[:, None] * omega[None, :]
        return np.concatenate([np.sin(a), np.cos(a)], axis=-1)

    return jnp.asarray(np.concatenate([emb(r), emb(col)], axis=-1).astype(np.float32))


def kernel(x, c, ctx, c_ctx, ada_w, ada_b, norm1_g, norm2_g, w_in, w_out, hy_conv_w, hy_conv_b,
           hy_f_w1, hy_f_b1, hy_f_w2, hy_f_b2, hy_f_w3, hy_skip, gla_wa2, gla_ba2, gla_norm_g,
           router_w, router_b, exp_w1, exp_w3, exp_w2, sh_w1, sh_w3, sh_w2, final_g):
    b, l, d = x.shape
    lc = ctx.shape[1]
    depth = ada_w.shape[0]
    n_lat, n_ctx = b * l, b * lc
    n_all = n_lat + n_ctx
    hy_width = hy_skip.shape[-1]
    hy_cols = hy_conv_w.shape[-1]
    qk = gla_wa2.shape[-1]
    wv = d - hy_width
    g_cols = 2 * qk + 2 * wv
    tm = math.gcd(math.gcd(l, lc), TOKEN_TILE)
    tm_proj = math.gcd(math.gcd(l, n_ctx), PROJ_TILE)
    cb = min(hy_width, MXU_DIM)

    cond = jnp.concatenate([c, c_ctx[None, :]], axis=0)
    n_rows = -(-(b + 1) // SUBLANES) * SUBLANES
    cond = jnp.concatenate([cond, jnp.zeros((n_rows - b - 1, d), F32)], axis=0)
    mods = ada_table(cond, ada_w, ada_b)

    xs = (x.reshape(n_lat, d), _grid_pos_emb(l // GRID_W, d), ctx.reshape(n_ctx, d))

    tabs = {}
    for length in (l, lc):
        if length not in tabs:
            tabs[length] = _trig_tables(length, True) + _trig_tables(length, False)
    zero_state = jnp.zeros((b, 2, wv, qk), F32)

    for layer in range(depth):
        last = layer == depth - 1
        mod = mods[layer][:, None, :]
        g1 = norm1_g[layer].reshape(1, d)
        g2 = norm2_g[layer].reshape(1, d)
        w_l = w_in[layer].astype(BF16)
        wa2 = gla_wa2[layer]
        rank = wa2.shape[1]
        wa2_bd = jnp.zeros((2 * rank, 2 * qk), F32)
        wa2_bd = wa2_bd.at[:rank, :qk].set(wa2[0]).at[rank:, qk:].set(wa2[1])
        ba2_cat = gla_ba2[layer].reshape(1, 2 * qk)
        outs = in_proj(xs, g1, mod, w_l[:, :hy_cols], w_l[:, hy_cols:hy_cols + g_cols],
                       w_l[:, hy_cols + g_cols:], wa2_bd, ba2_cat, tm_proj, l, n_lat, b)
        zh, zg, la = outs[:3]
        if layer == 0:
            xs = outs[3]

        filt = (hy_f_w1[layer], hy_f_b1[layer], hy_f_w2[layer], hy_f_b2[layer], hy_f_w3[layer],
                hy_skip[layer])
        yc_gla, s_ctx = gla_mixer(zg, la, zero_state, gla_norm_g[layer], lc, b, n_lat)
        y_gla, _ = gla_mixer(zg, la, s_ctx, gla_norm_g[layer], l, b, 0)
        ct, st, c0, s0 = tabs[l]
        gr, gi = hyena_spectra(l, *filt, c0, s0)
        y_hy = hyena_mixer(zh, hy_conv_w[layer], hy_conv_b[layer], ct, st, gr, gi, l, b, 0, cb)
        if last:
            n_tok = n_lat
            y_ctx = (y_hy, y_gla)
        else:
            n_tok = n_all
            ct, st, c0, s0 = tabs[lc]
            gr, gi = hyena_spectra(lc, *filt, c0, s0)
            yc_hy = hyena_mixer(zh, hy_conv_w[layer], hy_conv_b[layer], ct, st, gr, gi,
                                lc, b, n_lat, cb)
            y_ctx = (yc_hy, yc_gla)
        w_o = w_out[layer].astype(BF16)
        xs_mid, u = out_proj((y_hy, y_gla), y_ctx, w_o[:hy_width], w_o[hy_width:], xs, mod, g2,
                             tm, l, n_lat, b, n_tok)
        xs = moe_layer(u, xs_mid, mod, router_w[layer], router_b[layer], exp_w1, exp_w3, exp_w2,
                       layer, sh_w1[layer], sh_w3[layer], sh_w2[layer],
                       final_g, last, tm, l, n_lat, b, n_tok)
    return xs[:n_lat].reshape(b, l, d)
```

```python
import functools
import math

import numpy as np
import jax
import jax.numpy as jnp
from jax import lax
from jax.experimental import pallas as pl
from jax.experimental.pallas import tpu as pltpu

GRID_W = 64
EPS = 1e-6

HY_EMB = 33
HY_EMB_PAD = 40
HY_FREQ = 1.0
HY_TARGET = 1e-2
HY_FAST_PCT = 0.3
HY_SLOW_PCT = 1.5
HY_MIN_DECAY = math.log(1.0 / HY_TARGET) / HY_SLOW_PCT
HY_MAX_DECAY = math.log(1.0 / HY_TARGET) / HY_FAST_PCT

GLA_HEADS = 4
GLA_TAU = 16.0
GLA_CHUNK = 64
GLA_GROUP = 4
GLA_STATE_UNROLL = 4

TOP_K = 8
N_GROUPS = 8
TOPK_GROUPS = 4
ROUTED_SCALE = 2.5

SUBLANES = 8
LANES = 128
MXU_DIM = 256
VMEM_LIMIT = 56 * 1024 * 1024

TOKEN_TILE = 256
PROJ_TILE = 512
DEST_TILE = 1024
ADA_COLS = 1024
SPECTRA_ROWS = 256
HY_FREQ_CHUNK = 1024
MOE_ROWS = 512

F32 = jnp.float32
BF16 = jnp.bfloat16


def _cparams(sem, vmem=None):
    return pltpu.CompilerParams(dimension_semantics=sem, vmem_limit_bytes=vmem)


def _split(a):
    hi = a.astype(BF16)
    lo = (a - hi.astype(F32)).astype(BF16)
    return hi, lo


def _dot(a, b):
    return jnp.dot(a, b, preferred_element_type=F32)


def _dot3(a, b):
    ah, al = _split(a)
    bh, bl = _split(b)
    return _dot(ah, bh) + _dot(ah, bl) + _dot(al, bh)


def _silu(x):
    return x * (1.0 / (1.0 + jnp.exp(-x)))


def _rms(x, g):
    return x * lax.rsqrt(jnp.mean(x * x, axis=-1, keepdims=True) + EPS) * g


def _ada_kernel(c_ref, w_ref, b_ref, o_ref):
    o_ref[0] = _dot3(_silu(c_ref[...]), w_ref[0]) + b_ref[0]


def ada_table(cond_rows, ada_w, ada_b):
    depth, d, six_d = ada_w.shape
    r = cond_rows.shape[0]
    tn = ADA_COLS
    return pl.pallas_call(
        _ada_kernel,
        grid=(depth, six_d // tn),
        in_specs=[pl.BlockSpec((r, d), lambda l, j: (0, 0)),
                  pl.BlockSpec((1, d, tn), lambda l, j: (l, 0, j)),
                  pl.BlockSpec((1, 1, tn), lambda l, j: (l, 0, j))],
        out_specs=pl.BlockSpec((1, r, tn), lambda l, j: (l, 0, j)),
        out_shape=jax.ShapeDtypeStruct((depth, r, six_d), F32),
        compiler_params=_cparams(("arbitrary", "arbitrary")),
        name="ada_table",
    )(cond_rows, ada_w, ada_b.reshape(depth, 1, six_d))


def _mod_row(i, tm, seq, n_lat_tiles, b):
    return jnp.where(i < n_lat_tiles, (i * tm) // seq, b)


def _inproj_embed_kernel(n_lat_tiles, x_ref, p_ref, c_ref, *rest):
    xs_ref = rest[-1]
    i = pl.program_id(0)

    @pl.when(i < n_lat_tiles)
    def _():
        xs_ref[...] = x_ref[...] + p_ref[...]

    @pl.when(i >= n_lat_tiles)
    def _():
        xs_ref[...] = c_ref[...]

    _inproj_kernel(xs_ref, *rest[:-1])


def _inproj_kernel(x_ref, g_ref, sh_ref, sc_ref, wh_ref, wg_ref, wa_ref, wa2_ref, ba2_ref,
                   zh_ref, zg_ref, la_ref):
    x = x_ref[...]
    h = _rms(x, g_ref[...]) * (1.0 + sc_ref[0]) + sh_ref[0]
    hb = h.astype(BF16)
    zh_ref[...] = _dot(hb, wh_ref[...]).astype(BF16)
    zg_ref[...] = _dot(hb, wg_ref[...]).astype(BF16)
    za = _dot(hb, wa_ref[...])
    t = _dot3(za, wa2_ref[...]) + ba2_ref[...]
    la_ref[...] = (jnp.minimum(t, 0.0) - jnp.log(1.0 + jnp.exp(-jnp.abs(t)))) * (1.0 / GLA_TAU)


def in_proj(tokens, norm_g, mod, w_hy, w_g, w_a, wa2_bd, ba2_cat, tm, seq, n_lat, b):
    embed = isinstance(tokens, tuple)
    nlt = n_lat // tm
    if embed:
        x2, pos, ctx2 = tokens
        d = x2.shape[1]
        n = x2.shape[0] + ctx2.shape[0]
        lt = pos.shape[0] // tm
        tok_specs = [pl.BlockSpec((tm, d), lambda i: (jnp.minimum(i, nlt - 1), 0)),
                     pl.BlockSpec((tm, d), lambda i: (i % lt, 0)),
                     pl.BlockSpec((tm, d), lambda i: (jnp.maximum(i - nlt, 0), 0))]
        body = functools.partial(_inproj_embed_kernel, nlt)
    else:
        n, d = tokens.shape
        tokens = (tokens,)
        tok_specs = [pl.BlockSpec((tm, d), lambda i: (i, 0))]
        body = _inproj_kernel
    row = functools.partial(_mod_row, tm=tm, seq=seq, n_lat_tiles=nlt, b=b)
    cols = (w_hy.shape[1], w_g.shape[1], wa2_bd.shape[1]) + ((d,) if embed else ())
    dtypes = (BF16, BF16, F32) + ((F32,) if embed else ())
    full = lambda a: pl.BlockSpec(a.shape, lambda i: (0,) * a.ndim)
    return pl.pallas_call(
        body,
        grid=(n // tm,),
        in_specs=tok_specs + [full(norm_g),
                              pl.BlockSpec((1, 1, d), lambda i: (row(i), 0, 0)),
                              pl.BlockSpec((1, 1, d), lambda i: (row(i), 0, 1)),
                              full(w_hy), full(w_g), full(w_a), full(wa2_bd), full(ba2_cat)],
        out_specs=[pl.BlockSpec((tm, c), lambda i: (i, 0)) for c in cols],
        out_shape=[jax.ShapeDtypeStruct((n, c), t) for c, t in zip(cols, dtypes)],
        compiler_params=_cparams(("arbitrary",), VMEM_LIMIT),
        name="in_proj",
    )(*tokens, norm_g, mod, mod, w_hy, w_g, w_a, wa2_bd, ba2_cat)


def _trig_tables(l, half_shift):
    k = np.arange(l, dtype=np.int64)[:, None]
    n = np.arange(l, dtype=np.int64)[None, :]
    m = ((2 * k + 1) * (2 * n + (1 if half_shift else 0))) % (8 * l)
    ang = m.astype(np.float64) * (2.0 * math.pi / (8 * l))
    return (jnp.asarray(np.cos(ang).astype(np.float32).astype(BF16)),
            jnp.asarray(np.sin(ang).astype(np.float32).astype(BF16)))


def _filter_kernel(n_orders, width, l,
                   pos_ref, t_ref, dl_ref, w1_ref, b1_ref, w2_ref, b2_ref, w3_ref, skip_ref,
                   c0_ref, s0_ref, gr_ref, gi_ref, hs_ref, hd_ref):
    j = pl.program_id(0)

    @pl.when(j == 0)
    def _():
        hid = jnp.sin(HY_FREQ * (_dot3(pos_ref[...], w1_ref[...]) + b1_ref[...]))
        hid = jnp.sin(HY_FREQ * (_dot3(hid, w2_ref[...]) + b2_ref[...]))
        h = _dot3(hid, w3_ref[...])
        decay = jnp.exp(-t_ref[...] * dl_ref[...])
        for o in range(n_orders):
            hf = h[:, (2 * o) * width:(2 * o + 1) * width] * decay
            hb = h[:, (2 * o + 1) * width:(2 * o + 2) * width] * decay
            hs_ref[:, o * width:(o + 1) * width] = (hf + hb).astype(BF16)
            hd_ref[:, o * width:(o + 1) * width] = (hb - hf).astype(BF16)

    scale = 1.0 / l
    gr = _dot(c0_ref[...], hs_ref[...])
    gi = _dot(s0_ref[...], hd_ref[...])
    for o in range(n_orders):
        gr_ref[o] = (gr[:, o * width:(o + 1) * width] + skip_ref[o]) * scale
        gi_ref[o] = gi[:, o * width:(o + 1) * width] * scale


def hyena_spectra(l, w1, b1, w2, b2, w3, skip, c0, s0):
    width = skip.shape[-1]
    n_orders = skip.shape[0]
    t64 = np.linspace(0.0, 1.0, l)[:, None]
    bands = (HY_EMB - 1) // 2
    w = 2.0 * math.pi * np.arange(l, dtype=np.float64)[:, None] / l
    f = np.linspace(1e-4, bands - 1, bands)[None, :]
    pos = jnp.asarray(np.concatenate([t64, np.cos(f * w), -np.sin(f * w),
                                      np.zeros((l, HY_EMB_PAD - HY_EMB))], axis=-1).astype(np.float32))
    t = jnp.asarray(t64.astype(np.float32))
    w1p = jnp.concatenate([w1, jnp.zeros((HY_EMB_PAD - HY_EMB, w1.shape[1]), F32)], axis=0)
    deltas = jnp.asarray(np.linspace(HY_MIN_DECAY, HY_MAX_DECAY, width)[None, :].astype(np.float32))
    tk = min(l, SPECTRA_ROWS)
    full = lambda a: pl.BlockSpec(a.shape, lambda j: (0,) * a.ndim)
    skip3 = skip.reshape(n_orders, 1, width)
    args = (pos, t, deltas, w1p, b1.reshape(1, -1), w2, b2.reshape(1, -1), w3, skip3)
    return pl.pallas_call(
        functools.partial(_filter_kernel, n_orders, width, l),
        grid=(l // tk,),
        in_specs=[full(a) for a in args] + [pl.BlockSpec((tk, l), lambda j: (j, 0)),
                                            pl.BlockSpec((tk, l), lambda j: (j, 0))],
        out_specs=[pl.BlockSpec((n_orders, tk, width), lambda j: (0, j, 0)),
                   pl.BlockSpec((n_orders, tk, width), lambda j: (0, j, 0))],
        out_shape=[jax.ShapeDtypeStruct((n_orders, l, width), F32)] * 2,
        scratch_shapes=[pltpu.VMEM((l, n_orders * width), BF16),
                        pltpu.VMEM((l, n_orders * width), BF16)],
        compiler_params=_cparams(("arbitrary",), VMEM_LIMIT),
        name="hyena_spectra",
    )(*args, c0, s0)


def _hyena_kernel(l, fc, zv_ref, z1_ref, z2_ref, wv_ref, w1_ref, w2_ref, bv_ref, b1_ref, b2_ref,
                  c_ref, s_ref, gr_ref, gi_ref, y_ref, ub_ref, a_ref, b_ref, x1_ref, x2_ref):
    rows = lax.broadcasted_iota(jnp.int32, (l, 1), 0)

    def conv3(z_ref, w_ref, bias_ref):
        z = z_ref[...].astype(F32)
        zm = jnp.where(rows == 0, 0.0, pltpu.roll(z, 1, 0))
        zp = jnp.where(rows == l - 1, 0.0, pltpu.roll(z, l - 1, 0))
        return zm * w_ref[0:1, :] + z * w_ref[1:2, :] + zp * w_ref[2:3, :] + bias_ref[...]

    ub_ref[...] = conv3(zv_ref, wv_ref, bv_ref).astype(BF16)
    x1_ref[...] = conv3(z1_ref, w1_ref, b1_ref)
    x2_ref[...] = conv3(z2_ref, w2_ref, b2_ref)
    n = l // fc

    def forward(o):
        def body(c, carry):
            r0 = pl.multiple_of(c * fc, fc)
            ur = _dot(c_ref[pl.ds(r0, fc), :], ub_ref[...])
            us = _dot(s_ref[pl.ds(r0, fc), :], ub_ref[...])
            gr = gr_ref[o, pl.ds(r0, fc), :]
            gi = gi_ref[o, pl.ds(r0, fc), :]
            a_ref[pl.ds(r0, fc), :] = (ur * gr + us * gi).astype(BF16)
            b_ref[pl.ds(r0, fc), :] = (us * gr - ur * gi).astype(BF16)
            return carry
        lax.fori_loop(0, n, body, 0)

    def inverse(gate_ref, dst_ref):
        def body(c, carry):
            r0 = pl.multiple_of(c * fc, fc)
            lc = (_dot(c_ref[pl.ds(r0, fc), :], a_ref[...])
                  + _dot(s_ref[pl.ds(r0, fc), :], b_ref[...]))
            dst_ref[pl.ds(r0, fc), :] = (gate_ref[pl.ds(r0, fc), :] * lc).astype(BF16)
            return carry
        lax.fori_loop(0, n, body, 0)

    forward(0)
    inverse(x1_ref, ub_ref)
    forward(1)
    inverse(x2_ref, y_ref)


def hyena_mixer(zh, conv_w, conv_b, ctab, stab, gr, gi, l, n_seq, row0, cb):
    width = gr.shape[-1]
    n_orders = gr.shape[0]
    ncb = width // cb
    sb0 = row0 // l
    conv_b2 = conv_b.reshape(1, -1)
    zspec = lambda part: pl.BlockSpec((l, cb), lambda c, s: (sb0 + s, part * ncb + c))
    wspec = lambda part: pl.BlockSpec((3, cb), lambda c, s: (0, part * ncb + c))
    bspec = lambda part: pl.BlockSpec((1, cb), lambda c, s: (0, part * ncb + c))
    once = pl.Buffered(1)
    tspec = pl.BlockSpec((l, l), lambda c, s: (0, 0), pipeline_mode=once)
    gspec = pl.BlockSpec((n_orders, l, cb), lambda c, s: (0, 0, c), pipeline_mode=once)
    return pl.pallas_call(
        functools.partial(_hyena_kernel, l, min(l, HY_FREQ_CHUNK)),
        grid=(ncb, n_seq),
        in_specs=[zspec(0), zspec(1), zspec(2), wspec(0), wspec(1), wspec(2),
                  bspec(0), bspec(1), bspec(2), tspec, tspec, gspec, gspec],
        out_specs=pl.BlockSpec((l, cb), lambda c, s: (s, c)),
        out_shape=jax.ShapeDtypeStruct((n_seq * l, width), BF16),
        scratch_shapes=[pltpu.VMEM((l, cb), BF16), pltpu.VMEM((l, cb), BF16),
                        pltpu.VMEM((l, cb), BF16), pltpu.VMEM((l, cb), F32),
                        pltpu.VMEM((l, cb), F32)],
        compiler_params=_cparams(("arbitrary", "arbitrary"), VMEM_LIMIT),
        name="hyena_mixer",
    )(zh, zh, zh, conv_w, conv_w, conv_w, conv_b2, conv_b2, conv_b2, ctab, stab, gr, gi)


def _gla_kernel(l, qk, wv, dk, dv,
                zg_ref, la_ref, s0_ref, g_ref, y_ref, sfin_ref,
                of_ref, ob_ref, qe_ref, ks_ref, dec_ref, st_ref):
    ch = GLA_CHUNK
    n = l // ch
    heads = qk // dk
    grp = min(GLA_GROUP, n)
    rg = grp * ch
    r_i = lax.broadcasted_iota(jnp.int32, (rg, rg), 0)
    c_i = lax.broadcasted_iota(jnp.int32, (rg, rg), 1)
    same = (r_i // ch) == (c_i // ch)
    low = same & (r_i >= c_i)
    upp = same & (r_i <= c_i)
    masks = (low, upp)
    cum2 = tuple(jnp.concatenate([jnp.concatenate([m.astype(BF16)] * 2, axis=1),
                                  jnp.concatenate([same.astype(BF16)] * 2, axis=1)], axis=0)
                 for m in masks)
    lane_head = lax.broadcasted_iota(jnp.int32, (1, qk), 1) // dk
    scale = dk ** -0.5
    o_refs = (of_ref, ob_ref)

    def local(g, d):
        r0 = pl.multiple_of(g * rg, rg)
        q = zg_ref[pl.ds(r0, rg), 0:qk].astype(F32) * scale
        k = zg_ref[pl.ds(r0, rg), qk:2 * qk].astype(F32)
        v = zg_ref[pl.ds(r0, rg), 2 * qk:2 * qk + wv]
        la = la_ref[pl.ds(r0, rg), d * qk:(d + 1) * qk]
        lh, ll = _split(la)
        bt = _dot(cum2[d], jnp.concatenate([lh, ll], axis=0))
        b, tot = bt[:rg], bt[rg:]
        qe = (q * jnp.exp(b)).astype(BF16)
        ke = (k * jnp.exp(-b)).astype(BF16)
        qe_ref[d, pl.ds(r0, rg), :] = qe
        ks_ref[d, pl.ds(r0, rg), :] = (k * jnp.exp(tot - b)).astype(BF16)
        dec = jnp.exp(tot)
        for c in range(grp):
            dec_ref[d, pl.ds(pl.multiple_of((g * grp + c) * 8, 8), 8), :] = dec[c * ch:c * ch + 8]
        parts = []
        for h in range(heads):
            qh = jnp.where(lane_head == h, qe, jnp.zeros_like(qe))
            att = lax.dot_general(qh, ke, (((1,), (1,)), ((), ())), preferred_element_type=F32)
            att = jnp.where(masks[d], att, 0.0).astype(BF16)
            parts.append(_dot(att, v[:, h * dv:(h + 1) * dv]))
        o_refs[d][pl.ds(r0, rg), :] = jnp.concatenate(parts, axis=1)

    def local_body(g, carry):
        local(g, 0)
        local(g, 1)
        return carry

    lax.fori_loop(0, n // grp, local_body, 0)

    bd = (lax.broadcasted_iota(jnp.int32, (wv, qk), 0) // dv
          == lax.broadcasted_iota(jnp.int32, (wv, qk), 1) // dk).astype(F32)
    st_ref[...] = s0_ref[0]

    def carry_state(c, d):
        r0 = pl.multiple_of(c * ch, ch)
        st = st_ref[d]
        o = lax.dot_general(qe_ref[d, pl.ds(r0, ch), :], st.astype(BF16), (((1,), (1,)), ((), ())),
                            preferred_element_type=F32)
        o_refs[d][pl.ds(r0, ch), :] += o
        v = zg_ref[pl.ds(r0, ch), 2 * qk:2 * qk + wv]
        upd = lax.dot_general(v, ks_ref[d, pl.ds(r0, ch), :], (((0,), (0,)), ((), ())),
                              preferred_element_type=F32)
        dec = dec_ref[d, pl.ds(pl.multiple_of(c * 8, 8), 1), :]
        st_ref[d] = st * dec + upd * bd

    su = min(GLA_STATE_UNROLL, n)

    def finish(c):
        r0 = pl.multiple_of(c * ch, ch)
        o = of_ref[pl.ds(r0, ch), :] + ob_ref[pl.ds(r0, ch), :]
        r = zg_ref[pl.ds(r0, ch), 2 * qk + wv:2 * qk + 2 * wv].astype(F32)
        parts = [_rms(o[:, h * dv:(h + 1) * dv], g_ref[...]) for h in range(heads)]
        y_ref[pl.ds(r0, ch), :] = (jnp.concatenate(parts, axis=1) * _silu(r)).astype(BF16)

    def state_body(done, i, carry):
        for u in range(su):
            carry_state(i * su + u, 0)
            carry_state(n - 1 - (i * su + u), 1)
        if done:
            for u in range(su):
                finish(i * su + u)
                finish(n - 1 - (i * su + u))
        return carry

    trips = n // su
    if trips % 2 == 0:
        lax.fori_loop(0, trips // 2, functools.partial(state_body, False), 0)
        lax.fori_loop(trips // 2, trips, functools.partial(state_body, True), 0)
    else:
        lax.fori_loop(0, trips, functools.partial(state_body, False), 0)
        lax.fori_loop(0, n, lambda c, carry: (finish(c), carry)[1], 0)
    sfin_ref[0] = st_ref[...]


def gla_mixer(zg, la, s0, norm_g, l, n_seq, row0):
    qk = la.shape[1] // 2
    wv = (zg.shape[1] - 2 * qk) // 2
    dk, dv = qk // GLA_HEADS, wv // GLA_HEADS
    sb0 = row0 // l
    return pl.pallas_call(
        functools.partial(_gla_kernel, l, qk, wv, dk, dv),
        grid=(n_seq,),
        in_specs=[pl.BlockSpec((l, zg.shape[1]), lambda s: (sb0 + s, 0)),
                  pl.BlockSpec((l, la.shape[1]), lambda s: (sb0 + s, 0)),
                  pl.BlockSpec((1, 2, wv, qk), lambda s: (s, 0, 0, 0)),
                  pl.BlockSpec((1, dv), lambda s: (0, 0))],
        out_specs=[pl.BlockSpec((l, wv), lambda s: (s, 0)),
                   pl.BlockSpec((1, 2, wv, qk), lambda s: (s, 0, 0, 0))],
        out_shape=[jax.ShapeDtypeStruct((n_seq * l, wv), BF16),
                   jax.ShapeDtypeStruct((n_seq, 2, wv, qk), F32)],
        scratch_shapes=[pltpu.VMEM((l, wv), F32), pltpu.VMEM((l, wv), F32),
                        pltpu.VMEM((2, l, qk), BF16), pltpu.VMEM((2, l, qk), BF16),
                        pltpu.VMEM((2, l // GLA_CHUNK * 8, qk), F32),
                        pltpu.VMEM((2, wv, qk), F32)],
        compiler_params=_cparams(("arbitrary",), VMEM_LIMIT),
        name="gla_mixer",
    )(zg, la, s0, norm_g.reshape(1, dv))


def _outproj_kernel(n_lat_tiles, yh_ref, yg_ref, yhc_ref, ygc_ref, wh_ref, wg_ref, x_ref, g1_ref,
                    g_ref, sh_ref, sc_ref, xo_ref, u_ref):
    i = pl.program_id(0)

    def finish(yh, yg):
        delta = _dot(yh[...], wh_ref[...]) + _dot(yg[...], wg_ref[...])
        x = x_ref[...] + g1_ref[0] * delta
        xo_ref[...] = x
        u_ref[...] = _rms(x, g_ref[...]) * (1.0 + sc_ref[0]) + sh_ref[0]

    @pl.when(i < n_lat_tiles)
    def _():
        finish(yh_ref, yg_ref)

    @pl.when(i >= n_lat_tiles)
    def _():
        finish(yhc_ref, ygc_ref)


def out_proj(y_lat, y_ctx, w_hy, w_gla, xs, mod, norm_g, tm, seq, n_lat, b, n_tok):
    d = xs.shape[1]
    nlt = n_lat // tm
    row = functools.partial(_mod_row, tm=tm, seq=seq, n_lat_tiles=nlt, b=b)
    full = lambda a: pl.BlockSpec(a.shape, lambda i: (0,) * a.ndim)
    modspec = lambda col: pl.BlockSpec((1, 1, d), lambda i: (row(i), 0, col))
    tile = lambda w: pl.BlockSpec((tm, w), lambda i: (i, 0))
    lat = lambda a: pl.BlockSpec((tm, a.shape[1]), lambda i: (jnp.minimum(i, nlt - 1), 0))
    ctx = lambda a: pl.BlockSpec((tm, a.shape[1]), lambda i: (jnp.maximum(i - nlt, 0), 0))
    return pl.pallas_call(
        functools.partial(_outproj_kernel, nlt),
        grid=(n_tok // tm,),
        in_specs=[lat(y_lat[0]), lat(y_lat[1]), ctx(y_ctx[0]), ctx(y_ctx[1]),
                  full(w_hy), full(w_gla), tile(d),
                  modspec(2), full(norm_g), modspec(3), modspec(4)],
        out_specs=[tile(d), tile(d)],
        out_shape=[jax.ShapeDtypeStruct((n_tok, d), F32)] * 2,
        compiler_params=_cparams(("arbitrary",), VMEM_LIMIT),
        name="out_proj",
    )(*y_lat, *y_ctx, w_hy, w_gla, xs, mod, norm_g, mod, mod)


def _router_kernel(n_exp, tm, u_ref, wh_ref, wl_ref, b_ref, tri_ref,
                   idx_ref, gate_ref, rank_ref, cnt_ref):
    i = pl.program_id(0)

    @pl.when(i == 0)
    def _():
        cnt_ref[...] = jnp.zeros_like(cnt_ref)

    uh, ul = _split(u_ref[...])
    nt = (((1,), (1,)), ((), ()))
    logits = (lax.dot_general(wh_ref[...], uh, nt, preferred_element_type=F32)
              + lax.dot_general(wh_ref[...], ul, nt, preferred_element_type=F32)
              + lax.dot_general(wl_ref[...], uh, nt, preferred_element_type=F32))
    s = 1.0 / (1.0 + jnp.exp(-logits))
    sel = s + b_ref[...]
    gsz = n_exp // N_GROUPS
    neg = -jnp.inf
    rows_g = lax.broadcasted_iota(jnp.int32, (gsz, tm), 0)
    blocks, gscore = [], []
    for g in range(N_GROUPS):
        blk = sel[g * gsz:(g + 1) * gsz, :]
        m1 = jnp.max(blk, axis=0, keepdims=True)
        i1 = jnp.min(jnp.where(blk == m1, rows_g, gsz), axis=0, keepdims=True)
        m2 = jnp.max(jnp.where(rows_g == i1, neg, blk), axis=0, keepdims=True)
        blocks.append(blk)
        gscore.append(m1 + m2)
    masked = []
    for g in range(N_GROUPS):
        beaten = jnp.zeros((1, tm), jnp.int32)
        for h in range(N_GROUPS):
            if h == g:
                continue
            wins = (gscore[h] > gscore[g]) | ((gscore[h] == gscore[g]) & (h < g))
            beaten = beaten + wins.astype(jnp.int32)
        masked.append(jnp.where(beaten < TOPK_GROUPS, blocks[g], neg))
    sel = jnp.concatenate(masked, axis=0)

    rows = lax.broadcasted_iota(jnp.int32, (n_exp, tm), 0)
    chosen = jnp.zeros((n_exp, tm), F32)
    idxs, gates = [], []
    for _ in range(TOP_K):
        m = jnp.max(sel, axis=0, keepdims=True)
        ik = jnp.min(jnp.where(sel == m, rows, n_exp), axis=0, keepdims=True)
        hit = rows == ik
        gates.append(jnp.sum(jnp.where(hit, s, 0.0), axis=0, keepdims=True))
        sel = jnp.where(hit, neg, sel)
        chosen = jnp.where(hit, 1.0, chosen)
        idxs.append(ik)
    gsum = gates[0]
    for g in gates[1:]:
        gsum = gsum + g
    inv = ROUTED_SCALE / gsum
    before = _dot(chosen.astype(BF16), tri_ref[...]) + cnt_ref[...]
    for k in range(TOP_K):
        idx_ref[k:k + 1, :] = idxs[k]
        gate_ref[k:k + 1, :] = gates[k] * inv
        rk = jnp.sum(jnp.where(rows == idxs[k], before, 0.0), axis=0, keepdims=True)
        rank_ref[k:k + 1, :] = rk.astype(jnp.int32)
    cnt_ref[...] += jnp.sum(chosen, axis=1, keepdims=True)


def moe_router(u, wr_hi, wr_lo, b_r, n_tok, tm):
    n_exp, d = wr_hi.shape
    tri = (lax.broadcasted_iota(jnp.int32, (tm, tm), 0)
           < lax.broadcasted_iota(jnp.int32, (tm, tm), 1)).astype(BF16)
    full = lambda a: pl.BlockSpec(a.shape, lambda i: (0,) * a.ndim)
    tok = pl.BlockSpec((TOP_K, tm), lambda i: (0, i))
    return pl.pallas_call(
        functools.partial(_router_kernel, n_exp, tm),
        grid=(n_tok // tm,),
        in_specs=[pl.BlockSpec((tm, d), lambda i: (i, 0)), full(wr_hi), full(wr_lo),
                  pl.BlockSpec((n_exp, 1), lambda i: (0, 0)), full(tri)],
        out_specs=[tok, tok, tok, pl.BlockSpec((n_exp, 1), lambda i: (0, 0))],
        out_shape=[jax.ShapeDtypeStruct((TOP_K, n_tok), jnp.int32),
                   jax.ShapeDtypeStruct((TOP_K, n_tok), F32),
                   jax.ShapeDtypeStruct((TOP_K, n_tok), jnp.int32),
                   jax.ShapeDtypeStruct((n_exp, 1), F32)],
        compiler_params=_cparams(("arbitrary",), VMEM_LIMIT),
        name="moe_router",
    )(u, wr_hi, wr_lo, b_r.reshape(n_exp, 1), tri)


def _start_row_copy(src, src_row, dst, dst_row, sem, queue):
    pltpu.async_copy(src.at[src_row], dst.at[dst_row], sem, priority=queue)


def _store_rows_tiled(ref, x):
    for j in range(ref.shape[-2]):
        ref[:, j, :] = x[:, j * LANES:(j + 1) * LANES]


def _load_rows_tiled(ref):
    return jnp.concatenate([ref[:, j, :] for j in range(ref.shape[-2])], axis=1)


def _pack_bf16_pairs(x):
    w = x.shape[1] // 2
    lo = pltpu.bitcast(x[:, :w].astype(BF16).astype(F32), jnp.uint32) >> 16
    hi = pltpu.bitcast(x[:, w:].astype(BF16).astype(F32), jnp.uint32)
    return hi | lo


def _unpack_bf16_pairs(p):
    lo = pltpu.bitcast(p << 16, F32)
    hi = pltpu.bitcast(p & jnp.uint32(0xFFFF0000), F32)
    return jnp.concatenate([lo, hi], axis=1).astype(BF16)


def _dest_kernel(n_exp, tm, idx_ref, rank_ref, ps_ref, dest_ref):
    rows = lax.broadcasted_iota(jnp.int32, (n_exp, tm), 0)
    ps = ps_ref[...]
    for k in range(TOP_K):
        base = jnp.sum(jnp.where(rows == idx_ref[k:k + 1, :], ps, 0), axis=0, keepdims=True)
        dest_ref[k:k + 1, :] = base + rank_ref[k:k + 1, :]


def moe_dest(idx, rank, pad_start, tm):
    n_tok = idx.shape[1]
    n_exp = pad_start.shape[0]
    tok = pl.BlockSpec((TOP_K, tm), lambda i: (0, i))
    return pl.pallas_call(
        functools.partial(_dest_kernel, n_exp, tm),
        grid=(n_tok // tm,),
        in_specs=[tok, tok, pl.BlockSpec((n_exp, 1), lambda i: (0, 0))],
        out_specs=tok,
        out_shape=jax.ShapeDtypeStruct((TOP_K, n_tok), jnp.int32),
        compiler_params=_cparams(("arbitrary",)),
        name="moe_dest",
    )(idx, rank, pad_start.reshape(n_exp, 1))


def _wait_rows(src, dst, sem, tm):
    for _ in range(TOP_K):
        pltpu.make_async_copy(src.at[pl.ds(0, tm)], dst.at[pl.ds(0, tm)], sem).wait()


ZERO_ROWS = MOE_ROWS // 2


def _zero_padding(n_exp, ps_ref, pe_ref, xs_ref, z_ref, sem, wait):
    def go(copy):
        copy.wait() if wait else copy.start()

    def per_expert(e, carry):
        s = ps_ref[e]
        m = pe_ref[e] - s
        bit = ZERO_ROWS
        while bit:
            start = s + (m & ~(2 * bit - 1))

            @pl.when((m & bit) != 0)
            def _():
                go(pltpu.make_async_copy(z_ref.at[pl.ds(0, bit)], xs_ref.at[pl.ds(start, bit)], sem))
            bit //= 2
        return carry

    lax.fori_loop(0, n_exp, per_expert, 0)

    tail = pe_ref[n_exp - 1]

    def per_chunk(j, carry):
        go(pltpu.make_async_copy(z_ref, xs_ref.at[pl.ds(tail + j * ZERO_ROWS, ZERO_ROWS)], sem))
        return carry

    lax.fori_loop(0, (xs_ref.shape[0] - tail) // ZERO_ROWS, per_chunk, 0)


def _dispatch_kernel(tm, n_steps, n_exp, ps_ref, pe_ref, dest_ref, u_ref, xs_ref, pk_ref, z_ref,
                     sems, zsem):
    i = pl.program_id(0)
    slot = i % 2
    pk = pk_ref.at[slot]
    sem = sems.at[slot]

    @pl.when(i == 0)
    def _():
        z_ref[...] = jnp.zeros_like(z_ref)
        _zero_padding(n_exp, ps_ref, pe_ref, xs_ref, z_ref, zsem, wait=False)

    @pl.when(i >= 2)
    def _():
        _wait_rows(pk, xs_ref, sem, tm)

    _store_rows_tiled(pk, _pack_bf16_pairs(u_ref[...]))

    def issue(t, carry):
        for k in range(TOP_K):
            _start_row_copy(pk, t, xs_ref, dest_ref[0, 0, k * tm + t], sem, k % 2)
        return carry

    lax.fori_loop(0, tm, issue, 0)

    @pl.when(i == n_steps - 1)
    def _():
        if n_steps > 1:
            _wait_rows(pk_ref.at[1 - slot], xs_ref, sems.at[1 - slot], tm)
        _wait_rows(pk, xs_ref, sem, tm)
        _zero_padding(n_exp, ps_ref, pe_ref, xs_ref, z_ref, zsem, wait=True)


def moe_dispatch(u, dest, pad_from, pad_end, n_tok, n_slots, tm):
    d = u.shape[1]
    sub = d // 2 // LANES
    n_exp = pad_end.shape[0]
    grid_spec = pltpu.PrefetchScalarGridSpec(
        num_scalar_prefetch=2,
        grid=(n_tok // tm,),
        in_specs=[pl.BlockSpec((1, 1, TOP_K * tm), lambda i, *_: (i, 0, 0), memory_space=pltpu.SMEM),
                  pl.BlockSpec((tm, d), lambda i, *_: (i, 0))],
        out_specs=pl.BlockSpec(memory_space=pl.ANY),
        scratch_shapes=[pltpu.VMEM((2, tm, sub, LANES), jnp.uint32),
                        pltpu.VMEM((ZERO_ROWS, sub, LANES), jnp.uint32),
                        pltpu.SemaphoreType.DMA((2,)), pltpu.SemaphoreType.DMA])
    return pl.pallas_call(
        functools.partial(_dispatch_kernel, tm, n_tok // tm, n_exp),
        grid_spec=grid_spec,
        out_shape=jax.ShapeDtypeStruct((n_slots, sub, LANES), jnp.uint32),
        compiler_params=_cparams(("arbitrary",)),
        name="moe_dispatch",
    )(pad_from, pad_end, dest, u)


X_RING = 3


def _gmm_kernel(layer, be_ref, nx_ref, fl_ref, nu_ref, x_hbm, w1_hbm, w3_hbm, w2_hbm, y_ref,
                w1f, w3f, w2f, w1b, w3b, w2b, sems, xbuf, xsems):
    i = pl.program_id(0)
    fl = fl_ref[i]
    rows = xbuf.shape[1]
    sub = rows // MOE_ROWS
    nu = nu_ref[0]

    def xcopy(j):
        s = j % X_RING
        return pltpu.make_async_copy(x_hbm.at[pl.ds(pl.multiple_of(j * rows, rows), rows)],
                                     xbuf.at[s], xsems.at[s])

    def fetch(e, s):
        return (pltpu.make_async_copy(w1_hbm.at[layer, e], w1f.at[s], sems.at[s]),
                pltpu.make_async_copy(w3_hbm.at[layer, e], w3f.at[s], sems.at[s]),
                pltpu.make_async_copy(w2_hbm.at[layer, e], w2f.at[s], sems.at[s]))

    @pl.when(i < nu)
    def _():
        @pl.when(i == 0)
        def _():
            for j in range(X_RING - 1):
                @pl.when(j < nu)
                def _():
                    xcopy(j).start()

        @pl.when(i + X_RING - 1 < nu)
        def _():
            xcopy(i + X_RING - 1).start()

        @pl.when((fl & 1) != 0)
        def _():
            s = (fl >> 1) & 1

            @pl.when(i == 0)
            def _():
                for c in fetch(be_ref[i], s):
                    c.start()

            for c in fetch(be_ref[i], s):
                c.wait()

            @pl.when((fl & 4) != 0)
            def _():
                for c in fetch(nx_ref[i], 1 - s):
                    c.start()

            w1b[...] = w1f[s].astype(BF16)
            w3b[...] = w3f[s].astype(BF16)
            w2b[...] = w2f[s].astype(BF16)

        xcopy(i).wait()
        x_ref = xbuf.at[i % X_RING]
        x = _unpack_bf16_pairs(jnp.concatenate(
            [x_ref[pl.ds(j, MOE_ROWS, stride=sub), :] for j in range(sub)], axis=1))
        h = _silu(_dot(x, w1b[...])) * _dot(x, w3b[...])
        y = _pack_bf16_pairs(_dot(h.astype(BF16), w2b[...]))
        for j in range(sub):
            y_ref[pl.ds(j, MOE_ROWS, stride=sub), :] = y[:, j * LANES:(j + 1) * LANES]

    @pl.when(i >= nu)
    def _():
        y_ref[...] = jnp.zeros_like(y_ref)


def moe_experts(x_sorted, block_e, n_used, w1, w3, w2, layer):
    n_blocks = block_e.shape[0]
    _, _, d, f = w1.shape
    n_slots, sub, _ = x_sorted.shape
    idx = jnp.arange(n_blocks, dtype=jnp.int32)
    first = (idx == 0) | (block_e != jnp.roll(block_e, 1))
    seg = jnp.cumsum(first.astype(jnp.int32)) - 1
    first_at_or_after = lax.cummin(jnp.where(first, idx, n_blocks)[::-1])[::-1]
    nxt_idx = jnp.concatenate([first_at_or_after[1:], jnp.full((1,), n_blocks, jnp.int32)])
    has_next = nxt_idx < n_used[0]
    nxt_e = block_e[jnp.minimum(nxt_idx, n_blocks - 1)]
    flags = (first.astype(jnp.int32) | ((seg & 1) << 1) | (has_next.astype(jnp.int32) << 2))
    grid_spec = pltpu.PrefetchScalarGridSpec(
        num_scalar_prefetch=4,
        grid=(n_blocks,),
        in_specs=[pl.BlockSpec(memory_space=pl.ANY),
                  pl.BlockSpec(memory_space=pl.ANY),
                  pl.BlockSpec(memory_space=pl.ANY),
                  pl.BlockSpec(memory_space=pl.ANY)],
        out_specs=pl.BlockSpec((MOE_ROWS * sub, LANES), lambda i, *_: (i, 0)),
        scratch_shapes=[pltpu.VMEM((2, d, f), F32), pltpu.VMEM((2, d, f), F32),
                        pltpu.VMEM((2, f, d), F32),
                        pltpu.VMEM((d, f), BF16), pltpu.VMEM((d, f), BF16),
                        pltpu.VMEM((f, d), BF16), pltpu.SemaphoreType.DMA((2,)),
                        pltpu.VMEM((X_RING, MOE_ROWS * sub, LANES), jnp.uint32),
                        pltpu.SemaphoreType.DMA((X_RING,))])
    y = pl.pallas_call(
        functools.partial(_gmm_kernel, layer),
        grid_spec=grid_spec,
        out_shape=jax.ShapeDtypeStruct((n_slots * sub, LANES), jnp.uint32),
        compiler_params=_cparams(("arbitrary",), VMEM_LIMIT),
        name="moe_experts",
    )(block_e, nxt_e, flags, n_used, x_sorted.reshape(n_slots * sub, LANES), w1, w3, w2)
    return y.reshape(n_slots, sub, LANES)


def _combine_kernel(tm, n_steps, final, dest_ref, nxt_ref, y_ref, gate_ref, u_ref, ws1_ref, ws3_ref,
                    ws2_ref, x_ref, g2_ref, fg_ref, o_ref, buf_ref, routed_ref, sems):
    i = pl.program_id(0)
    slot = i % 2
    grp = SUBLANES

    def gather(d_ref, s, g):
        for tt in range(grp):
            t = g * grp + tt
            for k in range(TOP_K):
                _start_row_copy(y_ref, d_ref[0, 0, k * tm + t], buf_ref.at[s, k], t, sems.at[s], k % 2)

    def reduce(g, s):
        r0 = pl.multiple_of(g * grp, grp)
        gt = gate_ref[pl.ds(r0, grp), :]
        lo = hi = None
        for k in range(TOP_K):
            p = buf_ref[s, k, pl.ds(r0, grp)]
            gk = gt[:, k:k + 1][:, :, None]
            lo_k = pltpu.bitcast(p << 16, F32) * gk
            hi_k = pltpu.bitcast(p & jnp.uint32(0xFFFF0000), F32) * gk
            lo = lo_k if lo is None else lo + lo_k
            hi = hi_k if hi is None else hi + hi_k
        sub = lo.shape[1]
        routed_ref[pl.ds(r0, grp), 0:sub, :] = lo
        routed_ref[pl.ds(r0, grp), sub:2 * sub, :] = hi

    @pl.when(i == 0)
    def _():
        lax.fori_loop(0, tm // grp, lambda g, c: (gather(dest_ref, slot, g), c)[1], 0)

    _wait_rows(y_ref, buf_ref.at[slot, 0], sems.at[slot], tm)

    @pl.when(i + 1 < n_steps)
    def _():
        def both(g, c):
            gather(nxt_ref, 1 - slot, g)
            reduce(g, slot)
            return c
        lax.fori_loop(0, tm // grp, both, 0)

    @pl.when(i + 1 >= n_steps)
    def _():
        lax.fori_loop(0, tm // grp, lambda g, c: (reduce(g, slot), c)[1], 0)

    ub = u_ref[...].astype(BF16)
    hs = _silu(_dot(ub, ws1_ref[...])) * _dot(ub, ws3_ref[...])
    shared = _dot(hs.astype(BF16), ws2_ref[...])
    x = x_ref[...] + g2_ref[0] * (_load_rows_tiled(routed_ref) + shared)
    o_ref[...] = _rms(x, fg_ref[...]) if final else x


def moe_combine(y_sorted, dest, gate_rep, u, ws1, ws3, ws2, xs, mod, final_g, final,
                tm, seq, n_lat, b, n_tok):
    d = u.shape[1]
    nlt = n_lat // tm
    row = functools.partial(_mod_row, tm=tm, seq=seq, n_lat_tiles=nlt, b=b)
    full = lambda a: pl.BlockSpec(a.shape, lambda i: (0,) * a.ndim)
    tile = pl.BlockSpec((tm, d), lambda i: (i, 0))
    n_steps = n_tok // tm
    return pl.pallas_call(
        functools.partial(_combine_kernel, tm, n_steps, final),
        grid=(n_steps,),
        in_specs=[pl.BlockSpec((1, 1, TOP_K * tm), lambda i: (i, 0, 0), memory_space=pltpu.SMEM),
                  pl.BlockSpec((1, 1, TOP_K * tm),
                               lambda i: (jnp.minimum(i + 1, n_steps - 1), 0, 0),
                               memory_space=pltpu.SMEM),
                  pl.BlockSpec(memory_space=pl.ANY),
                  pl.BlockSpec((tm, TOP_K), lambda i: (i, 0)),
                  tile, full(ws1), full(ws3), full(ws2), tile,
                  pl.BlockSpec((1, 1, d), lambda i: (row(i), 0, 5)),
                  full(final_g)],
        out_specs=tile,
        out_shape=jax.ShapeDtypeStruct((n_tok, d), F32),
        scratch_shapes=[pltpu.VMEM((2, TOP_K, tm) + y_sorted.shape[1:], jnp.uint32),
                        pltpu.VMEM((tm, d // LANES, LANES), F32), pltpu.SemaphoreType.DMA((2,))],
        compiler_params=_cparams(("arbitrary",), VMEM_LIMIT),
        name="moe_combine",
    )(dest, dest, y_sorted, gate_rep, u, ws1, ws3, ws2, xs, mod, final_g)


def moe_layer(u, xs, mod, router_w, router_b, w1, w3, w2, layer, ws1, ws3, ws2, final_g, final,
              tm, seq, n_lat, b, n_tok):
    n_exp = router_w.shape[1]
    wr_hi, wr_lo = _split(router_w.T)
    idx, gate, rank, cnt = moe_router(u, wr_hi, wr_lo, router_b, n_tok, tm)
    counts = cnt[:, 0].astype(jnp.int32)
    padded = (counts + MOE_ROWS - 1) // MOE_ROWS * MOE_ROWS
    pad_end = jnp.cumsum(padded)
    pad_start = pad_end - padded
    dest = moe_dest(idx, rank, pad_start, math.gcd(n_tok, DEST_TILE))
    n_blocks = -(-(n_tok * TOP_K + n_exp * (MOE_ROWS - 1)) // MOE_ROWS)
    block_start = jnp.arange(n_blocks, dtype=jnp.int32) * MOE_ROWS
    block_e = jnp.minimum(jnp.sum((pad_end[None, :] <= block_start[:, None]).astype(jnp.int32), axis=1),
                          n_exp - 1)
    n_used = (pad_end[-1:] // MOE_ROWS).astype(jnp.int32)
    dest = dest.reshape(TOP_K, n_tok // tm, tm).transpose(1, 0, 2).reshape(n_tok // tm, 1, TOP_K * tm)
    x_sorted = moe_dispatch(u, dest, pad_start + counts, pad_end, n_tok, n_blocks * MOE_ROWS, tm)
    y_sorted = moe_experts(x_sorted, block_e, n_used, w1, w3, w2, layer)
    return moe_combine(y_sorted, dest, gate.T, u, ws1.astype(BF16), ws3.astype(BF16),
                       ws2.astype(BF16), xs, mod, final_g.reshape(1, -1), final,
                       tm, seq, n_lat, b, n_tok)


def _grid_pos_emb(rows, d):
    r, col = np.meshgrid(np.arange(rows, dtype=np.float64), np.arange(GRID_W, dtype=np.float64),
                         indexing='ij')
    quarter = d // 4
    omega = 1.0 / (10000.0 ** (np.arange(quarter, dtype=np.float64) / quarter))

    def emb(p):
        a = p.reshape(-1)[:, None] * omega[None, :]
        return np.concatenate([np.sin(a), np.cos(a)], axis=-1)

    return jnp.asarray(np.concatenate([emb(r), emb(col)], axis=-1).astype(np.float32))


def kernel(x, c, ctx, c_ctx, ada_w, ada_b, norm1_g, norm2_g, w_in, w_out, hy_conv_w, hy_conv_b,
           hy_f_w1, hy_f_b1, hy_f_w2, hy_f_b2, hy_f_w3, hy_skip, gla_wa2, gla_ba2, gla_norm_g,
           router_w, router_b, exp_w1, exp_w3, exp_w2, sh_w1, sh_w3, sh_w2, final_g):
    b, l, d = x.shape
    lc = ctx.shape[1]
    depth = ada_w.shape[0]
    n_lat, n_ctx = b * l, b * lc
    n_all = n_lat + n_ctx
    hy_width = hy_skip.shape[-1]
    hy_cols = hy_conv_w.shape[-1]
    qk = gla_wa2.shape[-1]
    wv = d - hy_width
    g_cols = 2 * qk + 2 * wv
    tm = math.gcd(math.gcd(l, lc), TOKEN_TILE)
    tm_proj = math.gcd(math.gcd(l, n_ctx), PROJ_TILE)
    cb = min(hy_width, MXU_DIM)

    cond = jnp.concatenate([c, c_ctx[None, :]], axis=0)
    n_rows = -(-(b + 1) // SUBLANES) * SUBLANES
    cond = jnp.concatenate([cond, jnp.zeros((n_rows - b - 1, d), F32)], axis=0)
    mods = ada_table(cond, ada_w, ada_b)

    xs = (x.reshape(n_lat, d), _grid_pos_emb(l // GRID_W, d), ctx.reshape(n_ctx, d))

    tabs = {}
    for length in (l, lc):
        if length not in tabs:
            tabs[length] = _trig_tables(length, True) + _trig_tables(length, False)
    zero_state = jnp.zeros((b, 2, wv, qk), F32)

    for layer in range(depth):
        last = layer == depth - 1
        mod = mods[layer][:, None, :]
        g1 = norm1_g[layer].reshape(1, d)
        g2 = norm2_g[layer].reshape(1, d)
        w_l = w_in[layer].astype(BF16)
        wa2 = gla_wa2[layer]
        rank = wa2.shape[1]
        wa2_bd = jnp.zeros((2 * rank, 2 * qk), F32)
        wa2_bd = wa2_bd.at[:rank, :qk].set(wa2[0]).at[rank:, qk:].set(wa2[1])
        ba2_cat = gla_ba2[layer].reshape(1, 2 * qk)
        outs = in_proj(xs, g1, mod, w_l[:, :hy_cols], w_l[:, hy_cols:hy_cols + g_cols],
                       w_l[:, hy_cols + g_cols:], wa2_bd, ba2_cat, tm_proj, l, n_lat, b)
        zh, zg, la = outs[:3]
        if layer == 0:
            xs = outs[3]

        filt = (hy_f_w1[layer], hy_f_b1[layer], hy_f_w2[layer], hy_f_b2[layer], hy_f_w3[layer],
                hy_skip[layer])
        yc_gla, s_ctx = gla_mixer(zg, la, zero_state, gla_norm_g[layer], lc, b, n_lat)
        y_gla, _ = gla_mixer(zg, la, s_ctx, gla_norm_g[layer], l, b, 0)
        ct, st, c0, s0 = tabs[l]
        gr, gi = hyena_spectra(l, *filt, c0, s0)
        y_hy = hyena_mixer(zh, hy_conv_w[layer], hy_conv_b[layer], ct, st, gr, gi, l, b, 0, cb)
        if last:
            n_tok = n_lat
            y_ctx = (y_hy, y_gla)
        else:
            n_tok = n_all
            ct, st, c0, s0 = tabs[lc]
            gr, gi = hyena_spectra(lc, *filt, c0, s0)
            yc_hy = hyena_mixer(zh, hy_conv_w[layer], hy_conv_b[layer], ct, st, gr, gi,
                                lc, b, n_lat, cb)
            y_ctx = (yc_hy, yc_gla)
        w_o = w_out[layer].astype(BF16)
        xs_mid, u = out_proj((y_hy, y_gla), y_ctx, w_o[:hy_width], w_o[hy_width:], xs, mod, g2,
                             tm, l, n_lat, b, n_tok)
        xs = moe_layer(u, xs_mid, mod, router_w[layer], router_b[layer], exp_w1, exp_w3, exp_w2,
                       layer, sh_w1[layer], sh_w3[layer], sh_w2[layer],
                       final_g, last, tm, l, n_lat, b, n_tok)
    return xs[:n_lat].reshape(b, l, d)
```
